```python
import jax, jax.numpy as jnp
from jax import lax
import numpy as np

D_MODEL = 1024
BATCH = 16
SEQ = 256
DEPTH = 1
DEC_BATCH = 4
DEC_SEQ = 2048
PAST_LEN = 256

GRID_W = 64
MIX_W = D_MODEL
ATTN_W = MIX_W // 2
LRU_W = MIX_W - ATTN_W
HEAD_DIM = 64
N_HEADS = ATTN_W // HEAD_DIM
N_KV_HEADS = 2
KV_GROUP = N_HEADS // N_KV_HEADS
KV_W = N_KV_HEADS * HEAD_DIM
LRU_BLOCKS = 8
LRU_BLOCK_W = LRU_W // LRU_BLOCKS
CONV_W = 4
LRU_C = 8.0
IN_W = ATTN_W + 2 * KV_W + 2 * LRU_W
N_GROUPS = 4
EXPERTS_PER_GROUP = 8
N_EXPERTS = N_GROUPS * EXPERTS_PER_GROUP
TOP_K = 2
EXPERT_FF = D_MODEL // 4
Q_BLOCK = 128
ROPE_THETA = 10000.0
EPS = 1e-6

kernel_name = 'hymba_flow_hybrid_step'

F32 = jnp.float32


def rms_norm(x, g):
    x32 = x.astype(F32)
    y = x32 * lax.rsqrt(jnp.mean(x32 * x32, axis=-1, keepdims=True) + EPS)
    return (y * g.astype(F32)).astype(x.dtype)


def adaln(cvec, w, b):
    m = jax.nn.silu(cvec) @ w + b
    m = m.reshape(cvec.shape[:-1] + (6, D_MODEL))
    if cvec.ndim == 2:
        m = m[:, None]
    return tuple(m[..., i, :] for i in range(6))


def modulate(h, shift, scale):
    return h * (1 + scale) + shift


def axial_rope_tables(length):
    rows = length // GRID_W
    r, col = jnp.meshgrid(jnp.arange(rows), jnp.arange(GRID_W), indexing='ij')
    r = r.reshape(-1).astype(F32)
    col = col.reshape(-1).astype(F32)
    half = HEAD_DIM // 2
    inv = ROPE_THETA ** (-jnp.arange(0, half, 2, dtype=F32) / half)
    ang_r = r[:, None] * inv
    ang_c = col[:, None] * inv
    ang = jnp.concatenate([ang_r, ang_r, ang_c, ang_c], axis=-1)
    return jnp.cos(ang), jnp.sin(ang)


def apply_rope(x, cos, sin):
    x32 = x.astype(F32)
    x1, x2, x3, x4 = jnp.split(x32, 4, axis=-1)
    rot = jnp.concatenate([-x2, x1, -x4, x3], axis=-1)
    return (x32 * cos[:, None, :] + rot * sin[:, None, :]).astype(x.dtype)


def attend(q, k, v):
    B, S, H, Dh = q.shape
    nb = S // Q_BLOCK
    qb = q.reshape(B, nb, Q_BLOCK, N_KV_HEADS, KV_GROUP, Dh).transpose(1, 0, 2, 3, 4, 5)
    k32 = k.astype(F32)
    v32 = v.astype(F32)
    scale = Dh ** -0.5

    def block(qblk):
        s = jnp.einsum('bqkgd,btkd->bkgqt', qblk.astype(F32), k32) * scale
        p = jax.nn.softmax(s, axis=-1)
        return jnp.einsum('bkgqt,btkd->bqkgd', p, v32)

    o = lax.map(block, qb)
    return o.transpose(1, 0, 2, 3, 4, 5).reshape(B, S, H * Dh).astype(q.dtype)


def dw_conv(x, w, b):
    y = lax.conv_general_dilated(
        x, w[:, None, :].astype(x.dtype), window_strides=(1,),
        padding=[((CONV_W - 1) // 2, CONV_W // 2)],
        dimension_numbers=('NWC', 'WIO', 'NWC'), feature_group_count=x.shape[-1])
    return y + b


def linear_scan(a, b, h0, reverse):
    if reverse:
        b = b.at[:, -1].add(a[:, -1] * h0)
    else:
        b = b.at[:, 0].add(a[:, 0] * h0)

    def comb(left, right):
        a1, b1 = left
        a2, b2 = right
        return a1 * a2, a2 * b1 + b2

    _, h = lax.associative_scan(comb, (a, b), reverse=reverse, axis=1)
    return h


def rg_lru(xc, h0s, wa, ba, wx, bx, lam):
    B, L, W = xc.shape
    x32 = xc.astype(F32)
    xb = x32.reshape(B, L, LRU_BLOCKS, LRU_BLOCK_W)
    outs, finals = [], []
    for d in range(2):
        r = jax.nn.sigmoid(jnp.einsum('blhi,hij->blhj', xb, wa[d].astype(F32)).reshape(B, L, W) + ba[d].astype(F32))
        i = jax.nn.sigmoid(jnp.einsum('blhi,hij->blhj', xb, wx[d].astype(F32)).reshape(B, L, W) + bx[d].astype(F32))
        log_a = LRU_C * r * jax.nn.log_sigmoid(lam[d].astype(F32))
        a = jnp.exp(log_a)
        mult = jnp.sqrt(-jnp.expm1(2.0 * log_a))
        h = linear_scan(a, mult * i * x32, h0s[:, d].astype(F32), reverse=(d == 1))
        outs.append(h)
        finals.append(h[:, 0] if d == 1 else h[:, -1])
    return (outs[0] + outs[1]).astype(xc.dtype), jnp.stack(finals, axis=1).astype(xc.dtype)


def moe(h, wg, bg, we, be, w_gate, w_up, w_down):
    shp = h.shape
    x = h.reshape(-1, D_MODEL)
    x32 = x.astype(F32)
    p_grp = jax.nn.softmax(x32 @ wg.astype(F32) + bg.astype(F32), axis=-1)
    g_sel = jnp.argmax(p_grp, axis=-1)
    p_sel = jnp.take_along_axis(p_grp, g_sel[:, None], axis=-1)[:, 0]
    fine = (x32 @ we.astype(F32) + be.astype(F32)).reshape(-1, N_GROUPS, EXPERTS_PER_GROUP)
    fine_sel = jnp.take_along_axis(fine, g_sel[:, None, None], axis=1)[:, 0]
    top_v, top_i = lax.top_k(fine_sel, TOP_K)
    w_fine = jax.nn.softmax(top_v, axis=-1) * p_sel[:, None]
    expert_idx = g_sel[:, None] * EXPERTS_PER_GROUP + top_i
    combine = jnp.sum(jax.nn.one_hot(expert_idx, N_EXPERTS, dtype=F32) * w_fine[..., None], axis=1)
    hg = jnp.einsum('nd,edf->nef', x, w_gate)
    hu = jnp.einsum('nd,edf->nef', x, w_up)
    act = jax.nn.silu(hg) * hu * combine[:, :, None].astype(x.dtype)
    y = jnp.einsum('nef,efd->nd', act, w_down)
    return y.reshape(shp)


def trunk_layer(x, cvec, lp, ctx_k=None, ctx_v=None, ctx_state=None):
    latent = ctx_k is not None
    B, L, _ = x.shape
    sh1, sc1, g1, sh2, sc2, g2 = adaln(cvec, lp['w_mod'], lp['b_mod'])
    h = modulate(rms_norm(x, lp['norm1']), sh1, sc1)
    z = h @ lp['w_in']
    q, k, v, xr, gb = jnp.split(z, [ATTN_W, ATTN_W + KV_W, ATTN_W + 2 * KV_W, ATTN_W + 2 * KV_W + LRU_W], axis=-1)
    q = rms_norm(q.reshape(B, L, N_HEADS, HEAD_DIM), lp['q_norm'])
    k = rms_norm(k.reshape(B, L, N_KV_HEADS, HEAD_DIM), lp['k_norm'])
    v = v.reshape(B, L, N_KV_HEADS, HEAD_DIM)
    if latent:
        cos, sin = axial_rope_tables(L)
        q = apply_rope(q, cos, sin)
        k = apply_rope(k, cos, sin)
        k_all = jnp.concatenate([k, ctx_k.astype(k.dtype)], axis=1)
        v_all = jnp.concatenate([v, ctx_v.astype(v.dtype)], axis=1)
        h0 = ctx_state
    else:
        k_all, v_all = k, v
        h0 = jnp.zeros((B, 2, LRU_W), x.dtype)
    attn = attend(q, k_all, v_all)
    xc = dw_conv(xr, lp['conv_w'], lp['conv_b'])
    lru, finals = rg_lru(xc, h0, lp['lru_wa'], lp['lru_ba'], lp['lru_wx'], lp['lru_bx'], lp['lru_lambda'])
    rec = lru * jax.nn.gelu(gb)
    x = x + g1 * (jnp.concatenate([attn, rec], axis=-1) @ lp['w_out'])
    h2 = modulate(rms_norm(x, lp['norm2']), sh2, sc2)
    x = x + g2 * moe(h2, lp['router_grp_w'], lp['router_grp_b'], lp['router_exp_w'], lp['router_exp_b'],
                     lp['exp_w_gate'], lp['exp_w_up'], lp['exp_w_down'])
    if latent:
        return x
    return x, k, v, finals


def setup_inputs(seed: int = 0) -> dict:
    key = jax.random.key(seed)
    ks = jax.random.split(key, 32)
    nrm = lambda k, shape, s: jax.random.normal(k, shape, F32) * s
    u = jax.random.uniform(ks[20], (DEPTH, 2, LRU_W), F32, 0.9, 0.999)
    a_base = u ** (1.0 / LRU_C)
    lam = jnp.log(a_base) - jnp.log1p(-a_base)
    return {
        'x_prompt': nrm(ks[0], (BATCH, SEQ, D_MODEL), 1.0),
        'x_sample': nrm(ks[1], (DEC_BATCH, DEC_SEQ, D_MODEL), 1.0),
        'cache_k': nrm(ks[2], (DEC_BATCH, DEPTH, PAST_LEN, N_KV_HEADS, HEAD_DIM), 1.0),
        'cache_v': nrm(ks[3], (DEC_BATCH, DEPTH, PAST_LEN, N_KV_HEADS, HEAD_DIM), 1.0),
        'state_lru': nrm(ks[4], (DEC_BATCH, DEPTH, 2, LRU_W), 0.5),
        'c': nrm(ks[5], (DEC_BATCH, D_MODEL), 1.0),
        'c_ctx': nrm(ks[6], (D_MODEL,), 1.0),
        'w_mod': nrm(ks[7], (DEPTH, D_MODEL, 6 * D_MODEL), 0.5 * D_MODEL ** -0.5),
        'b_mod': nrm(ks[8], (DEPTH, 6 * D_MODEL), 0.02),
        'norm1': 1.0 + nrm(ks[9], (DEPTH, D_MODEL), 0.05),
        'norm2': 1.0 + nrm(ks[10], (DEPTH, D_MODEL), 0.05),
        'w_in': nrm(ks[11], (DEPTH, D_MODEL, IN_W), D_MODEL ** -0.5),
        'q_norm': 1.0 + nrm(ks[12], (DEPTH, HEAD_DIM), 0.05),
        'k_norm': 1.0 + nrm(ks[13], (DEPTH, HEAD_DIM), 0.05),
        'conv_w': nrm(ks[14], (DEPTH, CONV_W, LRU_W), CONV_W ** -0.5),
        'conv_b': nrm(ks[15], (DEPTH, LRU_W), 0.02),
        'lru_wa': nrm(ks[16], (DEPTH, 2, LRU_BLOCKS, LRU_BLOCK_W, LRU_BLOCK_W), LRU_BLOCK_W ** -0.5),
        'lru_ba': nrm(ks[17], (DEPTH, 2, LRU_W), 0.02),
        'lru_wx': nrm(ks[18], (DEPTH, 2, LRU_BLOCKS, LRU_BLOCK_W, LRU_BLOCK_W), LRU_BLOCK_W ** -0.5),
        'lru_bx': nrm(ks[19], (DEPTH, 2, LRU_W), 0.02),
        'lru_lambda': lam,
        'w_out': nrm(ks[21], (DEPTH, MIX_W, D_MODEL), MIX_W ** -0.5),
        'router_grp_w': nrm(ks[22], (DEPTH, D_MODEL, N_GROUPS), D_MODEL ** -0.5),
        'router_grp_b': nrm(ks[23], (DEPTH, N_GROUPS), 0.01),
        'router_exp_w': nrm(ks[24], (DEPTH, D_MODEL, N_EXPERTS), D_MODEL ** -0.5),
        'router_exp_b': nrm(ks[25], (DEPTH, N_EXPERTS), 0.01),
        'exp_w_gate': nrm(ks[26], (DEPTH, N_EXPERTS, D_MODEL, EXPERT_FF), D_MODEL ** -0.5),
        'exp_w_up': nrm(ks[27], (DEPTH, N_EXPERTS, D_MODEL, EXPERT_FF), D_MODEL ** -0.5),
        'exp_w_down': nrm(ks[28], (DEPTH, N_EXPERTS, EXPERT_FF, D_MODEL), EXPERT_FF ** -0.5),
    }


def reference(x_prompt, x_sample, cache_k, cache_v, state_lru, c, c_ctx, w_mod, b_mod, norm1, norm2,
              w_in, q_norm, k_norm, conv_w, conv_b, lru_wa, lru_ba, lru_wx, lru_bx, lru_lambda, w_out,
              router_grp_w, router_grp_b, router_exp_w, router_exp_b, exp_w_gate, exp_w_up, exp_w_down):
    y_prompt = x_prompt
    y_sample = x_sample
    ks_, vs_, ss_ = [], [], []
    for l in range(DEPTH):
        lp = {
            'w_mod': w_mod[l], 'b_mod': b_mod[l], 'norm1': norm1[l], 'norm2': norm2[l],
            'w_in': w_in[l], 'q_norm': q_norm[l], 'k_norm': k_norm[l],
            'conv_w': conv_w[l], 'conv_b': conv_b[l],
            'lru_wa': lru_wa[l], 'lru_ba': lru_ba[l], 'lru_wx': lru_wx[l], 'lru_bx': lru_bx[l],
            'lru_lambda': lru_lambda[l], 'w_out': w_out[l],
            'router_grp_w': router_grp_w[l], 'router_grp_b': router_grp_b[l],
            'router_exp_w': router_exp_w[l], 'router_exp_b': router_exp_b[l],
            'exp_w_gate': exp_w_gate[l], 'exp_w_up': exp_w_up[l], 'exp_w_down': exp_w_down[l],
        }
        y_prompt, k_l, v_l, s_l = trunk_layer(y_prompt, c_ctx, lp)
        ks_.append(k_l)
        vs_.append(v_l)
        ss_.append(s_l)
        y_sample = trunk_layer(y_sample, c, lp, cache_k[:, l], cache_v[:, l], state_lru[:, l])
    new_cache_k = jnp.stack(ks_, axis=1)
    new_cache_v = jnp.stack(vs_, axis=1)
    new_state_lru = jnp.stack(ss_, axis=1)
    return (y_prompt, y_sample, new_cache_k, new_cache_v, new_state_lru)
```

```python
import functools

import jax
import jax.numpy as jnp
from jax import lax
from jax.experimental import pallas as pl
from jax.experimental.pallas import tpu as pltpu

F32 = jnp.float32
BF16 = jnp.bfloat16

D_MODEL = 1024
GRID_W = 64
ATTN_W = 512
LRU_W = 512
HEAD_DIM = 64
N_HEADS = 8
N_KV_HEADS = 2
KV_W = N_KV_HEADS * HEAD_DIM
LRU_BLOCKS = 8
LRU_BLOCK_W = LRU_W // LRU_BLOCKS
CONV_W = 4
LRU_C = 8.0
IN_W = ATTN_W + 2 * KV_W + 2 * LRU_W
QK_W = ATTN_W + KV_W
N_GROUPS = 4
EXPERTS_PER_GROUP = 8
N_EXPERTS = N_GROUPS * EXPERTS_PER_GROUP
EXPERT_FF = D_MODEL // 4
ROPE_THETA = 10000.0
EPS = 1e-6

LANES = 128
SUBLANES = 8
MOD_ROWS = 8
VMEM_LIMIT = 48 * 1024 * 1024

TM_PRE = 256
TQ_ATT = 256
TC_LRU = 256
TM_MOE = 512


def _cparams(sem):
    return pltpu.CompilerParams(dimension_semantics=sem, vmem_limit_bytes=VMEM_LIMIT)


def _dot(a, b):
    return jnp.dot(a, b, preferred_element_type=F32)


def _dot_nt(a, b):
    return lax.dot_general(a, b, (((1,), (1,)), ((), ())), preferred_element_type=F32)


def _split_bf16(x):
    hi = x.astype(BF16)
    lo = (x - hi.astype(F32)).astype(BF16)
    return hi, lo


def _mod_kernel(c_ref, w_ref, b_ref, o_ref):
    c = c_ref[...]
    s = (c * jax.nn.sigmoid(c)).astype(BF16)
    o_ref[...] = _dot(s, w_ref[...].astype(BF16)) + b_ref[...]


def _modulation(cvec, w_mod, b_mod):
    n_out = w_mod.shape[1]
    tn = n_out // 4
    return pl.pallas_call(
        _mod_kernel,
        grid=(n_out // tn,),
        in_specs=[pl.BlockSpec((MOD_ROWS, D_MODEL), lambda j: (0, 0)),
                  pl.BlockSpec((D_MODEL, tn), lambda j: (0, j)),
                  pl.BlockSpec((1, tn), lambda j: (0, j))],
        out_specs=pl.BlockSpec((MOD_ROWS, tn), lambda j: (0, j)),
        out_shape=jax.ShapeDtypeStruct((MOD_ROWS, n_out), F32),
        compiler_params=_cparams(("arbitrary",)),
        name="modulation",
    )(cvec, w_mod, b_mod)


def _pre_kernel(*refs, rope):
    if rope:
        (x_ref, mod_ref, g1_ref, win_ref, ones_ref, gqk_ref, cos_ref, sin_ref,
         q_ref, kx_ref, vx_ref, kf_ref, vf_ref, xr_ref, gb_ref) = refs
    else:
        (x_ref, mod_ref, g1_ref, win_ref, ones_ref, gqk_ref,
         q_ref, kx_ref, vx_ref, kf_ref, vf_ref, xr_ref, gb_ref) = refs
    x = x_ref[...]
    m = mod_ref[0]
    ms = jnp.mean(x * x, axis=-1, keepdims=True)
    y = x * lax.rsqrt(ms + EPS) * g1_ref[...]
    h = y * (1.0 + m[1:2]) + m[0:1]
    z = _dot(h.astype(BF16), win_ref[...])

    qk = z[:, :QK_W]
    hi, lo = _split_bf16(qk * qk)
    ss = _dot(hi, ones_ref[...]) + _dot(lo, ones_ref[...])
    qk = qk * lax.rsqrt(ss * (1.0 / HEAD_DIM) + EPS) * gqk_ref[...]

    lane = lax.broadcasted_iota(jnp.int32, (x.shape[0], LANES), 1)
    cols = []
    for c in range(QK_W // LANES):
        xc = qk[:, c * LANES:(c + 1) * LANES]
        if rope:
            left = pltpu.roll(xc, LANES - HEAD_DIM // 4, 1)
            right = pltpu.roll(xc, HEAD_DIM // 4, 1)
            rot = jnp.where((lane // (HEAD_DIM // 4)) % 2 == 0, left, right)
            xc = xc * cos_ref[...] + rot * sin_ref[...]
        cols.append(xc)
    for c in range(ATTN_W // LANES):
        q_ref[:, c * LANES:(c + 1) * LANES] = (cols[c] * (HEAD_DIM ** -0.5)).astype(BF16)

    lo_half = lane < HEAD_DIM
    for col, fref, xref in ((cols[ATTN_W // LANES], kf_ref, kx_ref),
                            (z[:, QK_W:QK_W + KV_W], vf_ref, vx_ref)):
        fref[...] = col
        swapped = pltpu.roll(col, HEAD_DIM, 1)
        xref[:, 0 * LANES:1 * LANES] = jnp.where(lo_half, col, 0.0).astype(BF16)
        xref[:, 1 * LANES:2 * LANES] = jnp.where(lo_half, 0.0, swapped).astype(BF16)
        xref[:, 2 * LANES:3 * LANES] = jnp.where(lo_half, swapped, 0.0).astype(BF16)
        xref[:, 3 * LANES:4 * LANES] = jnp.where(lo_half, 0.0, col).astype(BF16)

    xr_ref[...] = z[:, QK_W + KV_W:QK_W + KV_W + LRU_W]
    gb_ref[...] = z[:, QK_W + KV_W + LRU_W:]


def _pre(x, mod3, mod_row, g1, w_in, ones_qk, gqk, tables, seq_len):
    n = x.shape[0]
    tm = TM_PRE
    tiles_per_seq = seq_len // tm
    rope = tables is not None
    const = lambda i: (0, 0)
    in_specs = [pl.BlockSpec((tm, D_MODEL), lambda i: (i, 0)),
                pl.BlockSpec((1, 6, D_MODEL), lambda i: (mod_row(i // tiles_per_seq), 0, 0)),
                pl.BlockSpec((1, D_MODEL), const),
                pl.BlockSpec((D_MODEL, IN_W), const),
                pl.BlockSpec((QK_W, QK_W), const),
                pl.BlockSpec((1, QK_W), const)]
    args = [x, mod3, g1, w_in, ones_qk, gqk]
    if rope:
        in_specs += [pl.BlockSpec((tm, LANES), lambda i: (i % tiles_per_seq, 0))] * 2
        args += list(tables)
    row = lambda w: pl.BlockSpec((tm, w), lambda i: (i, 0))
    out_shape = [jax.ShapeDtypeStruct((n, ATTN_W), BF16),
                 jax.ShapeDtypeStruct((n, 4 * LANES), BF16),
                 jax.ShapeDtypeStruct((n, 4 * LANES), BF16),
                 jax.ShapeDtypeStruct((n, KV_W), F32),
                 jax.ShapeDtypeStruct((n, KV_W), F32),
                 jax.ShapeDtypeStruct((n, LRU_W), F32),
                 jax.ShapeDtypeStruct((n, LRU_W), F32)]
    out_specs = [row(ATTN_W), row(4 * LANES), row(4 * LANES), row(KV_W), row(KV_W),
                 row(LRU_W), row(LRU_W)]
    return pl.pallas_call(
        functools.partial(_pre_kernel, rope=rope),
        grid=(n // tm,),
        in_specs=in_specs, out_specs=out_specs, out_shape=out_shape,
        compiler_params=_cparams(("arbitrary",)),
        name="pre_rope" if rope else "pre",
    )(*args)


def _attn_kernel(*refs, n_seg):
    q_ref = refs[0]
    k_refs = refs[1:1 + n_seg]
    v_refs = refs[1 + n_seg:1 + 2 * n_seg]
    o_ref = refs[1 + 2 * n_seg]
    for c in range(ATTN_W // LANES):
        qc = q_ref[:, c * LANES:(c + 1) * LANES]
        g = c // 2
        acc = None
        for par in range(2):
            sl = slice((2 * g + par) * LANES, (2 * g + par + 1) * LANES)
            ss = [_dot_nt(qc, k[:, sl]) for k in k_refs]
            mx = functools.reduce(jnp.maximum, [jnp.max(s, axis=-1, keepdims=True) for s in ss])
            ps = [jnp.exp(s - mx) for s in ss]
            den = functools.reduce(lambda a, b: a + b, [jnp.sum(p, axis=-1, keepdims=True) for p in ps])
            o = functools.reduce(lambda a, b: a + b,
                                 [_dot(p.astype(BF16), v[:, sl]) for p, v in zip(ps, v_refs)])
            o = o / den
            acc = o if acc is None else acc + o
        o_ref[:, c * LANES:(c + 1) * LANES] = acc.astype(BF16)


def _attention(q, k_segs, v_segs, seq_len):
    n = q.shape[0]
    tq = TQ_ATT
    nq = seq_len // tq
    n_seg = len(k_segs)
    in_specs = [pl.BlockSpec((tq, ATTN_W), lambda b, i: (b * nq + i, 0))]
    for arr, t in list(k_segs) + list(v_segs):
        in_specs.append(pl.BlockSpec((t, 4 * LANES), lambda b, i: (b, 0)))
    return pl.pallas_call(
        functools.partial(_attn_kernel, n_seg=n_seg),
        grid=(n // seq_len, nq),
        in_specs=in_specs,
        out_specs=pl.BlockSpec((tq, ATTN_W), lambda b, i: (b * nq + i, 0)),
        out_shape=jax.ShapeDtypeStruct((n, ATTN_W), BF16),
        compiler_params=_cparams(("arbitrary", "arbitrary")),
        name="attention_%dseg" % n_seg,
    )(q, *[a for a, _ in k_segs], *[a for a, _ in v_segs])


def _log_sigmoid(x):
    return jnp.minimum(x, 0.0) - jnp.log1p(jnp.exp(-jnp.abs(x)))


def _tile_scan(a, b, reverse):
    row = lax.broadcasted_iota(jnp.int32, a.shape, 0)
    d = 1
    while d < SUBLANES:
        if reverse:
            keep = row < SUBLANES - d
            shift = SUBLANES - d
        else:
            keep = row >= d
            shift = d
        a_sh = jnp.where(keep, pltpu.roll(a, shift, 0), 1.0)
        b_sh = jnp.where(keep, pltpu.roll(b, shift, 0), 0.0)
        b = a * b_sh + b
        a = a * a_sh
        d *= 2
    return a, b


def _lru_kernel(xr_ref, gb_ref, h0_ref, cw_ref, cb_ref, wf_ref, wb_ref, bf_ref, bb_ref, lam_ref,
                rec_ref, fin_ref, xpad_s, xc_s, hf_s, a_s, b_s, *, seq_len):
    tc = TC_LRU
    n_chunks = seq_len // tc
    n_tiles = tc // SUBLANES
    zpad = jnp.zeros((SUBLANES, LRU_W), F32)
    xpad_s[0:SUBLANES, :] = zpad
    xpad_s[SUBLANES:SUBLANES + seq_len, :] = xr_ref[...]
    xpad_s[SUBLANES + seq_len:2 * SUBLANES + seq_len, :] = zpad

    cl = LRU_C * _log_sigmoid(lam_ref[...])

    def gates(xcc, w_ref, bias_ref, cl_d):
        g = _dot(xcc.astype(BF16), w_ref[...]) + bias_ref[...]
        r = jax.nn.sigmoid(g[:, :LRU_W])
        i = jax.nn.sigmoid(g[:, LRU_W:])
        log_a = r * cl_d
        a = jnp.exp(log_a)
        a_s[...] = a
        b_s[...] = jnp.sqrt(1.0 - a * a) * i * xcc

    h = h0_ref[0, 0:1, :]
    for c in range(n_chunks):
        base = c * tc
        xcc = cb_ref[...] + functools.reduce(
            lambda u, v: u + v,
            [cw_ref[j:j + 1, :] * xpad_s[base + SUBLANES - 1 + j:base + SUBLANES - 1 + j + tc, :]
             for j in range(CONV_W)])
        xc_s[base:base + tc, :] = xcc
        gates(xcc, wf_ref, bf_ref, cl[0:1])

        def fwd_tile(t, hc, base=base):
            r0 = pl.multiple_of(t * SUBLANES, SUBLANES)
            ca, cb = _tile_scan(a_s[pl.ds(r0, SUBLANES), :], b_s[pl.ds(r0, SUBLANES), :], False)
            hh = ca * hc + cb
            hf_s[pl.ds(base + r0, SUBLANES), :] = hh
            return hh[SUBLANES - 1:SUBLANES, :]

        h = lax.fori_loop(0, n_tiles, fwd_tile, h)
    fin_ref[0, 0:1, :] = h

    h = h0_ref[0, 1:2, :]
    for c in reversed(range(n_chunks)):
        base = c * tc
        gates(xc_s[base:base + tc, :], wb_ref, bb_ref, cl[1:2])

        def bwd_tile(t, hc, base=base):
            r0 = pl.multiple_of((n_tiles - 1 - t) * SUBLANES, SUBLANES)
            ca, cb = _tile_scan(a_s[pl.ds(r0, SUBLANES), :], b_s[pl.ds(r0, SUBLANES), :], True)
            hh = ca * hc + cb
            gate = jax.nn.gelu(gb_ref[pl.ds(base + r0, SUBLANES), :], approximate=True)
            rec_ref[pl.ds(base + r0, SUBLANES), :] = (
                (hf_s[pl.ds(base + r0, SUBLANES), :] + hh) * gate).astype(rec_ref.dtype)
            return hh[0:1, :]

        h = lax.fori_loop(0, n_tiles, bwd_tile, h)
    fin_ref[0, 1:2, :] = h


def _lru(xr, gb, h0, conv_w, conv_b, wf, wb, bf, bb, lam, seq_len):
    n = xr.shape[0]
    batch = n // seq_len
    const = lambda b: (0, 0)
    seq = pl.BlockSpec((seq_len, LRU_W), lambda b: (b, 0))
    st = pl.BlockSpec((1, 2, LRU_W), lambda b: (b, 0, 0))
    return pl.pallas_call(
        functools.partial(_lru_kernel, seq_len=seq_len),
        grid=(batch,),
        in_specs=[seq, seq, st,
                  pl.BlockSpec((CONV_W, LRU_W), const), pl.BlockSpec((1, LRU_W), const),
                  pl.BlockSpec((LRU_W, 2 * LRU_W), const), pl.BlockSpec((LRU_W, 2 * LRU_W), const),
                  pl.BlockSpec((1, 2 * LRU_W), const), pl.BlockSpec((1, 2 * LRU_W), const),
                  pl.BlockSpec((2, LRU_W), const)],
        out_specs=[seq, st],
        out_shape=[jax.ShapeDtypeStruct((n, LRU_W), BF16),
                   jax.ShapeDtypeStruct((batch, 2, LRU_W), F32)],
        scratch_shapes=[pltpu.VMEM((seq_len + 2 * SUBLANES, LRU_W), F32),
                        pltpu.VMEM((seq_len, LRU_W), F32),
                        pltpu.VMEM((seq_len, LRU_W), F32),
                        pltpu.VMEM((TC_LRU, LRU_W), F32),
                        pltpu.VMEM((TC_LRU, LRU_W), F32)],
        compiler_params=_cparams(("arbitrary",)),
        name="lru_%d" % seq_len,
    )(xr, gb, h0, conv_w, conv_b, wf, wb, bf, bb, lam)


def _post_kernel(attn_ref, rec_ref, x_ref, mod_ref, g2_ref, wo_ref, wr_hi_ref, wr_lo_ref, br_ref,
                 x1_ref, h2_ref, comb_ref):
    m = mod_ref[0]
    u = _dot(attn_ref[...], wo_ref[:ATTN_W, :]) + _dot(rec_ref[...], wo_ref[ATTN_W:, :])
    x1 = x_ref[...] + m[2:3] * u
    x1_ref[...] = x1
    ms = jnp.mean(x1 * x1, axis=-1, keepdims=True)
    h2 = x1 * lax.rsqrt(ms + EPS) * g2_ref[...]
    h2 = h2 * (1.0 + m[4:5]) + m[3:4]
    hi, lo = _split_bf16(h2)
    h2_ref[...] = hi

    logits = (_dot(hi, wr_hi_ref[...]) + _dot(hi, wr_lo_ref[...]) + _dot(lo, wr_hi_ref[...])
              + br_ref[...])
    lane = lax.broadcasted_iota(jnp.int32, logits.shape, 1)
    neg = -jnp.inf
    big = jnp.int32(1 << 20)
    gmask = (lane >= N_EXPERTS) & (lane < N_EXPERTS + N_GROUPS)
    gl = jnp.where(gmask, logits, neg)
    gmax = jnp.max(gl, axis=-1, keepdims=True)
    gidx = jnp.min(jnp.where(gl == gmax, lane - N_EXPERTS, big), axis=-1, keepdims=True)
    p_sel = 1.0 / jnp.sum(jnp.where(gmask, jnp.exp(gl - gmax), 0.0), axis=-1, keepdims=True)

    emask = (lane < N_EXPERTS) & ((lane // EXPERTS_PER_GROUP) == gidx)
    el = jnp.where(emask, logits, neg)
    v1 = jnp.max(el, axis=-1, keepdims=True)
    i1 = jnp.min(jnp.where(el == v1, lane, big), axis=-1, keepdims=True)
    el2 = jnp.where(lane == i1, neg, el)
    v2 = jnp.max(el2, axis=-1, keepdims=True)
    i2 = jnp.min(jnp.where(el2 == v2, lane, big), axis=-1, keepdims=True)
    e2 = jnp.exp(v2 - v1)
    w1 = p_sel / (1.0 + e2)
    w2 = p_sel * e2 / (1.0 + e2)
    comb_ref[...] = jnp.where(lane == i1, w1, 0.0) + jnp.where(lane == i2, w2, 0.0)


def _post(attn, rec, x, mod3, mod_row, g2, w_out, wr_hi, wr_lo, br, seq_len):
    n = x.shape[0]
    tm = TM_PRE
    tiles_per_seq = seq_len // tm
    const = lambda i: (0, 0)
    row = lambda w: pl.BlockSpec((tm, w), lambda i: (i, 0))
    return pl.pallas_call(
        _post_kernel,
        grid=(n // tm,),
        in_specs=[row(ATTN_W), row(LRU_W), row(D_MODEL),
                  pl.BlockSpec((1, 6, D_MODEL), lambda i: (mod_row(i // tiles_per_seq), 0, 0)),
                  pl.BlockSpec((1, D_MODEL), const),
                  pl.BlockSpec((D_MODEL, D_MODEL), const),
                  pl.BlockSpec((D_MODEL, LANES), const),
                  pl.BlockSpec((D_MODEL, LANES), const),
                  pl.BlockSpec((1, LANES), const)],
        out_specs=[row(D_MODEL), row(D_MODEL), row(LANES)],
        out_shape=[jax.ShapeDtypeStruct((n, D_MODEL), F32),
                   jax.ShapeDtypeStruct((n, D_MODEL), BF16),
                   jax.ShapeDtypeStruct((n, LANES), F32)],
        compiler_params=_cparams(("arbitrary",)),
        name="post",
    )(attn, rec, x, mod3, g2, w_out, wr_hi, wr_lo, br)


def _moe_kernel(h2_ref, comb_ref, x1_ref, mod_ref, wg_ref, wu_ref, wd_ref, y_ref, acc_ref):
    e = pl.program_id(1)

    @pl.when(e == 0)
    def _():
        acc_ref[...] = jnp.zeros_like(acc_ref)

    h2 = h2_ref[...]
    hg = _dot(h2, wg_ref[0])
    hu = _dot(h2, wu_ref[0])
    comb = comb_ref[...]
    lane = lax.broadcasted_iota(jnp.int32, comb.shape, 1)
    w = jnp.sum(jnp.where(lane == e, comb, 0.0), axis=-1, keepdims=True)
    act = (hg * jax.nn.sigmoid(hg)) * hu * w
    acc_ref[...] += _dot(act.astype(BF16), wd_ref[0])

    @pl.when(e == N_EXPERTS - 1)
    def _():
        y_ref[...] = x1_ref[...] + mod_ref[0][5:6] * acc_ref[...]


def _moe(h2, comb, x1, mod3, mod_row, wg, wu, wd, seq_len):
    n = h2.shape[0]
    tm = min(TM_MOE, seq_len)
    tiles_per_seq = seq_len // tm
    row = lambda w: pl.BlockSpec((tm, w), lambda i, e: (i, 0))
    return pl.pallas_call(
        _moe_kernel,
        grid=(n // tm, N_EXPERTS),
        in_specs=[row(D_MODEL), row(LANES), row(D_MODEL),
                  pl.BlockSpec((1, 6, D_MODEL), lambda i, e: (mod_row(i // tiles_per_seq), 0, 0)),
                  pl.BlockSpec((1, D_MODEL, EXPERT_FF), lambda i, e: (e, 0, 0)),
                  pl.BlockSpec((1, D_MODEL, EXPERT_FF), lambda i, e: (e, 0, 0)),
                  pl.BlockSpec((1, EXPERT_FF, D_MODEL), lambda i, e: (e, 0, 0))],
        out_specs=row(D_MODEL),
        out_shape=jax.ShapeDtypeStruct((n, D_MODEL), F32),
        scratch_shapes=[pltpu.VMEM((tm, D_MODEL), F32)],
        compiler_params=_cparams(("arbitrary", "arbitrary")),
        name="moe",
    )(h2, comb, x1, mod3, wg, wu, wd)


def _rope_tables(length):
    rows = length // GRID_W
    r, col = jnp.meshgrid(jnp.arange(rows), jnp.arange(GRID_W), indexing='ij')
    r = r.reshape(-1).astype(F32)
    col = col.reshape(-1).astype(F32)
    half = HEAD_DIM // 2
    inv = ROPE_THETA ** (-jnp.arange(0, half, 2, dtype=F32) / half)
    ang_r = r[:, None] * inv
    ang_c = col[:, None] * inv
    ang = jnp.concatenate([ang_r, ang_r, ang_c, ang_c], axis=-1)
    sign = jnp.where((jnp.arange(HEAD_DIM) // (HEAD_DIM // 4)) % 2 == 0, -1.0, 1.0).astype(F32)
    cos = jnp.tile(jnp.cos(ang), (1, LANES // HEAD_DIM))
    sin = jnp.tile(jnp.sin(ang) * sign, (1, LANES // HEAD_DIM))
    return cos, sin


def _block_diag(w):
    eye = jnp.eye(LRU_BLOCKS, dtype=w.dtype)
    return jnp.einsum('hij,hg->higj', w, eye).reshape(LRU_W, LRU_W)


def _expand_heads(kv):
    b, t, _ = kv.shape
    h0 = kv[..., :HEAD_DIM]
    h1 = kv[..., HEAD_DIM:]
    z = jnp.zeros_like(h0)
    out = jnp.concatenate([h0, z, z, h0, h1, z, z, h1], axis=-1)
    return out.reshape(b * t, 4 * LANES).astype(BF16)


def kernel(x_prompt, x_sample, cache_k, cache_v, state_lru, c, c_ctx, w_mod, b_mod, norm1, norm2,
           w_in, q_norm, k_norm, conv_w, conv_b, lru_wa, lru_ba, lru_wx, lru_bx, lru_lambda, w_out,
           router_grp_w, router_grp_b, router_exp_w, router_exp_b, exp_w_gate, exp_w_up, exp_w_down):
    batch, seq, _ = x_prompt.shape
    dec_batch, dec_seq, _ = x_sample.shape
    past = cache_k.shape[2]
    depth = w_mod.shape[0]
    assert depth == 1

    cvec = jnp.concatenate(
        [c_ctx[None, :], c, jnp.zeros((MOD_ROWS - 1 - dec_batch, D_MODEL), F32)], axis=0)
    mod3 = _modulation(cvec, w_mod[0], b_mod[0][None, :]).reshape(MOD_ROWS, 6, D_MODEL)

    w_in_b = w_in[0].astype(BF16)
    w_out_b = w_out[0].astype(BF16)
    head_id = jnp.arange(QK_W) // HEAD_DIM
    ones_qk = (head_id[:, None] == head_id[None, :]).astype(BF16)
    gqk = jnp.concatenate([jnp.tile(q_norm[0], N_HEADS), jnp.tile(k_norm[0], N_KV_HEADS)])[None, :]
    wf = jnp.concatenate([_block_diag(lru_wa[0, 0]), _block_diag(lru_wx[0, 0])], axis=1).astype(BF16)
    wb = jnp.concatenate([_block_diag(lru_wa[0, 1]), _block_diag(lru_wx[0, 1])], axis=1).astype(BF16)
    bf = jnp.concatenate([lru_ba[0, 0], lru_bx[0, 0]])[None, :]
    bb = jnp.concatenate([lru_ba[0, 1], lru_bx[0, 1]])[None, :]
    pad = LANES - N_EXPERTS - N_GROUPS
    wr = jnp.concatenate([router_exp_w[0], router_grp_w[0], jnp.zeros((D_MODEL, pad), F32)], axis=1)
    wr_hi = wr.astype(BF16)
    wr_lo = (wr - wr_hi.astype(F32)).astype(BF16)
    br = jnp.concatenate([router_exp_b[0], router_grp_b[0], jnp.zeros((pad,), F32)])[None, :]
    wg = exp_w_gate[0].astype(BF16)
    wu = exp_w_up[0].astype(BF16)
    wd = exp_w_down[0].astype(BF16)
    g1 = norm1[0][None, :]
    g2 = norm2[0][None, :]
    cw = conv_w[0]
    cb = conv_b[0][None, :]
    lam = lru_lambda[0]

    def layer(x, seq_len, mod_row, tables, extra_k, extra_v, h0):
        q, kx, vx, kf, vf, xr, gb = _pre(x, mod3, mod_row, g1, w_in_b, ones_qk, gqk, tables, seq_len)
        k_segs = [(kx, seq_len)] + extra_k
        v_segs = [(vx, seq_len)] + extra_v
        attn = _attention(q, k_segs, v_segs, seq_len)
        rec, fin = _lru(xr, gb, h0, cw, cb, wf, wb, bf, bb, lam, seq_len)
        x1, h2, comb = _post(attn, rec, x, mod3, mod_row, g2, w_out_b, wr_hi, wr_lo, br, seq_len)
        y = _moe(h2, comb, x1, mod3, mod_row, wg, wu, wd, seq_len)
        return y, kf, vf, fin

    xp = x_prompt.reshape(batch * seq, D_MODEL)
    yp, kf, vf, fin = layer(xp, seq, lambda b: 0, None, [], [],
                            jnp.zeros((batch, 2, LRU_W), F32))

    xs = x_sample.reshape(dec_batch * dec_seq, D_MODEL)
    ck = _expand_heads(cache_k[:, 0].reshape(dec_batch, past, KV_W))
    cv = _expand_heads(cache_v[:, 0].reshape(dec_batch, past, KV_W))
    ys, _, _, _ = layer(xs, dec_seq, lambda b: b + 1, _rope_tables(dec_seq),
                        [(ck, past)], [(cv, past)], state_lru[:, 0])

    return (yp.reshape(batch, seq, D_MODEL),
            ys.reshape(dec_batch, dec_seq, D_MODEL),
            kf.reshape(batch, 1, seq, N_KV_HEADS, HEAD_DIM),
            vf.reshape(batch, 1, seq, N_KV_HEADS, HEAD_DIM),
            fin.reshape(batch, 1, 2, LRU_W))
```

```python
import functools

import jax
import jax.numpy as jnp
from jax import lax
from jax.experimental import pallas as pl
from jax.experimental.pallas import tpu as pltpu

F32 = jnp.float32
BF16 = jnp.bfloat16

D_MODEL = 1024
GRID_W = 64
ATTN_W = 512
LRU_W = 512
HEAD_DIM = 64
N_HEADS = 8
N_KV_HEADS = 2
KV_W = N_KV_HEADS * HEAD_DIM
LRU_BLOCKS = 8
LRU_BLOCK_W = LRU_W // LRU_BLOCKS
CONV_W = 4
LRU_C = 8.0
IN_W = ATTN_W + 2 * KV_W + 2 * LRU_W
QK_W = ATTN_W + KV_W
N_GROUPS = 4
EXPERTS_PER_GROUP = 8
N_EXPERTS = N_GROUPS * EXPERTS_PER_GROUP
TOP_K = 2
EXPERT_FF = D_MODEL // 4
ROPE_THETA = 10000.0
EPS = 1e-6

LANES = 128
SUBLANES = 8
MOD_ROWS = 8
VMEM_LIMIT = 48 * 1024 * 1024

TM_PRE = 256
TQ_ATT = 256
TC_LRU = 256
TR_MOE = 256
CH_DISPATCH = 512


def _cparams(sem):
    return pltpu.CompilerParams(dimension_semantics=sem, vmem_limit_bytes=VMEM_LIMIT)


def _dot(a, b):
    return jnp.dot(a, b, preferred_element_type=F32)


def _dot_nt(a, b):
    return lax.dot_general(a, b, (((1,), (1,)), ((), ())), preferred_element_type=F32)


def _split_bf16(x):
    hi = x.astype(BF16)
    lo = (x - hi.astype(F32)).astype(BF16)
    return hi, lo


def _mod_kernel(c_ref, w_ref, b_ref, o_ref):
    c = c_ref[...]
    s = (c * jax.nn.sigmoid(c)).astype(BF16)
    o_ref[...] = _dot(s, w_ref[...].astype(BF16)) + b_ref[...]


def _modulation(cvec, w_mod, b_mod):
    n_out = w_mod.shape[1]
    tn = n_out // 4
    return pl.pallas_call(
        _mod_kernel,
        grid=(n_out // tn,),
        in_specs=[pl.BlockSpec((MOD_ROWS, D_MODEL), lambda j: (0, 0)),
                  pl.BlockSpec((D_MODEL, tn), lambda j: (0, j)),
                  pl.BlockSpec((1, tn), lambda j: (0, j))],
        out_specs=pl.BlockSpec((MOD_ROWS, tn), lambda j: (0, j)),
        out_shape=jax.ShapeDtypeStruct((MOD_ROWS, n_out), F32),
        compiler_params=_cparams(("arbitrary",)),
        name="modulation",
    )(cvec, w_mod, b_mod)


def _pre_kernel(*refs, rope):
    if rope:
        (x_ref, mod_ref, g1_ref, win_ref, ones_ref, gqk_ref, cos_ref, sin_ref,
         q_ref, kx_ref, vx_ref, kf_ref, vf_ref, xr_ref, gb_ref) = refs
    else:
        (x_ref, mod_ref, g1_ref, win_ref, ones_ref, gqk_ref,
         q_ref, kx_ref, vx_ref, kf_ref, vf_ref, xr_ref, gb_ref) = refs
    x = x_ref[...]
    m = mod_ref[0]
    ms = jnp.mean(x * x, axis=-1, keepdims=True)
    y = x * lax.rsqrt(ms + EPS) * g1_ref[...]
    h = y * (1.0 + m[1:2]) + m[0:1]
    z = _dot(h.astype(BF16), win_ref[...])

    qk = z[:, :QK_W]
    hi, lo = _split_bf16(qk * qk)
    ss = _dot(hi, ones_ref[...]) + _dot(lo, ones_ref[...])
    qk = qk * lax.rsqrt(ss * (1.0 / HEAD_DIM) + EPS) * gqk_ref[...]

    lane = lax.broadcasted_iota(jnp.int32, (x.shape[0], LANES), 1)
    cols = []
    for c in range(QK_W // LANES):
        xc = qk[:, c * LANES:(c + 1) * LANES]
        if rope:
            left = pltpu.roll(xc, LANES - HEAD_DIM // 4, 1)
            right = pltpu.roll(xc, HEAD_DIM // 4, 1)
            rot = jnp.where((lane // (HEAD_DIM // 4)) % 2 == 0, left, right)
            xc = xc * cos_ref[...] + rot * sin_ref[...]
        cols.append(xc)
    for c in range(ATTN_W // LANES):
        q_ref[:, c * LANES:(c + 1) * LANES] = (cols[c] * (HEAD_DIM ** -0.5)).astype(BF16)

    lo_half = lane < HEAD_DIM
    for col, fref, xref in ((cols[ATTN_W // LANES], kf_ref, kx_ref),
                            (z[:, QK_W:QK_W + KV_W], vf_ref, vx_ref)):
        fref[...] = col
        swapped = pltpu.roll(col, HEAD_DIM, 1)
        xref[:, 0 * LANES:1 * LANES] = jnp.where(lo_half, col, 0.0).astype(BF16)
        xref[:, 1 * LANES:2 * LANES] = jnp.where(lo_half, 0.0, swapped).astype(BF16)
        xref[:, 2 * LANES:3 * LANES] = jnp.where(lo_half, swapped, 0.0).astype(BF16)
        xref[:, 3 * LANES:4 * LANES] = jnp.where(lo_half, 0.0, col).astype(BF16)

    xr_ref[...] = z[:, QK_W + KV_W:QK_W + KV_W + LRU_W]
    gb_ref[...] = z[:, QK_W + KV_W + LRU_W:]


def _pre(x, mod3, mod_row, g1, w_in, ones_qk, gqk, tables, seq_len):
    n = x.shape[0]
    tm = TM_PRE
    tiles_per_seq = seq_len // tm
    rope = tables is not None
    const = lambda i: (0, 0)
    in_specs = [pl.BlockSpec((tm, D_MODEL), lambda i: (i, 0)),
                pl.BlockSpec((1, 6, D_MODEL), lambda i: (mod_row(i // tiles_per_seq), 0, 0)),
                pl.BlockSpec((1, D_MODEL), const),
                pl.BlockSpec((D_MODEL, IN_W), const),
                pl.BlockSpec((QK_W, QK_W), const),
                pl.BlockSpec((1, QK_W), const)]
    args = [x, mod3, g1, w_in, ones_qk, gqk]
    if rope:
        in_specs += [pl.BlockSpec((tm, LANES), lambda i: (i % tiles_per_seq, 0))] * 2
        args += list(tables)
    row = lambda w: pl.BlockSpec((tm, w), lambda i: (i, 0))
    out_shape = [jax.ShapeDtypeStruct((n, ATTN_W), BF16),
                 jax.ShapeDtypeStruct((n, 4 * LANES), BF16),
                 jax.ShapeDtypeStruct((n, 4 * LANES), BF16),
                 jax.ShapeDtypeStruct((n, KV_W), F32),
                 jax.ShapeDtypeStruct((n, KV_W), F32),
                 jax.ShapeDtypeStruct((n, LRU_W), F32),
                 jax.ShapeDtypeStruct((n, LRU_W), F32)]
    out_specs = [row(ATTN_W), row(4 * LANES), row(4 * LANES), row(KV_W), row(KV_W),
                 row(LRU_W), row(LRU_W)]
    return pl.pallas_call(
        functools.partial(_pre_kernel, rope=rope),
        grid=(n // tm,),
        in_specs=in_specs, out_specs=out_specs, out_shape=out_shape,
        compiler_params=_cparams(("arbitrary",)),
        name="pre_rope" if rope else "pre",
    )(*args)


def _attn_kernel(*refs, n_seg):
    q_ref = refs[0]
    k_refs = refs[1:1 + n_seg]
    v_refs = refs[1 + n_seg:1 + 2 * n_seg]
    o_ref = refs[1 + 2 * n_seg]
    for c in range(ATTN_W // LANES):
        qc = q_ref[:, c * LANES:(c + 1) * LANES]
        g = c // 2
        acc = None
        for par in range(2):
            sl = slice((2 * g + par) * LANES, (2 * g + par + 1) * LANES)
            ss = [_dot_nt(qc, k[:, sl]) for k in k_refs]
            mx = functools.reduce(jnp.maximum, [jnp.max(s, axis=-1, keepdims=True) for s in ss])
            ps = [jnp.exp(s - mx) for s in ss]
            den = functools.reduce(lambda a, b: a + b, [jnp.sum(p, axis=-1, keepdims=True) for p in ps])
            o = functools.reduce(lambda a, b: a + b,
                                 [_dot(p.astype(BF16), v[:, sl]) for p, v in zip(ps, v_refs)])
            o = o / den
            acc = o if acc is None else acc + o
        o_ref[:, c * LANES:(c + 1) * LANES] = acc.astype(BF16)


def _attention(q, k_segs, v_segs, seq_len):
    n = q.shape[0]
    tq = TQ_ATT
    nq = seq_len // tq
    n_seg = len(k_segs)
    in_specs = [pl.BlockSpec((tq, ATTN_W), lambda b, i: (b * nq + i, 0))]
    for arr, t in list(k_segs) + list(v_segs):
        in_specs.append(pl.BlockSpec((t, 4 * LANES), lambda b, i: (b, 0)))
    return pl.pallas_call(
        functools.partial(_attn_kernel, n_seg=n_seg),
        grid=(n // seq_len, nq),
        in_specs=in_specs,
        out_specs=pl.BlockSpec((tq, ATTN_W), lambda b, i: (b * nq + i, 0)),
        out_shape=jax.ShapeDtypeStruct((n, ATTN_W), BF16),
        compiler_params=_cparams(("arbitrary", "arbitrary")),
        name="attention_%dseg" % n_seg,
    )(q, *[a for a, _ in k_segs], *[a for a, _ in v_segs])


def _log_sigmoid(x):
    return jnp.minimum(x, 0.0) - jnp.log1p(jnp.exp(-jnp.abs(x)))


def _tile_scan(a, b, reverse):
    row = lax.broadcasted_iota(jnp.int32, a.shape, 0)
    d = 1
    while d < SUBLANES:
        if reverse:
            keep = row < SUBLANES - d
            shift = SUBLANES - d
        else:
            keep = row >= d
            shift = d
        a_sh = jnp.where(keep, pltpu.roll(a, shift, 0), 1.0)
        b_sh = jnp.where(keep, pltpu.roll(b, shift, 0), 0.0)
        b = a * b_sh + b
        a = a * a_sh
        d *= 2
    return a, b


def _lru_kernel(xr_ref, gb_ref, h0_ref, cw_ref, cb_ref, wf_ref, wb_ref, bf_ref, bb_ref, lam_ref,
                rec_ref, fin_ref, xpad_s, xc_s, hf_s, a_s, b_s, *, seq_len):
    tc = TC_LRU
    n_chunks = seq_len // tc
    n_tiles = tc // SUBLANES
    zpad = jnp.zeros((SUBLANES, LRU_W), F32)
    xpad_s[0:SUBLANES, :] = zpad
    xpad_s[SUBLANES:SUBLANES + seq_len, :] = xr_ref[...]
    xpad_s[SUBLANES + seq_len:2 * SUBLANES + seq_len, :] = zpad

    cl = LRU_C * _log_sigmoid(lam_ref[...])

    def gates(xcc, w_ref, bias_ref, cl_d):
        g = _dot(xcc.astype(BF16), w_ref[...]) + bias_ref[...]
        r = jax.nn.sigmoid(g[:, :LRU_W])
        i = jax.nn.sigmoid(g[:, LRU_W:])
        log_a = r * cl_d
        a = jnp.exp(log_a)
        a_s[...] = a
        b_s[...] = jnp.sqrt(1.0 - a * a) * i * xcc

    h = h0_ref[0, 0:1, :]
    for c in range(n_chunks):
        base = c * tc
        xcc = cb_ref[...] + functools.reduce(
            lambda u, v: u + v,
            [cw_ref[j:j + 1, :] * xpad_s[base + SUBLANES - 1 + j:base + SUBLANES - 1 + j + tc, :]
             for j in range(CONV_W)])
        xc_s[base:base + tc, :] = xcc
        gates(xcc, wf_ref, bf_ref, cl[0:1])

        def fwd_tile(t, hc, base=base):
            r0 = pl.multiple_of(t * SUBLANES, SUBLANES)
            ca, cb = _tile_scan(a_s[pl.ds(r0, SUBLANES), :], b_s[pl.ds(r0, SUBLANES), :], False)
            hh = ca * hc + cb
            hf_s[pl.ds(base + r0, SUBLANES), :] = hh
            return hh[SUBLANES - 1:SUBLANES, :]

        h = lax.fori_loop(0, n_tiles, fwd_tile, h)
    fin_ref[0, 0:1, :] = h

    h = h0_ref[0, 1:2, :]
    for c in reversed(range(n_chunks)):
        base = c * tc
        gates(xc_s[base:base + tc, :], wb_ref, bb_ref, cl[1:2])

        def bwd_tile(t, hc, base=base):
            r0 = pl.multiple_of((n_tiles - 1 - t) * SUBLANES, SUBLANES)
            ca, cb = _tile_scan(a_s[pl.ds(r0, SUBLANES), :], b_s[pl.ds(r0, SUBLANES), :], True)
            hh = ca * hc + cb
            gate = jax.nn.gelu(gb_ref[pl.ds(base + r0, SUBLANES), :], approximate=True)
            rec_ref[pl.ds(base + r0, SUBLANES), :] = (
                (hf_s[pl.ds(base + r0, SUBLANES), :] + hh) * gate).astype(rec_ref.dtype)
            return hh[0:1, :]

        h = lax.fori_loop(0, n_tiles, bwd_tile, h)
    fin_ref[0, 1:2, :] = h


def _lru(xr, gb, h0, conv_w, conv_b, wf, wb, bf, bb, lam, seq_len):
    n = xr.shape[0]
    batch = n // seq_len
    const = lambda b: (0, 0)
    seq = pl.BlockSpec((seq_len, LRU_W), lambda b: (b, 0))
    st = pl.BlockSpec((1, 2, LRU_W), lambda b: (b, 0, 0))
    return pl.pallas_call(
        functools.partial(_lru_kernel, seq_len=seq_len),
        grid=(batch,),
        in_specs=[seq, seq, st,
                  pl.BlockSpec((CONV_W, LRU_W), const), pl.BlockSpec((1, LRU_W), const),
                  pl.BlockSpec((LRU_W, 2 * LRU_W), const), pl.BlockSpec((LRU_W, 2 * LRU_W), const),
                  pl.BlockSpec((1, 2 * LRU_W), const), pl.BlockSpec((1, 2 * LRU_W), const),
                  pl.BlockSpec((2, LRU_W), const)],
        out_specs=[seq, st],
        out_shape=[jax.ShapeDtypeStruct((n, LRU_W), BF16),
                   jax.ShapeDtypeStruct((batch, 2, LRU_W), F32)],
        scratch_shapes=[pltpu.VMEM((seq_len + 2 * SUBLANES, LRU_W), F32),
                        pltpu.VMEM((seq_len, LRU_W), F32),
                        pltpu.VMEM((seq_len, LRU_W), F32),
                        pltpu.VMEM((TC_LRU, LRU_W), F32),
                        pltpu.VMEM((TC_LRU, LRU_W), F32)],
        compiler_params=_cparams(("arbitrary",)),
        name="lru_%d" % seq_len,
    )(xr, gb, h0, conv_w, conv_b, wf, wb, bf, bb, lam)


def _post_kernel(attn_ref, rec_ref, x_ref, mod_ref, g2_ref, wo_ref, wr_hi_ref, wr_lo_ref, br_ref,
                 tri_ref, cnt_in_ref, x1_ref, h2_ref, route_ref, cnt_ref, cnt_s):
    @pl.when(pl.program_id(0) == 0)
    def _():
        cnt_s[...] = cnt_in_ref[...]

    m = mod_ref[0]
    u = _dot(attn_ref[...], wo_ref[:ATTN_W, :]) + _dot(rec_ref[...], wo_ref[ATTN_W:, :])
    x1 = x_ref[...] + m[2:3] * u
    x1_ref[...] = x1
    ms = jnp.mean(x1 * x1, axis=-1, keepdims=True)
    h2 = x1 * lax.rsqrt(ms + EPS) * g2_ref[...]
    h2 = h2 * (1.0 + m[4:5]) + m[3:4]
    hi, lo = _split_bf16(h2)
    h2_ref[...] = h2

    logits = (_dot(hi, wr_hi_ref[...]) + _dot(hi, wr_lo_ref[...]) + _dot(lo, wr_hi_ref[...])
              + br_ref[...])
    lane = lax.broadcasted_iota(jnp.int32, logits.shape, 1)
    neg = -jnp.inf
    big = jnp.int32(1 << 20)
    gmask = (lane >= N_EXPERTS) & (lane < N_EXPERTS + N_GROUPS)
    gl = jnp.where(gmask, logits, neg)
    gmax = jnp.max(gl, axis=-1, keepdims=True)
    gidx = jnp.min(jnp.where(gl == gmax, lane - N_EXPERTS, big), axis=-1, keepdims=True)
    p_sel = 1.0 / jnp.sum(jnp.where(gmask, jnp.exp(gl - gmax), 0.0), axis=-1, keepdims=True)

    emask = (lane < N_EXPERTS) & ((lane // EXPERTS_PER_GROUP) == gidx)
    el = jnp.where(emask, logits, neg)
    v1 = jnp.max(el, axis=-1, keepdims=True)
    i1 = jnp.min(jnp.where(el == v1, lane, big), axis=-1, keepdims=True)
    el2 = jnp.where(lane == i1, neg, el)
    v2 = jnp.max(el2, axis=-1, keepdims=True)
    i2 = jnp.min(jnp.where(el2 == v2, lane, big), axis=-1, keepdims=True)
    e2 = jnp.exp(v2 - v1)
    w1 = p_sel / (1.0 + e2)
    w2 = p_sel * e2 / (1.0 + e2)

    oh1 = lane == i1
    oh2 = lane == i2
    oh = jnp.where(oh1, 1.0, 0.0) + jnp.where(oh2, 1.0, 0.0)
    before = _dot(tri_ref[...], oh.astype(BF16)) + cnt_s[...]
    rank1 = jnp.sum(jnp.where(oh1, before, 0.0), axis=-1, keepdims=True)
    rank2 = jnp.sum(jnp.where(oh2, before, 0.0), axis=-1, keepdims=True)
    cnt = cnt_s[...] + jnp.sum(oh, axis=0, keepdims=True)
    cnt_s[...] = cnt
    cnt_ref[...] = cnt
    fields = (i1.astype(F32), i2.astype(F32), rank1, rank2, w1, w2)
    route = jnp.zeros(logits.shape, F32)
    for k, val in enumerate(fields):
        route = jnp.where(lane == k, val, route)
    route_ref[...] = route


def _post(attn, rec, x, mod3, mod_row, g2, w_out, wr_hi, wr_lo, br, tri, cnt_in, seq_len):
    n = x.shape[0]
    tm = TM_PRE
    tiles_per_seq = seq_len // tm
    const = lambda i: (0, 0)
    row = lambda w: pl.BlockSpec((tm, w), lambda i: (i, 0))
    return pl.pallas_call(
        _post_kernel,
        grid=(n // tm,),
        in_specs=[row(ATTN_W), row(LRU_W), row(D_MODEL),
                  pl.BlockSpec((1, 6, D_MODEL), lambda i: (mod_row(i // tiles_per_seq), 0, 0)),
                  pl.BlockSpec((1, D_MODEL), const),
                  pl.BlockSpec((D_MODEL, D_MODEL), const),
                  pl.BlockSpec((D_MODEL, LANES), const),
                  pl.BlockSpec((D_MODEL, LANES), const),
                  pl.BlockSpec((1, LANES), const),
                  pl.BlockSpec((tm, tm), const),
                  pl.BlockSpec((1, LANES), const)],
        out_specs=[row(D_MODEL), row(D_MODEL), row(LANES), pl.BlockSpec((1, LANES), const)],
        out_shape=[jax.ShapeDtypeStruct((n, D_MODEL), F32),
                   jax.ShapeDtypeStruct((n, D_MODEL), F32),
                   jax.ShapeDtypeStruct((n, LANES), F32),
                   jax.ShapeDtypeStruct((1, LANES), F32)],
        scratch_shapes=[pltpu.VMEM((1, LANES), F32)],
        compiler_params=_cparams(("arbitrary",)),
        name="post",
    )(attn, rec, x, mod3, g2, w_out, wr_hi, wr_lo, br, tri, cnt_in)


def _row_copy(src_ref, src_row, dst_ref, dst_row, sem):
    return pltpu.make_async_copy(src_ref.at[pl.ds(src_row, 1), :], dst_ref.at[pl.ds(dst_row, 1), :], sem)


def _wait_rows(src_ref, dst_ref, sem, count):
    def body(j, carry):
        _row_copy(src_ref, 0, dst_ref, 0, sem).wait()
        return carry
    lax.fori_loop(0, count, body, 0)


def _dispatch_kernel(slots_ref, zfill_ref, nu_ref, h2p_ref, h2s_ref, xs_ref, zero_s, sem, *, n_prompt):
    i = pl.program_id(0)
    ch = CH_DISPATCH

    @pl.when(i == 0)
    def _():
        zero_s[...] = jnp.zeros_like(zero_s)

        def zero_tile(t):
            return pltpu.make_async_copy(
                zero_s, xs_ref.at[pl.ds(pl.multiple_of(t * TR_MOE, TR_MOE), TR_MOE), :], sem)

        for e in range(N_EXPERTS):
            @pl.when(zfill_ref[e] >= 0)
            def _():
                zero_tile(zfill_ref[e]).start()
        n_all = xs_ref.shape[0] // TR_MOE
        lax.fori_loop(nu_ref[0], n_all, lambda t, c: (zero_tile(t).start(), c)[1], 0)
        for e in range(N_EXPERTS):
            @pl.when(zfill_ref[e] >= 0)
            def _():
                zero_tile(0).wait()
        lax.fori_loop(nu_ref[0], n_all, lambda t, c: (zero_tile(0).wait(), c)[1], 0)

    def scatter(src_ref, row0):
        def body(j, carry):
            n = i * ch + j
            _row_copy(src_ref, n - row0, xs_ref, slots_ref[2 * n], sem).start()
            _row_copy(src_ref, n - row0, xs_ref, slots_ref[2 * n + 1], sem).start()
            return carry
        lax.fori_loop(0, ch, body, 0)
        _wait_rows(src_ref, xs_ref, sem, 2 * ch)

    @pl.when(i < n_prompt // ch)
    def _():
        scatter(h2p_ref, 0)

    @pl.when(i >= n_prompt // ch)
    def _():
        scatter(h2s_ref, n_prompt)


def _dispatch(slots, zfill, n_used, h2p, h2s, n_rows):
    n_prompt = h2p.shape[0]
    n = n_prompt + h2s.shape[0]
    return pl.pallas_call(
        functools.partial(_dispatch_kernel, n_prompt=n_prompt),
        grid_spec=pltpu.PrefetchScalarGridSpec(
            num_scalar_prefetch=3,
            grid=(n // CH_DISPATCH,),
            in_specs=[pl.BlockSpec(memory_space=pl.ANY), pl.BlockSpec(memory_space=pl.ANY)],
            out_specs=pl.BlockSpec(memory_space=pl.ANY),
            scratch_shapes=[pltpu.VMEM((TR_MOE, D_MODEL), F32), pltpu.SemaphoreType.DMA]),
        out_shape=jax.ShapeDtypeStruct((n_rows, D_MODEL), F32),
        compiler_params=_cparams(("arbitrary",)),
        name="dispatch",
    )(slots, zfill, n_used, h2p, h2s)


def _ffn_kernel(te_ref, nu_ref, xs_ref, wg_ref, wu_ref, wd_ref, ys_ref, wg_s, wu_s, wd_s):
    t = pl.program_id(0)

    @pl.when(t < nu_ref[0])
    def _():
        prev = te_ref[jnp.maximum(t - 1, 0)]

        @pl.when((t == 0) | (te_ref[t] != prev))
        def _():
            wg_s[...] = wg_ref[0].astype(BF16)
            wu_s[...] = wu_ref[0].astype(BF16)
            wd_s[...] = wd_ref[0].astype(BF16)

        x = xs_ref[...].astype(BF16)
        hg = _dot(x, wg_s[...])
        hu = _dot(x, wu_s[...])
        act = (hg * jax.nn.sigmoid(hg)) * hu
        ys_ref[...] = _dot(act.astype(BF16), wd_s[...])

    @pl.when(t >= nu_ref[0])
    def _():
        ys_ref[...] = jnp.zeros_like(ys_ref)


def _ffn(tile_expert, n_used, xs, wg, wu, wd, n_tiles):
    tr = TR_MOE
    rows = lambda t, te, nu: (jnp.minimum(t, nu[0] - 1), 0)
    wsel = lambda t, te, nu: (te[t], 0, 0)
    return pl.pallas_call(
        _ffn_kernel,
        grid_spec=pltpu.PrefetchScalarGridSpec(
            num_scalar_prefetch=2,
            grid=(n_tiles,),
            in_specs=[pl.BlockSpec((tr, D_MODEL), rows),
                      pl.BlockSpec((1, D_MODEL, EXPERT_FF), wsel),
                      pl.BlockSpec((1, D_MODEL, EXPERT_FF), wsel),
                      pl.BlockSpec((1, EXPERT_FF, D_MODEL), wsel)],
            out_specs=pl.BlockSpec((tr, D_MODEL), lambda t, te, nu: (t, 0)),
            scratch_shapes=[pltpu.VMEM((D_MODEL, EXPERT_FF), BF16),
                            pltpu.VMEM((D_MODEL, EXPERT_FF), BF16),
                            pltpu.VMEM((EXPERT_FF, D_MODEL), BF16)]),
        out_shape=jax.ShapeDtypeStruct((n_tiles * tr, D_MODEL), F32),
        compiler_params=_cparams(("arbitrary",)),
        name="ffn",
    )(tile_expert, n_used, xs, wg, wu, wd)


def _combine_kernel(slots_ref, ys_ref, x1_ref, route_ref, mod_ref, y_ref, b1_s, b2_s, sem):
    i = pl.program_id(0)
    tm = x1_ref.shape[0]

    def body(j, carry):
        n = i * tm + j
        _row_copy(ys_ref, slots_ref[2 * n], b1_s, j, sem).start()
        _row_copy(ys_ref, slots_ref[2 * n + 1], b2_s, j, sem).start()
        return carry
    lax.fori_loop(0, tm, body, 0)
    _wait_rows(ys_ref, b1_s, sem, 2 * tm)

    route = route_ref[...]
    moe = route[:, 4:5] * b1_s[...] + route[:, 5:6] * b2_s[...]
    y_ref[...] = x1_ref[...] + mod_ref[0][5:6] * moe


def _combine(slots, ys, x1, route, mod3, mod_row, seq_len):
    n = x1.shape[0]
    tm = TM_PRE
    tiles_per_seq = seq_len // tm
    row = lambda w: pl.BlockSpec((tm, w), lambda i, s: (i, 0))
    return pl.pallas_call(
        _combine_kernel,
        grid_spec=pltpu.PrefetchScalarGridSpec(
            num_scalar_prefetch=1,
            grid=(n // tm,),
            in_specs=[pl.BlockSpec(memory_space=pl.ANY), row(D_MODEL), row(LANES),
                      pl.BlockSpec((1, 6, D_MODEL), lambda i, s: (mod_row(i // tiles_per_seq), 0, 0))],
            out_specs=row(D_MODEL),
            scratch_shapes=[pltpu.VMEM((tm, D_MODEL), F32), pltpu.VMEM((tm, D_MODEL), F32),
                            pltpu.SemaphoreType.DMA]),
        out_shape=jax.ShapeDtypeStruct((n, D_MODEL), F32),
        compiler_params=_cparams(("arbitrary",)),
        name="combine",
    )(slots, ys, x1, route, mod3)


def _rope_tables(length):
    rows = length // GRID_W
    r, col = jnp.meshgrid(jnp.arange(rows), jnp.arange(GRID_W), indexing='ij')
    r = r.reshape(-1).astype(F32)
    col = col.reshape(-1).astype(F32)
    half = HEAD_DIM // 2
    inv = ROPE_THETA ** (-jnp.arange(0, half, 2, dtype=F32) / half)
    ang_r = r[:, None] * inv
    ang_c = col[:, None] * inv
    ang = jnp.concatenate([ang_r, ang_r, ang_c, ang_c], axis=-1)
    sign = jnp.where((jnp.arange(HEAD_DIM) // (HEAD_DIM // 4)) % 2 == 0, -1.0, 1.0).astype(F32)
    cos = jnp.tile(jnp.cos(ang), (1, LANES // HEAD_DIM))
    sin = jnp.tile(jnp.sin(ang) * sign, (1, LANES // HEAD_DIM))
    return cos, sin


def _block_diag(w):
    eye = jnp.eye(LRU_BLOCKS, dtype=w.dtype)
    return jnp.einsum('hij,hg->higj', w, eye).reshape(LRU_W, LRU_W)


def _expand_heads(kv):
    b, t, _ = kv.shape
    h0 = kv[..., :HEAD_DIM]
    h1 = kv[..., HEAD_DIM:]
    z = jnp.zeros_like(h0)
    out = jnp.concatenate([h0, z, z, h0, h1, z, z, h1], axis=-1)
    return out.reshape(b * t, 4 * LANES).astype(BF16)


def kernel(x_prompt, x_sample, cache_k, cache_v, state_lru, c, c_ctx, w_mod, b_mod, norm1, norm2,
           w_in, q_norm, k_norm, conv_w, conv_b, lru_wa, lru_ba, lru_wx, lru_bx, lru_lambda, w_out,
           router_grp_w, router_grp_b, router_exp_w, router_exp_b, exp_w_gate, exp_w_up, exp_w_down):
    batch, seq, _ = x_prompt.shape
    dec_batch, dec_seq, _ = x_sample.shape
    past = cache_k.shape[2]
    depth = w_mod.shape[0]
    assert depth == 1

    cvec = jnp.concatenate(
        [c_ctx[None, :], c, jnp.zeros((MOD_ROWS - 1 - dec_batch, D_MODEL), F32)], axis=0)
    mod3 = _modulation(cvec, w_mod[0], b_mod[0][None, :]).reshape(MOD_ROWS, 6, D_MODEL)

    w_in_b = w_in[0].astype(BF16)
    w_out_b = w_out[0].astype(BF16)
    head_id = jnp.arange(QK_W) // HEAD_DIM
    ones_qk = (head_id[:, None] == head_id[None, :]).astype(BF16)
    gqk = jnp.concatenate([jnp.tile(q_norm[0], N_HEADS), jnp.tile(k_norm[0], N_KV_HEADS)])[None, :]
    wf = jnp.concatenate([_block_diag(lru_wa[0, 0]), _block_diag(lru_wx[0, 0])], axis=1).astype(BF16)
    wb = jnp.concatenate([_block_diag(lru_wa[0, 1]), _block_diag(lru_wx[0, 1])], axis=1).astype(BF16)
    bf = jnp.concatenate([lru_ba[0, 0], lru_bx[0, 0]])[None, :]
    bb = jnp.concatenate([lru_ba[0, 1], lru_bx[0, 1]])[None, :]
    pad = LANES - N_EXPERTS - N_GROUPS
    wr = jnp.concatenate([router_exp_w[0], router_grp_w[0], jnp.zeros((D_MODEL, pad), F32)], axis=1)
    wr_hi = wr.astype(BF16)
    wr_lo = (wr - wr_hi.astype(F32)).astype(BF16)
    br = jnp.concatenate([router_exp_b[0], router_grp_b[0], jnp.zeros((pad,), F32)])[None, :]
    g1 = norm1[0][None, :]
    g2 = norm2[0][None, :]
    cw = conv_w[0]
    cb = conv_b[0][None, :]
    lam = lru_lambda[0]
    tri = (jnp.arange(TM_PRE)[:, None] > jnp.arange(TM_PRE)[None, :]).astype(BF16)

    def mixers(x, seq_len, mod_row, tables, extra_k, extra_v, h0, cnt_in):
        q, kx, vx, kf, vf, xr, gb = _pre(x, mod3, mod_row, g1, w_in_b, ones_qk, gqk, tables, seq_len)
        k_segs = [(kx, seq_len)] + extra_k
        v_segs = [(vx, seq_len)] + extra_v
        attn = _attention(q, k_segs, v_segs, seq_len)
        rec, fin = _lru(xr, gb, h0, cw, cb, wf, wb, bf, bb, lam, seq_len)
        x1, h2, route, cnt = _post(attn, rec, x, mod3, mod_row, g2, w_out_b, wr_hi, wr_lo, br,
                                   tri, cnt_in, seq_len)
        return x1, h2, route, cnt, kf, vf, fin

    mod_row_p = lambda b: 0
    mod_row_s = lambda b: b + 1
    xp = x_prompt.reshape(batch * seq, D_MODEL)
    x1p, h2p, route_p, cnt_p, kf, vf, fin = mixers(
        xp, seq, mod_row_p, None, [], [], jnp.zeros((batch, 2, LRU_W), F32),
        jnp.zeros((1, LANES), F32))
    xs = x_sample.reshape(dec_batch * dec_seq, D_MODEL)
    ck = _expand_heads(cache_k[:, 0].reshape(dec_batch, past, KV_W))
    cv = _expand_heads(cache_v[:, 0].reshape(dec_batch, past, KV_W))
    x1s, h2s, route_s, cnt_all, _, _, _ = mixers(
        xs, dec_seq, mod_row_s, _rope_tables(dec_seq), [(ck, past)], [(cv, past)],
        state_lru[:, 0], cnt_p)

    n_prompt = batch * seq
    n_tok = n_prompt + dec_batch * dec_seq
    n_tiles = (TOP_K * n_tok + N_EXPERTS * (TR_MOE - 1)) // TR_MOE
    cnt = cnt_all[0, :N_EXPERTS].astype(jnp.int32)
    ntile = (cnt + TR_MOE - 1) // TR_MOE
    tile_end = jnp.cumsum(ntile)
    seg_start = (tile_end - ntile) * TR_MOE
    n_used = tile_end[-1:]
    t_ids = jnp.minimum(jnp.arange(n_tiles, dtype=jnp.int32), n_used[0] - 1)
    tile_expert = jnp.sum((t_ids[:, None] >= tile_end[None, :]).astype(jnp.int32), axis=1)
    zfill = jnp.where(ntile > 0, tile_end - 1, -1)

    def slots_of(route):
        ids = route[:, :2 * TOP_K].astype(jnp.int32)
        return (seg_start[ids[:, :TOP_K]] + ids[:, TOP_K:]).reshape(-1)

    slots_p = slots_of(route_p)
    slots_s = slots_of(route_s)
    xsort = _dispatch(jnp.concatenate([slots_p, slots_s]), zfill, n_used, h2p, h2s,
                      n_tiles * TR_MOE)
    ysort = _ffn(tile_expert, n_used, xsort, exp_w_gate[0], exp_w_up[0], exp_w_down[0], n_tiles)
    yp = _combine(slots_p, ysort, x1p, route_p, mod3, mod_row_p, seq)
    ys = _combine(slots_s, ysort, x1s, route_s, mod3, mod_row_s, dec_seq)

    return (yp.reshape(batch, seq, D_MODEL),
            ys.reshape(dec_batch, dec_seq, D_MODEL),
            kf.reshape(batch, 1, seq, N_KV_HEADS, HEAD_DIM),
            vf.reshape(batch, 1, seq, N_KV_HEADS, HEAD_DIM),
            fin.reshape(batch, 1, 2, LRU_W))
```

```python
import functools

import jax
import jax.numpy as jnp
from jax import lax
from jax.experimental import pallas as pl
from jax.experimental.pallas import tpu as pltpu

F32 = jnp.float32
BF16 = jnp.bfloat16

D_MODEL = 1024
GRID_W = 64
ATTN_W = 512
LRU_W = 512
HEAD_DIM = 64
N_HEADS = 8
N_KV_HEADS = 2
KV_W = N_KV_HEADS * HEAD_DIM
LRU_BLOCKS = 8
LRU_BLOCK_W = LRU_W // LRU_BLOCKS
CONV_W = 4
LRU_C = 8.0
IN_W = ATTN_W + 2 * KV_W + 2 * LRU_W
QK_W = ATTN_W + KV_W
N_GROUPS = 4
EXPERTS_PER_GROUP = 8
N_EXPERTS = N_GROUPS * EXPERTS_PER_GROUP
TOP_K = 2
EXPERT_FF = D_MODEL // 4
ROPE_THETA = 10000.0
EPS = 1e-6

LANES = 128
SUBLANES = 8
MOD_ROWS = 8
VMEM_LIMIT = 48 * 1024 * 1024

TM_PRE = 256
TQ_ATT = 256
TC_LRU = 256
TR_MOE = 256
CH_DISPATCH = 512


def _cparams(sem):
    return pltpu.CompilerParams(dimension_semantics=sem, vmem_limit_bytes=VMEM_LIMIT)


def _dot(a, b):
    return jnp.dot(a, b, preferred_element_type=F32)


def _dot_nt(a, b):
    return lax.dot_general(a, b, (((1,), (1,)), ((), ())), preferred_element_type=F32)


def _split_bf16(x):
    hi = x.astype(BF16)
    lo = (x - hi.astype(F32)).astype(BF16)
    return hi, lo


def _mod_kernel(c_ref, w_ref, b_ref, o_ref):
    c = c_ref[...]
    s = (c * jax.nn.sigmoid(c)).astype(BF16)
    o_ref[...] = _dot(s, w_ref[...].astype(BF16)) + b_ref[...]


def _modulation(cvec, w_mod, b_mod):
    n_out = w_mod.shape[1]
    tn = n_out // 4
    return pl.pallas_call(
        _mod_kernel,
        grid=(n_out // tn,),
        in_specs=[pl.BlockSpec((MOD_ROWS, D_MODEL), lambda j: (0, 0)),
                  pl.BlockSpec((D_MODEL, tn), lambda j: (0, j)),
                  pl.BlockSpec((1, tn), lambda j: (0, j))],
        out_specs=pl.BlockSpec((MOD_ROWS, tn), lambda j: (0, j)),
        out_shape=jax.ShapeDtypeStruct((MOD_ROWS, n_out), F32),
        compiler_params=_cparams(("arbitrary",)),
        name="modulation",
    )(cvec, w_mod, b_mod)


def _pre_kernel(*refs, rope):
    if rope:
        (x_ref, mod_ref, g1_ref, win_ref, ones_ref, gqk_ref, cos_ref, sin_ref,
         q_ref, kx_ref, vx_ref, kf_ref, vf_ref, xr_ref, gb_ref) = refs
    else:
        (x_ref, mod_ref, g1_ref, win_ref, ones_ref, gqk_ref,
         q_ref, kx_ref, vx_ref, kf_ref, vf_ref, xr_ref, gb_ref) = refs
    x = x_ref[...]
    m = mod_ref[0]
    ms = jnp.mean(x * x, axis=-1, keepdims=True)
    y = x * lax.rsqrt(ms + EPS) * g1_ref[...]
    h = y * (1.0 + m[1:2]) + m[0:1]
    z = _dot(h.astype(BF16), win_ref[...])

    qk = z[:, :QK_W]
    hi, lo = _split_bf16(qk * qk)
    ss = _dot(hi, ones_ref[...]) + _dot(lo, ones_ref[...])
    qk = qk * lax.rsqrt(ss * (1.0 / HEAD_DIM) + EPS) * gqk_ref[...]

    lane = lax.broadcasted_iota(jnp.int32, (x.shape[0], LANES), 1)
    cols = []
    for c in range(QK_W // LANES):
        xc = qk[:, c * LANES:(c + 1) * LANES]
        if rope:
            left = pltpu.roll(xc, LANES - HEAD_DIM // 4, 1)
            right = pltpu.roll(xc, HEAD_DIM // 4, 1)
            rot = jnp.where((lane // (HEAD_DIM // 4)) % 2 == 0, left, right)
            xc = xc * cos_ref[...] + rot * sin_ref[...]
        cols.append(xc)
    for c in range(ATTN_W // LANES):
        q_ref[:, c * LANES:(c + 1) * LANES] = (cols[c] * (HEAD_DIM ** -0.5)).astype(BF16)

    lo_half = lane < HEAD_DIM
    for col, fref, xref in ((cols[ATTN_W // LANES], kf_ref, kx_ref),
                            (z[:, QK_W:QK_W + KV_W], vf_ref, vx_ref)):
        fref[...] = col
        swapped = pltpu.roll(col, HEAD_DIM, 1)
        xref[:, 0 * LANES:1 * LANES] = jnp.where(lo_half, col, 0.0).astype(BF16)
        xref[:, 1 * LANES:2 * LANES] = jnp.where(lo_half, 0.0, swapped).astype(BF16)
        xref[:, 2 * LANES:3 * LANES] = jnp.where(lo_half, swapped, 0.0).astype(BF16)
        xref[:, 3 * LANES:4 * LANES] = jnp.where(lo_half, 0.0, col).astype(BF16)

    xr_ref[...] = z[:, QK_W + KV_W:QK_W + KV_W + LRU_W]
    gb_ref[...] = z[:, QK_W + KV_W + LRU_W:]


def _pre(x, mod3, mod_row, g1, w_in, ones_qk, gqk, tables, seq_len):
    n = x.shape[0]
    tm = TM_PRE
    tiles_per_seq = seq_len // tm
    rope = tables is not None
    const = lambda i: (0, 0)
    in_specs = [pl.BlockSpec((tm, D_MODEL), lambda i: (i, 0)),
                pl.BlockSpec((1, 6, D_MODEL), lambda i: (mod_row(i // tiles_per_seq), 0, 0)),
                pl.BlockSpec((1, D_MODEL), const),
                pl.BlockSpec((D_MODEL, IN_W), const),
                pl.BlockSpec((QK_W, QK_W), const),
                pl.BlockSpec((1, QK_W), const)]
    args = [x, mod3, g1, w_in, ones_qk, gqk]
    if rope:
        in_specs += [pl.BlockSpec((tm, LANES), lambda i: (i % tiles_per_seq, 0))] * 2
        args += list(tables)
    row = lambda w: pl.BlockSpec((tm, w), lambda i: (i, 0))
    out_shape = [jax.ShapeDtypeStruct((n, ATTN_W), BF16),
                 jax.ShapeDtypeStruct((n, 4 * LANES), BF16),
                 jax.ShapeDtypeStruct((n, 4 * LANES), BF16),
                 jax.ShapeDtypeStruct((n, KV_W), F32),
                 jax.ShapeDtypeStruct((n, KV_W), F32),
                 jax.ShapeDtypeStruct((n, LRU_W), F32),
                 jax.ShapeDtypeStruct((n, LRU_W), F32)]
    out_specs = [row(ATTN_W), row(4 * LANES), row(4 * LANES), row(KV_W), row(KV_W),
                 row(LRU_W), row(LRU_W)]
    return pl.pallas_call(
        functools.partial(_pre_kernel, rope=rope),
        grid=(n // tm,),
        in_specs=in_specs, out_specs=out_specs, out_shape=out_shape,
        compiler_params=_cparams(("arbitrary",)),
        name="pre_rope" if rope else "pre",
    )(*args)


def _attn_kernel(*refs, n_seg):
    q_ref = refs[0]
    k_refs = refs[1:1 + n_seg]
    v_refs = refs[1 + n_seg:1 + 2 * n_seg]
    o_ref = refs[1 + 2 * n_seg]
    for c in range(ATTN_W // LANES):
        qc = q_ref[:, c * LANES:(c + 1) * LANES]
        g = c // 2
        acc = None
        for par in range(2):
            sl = slice((2 * g + par) * LANES, (2 * g + par + 1) * LANES)
            ss = [_dot_nt(qc, k[:, sl]) for k in k_refs]
            mx = functools.reduce(jnp.maximum, [jnp.max(s, axis=-1, keepdims=True) for s in ss])
            ps = [jnp.exp(s - mx) for s in ss]
            den = functools.reduce(lambda a, b: a + b, [jnp.sum(p, axis=-1, keepdims=True) for p in ps])
            o = functools.reduce(lambda a, b: a + b,
                                 [_dot(p.astype(BF16), v[:, sl]) for p, v in zip(ps, v_refs)])
            o = o / den
            acc = o if acc is None else acc + o
        o_ref[:, c * LANES:(c + 1) * LANES] = acc.astype(BF16)


def _attention(q, k_segs, v_segs, seq_len):
    n = q.shape[0]
    tq = TQ_ATT
    nq = seq_len // tq
    n_seg = len(k_segs)
    in_specs = [pl.BlockSpec((tq, ATTN_W), lambda b, i: (b * nq + i, 0))]
    for arr, t in list(k_segs) + list(v_segs):
        in_specs.append(pl.BlockSpec((t, 4 * LANES), lambda b, i: (b, 0)))
    return pl.pallas_call(
        functools.partial(_attn_kernel, n_seg=n_seg),
        grid=(n // seq_len, nq),
        in_specs=in_specs,
        out_specs=pl.BlockSpec((tq, ATTN_W), lambda b, i: (b * nq + i, 0)),
        out_shape=jax.ShapeDtypeStruct((n, ATTN_W), BF16),
        compiler_params=_cparams(("arbitrary", "arbitrary")),
        name="attention_%dseg" % n_seg,
    )(q, *[a for a, _ in k_segs], *[a for a, _ in v_segs])


def _log_sigmoid(x):
    return jnp.minimum(x, 0.0) - jnp.log1p(jnp.exp(-jnp.abs(x)))


def _tile_scan(a, b, reverse):
    row = lax.broadcasted_iota(jnp.int32, a.shape, 0)
    d = 1
    while d < SUBLANES:
        if reverse:
            keep = row < SUBLANES - d
            shift = SUBLANES - d
        else:
            keep = row >= d
            shift = d
        a_sh = jnp.where(keep, pltpu.roll(a, shift, 0), 1.0)
        b_sh = jnp.where(keep, pltpu.roll(b, shift, 0), 0.0)
        b = a * b_sh + b
        a = a * a_sh
        d *= 2
    return a, b


def _lru_kernel(xr_ref, gb_ref, h0_ref, cw_ref, cb_ref, wf_ref, wb_ref, bf_ref, bb_ref, lam_ref,
                rec_ref, fin_ref, xpad_s, xc_s, hf_s, a_s, b_s, *, seq_len):
    tc = TC_LRU
    n_chunks = seq_len // tc
    n_tiles = tc // SUBLANES
    zpad = jnp.zeros((SUBLANES, LRU_W), F32)
    xpad_s[0:SUBLANES, :] = zpad
    xpad_s[SUBLANES:SUBLANES + seq_len, :] = xr_ref[...]
    xpad_s[SUBLANES + seq_len:2 * SUBLANES + seq_len, :] = zpad

    cl = LRU_C * _log_sigmoid(lam_ref[...])

    def gates(xcc, w_ref, bias_ref, cl_d):
        g = _dot(xcc.astype(BF16), w_ref[...]) + bias_ref[...]
        r = jax.nn.sigmoid(g[:, :LRU_W])
        i = jax.nn.sigmoid(g[:, LRU_W:])
        log_a = r * cl_d
        a = jnp.exp(log_a)
        a_s[...] = a
        b_s[...] = jnp.sqrt(1.0 - a * a) * i * xcc

    h = h0_ref[0, 0:1, :]
    for c in range(n_chunks):
        base = c * tc
        xcc = cb_ref[...] + functools.reduce(
            lambda u, v: u + v,
            [cw_ref[j:j + 1, :] * xpad_s[base + SUBLANES - 1 + j:base + SUBLANES - 1 + j + tc, :]
             for j in range(CONV_W)])
        xc_s[base:base + tc, :] = xcc
        gates(xcc, wf_ref, bf_ref, cl[0:1])

        def fwd_tile(t, hc, base=base):
            r0 = pl.multiple_of(t * SUBLANES, SUBLANES)
            ca, cb = _tile_scan(a_s[pl.ds(r0, SUBLANES), :], b_s[pl.ds(r0, SUBLANES), :], False)
            hh = ca * hc + cb
            hf_s[pl.ds(base + r0, SUBLANES), :] = hh
            return hh[SUBLANES - 1:SUBLANES, :]

        h = lax.fori_loop(0, n_tiles, fwd_tile, h)
    fin_ref[0, 0:1, :] = h

    h = h0_ref[0, 1:2, :]
    for c in reversed(range(n_chunks)):
        base = c * tc
        gates(xc_s[base:base + tc, :], wb_ref, bb_ref, cl[1:2])

        def bwd_tile(t, hc, base=base):
            r0 = pl.multiple_of((n_tiles - 1 - t) * SUBLANES, SUBLANES)
            ca, cb = _tile_scan(a_s[pl.ds(r0, SUBLANES), :], b_s[pl.ds(r0, SUBLANES), :], True)
            hh = ca * hc + cb
            gate = jax.nn.gelu(gb_ref[pl.ds(base + r0, SUBLANES), :], approximate=True)
            rec_ref[pl.ds(base + r0, SUBLANES), :] = (
                (hf_s[pl.ds(base + r0, SUBLANES), :] + hh) * gate).astype(rec_ref.dtype)
            return hh[0:1, :]

        h = lax.fori_loop(0, n_tiles, bwd_tile, h)
    fin_ref[0, 1:2, :] = h


def _lru(xr, gb, h0, conv_w, conv_b, wf, wb, bf, bb, lam, seq_len):
    n = xr.shape[0]
    batch = n // seq_len
    const = lambda b: (0, 0)
    seq = pl.BlockSpec((seq_len, LRU_W), lambda b: (b, 0))
    st = pl.BlockSpec((1, 2, LRU_W), lambda b: (b, 0, 0))
    return pl.pallas_call(
        functools.partial(_lru_kernel, seq_len=seq_len),
        grid=(batch,),
        in_specs=[seq, seq, st,
                  pl.BlockSpec((CONV_W, LRU_W), const), pl.BlockSpec((1, LRU_W), const),
                  pl.BlockSpec((LRU_W, 2 * LRU_W), const), pl.BlockSpec((LRU_W, 2 * LRU_W), const),
                  pl.BlockSpec((1, 2 * LRU_W), const), pl.BlockSpec((1, 2 * LRU_W), const),
                  pl.BlockSpec((2, LRU_W), const)],
        out_specs=[seq, st],
        out_shape=[jax.ShapeDtypeStruct((n, LRU_W), BF16),
                   jax.ShapeDtypeStruct((batch, 2, LRU_W), F32)],
        scratch_shapes=[pltpu.VMEM((seq_len + 2 * SUBLANES, LRU_W), F32),
                        pltpu.VMEM((seq_len, LRU_W), F32),
                        pltpu.VMEM((seq_len, LRU_W), F32),
                        pltpu.VMEM((TC_LRU, LRU_W), F32),
                        pltpu.VMEM((TC_LRU, LRU_W), F32)],
        compiler_params=_cparams(("arbitrary",)),
        name="lru_%d" % seq_len,
    )(xr, gb, h0, conv_w, conv_b, wf, wb, bf, bb, lam)


def _post_kernel(attn_ref, rec_ref, x_ref, mod_ref, g2_ref, wo_ref, wr_hi_ref, wr_lo_ref, br_ref,
                 tri_ref, cnt_in_ref, x1_ref, h2_ref, route_ref, cnt_ref, cnt_s):
    @pl.when(pl.program_id(0) == 0)
    def _():
        cnt_s[...] = cnt_in_ref[...]

    m = mod_ref[0]
    u = _dot(attn_ref[...], wo_ref[:ATTN_W, :]) + _dot(rec_ref[...], wo_ref[ATTN_W:, :])
    x1 = x_ref[...] + m[2:3] * u
    x1_ref[...] = x1
    ms = jnp.mean(x1 * x1, axis=-1, keepdims=True)
    h2 = x1 * lax.rsqrt(ms + EPS) * g2_ref[...]
    h2 = h2 * (1.0 + m[4:5]) + m[3:4]
    hi, lo = _split_bf16(h2)
    h2_ref[...] = h2

    logits = (_dot(hi, wr_hi_ref[...]) + _dot(hi, wr_lo_ref[...]) + _dot(lo, wr_hi_ref[...])
              + br_ref[...])
    lane = lax.broadcasted_iota(jnp.int32, logits.shape, 1)
    neg = -jnp.inf
    big = jnp.int32(1 << 20)
    gmask = (lane >= N_EXPERTS) & (lane < N_EXPERTS + N_GROUPS)
    gl = jnp.where(gmask, logits, neg)
    gmax = jnp.max(gl, axis=-1, keepdims=True)
    gidx = jnp.min(jnp.where(gl == gmax, lane - N_EXPERTS, big), axis=-1, keepdims=True)
    p_sel = 1.0 / jnp.sum(jnp.where(gmask, jnp.exp(gl - gmax), 0.0), axis=-1, keepdims=True)

    emask = (lane < N_EXPERTS) & ((lane // EXPERTS_PER_GROUP) == gidx)
    el = jnp.where(emask, logits, neg)
    v1 = jnp.max(el, axis=-1, keepdims=True)
    i1 = jnp.min(jnp.where(el == v1, lane, big), axis=-1, keepdims=True)
    el2 = jnp.where(lane == i1, neg, el)
    v2 = jnp.max(el2, axis=-1, keepdims=True)
    i2 = jnp.min(jnp.where(el2 == v2, lane, big), axis=-1, keepdims=True)
    e2 = jnp.exp(v2 - v1)
    w1 = p_sel / (1.0 + e2)
    w2 = p_sel * e2 / (1.0 + e2)

    oh1 = lane == i1
    oh2 = lane == i2
    oh = jnp.where(oh1, 1.0, 0.0) + jnp.where(oh2, 1.0, 0.0)
    before = _dot(tri_ref[...], oh.astype(BF16)) + cnt_s[...]
    rank1 = jnp.sum(jnp.where(oh1, before, 0.0), axis=-1, keepdims=True)
    rank2 = jnp.sum(jnp.where(oh2, before, 0.0), axis=-1, keepdims=True)
    cnt = cnt_s[...] + jnp.sum(oh, axis=0, keepdims=True)
    cnt_s[...] = cnt
    cnt_ref[...] = cnt
    fields = (i1.astype(F32), i2.astype(F32), rank1, rank2, w1, w2)
    route = jnp.zeros(logits.shape, F32)
    for k, val in enumerate(fields):
        route = jnp.where(lane == k, val, route)
    route_ref[...] = route


def _post(attn, rec, x, mod3, mod_row, g2, w_out, wr_hi, wr_lo, br, tri, cnt_in, seq_len):
    n = x.shape[0]
    tm = TM_PRE
    tiles_per_seq = seq_len // tm
    const = lambda i: (0, 0)
    row = lambda w: pl.BlockSpec((tm, w), lambda i: (i, 0))
    return pl.pallas_call(
        _post_kernel,
        grid=(n // tm,),
        in_specs=[row(ATTN_W), row(LRU_W), row(D_MODEL),
                  pl.BlockSpec((1, 6, D_MODEL), lambda i: (mod_row(i // tiles_per_seq), 0, 0)),
                  pl.BlockSpec((1, D_MODEL), const),
                  pl.BlockSpec((D_MODEL, D_MODEL), const),
                  pl.BlockSpec((D_MODEL, LANES), const),
                  pl.BlockSpec((D_MODEL, LANES), const),
                  pl.BlockSpec((1, LANES), const),
                  pl.BlockSpec((tm, tm), const),
                  pl.BlockSpec((1, LANES), const)],
        out_specs=[row(D_MODEL), row(D_MODEL), row(LANES), pl.BlockSpec((1, LANES), const)],
        out_shape=[jax.ShapeDtypeStruct((n, D_MODEL), F32),
                   jax.ShapeDtypeStruct((n, D_MODEL), F32),
                   jax.ShapeDtypeStruct((n, LANES), F32),
                   jax.ShapeDtypeStruct((1, LANES), F32)],
        scratch_shapes=[pltpu.VMEM((1, LANES), F32)],
        compiler_params=_cparams(("arbitrary",)),
        name="post",
    )(attn, rec, x, mod3, g2, w_out, wr_hi, wr_lo, br, tri, cnt_in)


def _row_copy(src_ref, src_row, dst_ref, dst_row, sem):
    return pltpu.make_async_copy(src_ref.at[pl.ds(src_row, 1), :], dst_ref.at[pl.ds(dst_row, 1), :], sem)


def _token_slots(idx_ref, seg_ref, n):
    base = 2 * TOP_K * n
    return (seg_ref[idx_ref[base]] + idx_ref[base + 2], seg_ref[idx_ref[base + 1]] + idx_ref[base + 3])


def _dispatch_kernel(idx_ref, seg_ref, zfill_ref, nu_ref, h2p_ref, h2s_ref, xs_ref, zero_s, sem,
                     *, n_prompt):
    i = pl.program_id(0)
    ch = CH_DISPATCH

    @pl.when(i == 0)
    def _():
        zero_s[...] = jnp.zeros_like(zero_s)

        def zero_tile(t):
            return pltpu.make_async_copy(
                zero_s, xs_ref.at[pl.ds(pl.multiple_of(t * TR_MOE, TR_MOE), TR_MOE), :], sem)

        for e in range(N_EXPERTS):
            @pl.when(zfill_ref[e] >= 0)
            def _():
                zero_tile(zfill_ref[e]).start()
        n_all = xs_ref.shape[0] // TR_MOE
        lax.fori_loop(nu_ref[0], n_all, lambda t, c: (zero_tile(t).start(), c)[1], 0)
        for e in range(N_EXPERTS):
            @pl.when(zfill_ref[e] >= 0)
            def _():
                zero_tile(0).wait()
        lax.fori_loop(nu_ref[0], n_all, lambda t, c: (zero_tile(0).wait(), c)[1], 0)

    def scatter(src_ref):
        def body(j, carry):
            s1, s2 = _token_slots(idx_ref, seg_ref, i * ch + j)
            _row_copy(src_ref, j, xs_ref, s1, sem).start()
            _row_copy(src_ref, j, xs_ref, s2, sem).start()
            return carry
        lax.fori_loop(0, ch, body, 0, unroll=8)
        for _ in range(TOP_K):
            pltpu.make_async_copy(src_ref, xs_ref.at[pl.ds(0, ch), :], sem).wait()

    @pl.when(i < n_prompt // ch)
    def _():
        scatter(h2p_ref)

    @pl.when(i >= n_prompt // ch)
    def _():
        scatter(h2s_ref)


def _dispatch(idx, seg_start, zfill, n_used, h2p, h2s, n_rows):
    n_prompt = h2p.shape[0]
    n = n_prompt + h2s.shape[0]
    ch = CH_DISPATCH
    npc = n_prompt // ch
    last_p = npc - 1
    return pl.pallas_call(
        functools.partial(_dispatch_kernel, n_prompt=n_prompt),
        grid_spec=pltpu.PrefetchScalarGridSpec(
            num_scalar_prefetch=4,
            grid=(n // ch,),
            in_specs=[pl.BlockSpec((ch, D_MODEL), lambda i, *_: (jnp.minimum(i, last_p), 0)),
                      pl.BlockSpec((ch, D_MODEL), lambda i, *_: (jnp.maximum(i - npc, 0), 0))],
            out_specs=pl.BlockSpec(memory_space=pl.ANY),
            scratch_shapes=[pltpu.VMEM((TR_MOE, D_MODEL), F32), pltpu.SemaphoreType.DMA]),
        out_shape=jax.ShapeDtypeStruct((n_rows, D_MODEL), F32),
        compiler_params=_cparams(("arbitrary",)),
        name="dispatch",
    )(idx, seg_start, zfill, n_used, h2p, h2s)


def _ffn_kernel(te_ref, nu_ref, xs_ref, wg_ref, wu_ref, wd_ref, ys_ref, wg_s, wu_s, wd_s):
    t = pl.program_id(0)

    @pl.when(t < nu_ref[0])
    def _():
        prev = te_ref[jnp.maximum(t - 1, 0)]

        @pl.when((t == 0) | (te_ref[t] != prev))
        def _():
            wg_s[...] = wg_ref[0].astype(BF16)
            wu_s[...] = wu_ref[0].astype(BF16)
            wd_s[...] = wd_ref[0].astype(BF16)

        x = xs_ref[...].astype(BF16)
        hg = _dot(x, wg_s[...])
        hu = _dot(x, wu_s[...])
        act = (hg * jax.nn.sigmoid(hg)) * hu
        ys_ref[...] = _dot(act.astype(BF16), wd_s[...])

    @pl.when(t >= nu_ref[0])
    def _():
        ys_ref[...] = jnp.zeros_like(ys_ref)


def _ffn(tile_expert, n_used, xs, wg, wu, wd, n_tiles):
    tr = TR_MOE
    rows = lambda t, te, nu: (jnp.minimum(t, nu[0] - 1), 0)
    wsel = lambda t, te, nu: (te[t], 0, 0)
    return pl.pallas_call(
        _ffn_kernel,
        grid_spec=pltpu.PrefetchScalarGridSpec(
            num_scalar_prefetch=2,
            grid=(n_tiles,),
            in_specs=[pl.BlockSpec((tr, D_MODEL), rows),
                      pl.BlockSpec((1, D_MODEL, EXPERT_FF), wsel),
                      pl.BlockSpec((1, D_MODEL, EXPERT_FF), wsel),
                      pl.BlockSpec((1, EXPERT_FF, D_MODEL), wsel)],
            out_specs=pl.BlockSpec((tr, D_MODEL), lambda t, te, nu: (t, 0)),
            scratch_shapes=[pltpu.VMEM((D_MODEL, EXPERT_FF), BF16),
                            pltpu.VMEM((D_MODEL, EXPERT_FF), BF16),
                            pltpu.VMEM((EXPERT_FF, D_MODEL), BF16)]),
        out_shape=jax.ShapeDtypeStruct((n_tiles * tr, D_MODEL), F32),
        compiler_params=_cparams(("arbitrary",)),
        name="ffn",
    )(tile_expert, n_used, xs, wg, wu, wd)


def _combine_kernel(idx_ref, seg_ref, ys_ref, x1_ref, route_ref, mod_ref, y_ref, b1_s, b2_s, sems):
    i = pl.program_id(0)
    tm = x1_ref.shape[0]

    def gather(step, slot):
        def body(j, carry):
            s1, s2 = _token_slots(idx_ref, seg_ref, step * tm + j)
            _row_copy(ys_ref, s1, b1_s.at[slot], j, sems.at[slot]).start()
            _row_copy(ys_ref, s2, b2_s.at[slot], j, sems.at[slot]).start()
            return carry
        lax.fori_loop(0, tm, body, 0, unroll=8)

    @pl.when(i == 0)
    def _():
        gather(0, 0)

    @pl.when(i + 1 < pl.num_programs(0))
    def _():
        gather(i + 1, (i + 1) % 2)

    slot = i % 2
    for buf in (b1_s, b2_s):
        pltpu.make_async_copy(ys_ref.at[pl.ds(0, tm), :], buf.at[slot], sems.at[slot]).wait()
    route = route_ref[...]
    moe = route[:, 4:5] * b1_s[slot] + route[:, 5:6] * b2_s[slot]
    y_ref[...] = x1_ref[...] + mod_ref[0][5:6] * moe


def _combine(idx, seg_start, ys, x1, route, mod3, mod_row, seq_len):
    n = x1.shape[0]
    tm = TM_PRE
    tiles_per_seq = seq_len // tm
    row = lambda w: pl.BlockSpec((tm, w), lambda i, *_: (i, 0))
    return pl.pallas_call(
        _combine_kernel,
        grid_spec=pltpu.PrefetchScalarGridSpec(
            num_scalar_prefetch=2,
            grid=(n // tm,),
            in_specs=[pl.BlockSpec(memory_space=pl.ANY), row(D_MODEL), row(LANES),
                      pl.BlockSpec((1, 6, D_MODEL),
                                   lambda i, *_: (mod_row(i // tiles_per_seq), 0, 0))],
            out_specs=row(D_MODEL),
            scratch_shapes=[pltpu.VMEM((2, tm, D_MODEL), F32), pltpu.VMEM((2, tm, D_MODEL), F32),
                            pltpu.SemaphoreType.DMA((2,))]),
        out_shape=jax.ShapeDtypeStruct((n, D_MODEL), F32),
        compiler_params=_cparams(("arbitrary",)),
        name="combine",
    )(idx, seg_start, ys, x1, route, mod3)


def _rope_tables(length):
    rows = length // GRID_W
    r, col = jnp.meshgrid(jnp.arange(rows), jnp.arange(GRID_W), indexing='ij')
    r = r.reshape(-1).astype(F32)
    col = col.reshape(-1).astype(F32)
    half = HEAD_DIM // 2
    inv = ROPE_THETA ** (-jnp.arange(0, half, 2, dtype=F32) / half)
    ang_r = r[:, None] * inv
    ang_c = col[:, None] * inv
    ang = jnp.concatenate([ang_r, ang_r, ang_c, ang_c], axis=-1)
    sign = jnp.where((jnp.arange(HEAD_DIM) // (HEAD_DIM // 4)) % 2 == 0, -1.0, 1.0).astype(F32)
    cos = jnp.tile(jnp.cos(ang), (1, LANES // HEAD_DIM))
    sin = jnp.tile(jnp.sin(ang) * sign, (1, LANES // HEAD_DIM))
    return cos, sin


def _block_diag(w):
    eye = jnp.eye(LRU_BLOCKS, dtype=w.dtype)
    return jnp.einsum('hij,hg->higj', w, eye).reshape(LRU_W, LRU_W)


def _expand_heads(kv):
    b, t, _ = kv.shape
    h0 = kv[..., :HEAD_DIM]
    h1 = kv[..., HEAD_DIM:]
    z = jnp.zeros_like(h0)
    out = jnp.concatenate([h0, z, z, h0, h1, z, z, h1], axis=-1)
    return out.reshape(b * t, 4 * LANES).astype(BF16)


def kernel(x_prompt, x_sample, cache_k, cache_v, state_lru, c, c_ctx, w_mod, b_mod, norm1, norm2,
           w_in, q_norm, k_norm, conv_w, conv_b, lru_wa, lru_ba, lru_wx, lru_bx, lru_lambda, w_out,
           router_grp_w, router_grp_b, router_exp_w, router_exp_b, exp_w_gate, exp_w_up, exp_w_down):
    batch, seq, _ = x_prompt.shape
    dec_batch, dec_seq, _ = x_sample.shape
    past = cache_k.shape[2]
    depth = w_mod.shape[0]
    assert depth == 1

    cvec = jnp.concatenate(
        [c_ctx[None, :], c, jnp.zeros((MOD_ROWS - 1 - dec_batch, D_MODEL), F32)], axis=0)
    mod3 = _modulation(cvec, w_mod[0], b_mod[0][None, :]).reshape(MOD_ROWS, 6, D_MODEL)

    w_in_b = w_in[0].astype(BF16)
    w_out_b = w_out[0].astype(BF16)
    head_id = jnp.arange(QK_W) // HEAD_DIM
    ones_qk = (head_id[:, None] == head_id[None, :]).astype(BF16)
    gqk = jnp.concatenate([jnp.tile(q_norm[0], N_HEADS), jnp.tile(k_norm[0], N_KV_HEADS)])[None, :]
    wf = jnp.concatenate([_block_diag(lru_wa[0, 0]), _block_diag(lru_wx[0, 0])], axis=1).astype(BF16)
    wb = jnp.concatenate([_block_diag(lru_wa[0, 1]), _block_diag(lru_wx[0, 1])], axis=1).astype(BF16)
    bf = jnp.concatenate([lru_ba[0, 0], lru_bx[0, 0]])[None, :]
    bb = jnp.concatenate([lru_ba[0, 1], lru_bx[0, 1]])[None, :]
    pad = LANES - N_EXPERTS - N_GROUPS
    wr = jnp.concatenate([router_exp_w[0], router_grp_w[0], jnp.zeros((D_MODEL, pad), F32)], axis=1)
    wr_hi = wr.astype(BF16)
    wr_lo = (wr - wr_hi.astype(F32)).astype(BF16)
    br = jnp.concatenate([router_exp_b[0], router_grp_b[0], jnp.zeros((pad,), F32)])[None, :]
    g1 = norm1[0][None, :]
    g2 = norm2[0][None, :]
    cw = conv_w[0]
    cb = conv_b[0][None, :]
    lam = lru_lambda[0]
    tri = (jnp.arange(TM_PRE)[:, None] > jnp.arange(TM_PRE)[None, :]).astype(BF16)

    def mixers(x, seq_len, mod_row, tables, extra_k, extra_v, h0, cnt_in):
        q, kx, vx, kf, vf, xr, gb = _pre(x, mod3, mod_row, g1, w_in_b, ones_qk, gqk, tables, seq_len)
        k_segs = [(kx, seq_len)] + extra_k
        v_segs = [(vx, seq_len)] + extra_v
        attn = _attention(q, k_segs, v_segs, seq_len)
        rec, fin = _lru(xr, gb, h0, cw, cb, wf, wb, bf, bb, lam, seq_len)
        x1, h2, route, cnt = _post(attn, rec, x, mod3, mod_row, g2, w_out_b, wr_hi, wr_lo, br,
                                   tri, cnt_in, seq_len)
        return x1, h2, route, cnt, kf, vf, fin

    mod_row_p = lambda b: 0
    mod_row_s = lambda b: b + 1
    xp = x_prompt.reshape(batch * seq, D_MODEL)
    x1p, h2p, route_p, cnt_p, kf, vf, fin = mixers(
        xp, seq, mod_row_p, None, [], [], jnp.zeros((batch, 2, LRU_W), F32),
        jnp.zeros((1, LANES), F32))
    xs = x_sample.reshape(dec_batch * dec_seq, D_MODEL)
    ck = _expand_heads(cache_k[:, 0].reshape(dec_batch, past, KV_W))
    cv = _expand_heads(cache_v[:, 0].reshape(dec_batch, past, KV_W))
    x1s, h2s, route_s, cnt_all, _, _, _ = mixers(
        xs, dec_seq, mod_row_s, _rope_tables(dec_seq), [(ck, past)], [(cv, past)],
        state_lru[:, 0], cnt_p)

    n_prompt = batch * seq
    n_tok = n_prompt + dec_batch * dec_seq
    n_tiles = (TOP_K * n_tok + N_EXPERTS * (TR_MOE - 1)) // TR_MOE
    cnt = cnt_all[0, :N_EXPERTS].astype(jnp.int32)
    ntile = (cnt + TR_MOE - 1) // TR_MOE
    tile_end = jnp.cumsum(ntile)
    seg_start = (tile_end - ntile) * TR_MOE
    n_used = tile_end[-1:]
    t_ids = jnp.minimum(jnp.arange(n_tiles, dtype=jnp.int32), n_used[0] - 1)
    tile_expert = jnp.sum((t_ids[:, None] >= tile_end[None, :]).astype(jnp.int32), axis=1)
    zfill = jnp.where(ntile > 0, tile_end - 1, -1)

    idx_p = route_p[:, :2 * TOP_K].astype(jnp.int32).reshape(-1)
    idx_s = route_s[:, :2 * TOP_K].astype(jnp.int32).reshape(-1)
    xsort = _dispatch(jnp.concatenate([idx_p, idx_s]), seg_start, zfill, n_used, h2p, h2s,
                      n_tiles * TR_MOE)
    ysort = _ffn(tile_expert, n_used, xsort, exp_w_gate[0], exp_w_up[0], exp_w_down[0], n_tiles)
    yp = _combine(idx_p, seg_start, ysort, x1p, route_p, mod3, mod_row_p, seq)
    ys = _combine(idx_s, seg_start, ysort, x1s, route_s, mod3, mod_row_s, dec_seq)

    return (yp.reshape(batch, seq, D_MODEL),
            ys.reshape(dec_batch, dec_seq, D_MODEL),
            kf.reshape(batch, 1, seq, N_KV_HEADS, HEAD_DIM),
            vf.reshape(batch, 1, seq, N_KV_HEADS, HEAD_DIM),
            fin.reshape(batch, 1, 2, LRU_W))
```

```python
import functools

import jax
import jax.numpy as jnp
from jax import lax
from jax.experimental import pallas as pl
from jax.experimental.pallas import tpu as pltpu

F32 = jnp.float32
BF16 = jnp.bfloat16

D_MODEL = 1024
GRID_W = 64
ATTN_W = 512
LRU_W = 512
HEAD_DIM = 64
N_HEADS = 8
N_KV_HEADS = 2
KV_W = N_KV_HEADS * HEAD_DIM
LRU_BLOCKS = 8
LRU_BLOCK_W = LRU_W // LRU_BLOCKS
CONV_W = 4
LRU_C = 8.0
IN_W = ATTN_W + 2 * KV_W + 2 * LRU_W
QK_W = ATTN_W + KV_W
N_GROUPS = 4
EXPERTS_PER_GROUP = 8
N_EXPERTS = N_GROUPS * EXPERTS_PER_GROUP
TOP_K = 2
EXPERT_FF = D_MODEL // 4
ROPE_THETA = 10000.0
EPS = 1e-6

LANES = 128
SUBLANES = 8
MOD_ROWS = 8
VMEM_LIMIT = 48 * 1024 * 1024

TM_PRE = 256
TQ_ATT = 256
TC_LRU = 256
TR_MOE = 256
CH_DISPATCH = 512


def _cparams(sem):
    return pltpu.CompilerParams(dimension_semantics=sem, vmem_limit_bytes=VMEM_LIMIT)


def _dot(a, b):
    return jnp.dot(a, b, preferred_element_type=F32)


def _dot_nt(a, b):
    return lax.dot_general(a, b, (((1,), (1,)), ((), ())), preferred_element_type=F32)


ROW_TILE = (D_MODEL // LANES, LANES)


def _to_row_tiles(x):
    cols = jnp.stack([x[:, c * LANES:(c + 1) * LANES] for c in range(D_MODEL // LANES)], axis=0)
    return pltpu.einshape("ctl->tcl", cols)


def _from_row_tiles(x3):
    cols = pltpu.einshape("tcl->ctl", x3)
    return jnp.concatenate([cols[c] for c in range(D_MODEL // LANES)], axis=1)


def _split_bf16(x):
    hi = x.astype(BF16)
    lo = (x - hi.astype(F32)).astype(BF16)
    return hi, lo


def _mod_kernel(c_ref, w_ref, b_ref, o_ref):
    c = c_ref[...]
    s = (c * jax.nn.sigmoid(c)).astype(BF16)
    o_ref[...] = _dot(s, w_ref[...].astype(BF16)) + b_ref[...]


def _modulation(cvec, w_mod, b_mod):
    n_out = w_mod.shape[1]
    tn = n_out // 4
    return pl.pallas_call(
        _mod_kernel,
        grid=(n_out // tn,),
        in_specs=[pl.BlockSpec((MOD_ROWS, D_MODEL), lambda j: (0, 0)),
                  pl.BlockSpec((D_MODEL, tn), lambda j: (0, j)),
                  pl.BlockSpec((1, tn), lambda j: (0, j))],
        out_specs=pl.BlockSpec((MOD_ROWS, tn), lambda j: (0, j)),
        out_shape=jax.ShapeDtypeStruct((MOD_ROWS, n_out), F32),
        compiler_params=_cparams(("arbitrary",)),
        name="modulation",
    )(cvec, w_mod, b_mod)


def _pre_kernel(*refs, rope):
    if rope:
        (x_ref, mod_ref, g1_ref, win_ref, ones_ref, gqk_ref, cos_ref, sin_ref,
         q_ref, kx_ref, vx_ref, kf_ref, vf_ref, xr_ref, gb_ref) = refs
    else:
        (x_ref, mod_ref, g1_ref, win_ref, ones_ref, gqk_ref,
         q_ref, kx_ref, vx_ref, kf_ref, vf_ref, xr_ref, gb_ref) = refs
    x = x_ref[...]
    m = mod_ref[0]
    ms = jnp.mean(x * x, axis=-1, keepdims=True)
    y = x * lax.rsqrt(ms + EPS) * g1_ref[...]
    h = y * (1.0 + m[1:2]) + m[0:1]
    z = _dot(h.astype(BF16), win_ref[...])

    qk = z[:, :QK_W]
    hi, lo = _split_bf16(qk * qk)
    ss = _dot(hi, ones_ref[...]) + _dot(lo, ones_ref[...])
    qk = qk * lax.rsqrt(ss * (1.0 / HEAD_DIM) + EPS) * gqk_ref[...]

    lane = lax.broadcasted_iota(jnp.int32, (x.shape[0], LANES), 1)
    cols = []
    for c in range(QK_W // LANES):
        xc = qk[:, c * LANES:(c + 1) * LANES]
        if rope:
            left = pltpu.roll(xc, LANES - HEAD_DIM // 4, 1)
            right = pltpu.roll(xc, HEAD_DIM // 4, 1)
            rot = jnp.where((lane // (HEAD_DIM // 4)) % 2 == 0, left, right)
            xc = xc * cos_ref[...] + rot * sin_ref[...]
        cols.append(xc)
    for c in range(ATTN_W // LANES):
        q_ref[:, c * LANES:(c + 1) * LANES] = (cols[c] * (HEAD_DIM ** -0.5)).astype(BF16)

    lo_half = lane < HEAD_DIM
    for col, fref, xref in ((cols[ATTN_W // LANES], kf_ref, kx_ref),
                            (z[:, QK_W:QK_W + KV_W], vf_ref, vx_ref)):
        fref[...] = col
        swapped = pltpu.roll(col, HEAD_DIM, 1)
        xref[:, 0 * LANES:1 * LANES] = jnp.where(lo_half, col, 0.0).astype(BF16)
        xref[:, 1 * LANES:2 * LANES] = jnp.where(lo_half, 0.0, swapped).astype(BF16)
        xref[:, 2 * LANES:3 * LANES] = jnp.where(lo_half, swapped, 0.0).astype(BF16)
        xref[:, 3 * LANES:4 * LANES] = jnp.where(lo_half, 0.0, col).astype(BF16)

    xr_ref[...] = z[:, QK_W + KV_W:QK_W + KV_W + LRU_W]
    gb_ref[...] = z[:, QK_W + KV_W + LRU_W:]


def _pre(x, mod3, mod_row, g1, w_in, ones_qk, gqk, tables, seq_len):
    n = x.shape[0]
    tm = TM_PRE
    tiles_per_seq = seq_len // tm
    rope = tables is not None
    const = lambda i: (0, 0)
    in_specs = [pl.BlockSpec((tm, D_MODEL), lambda i: (i, 0)),
                pl.BlockSpec((1, 6, D_MODEL), lambda i: (mod_row(i // tiles_per_seq), 0, 0)),
                pl.BlockSpec((1, D_MODEL), const),
                pl.BlockSpec((D_MODEL, IN_W), const),
                pl.BlockSpec((QK_W, QK_W), const),
                pl.BlockSpec((1, QK_W), const)]
    args = [x, mod3, g1, w_in, ones_qk, gqk]
    if rope:
        in_specs += [pl.BlockSpec((tm, LANES), lambda i: (i % tiles_per_seq, 0))] * 2
        args += list(tables)
    row = lambda w: pl.BlockSpec((tm, w), lambda i: (i, 0))
    out_shape = [jax.ShapeDtypeStruct((n, ATTN_W), BF16),
                 jax.ShapeDtypeStruct((n, 4 * LANES), BF16),
                 jax.ShapeDtypeStruct((n, 4 * LANES), BF16),
                 jax.ShapeDtypeStruct((n, KV_W), F32),
                 jax.ShapeDtypeStruct((n, KV_W), F32),
                 jax.ShapeDtypeStruct((n, LRU_W), F32),
                 jax.ShapeDtypeStruct((n, LRU_W), F32)]
    out_specs = [row(ATTN_W), row(4 * LANES), row(4 * LANES), row(KV_W), row(KV_W),
                 row(LRU_W), row(LRU_W)]
    return pl.pallas_call(
        functools.partial(_pre_kernel, rope=rope),
        grid=(n // tm,),
        in_specs=in_specs, out_specs=out_specs, out_shape=out_shape,
        compiler_params=_cparams(("arbitrary",)),
        name="pre_rope" if rope else "pre",
    )(*args)


def _attn_kernel(*refs, n_seg):
    q_ref = refs[0]
    k_refs = refs[1:1 + n_seg]
    v_refs = refs[1 + n_seg:1 + 2 * n_seg]
    o_ref = refs[1 + 2 * n_seg]
    for c in range(ATTN_W // LANES):
        qc = q_ref[:, c * LANES:(c + 1) * LANES]
        g = c // 2
        acc = None
        for par in range(2):
            sl = slice((2 * g + par) * LANES, (2 * g + par + 1) * LANES)
            ss = [_dot_nt(qc, k[:, sl]) for k in k_refs]
            mx = functools.reduce(jnp.maximum, [jnp.max(s, axis=-1, keepdims=True) for s in ss])
            ps = [jnp.exp(s - mx) for s in ss]
            den = functools.reduce(lambda a, b: a + b, [jnp.sum(p, axis=-1, keepdims=True) for p in ps])
            o = functools.reduce(lambda a, b: a + b,
                                 [_dot(p.astype(BF16), v[:, sl]) for p, v in zip(ps, v_refs)])
            o = o / den
            acc = o if acc is None else acc + o
        o_ref[:, c * LANES:(c + 1) * LANES] = acc.astype(BF16)


def _attention(q, k_segs, v_segs, seq_len):
    n = q.shape[0]
    tq = TQ_ATT
    nq = seq_len // tq
    n_seg = len(k_segs)
    in_specs = [pl.BlockSpec((tq, ATTN_W), lambda b, i: (b * nq + i, 0))]
    for arr, t in list(k_segs) + list(v_segs):
        in_specs.append(pl.BlockSpec((t, 4 * LANES), lambda b, i: (b, 0)))
    return pl.pallas_call(
        functools.partial(_attn_kernel, n_seg=n_seg),
        grid=(n // seq_len, nq),
        in_specs=in_specs,
        out_specs=pl.BlockSpec((tq, ATTN_W), lambda b, i: (b * nq + i, 0)),
        out_shape=jax.ShapeDtypeStruct((n, ATTN_W), BF16),
        compiler_params=_cparams(("arbitrary", "arbitrary")),
        name="attention_%dseg" % n_seg,
    )(q, *[a for a, _ in k_segs], *[a for a, _ in v_segs])


def _log_sigmoid(x):
    return jnp.minimum(x, 0.0) - jnp.log1p(jnp.exp(-jnp.abs(x)))


def _tile_scan(a, b, reverse):
    row = lax.broadcasted_iota(jnp.int32, a.shape, 0)
    d = 1
    while d < SUBLANES:
        if reverse:
            keep = row < SUBLANES - d
            shift = SUBLANES - d
        else:
            keep = row >= d
            shift = d
        a_sh = jnp.where(keep, pltpu.roll(a, shift, 0), 1.0)
        b_sh = jnp.where(keep, pltpu.roll(b, shift, 0), 0.0)
        b = a * b_sh + b
        a = a * a_sh
        d *= 2
    return a, b


def _lru_kernel(xr_ref, gb_ref, h0_ref, cw_ref, cb_ref, wf_ref, wb_ref, bf_ref, bb_ref, lam_ref,
                rec_ref, fin_ref, xpad_s, xc_s, hf_s, a_s, b_s, *, seq_len):
    tc = TC_LRU
    n_chunks = seq_len // tc
    n_tiles = tc // SUBLANES
    zpad = jnp.zeros((SUBLANES, LRU_W), F32)
    xpad_s[0:SUBLANES, :] = zpad
    xpad_s[SUBLANES:SUBLANES + seq_len, :] = xr_ref[...]
    xpad_s[SUBLANES + seq_len:2 * SUBLANES + seq_len, :] = zpad

    cl = LRU_C * _log_sigmoid(lam_ref[...])

    def gates(xcc, w_ref, bias_ref, cl_d):
        g = _dot(xcc.astype(BF16), w_ref[...]) + bias_ref[...]
        r = jax.nn.sigmoid(g[:, :LRU_W])
        i = jax.nn.sigmoid(g[:, LRU_W:])
        log_a = r * cl_d
        a = jnp.exp(log_a)
        a_s[...] = a
        b_s[...] = jnp.sqrt(1.0 - a * a) * i * xcc

    h = h0_ref[0, 0:1, :]
    for c in range(n_chunks):
        base = c * tc
        xcc = cb_ref[...] + functools.reduce(
            lambda u, v: u + v,
            [cw_ref[j:j + 1, :] * xpad_s[base + SUBLANES - 1 + j:base + SUBLANES - 1 + j + tc, :]
             for j in range(CONV_W)])
        xc_s[base:base + tc, :] = xcc
        gates(xcc, wf_ref, bf_ref, cl[0:1])

        def fwd_tile(t, hc, base=base):
            r0 = pl.multiple_of(t * SUBLANES, SUBLANES)
            ca, cb = _tile_scan(a_s[pl.ds(r0, SUBLANES), :], b_s[pl.ds(r0, SUBLANES), :], False)
            hh = ca * hc + cb
            hf_s[pl.ds(base + r0, SUBLANES), :] = hh
            return hh[SUBLANES - 1:SUBLANES, :]

        h = lax.fori_loop(0, n_tiles, fwd_tile, h)
    fin_ref[0, 0:1, :] = h

    h = h0_ref[0, 1:2, :]
    for c in reversed(range(n_chunks)):
        base = c * tc
        gates(xc_s[base:base + tc, :], wb_ref, bb_ref, cl[1:2])

        def bwd_tile(t, hc, base=base):
            r0 = pl.multiple_of((n_tiles - 1 - t) * SUBLANES, SUBLANES)
            ca, cb = _tile_scan(a_s[pl.ds(r0, SUBLANES), :], b_s[pl.ds(r0, SUBLANES), :], True)
            hh = ca * hc + cb
            gate = jax.nn.gelu(gb_ref[pl.ds(base + r0, SUBLANES), :], approximate=True)
            rec_ref[pl.ds(base + r0, SUBLANES), :] = (
                (hf_s[pl.ds(base + r0, SUBLANES), :] + hh) * gate).astype(rec_ref.dtype)
            return hh[0:1, :]

        h = lax.fori_loop(0, n_tiles, bwd_tile, h)
    fin_ref[0, 1:2, :] = h


def _lru(xr, gb, h0, conv_w, conv_b, wf, wb, bf, bb, lam, seq_len):
    n = xr.shape[0]
    batch = n // seq_len
    const = lambda b: (0, 0)
    seq = pl.BlockSpec((seq_len, LRU_W), lambda b: (b, 0))
    st = pl.BlockSpec((1, 2, LRU_W), lambda b: (b, 0, 0))
    return pl.pallas_call(
        functools.partial(_lru_kernel, seq_len=seq_len),
        grid=(batch,),
        in_specs=[seq, seq, st,
                  pl.BlockSpec((CONV_W, LRU_W), const), pl.BlockSpec((1, LRU_W), const),
                  pl.BlockSpec((LRU_W, 2 * LRU_W), const), pl.BlockSpec((LRU_W, 2 * LRU_W), const),
                  pl.BlockSpec((1, 2 * LRU_W), const), pl.BlockSpec((1, 2 * LRU_W), const),
                  pl.BlockSpec((2, LRU_W), const)],
        out_specs=[seq, st],
        out_shape=[jax.ShapeDtypeStruct((n, LRU_W), BF16),
                   jax.ShapeDtypeStruct((batch, 2, LRU_W), F32)],
        scratch_shapes=[pltpu.VMEM((seq_len + 2 * SUBLANES, LRU_W), F32),
                        pltpu.VMEM((seq_len, LRU_W), F32),
                        pltpu.VMEM((seq_len, LRU_W), F32),
                        pltpu.VMEM((TC_LRU, LRU_W), F32),
                        pltpu.VMEM((TC_LRU, LRU_W), F32)],
        compiler_params=_cparams(("arbitrary",)),
        name="lru_%d" % seq_len,
    )(xr, gb, h0, conv_w, conv_b, wf, wb, bf, bb, lam)


def _post_kernel(attn_ref, rec_ref, x_ref, mod_ref, g2_ref, wo_ref, wr_hi_ref, wr_lo_ref, br_ref,
                 tri_ref, cnt_in_ref, x1_ref, h2_ref, route_ref, cnt_ref, cnt_s):
    @pl.when(pl.program_id(0) == 0)
    def _():
        cnt_s[...] = cnt_in_ref[...]

    m = mod_ref[0]
    u = _dot(attn_ref[...], wo_ref[:ATTN_W, :]) + _dot(rec_ref[...], wo_ref[ATTN_W:, :])
    x1 = x_ref[...] + m[2:3] * u
    x1_ref[...] = x1
    ms = jnp.mean(x1 * x1, axis=-1, keepdims=True)
    h2 = x1 * lax.rsqrt(ms + EPS) * g2_ref[...]
    h2 = h2 * (1.0 + m[4:5]) + m[3:4]
    hi, lo = _split_bf16(h2)
    h2_ref[...] = _to_row_tiles(h2)

    logits = (_dot(hi, wr_hi_ref[...]) + _dot(hi, wr_lo_ref[...]) + _dot(lo, wr_hi_ref[...])
              + br_ref[...])
    lane = lax.broadcasted_iota(jnp.int32, logits.shape, 1)
    neg = -jnp.inf
    big = jnp.int32(1 << 20)
    gmask = (lane >= N_EXPERTS) & (lane < N_EXPERTS + N_GROUPS)
    gl = jnp.where(gmask, logits, neg)
    gmax = jnp.max(gl, axis=-1, keepdims=True)
    gidx = jnp.min(jnp.where(gl == gmax, lane - N_EXPERTS, big), axis=-1, keepdims=True)
    p_sel = 1.0 / jnp.sum(jnp.where(gmask, jnp.exp(gl - gmax), 0.0), axis=-1, keepdims=True)

    emask = (lane < N_EXPERTS) & ((lane // EXPERTS_PER_GROUP) == gidx)
    el = jnp.where(emask, logits, neg)
    v1 = jnp.max(el, axis=-1, keepdims=True)
    i1 = jnp.min(jnp.where(el == v1, lane, big), axis=-1, keepdims=True)
    el2 = jnp.where(lane == i1, neg, el)
    v2 = jnp.max(el2, axis=-1, keepdims=True)
    i2 = jnp.min(jnp.where(el2 == v2, lane, big), axis=-1, keepdims=True)
    e2 = jnp.exp(v2 - v1)
    w1 = p_sel / (1.0 + e2)
    w2 = p_sel * e2 / (1.0 + e2)

    oh1 = lane == i1
    oh2 = lane == i2
    oh = jnp.where(oh1, 1.0, 0.0) + jnp.where(oh2, 1.0, 0.0)
    before = _dot(tri_ref[...], oh.astype(BF16)) + cnt_s[...]
    rank1 = jnp.sum(jnp.where(oh1, before, 0.0), axis=-1, keepdims=True)
    rank2 = jnp.sum(jnp.where(oh2, before, 0.0), axis=-1, keepdims=True)
    cnt = cnt_s[...] + jnp.sum(oh, axis=0, keepdims=True)
    cnt_s[...] = cnt
    cnt_ref[...] = cnt
    fields = (i1.astype(F32), i2.astype(F32), rank1, rank2, w1, w2)
    route = jnp.zeros(logits.shape, F32)
    for k, val in enumerate(fields):
        route = jnp.where(lane == k, val, route)
    route_ref[...] = route


def _post(attn, rec, x, mod3, mod_row, g2, w_out, wr_hi, wr_lo, br, tri, cnt_in, seq_len):
    n = x.shape[0]
    tm = TM_PRE
    tiles_per_seq = seq_len // tm
    const = lambda i: (0, 0)
    row = lambda w: pl.BlockSpec((tm, w), lambda i: (i, 0))
    return pl.pallas_call(
        _post_kernel,
        grid=(n // tm,),
        in_specs=[row(ATTN_W), row(LRU_W), row(D_MODEL),
                  pl.BlockSpec((1, 6, D_MODEL), lambda i: (mod_row(i // tiles_per_seq), 0, 0)),
                  pl.BlockSpec((1, D_MODEL), const),
                  pl.BlockSpec((D_MODEL, D_MODEL), const),
                  pl.BlockSpec((D_MODEL, LANES), const),
                  pl.BlockSpec((D_MODEL, LANES), const),
                  pl.BlockSpec((1, LANES), const),
                  pl.BlockSpec((tm, tm), const),
                  pl.BlockSpec((1, LANES), const)],
        out_specs=[row(D_MODEL), pl.BlockSpec((tm,) + ROW_TILE, lambda i: (i, 0, 0)), row(LANES),
                   pl.BlockSpec((1, LANES), const)],
        out_shape=[jax.ShapeDtypeStruct((n, D_MODEL), F32),
                   jax.ShapeDtypeStruct((n,) + ROW_TILE, F32),
                   jax.ShapeDtypeStruct((n, LANES), F32),
                   jax.ShapeDtypeStruct((1, LANES), F32)],
        scratch_shapes=[pltpu.VMEM((1, LANES), F32)],
        compiler_params=_cparams(("arbitrary",)),
        name="post",
    )(attn, rec, x, mod3, g2, w_out, wr_hi, wr_lo, br, tri, cnt_in)


def _row_copy(src_ref, src_row, dst_ref, dst_row, sem):
    return pltpu.make_async_copy(src_ref.at[pl.ds(src_row, 1)], dst_ref.at[pl.ds(dst_row, 1)], sem)


def _token_slots(idx_ref, seg_ref, n):
    base = 2 * TOP_K * n
    return (seg_ref[idx_ref[base]] + idx_ref[base + 2], seg_ref[idx_ref[base + 1]] + idx_ref[base + 3])


def _dispatch_kernel(idx_ref, seg_ref, zfill_ref, nu_ref, h2p_ref, h2s_ref, xs_ref, zero_s, sem,
                     *, n_prompt):
    i = pl.program_id(0)
    ch = CH_DISPATCH

    @pl.when(i == 0)
    def _():
        zero_s[...] = jnp.zeros_like(zero_s)

        def zero_tile(t):
            return pltpu.make_async_copy(
                zero_s, xs_ref.at[pl.ds(pl.multiple_of(t * TR_MOE, TR_MOE), TR_MOE)], sem)

        for e in range(N_EXPERTS):
            @pl.when(zfill_ref[e] >= 0)
            def _():
                zero_tile(zfill_ref[e]).start()
        n_all = xs_ref.shape[0] // TR_MOE
        lax.fori_loop(nu_ref[0], n_all, lambda t, c: (zero_tile(t).start(), c)[1], 0)
        for e in range(N_EXPERTS):
            @pl.when(zfill_ref[e] >= 0)
            def _():
                zero_tile(0).wait()
        lax.fori_loop(nu_ref[0], n_all, lambda t, c: (zero_tile(0).wait(), c)[1], 0)

    def scatter(src_ref):
        def body(j, carry):
            s1, s2 = _token_slots(idx_ref, seg_ref, i * ch + j)
            _row_copy(src_ref, j, xs_ref, s1, sem).start()
            _row_copy(src_ref, j, xs_ref, s2, sem).start()
            return carry
        lax.fori_loop(0, ch, body, 0, unroll=8)
        for _ in range(TOP_K):
            pltpu.make_async_copy(src_ref, xs_ref.at[pl.ds(0, ch)], sem).wait()

    @pl.when(i < n_prompt // ch)
    def _():
        scatter(h2p_ref)

    @pl.when(i >= n_prompt // ch)
    def _():
        scatter(h2s_ref)


def _dispatch(idx, seg_start, zfill, n_used, h2p, h2s, n_rows):
    n_prompt = h2p.shape[0]
    n = n_prompt + h2s.shape[0]
    ch = CH_DISPATCH
    npc = n_prompt // ch
    last_p = npc - 1
    return pl.pallas_call(
        functools.partial(_dispatch_kernel, n_prompt=n_prompt),
        grid_spec=pltpu.PrefetchScalarGridSpec(
            num_scalar_prefetch=4,
            grid=(n // ch,),
            in_specs=[pl.BlockSpec((ch,) + ROW_TILE, lambda i, *_: (jnp.minimum(i, last_p), 0, 0)),
                      pl.BlockSpec((ch,) + ROW_TILE, lambda i, *_: (jnp.maximum(i - npc, 0), 0, 0))],
            out_specs=pl.BlockSpec(memory_space=pl.ANY),
            scratch_shapes=[pltpu.VMEM((TR_MOE,) + ROW_TILE, F32), pltpu.SemaphoreType.DMA]),
        out_shape=jax.ShapeDtypeStruct((n_rows,) + ROW_TILE, F32),
        compiler_params=_cparams(("arbitrary",)),
        name="dispatch",
    )(idx, seg_start, zfill, n_used, h2p, h2s)


def _ffn_kernel(te_ref, nu_ref, xs_ref, wg_ref, wu_ref, wd_ref, ys_ref, wg_s, wu_s, wd_s):
    t = pl.program_id(0)

    @pl.when(t < nu_ref[0])
    def _():
        prev = te_ref[jnp.maximum(t - 1, 0)]

        @pl.when((t == 0) | (te_ref[t] != prev))
        def _():
            wg_s[...] = wg_ref[0].astype(BF16)
            wu_s[...] = wu_ref[0].astype(BF16)
            wd_s[...] = wd_ref[0].astype(BF16)

        x = _from_row_tiles(xs_ref[...]).astype(BF16)
        hg = _dot(x, wg_s[...])
        hu = _dot(x, wu_s[...])
        act = (hg * jax.nn.sigmoid(hg)) * hu
        ys_ref[...] = _to_row_tiles(_dot(act.astype(BF16), wd_s[...]))

    @pl.when(t >= nu_ref[0])
    def _():
        ys_ref[...] = jnp.zeros_like(ys_ref)


def _ffn(tile_expert, n_used, xs, wg, wu, wd, n_tiles):
    tr = TR_MOE
    rows = lambda t, te, nu: (jnp.minimum(t, nu[0] - 1), 0, 0)
    wsel = lambda t, te, nu: (te[t], 0, 0)
    return pl.pallas_call(
        _ffn_kernel,
        grid_spec=pltpu.PrefetchScalarGridSpec(
            num_scalar_prefetch=2,
            grid=(n_tiles,),
            in_specs=[pl.BlockSpec((tr,) + ROW_TILE, rows),
                      pl.BlockSpec((1, D_MODEL, EXPERT_FF), wsel),
                      pl.BlockSpec((1, D_MODEL, EXPERT_FF), wsel),
                      pl.BlockSpec((1, EXPERT_FF, D_MODEL), wsel)],
            out_specs=pl.BlockSpec((tr,) + ROW_TILE, lambda t, te, nu: (t, 0, 0)),
            scratch_shapes=[pltpu.VMEM((D_MODEL, EXPERT_FF), BF16),
                            pltpu.VMEM((D_MODEL, EXPERT_FF), BF16),
                            pltpu.VMEM((EXPERT_FF, D_MODEL), BF16)]),
        out_shape=jax.ShapeDtypeStruct((n_tiles * tr,) + ROW_TILE, F32),
        compiler_params=_cparams(("arbitrary",)),
        name="ffn",
    )(tile_expert, n_used, xs, wg, wu, wd)


def _combine_kernel(idx_ref, seg_ref, ys_ref, x1_ref, route_ref, mod_ref, y_ref, b1_s, b2_s, sems):
    i = pl.program_id(0)
    tm = x1_ref.shape[0]

    def gather(step, slot):
        def body(j, carry):
            s1, s2 = _token_slots(idx_ref, seg_ref, step * tm + j)
            _row_copy(ys_ref, s1, b1_s.at[slot], j, sems.at[slot]).start()
            _row_copy(ys_ref, s2, b2_s.at[slot], j, sems.at[slot]).start()
            return carry
        lax.fori_loop(0, tm, body, 0, unroll=8)

    @pl.when(i == 0)
    def _():
        gather(0, 0)

    @pl.when(i + 1 < pl.num_programs(0))
    def _():
        gather(i + 1, (i + 1) % 2)

    slot = i % 2
    for buf in (b1_s, b2_s):
        pltpu.make_async_copy(ys_ref.at[pl.ds(0, tm)], buf.at[slot], sems.at[slot]).wait()
    route = route_ref[...]
    moe = route[:, 4:5] * _from_row_tiles(b1_s[slot]) + route[:, 5:6] * _from_row_tiles(b2_s[slot])
    y_ref[...] = x1_ref[...] + mod_ref[0][5:6] * moe


def _combine(idx, seg_start, ys, x1, route, mod3, mod_row, seq_len):
    n = x1.shape[0]
    tm = TM_PRE
    tiles_per_seq = seq_len // tm
    row = lambda w: pl.BlockSpec((tm, w), lambda i, *_: (i, 0))
    return pl.pallas_call(
        _combine_kernel,
        grid_spec=pltpu.PrefetchScalarGridSpec(
            num_scalar_prefetch=2,
            grid=(n // tm,),
            in_specs=[pl.BlockSpec(memory_space=pl.ANY), row(D_MODEL), row(LANES),
                      pl.BlockSpec((1, 6, D_MODEL),
                                   lambda i, *_: (mod_row(i // tiles_per_seq), 0, 0))],
            out_specs=row(D_MODEL),
            scratch_shapes=[pltpu.VMEM((2, tm) + ROW_TILE, F32), pltpu.VMEM((2, tm) + ROW_TILE, F32),
                            pltpu.SemaphoreType.DMA((2,))]),
        out_shape=jax.ShapeDtypeStruct((n, D_MODEL), F32),
        compiler_params=_cparams(("arbitrary",)),
        name="combine",
    )(idx, seg_start, ys, x1, route, mod3)


def _rope_tables(length):
    rows = length // GRID_W
    r, col = jnp.meshgrid(jnp.arange(rows), jnp.arange(GRID_W), indexing='ij')
    r = r.reshape(-1).astype(F32)
    col = col.reshape(-1).astype(F32)
    half = HEAD_DIM // 2
    inv = ROPE_THETA ** (-jnp.arange(0, half, 2, dtype=F32) / half)
    ang_r = r[:, None] * inv
    ang_c = col[:, None] * inv
    ang = jnp.concatenate([ang_r, ang_r, ang_c, ang_c], axis=-1)
    sign = jnp.where((jnp.arange(HEAD_DIM) // (HEAD_DIM // 4)) % 2 == 0, -1.0, 1.0).astype(F32)
    cos = jnp.tile(jnp.cos(ang), (1, LANES // HEAD_DIM))
    sin = jnp.tile(jnp.sin(ang) * sign, (1, LANES // HEAD_DIM))
    return cos, sin


def _block_diag(w):
    eye = jnp.eye(LRU_BLOCKS, dtype=w.dtype)
    return jnp.einsum('hij,hg->higj', w, eye).reshape(LRU_W, LRU_W)


def _expand_heads(kv):
    b, t, _ = kv.shape
    h0 = kv[..., :HEAD_DIM]
    h1 = kv[..., HEAD_DIM:]
    z = jnp.zeros_like(h0)
    out = jnp.concatenate([h0, z, z, h0, h1, z, z, h1], axis=-1)
    return out.reshape(b * t, 4 * LANES).astype(BF16)


def kernel(x_prompt, x_sample, cache_k, cache_v, state_lru, c, c_ctx, w_mod, b_mod, norm1, norm2,
           w_in, q_norm, k_norm, conv_w, conv_b, lru_wa, lru_ba, lru_wx, lru_bx, lru_lambda, w_out,
           router_grp_w, router_grp_b, router_exp_w, router_exp_b, exp_w_gate, exp_w_up, exp_w_down):
    batch, seq, _ = x_prompt.shape
    dec_batch, dec_seq, _ = x_sample.shape
    past = cache_k.shape[2]
    depth = w_mod.shape[0]
    assert depth == 1

    cvec = jnp.concatenate(
        [c_ctx[None, :], c, jnp.zeros((MOD_ROWS - 1 - dec_batch, D_MODEL), F32)], axis=0)
    mod3 = _modulation(cvec, w_mod[0], b_mod[0][None, :]).reshape(MOD_ROWS, 6, D_MODEL)

    w_in_b = w_in[0].astype(BF16)
    w_out_b = w_out[0].astype(BF16)
    head_id = jnp.arange(QK_W) // HEAD_DIM
    ones_qk = (head_id[:, None] == head_id[None, :]).astype(BF16)
    gqk = jnp.concatenate([jnp.tile(q_norm[0], N_HEADS), jnp.tile(k_norm[0], N_KV_HEADS)])[None, :]
    wf = jnp.concatenate([_block_diag(lru_wa[0, 0]), _block_diag(lru_wx[0, 0])], axis=1).astype(BF16)
    wb = jnp.concatenate([_block_diag(lru_wa[0, 1]), _block_diag(lru_wx[0, 1])], axis=1).astype(BF16)
    bf = jnp.concatenate([lru_ba[0, 0], lru_bx[0, 0]])[None, :]
    bb = jnp.concatenate([lru_ba[0, 1], lru_bx[0, 1]])[None, :]
    pad = LANES - N_EXPERTS - N_GROUPS
    wr = jnp.concatenate([router_exp_w[0], router_grp_w[0], jnp.zeros((D_MODEL, pad), F32)], axis=1)
    wr_hi = wr.astype(BF16)
    wr_lo = (wr - wr_hi.astype(F32)).astype(BF16)
    br = jnp.concatenate([router_exp_b[0], router_grp_b[0], jnp.zeros((pad,), F32)])[None, :]
    g1 = norm1[0][None, :]
    g2 = norm2[0][None, :]
    cw = conv_w[0]
    cb = conv_b[0][None, :]
    lam = lru_lambda[0]
    tri = (jnp.arange(TM_PRE)[:, None] > jnp.arange(TM_PRE)[None, :]).astype(BF16)

    def mixers(x, seq_len, mod_row, tables, extra_k, extra_v, h0, cnt_in):
        q, kx, vx, kf, vf, xr, gb = _pre(x, mod3, mod_row, g1, w_in_b, ones_qk, gqk, tables, seq_len)
        k_segs = [(kx, seq_len)] + extra_k
        v_segs = [(vx, seq_len)] + extra_v
        attn = _attention(q, k_segs, v_segs, seq_len)
        rec, fin = _lru(xr, gb, h0, cw, cb, wf, wb, bf, bb, lam, seq_len)
        x1, h2, route, cnt = _post(attn, rec, x, mod3, mod_row, g2, w_out_b, wr_hi, wr_lo, br,
                                   tri, cnt_in, seq_len)
        return x1, h2, route, cnt, kf, vf, fin

    mod_row_p = lambda b: 0
    mod_row_s = lambda b: b + 1
    xp = x_prompt.reshape(batch * seq, D_MODEL)
    x1p, h2p, route_p, cnt_p, kf, vf, fin = mixers(
        xp, seq, mod_row_p, None, [], [], jnp.zeros((batch, 2, LRU_W), F32),
        jnp.zeros((1, LANES), F32))
    xs = x_sample.reshape(dec_batch * dec_seq, D_MODEL)
    ck = _expand_heads(cache_k[:, 0].reshape(dec_batch, past, KV_W))
    cv = _expand_heads(cache_v[:, 0].reshape(dec_batch, past, KV_W))
    x1s, h2s, route_s, cnt_all, _, _, _ = mixers(
        xs, dec_seq, mod_row_s, _rope_tables(dec_seq), [(ck, past)], [(cv, past)],
        state_lru[:, 0], cnt_p)

    n_prompt = batch * seq
    n_tok = n_prompt + dec_batch * dec_seq
    n_tiles = (TOP_K * n_tok + N_EXPERTS * (TR_MOE - 1)) // TR_MOE
    cnt = cnt_all[0, :N_EXPERTS].astype(jnp.int32)
    ntile = (cnt + TR_MOE - 1) // TR_MOE
    tile_end = jnp.cumsum(ntile)
    seg_start = (tile_end - ntile) * TR_MOE
    n_used = tile_end[-1:]
    t_ids = jnp.minimum(jnp.arange(n_tiles, dtype=jnp.int32), n_used[0] - 1)
    tile_expert = jnp.sum((t_ids[:, None] >= tile_end[None, :]).astype(jnp.int32), axis=1)
    zfill = jnp.where(ntile > 0, tile_end - 1, -1)

    idx_p = route_p[:, :2 * TOP_K].astype(jnp.int32).reshape(-1)
    idx_s = route_s[:, :2 * TOP_K].astype(jnp.int32).reshape(-1)
    xsort = _dispatch(jnp.concatenate([idx_p, idx_s]), seg_start, zfill, n_used, h2p, h2s,
                      n_tiles * TR_MOE)
    ysort = _ffn(tile_expert, n_used, xsort, exp_w_gate[0], exp_w_up[0], exp_w_down[0], n_tiles)
    yp = _combine(idx_p, seg_start, ysort, x1p, route_p, mod3, mod_row_p, seq)
    ys = _combine(idx_s, seg_start, ysort, x1s, route_s, mod3, mod_row_s, dec_seq)

    return (yp.reshape(batch, seq, D_MODEL),
            ys.reshape(dec_batch, dec_seq, D_MODEL),
            kf.reshape(batch, 1, seq, N_KV_HEADS, HEAD_DIM),
            vf.reshape(batch, 1, seq, N_KV_HEADS, HEAD_DIM),
            fin.reshape(batch, 1, 2, LRU_W))
```

```python
import functools

import jax
import jax.numpy as jnp
from jax import lax
from jax.experimental import pallas as pl
from jax.experimental.pallas import tpu as pltpu

F32 = jnp.float32
BF16 = jnp.bfloat16

D_MODEL = 1024
GRID_W = 64
ATTN_W = 512
LRU_W = 512
HEAD_DIM = 64
N_HEADS = 8
N_KV_HEADS = 2
KV_W = N_KV_HEADS * HEAD_DIM
LRU_BLOCKS = 8
LRU_BLOCK_W = LRU_W // LRU_BLOCKS
CONV_W = 4
LRU_C = 8.0
IN_W = ATTN_W + 2 * KV_W + 2 * LRU_W
QK_W = ATTN_W + KV_W
N_GROUPS = 4
EXPERTS_PER_GROUP = 8
N_EXPERTS = N_GROUPS * EXPERTS_PER_GROUP
TOP_K = 2
EXPERT_FF = D_MODEL // 4
ROPE_THETA = 10000.0
EPS = 1e-6

LANES = 128
SUBLANES = 8
MOD_ROWS = 8
VMEM_LIMIT = 48 * 1024 * 1024

TM_PRE = 256
TQ_ATT = 256
TC_LRU = 256
TR_MOE = 256
CH_DISPATCH = 1024


def _cparams(sem):
    return pltpu.CompilerParams(dimension_semantics=sem, vmem_limit_bytes=VMEM_LIMIT)


def _dot(a, b):
    return jnp.dot(a, b, preferred_element_type=F32)


def _dot_nt(a, b):
    return lax.dot_general(a, b, (((1,), (1,)), ((), ())), preferred_element_type=F32)


ROW_TILE = (D_MODEL // LANES, LANES)
Q_SCALE = HEAD_DIM ** -0.5 * 1.4426950408889634
ONES_LANE_EVEN = HEAD_DIM
ONES_LANE_ODD = 0


def _to_row_tiles(x):
    cols = jnp.stack([x[:, c * LANES:(c + 1) * LANES] for c in range(D_MODEL // LANES)], axis=0)
    return jnp.swapaxes(cols, 0, 1)


def _from_row_tiles(x3):
    cols = jnp.swapaxes(x3, 0, 1)
    return jnp.concatenate([cols[c] for c in range(D_MODEL // LANES)], axis=1)


def _sigmoid(x):
    return 0.5 * jnp.tanh(0.5 * x) + 0.5


def _split_bf16(x):
    hi = x.astype(BF16)
    lo = (x - hi.astype(F32)).astype(BF16)
    return hi, lo


def _mod_kernel(c_ref, w_ref, b_ref, o_ref):
    c = c_ref[...]
    s = (c * jax.nn.sigmoid(c)).astype(BF16)
    o_ref[...] = _dot(s, w_ref[...].astype(BF16)) + b_ref[...]


def _modulation(cvec, w_mod, b_mod):
    n_out = w_mod.shape[1]
    tn = n_out // 4
    return pl.pallas_call(
        _mod_kernel,
        grid=(n_out // tn,),
        in_specs=[pl.BlockSpec((MOD_ROWS, D_MODEL), lambda j: (0, 0)),
                  pl.BlockSpec((D_MODEL, tn), lambda j: (0, j)),
                  pl.BlockSpec((1, tn), lambda j: (0, j))],
        out_specs=pl.BlockSpec((MOD_ROWS, tn), lambda j: (0, j)),
        out_shape=jax.ShapeDtypeStruct((MOD_ROWS, n_out), F32),
        compiler_params=_cparams(("arbitrary",)),
        name="modulation",
    )(cvec, w_mod, b_mod)


def _pre_kernel(*refs, rope):
    if rope:
        (x_ref, mod_ref, g1_ref, win_ref, ones_ref, gqk_ref, cos_ref, sin_ref,
         q_ref, kx_ref, vx_ref, kf_ref, vf_ref, xr_ref, gb_ref) = refs
    else:
        (x_ref, mod_ref, g1_ref, win_ref, ones_ref, gqk_ref,
         q_ref, kx_ref, vx_ref, kf_ref, vf_ref, xr_ref, gb_ref) = refs
    x = x_ref[...]
    m = mod_ref[0]
    ms = jnp.mean(x * x, axis=-1, keepdims=True)
    y = x * lax.rsqrt(ms + EPS) * g1_ref[...]
    h = y * (1.0 + m[1:2]) + m[0:1]
    z = _dot(h.astype(BF16), win_ref[...])

    qk = z[:, :QK_W]
    hi, lo = _split_bf16(qk * qk)
    ss = _dot(hi, ones_ref[...]) + _dot(lo, ones_ref[...])
    qk = qk * lax.rsqrt(ss * (1.0 / HEAD_DIM) + EPS) * gqk_ref[...]

    lane = lax.broadcasted_iota(jnp.int32, (x.shape[0], LANES), 1)
    cols = []
    for c in range(QK_W // LANES):
        xc = qk[:, c * LANES:(c + 1) * LANES]
        if rope:
            left = pltpu.roll(xc, LANES - HEAD_DIM // 4, 1)
            right = pltpu.roll(xc, HEAD_DIM // 4, 1)
            rot = jnp.where((lane // (HEAD_DIM // 4)) % 2 == 0, left, right)
            xc = xc * cos_ref[...] + rot * sin_ref[...]
        cols.append(xc)
    for c in range(ATTN_W // LANES):
        q_ref[:, c * LANES:(c + 1) * LANES] = (cols[c] * Q_SCALE).astype(BF16)

    lo_half = lane < HEAD_DIM
    for col, fref, xref, one in ((cols[ATTN_W // LANES], kf_ref, kx_ref, 0.0),
                                 (z[:, QK_W:QK_W + KV_W], vf_ref, vx_ref, 1.0)):
        fref[...] = col
        swapped = pltpu.roll(col, HEAD_DIM, 1)
        fill_hi = jnp.where(lane == ONES_LANE_EVEN, one, 0.0)
        fill_lo = jnp.where(lane == ONES_LANE_ODD, one, 0.0)
        xref[:, 0 * LANES:1 * LANES] = jnp.where(lo_half, col, fill_hi).astype(BF16)
        xref[:, 1 * LANES:2 * LANES] = jnp.where(lo_half, fill_lo, swapped).astype(BF16)
        xref[:, 2 * LANES:3 * LANES] = jnp.where(lo_half, swapped, fill_hi).astype(BF16)
        xref[:, 3 * LANES:4 * LANES] = jnp.where(lo_half, fill_lo, col).astype(BF16)

    xr_ref[...] = z[:, QK_W + KV_W:QK_W + KV_W + LRU_W]
    gb_ref[...] = z[:, QK_W + KV_W + LRU_W:]


def _pre(x, mod3, mod_row, g1, w_in, ones_qk, gqk, tables, seq_len):
    n = x.shape[0]
    tm = TM_PRE
    tiles_per_seq = seq_len // tm
    rope = tables is not None
    const = lambda i: (0, 0)
    in_specs = [pl.BlockSpec((tm, D_MODEL), lambda i: (i, 0)),
                pl.BlockSpec((1, 6, D_MODEL), lambda i: (mod_row(i // tiles_per_seq), 0, 0)),
                pl.BlockSpec((1, D_MODEL), const),
                pl.BlockSpec((D_MODEL, IN_W), const),
                pl.BlockSpec((QK_W, QK_W), const),
                pl.BlockSpec((1, QK_W), const)]
    args = [x, mod3, g1, w_in, ones_qk, gqk]
    if rope:
        in_specs += [pl.BlockSpec((tm, LANES), lambda i: (i % tiles_per_seq, 0))] * 2
        args += list(tables)
    row = lambda w: pl.BlockSpec((tm, w), lambda i: (i, 0))
    out_shape = [jax.ShapeDtypeStruct((n, ATTN_W), BF16),
                 jax.ShapeDtypeStruct((n, 4 * LANES), BF16),
                 jax.ShapeDtypeStruct((n, 4 * LANES), BF16),
                 jax.ShapeDtypeStruct((n, KV_W), F32),
                 jax.ShapeDtypeStruct((n, KV_W), F32),
                 jax.ShapeDtypeStruct((n, LRU_W), F32),
                 jax.ShapeDtypeStruct((n, LRU_W), F32)]
    out_specs = [row(ATTN_W), row(4 * LANES), row(4 * LANES), row(KV_W), row(KV_W),
                 row(LRU_W), row(LRU_W)]
    return pl.pallas_call(
        functools.partial(_pre_kernel, rope=rope),
        grid=(n // tm,),
        in_specs=in_specs, out_specs=out_specs, out_shape=out_shape,
        compiler_params=_cparams(("arbitrary",)),
        name="pre_rope" if rope else "pre",
    )(*args)


def _attn_kernel(*refs, n_seg):
    q_ref = refs[0]
    k_refs = refs[1:1 + n_seg]
    v_refs = refs[1 + n_seg:1 + 2 * n_seg]
    o_ref = refs[1 + 2 * n_seg]
    lane = lax.broadcasted_iota(jnp.int32, (q_ref.shape[0], LANES), 1)
    for c in range(ATTN_W // LANES):
        qc = q_ref[:, c * LANES:(c + 1) * LANES]
        g = c // 2
        accs = []
        for par in range(2):
            sl = slice((2 * g + par) * LANES, (2 * g + par + 1) * LANES)
            ss = [_dot_nt(qc, k[:, sl]) for k in k_refs]
            mx = functools.reduce(jnp.maximum, [jnp.max(s, axis=-1, keepdims=True) for s in ss])
            ps = [jnp.exp2((s - mx).astype(BF16)) for s in ss]
            accs.append(functools.reduce(lambda a, b: a + b,
                                         [_dot(p, v[:, sl]) for p, v in zip(ps, v_refs)]))
        even = accs[0] / accs[0][:, ONES_LANE_EVEN:ONES_LANE_EVEN + 1]
        odd = accs[1] / accs[1][:, ONES_LANE_ODD:ONES_LANE_ODD + 1]
        o_ref[:, c * LANES:(c + 1) * LANES] = jnp.where(lane < HEAD_DIM, even, odd).astype(BF16)


def _attention(q, k_segs, v_segs, seq_len):
    n = q.shape[0]
    tq = TQ_ATT
    nq = seq_len // tq
    n_seg = len(k_segs)
    in_specs = [pl.BlockSpec((tq, ATTN_W), lambda b, i: (b * nq + i, 0))]
    for arr, t in list(k_segs) + list(v_segs):
        in_specs.append(pl.BlockSpec((t, 4 * LANES), lambda b, i: (b, 0)))
    return pl.pallas_call(
        functools.partial(_attn_kernel, n_seg=n_seg),
        grid=(n // seq_len, nq),
        in_specs=in_specs,
        out_specs=pl.BlockSpec((tq, ATTN_W), lambda b, i: (b * nq + i, 0)),
        out_shape=jax.ShapeDtypeStruct((n, ATTN_W), BF16),
        compiler_params=_cparams(("arbitrary", "arbitrary")),
        name="attention_%dseg" % n_seg,
    )(q, *[a for a, _ in k_segs], *[a for a, _ in v_segs])


def _log_sigmoid(x):
    return jnp.minimum(x, 0.0) - jnp.log1p(jnp.exp(-jnp.abs(x)))


def _tile_scan(a, b, reverse):
    row = lax.broadcasted_iota(jnp.int32, a.shape, 0)
    d = 1
    while d < SUBLANES:
        if reverse:
            keep = row < SUBLANES - d
            shift = SUBLANES - d
        else:
            keep = row >= d
            shift = d
        a_sh = jnp.where(keep, pltpu.roll(a, shift, 0), 1.0)
        b_sh = jnp.where(keep, pltpu.roll(b, shift, 0), 0.0)
        b = a * b_sh + b
        a = a * a_sh
        d *= 2
    return a, b


def _lru_kernel(xr_ref, gb_ref, h0_ref, cw_ref, cb_ref, wf_ref, wb_ref, bf_ref, bb_ref, lam_ref,
                rec_ref, fin_ref, xpad_s, xc_s, hf_s, a_s, b_s, *, seq_len):
    tc = TC_LRU
    n_chunks = seq_len // tc
    n_tiles = tc // SUBLANES
    zpad = jnp.zeros((SUBLANES, LRU_W), F32)
    xpad_s[0:SUBLANES, :] = zpad
    xpad_s[SUBLANES:SUBLANES + seq_len, :] = xr_ref[...]
    xpad_s[SUBLANES + seq_len:2 * SUBLANES + seq_len, :] = zpad

    cl = LRU_C * _log_sigmoid(lam_ref[...])

    def gates(xcc, w_ref, bias_ref, cl_d):
        g = _dot(xcc.astype(BF16), w_ref[...]) + bias_ref[...]
        r = _sigmoid(g[:, :LRU_W])
        i = _sigmoid(g[:, LRU_W:])
        log_a = r * cl_d
        a = jnp.exp(log_a)
        a_s[...] = a
        b_s[...] = jnp.sqrt(1.0 - a * a) * i * xcc

    h = h0_ref[0, 0:1, :]
    for c in range(n_chunks):
        base = c * tc
        xcc = cb_ref[...] + functools.reduce(
            lambda u, v: u + v,
            [cw_ref[j:j + 1, :] * xpad_s[base + SUBLANES - 1 + j:base + SUBLANES - 1 + j + tc, :]
             for j in range(CONV_W)])
        xc_s[base:base + tc, :] = xcc
        gates(xcc, wf_ref, bf_ref, cl[0:1])

        def fwd_tile(t, hc, base=base):
            r0 = pl.multiple_of(t * SUBLANES, SUBLANES)
            ca, cb = _tile_scan(a_s[pl.ds(r0, SUBLANES), :], b_s[pl.ds(r0, SUBLANES), :], False)
            hh = ca * hc + cb
            hf_s[pl.ds(base + r0, SUBLANES), :] = hh
            return hh[SUBLANES - 1:SUBLANES, :]

        h = lax.fori_loop(0, n_tiles, fwd_tile, h)
    fin_ref[0, 0:1, :] = h

    h = h0_ref[0, 1:2, :]
    for c in reversed(range(n_chunks)):
        base = c * tc
        gates(xc_s[base:base + tc, :], wb_ref, bb_ref, cl[1:2])

        def bwd_tile(t, hc, base=base):
            r0 = pl.multiple_of((n_tiles - 1 - t) * SUBLANES, SUBLANES)
            ca, cb = _tile_scan(a_s[pl.ds(r0, SUBLANES), :], b_s[pl.ds(r0, SUBLANES), :], True)
            hh = ca * hc + cb
            gate = jax.nn.gelu(gb_ref[pl.ds(base + r0, SUBLANES), :], approximate=True)
            rec_ref[pl.ds(base + r0, SUBLANES), :] = (
                (hf_s[pl.ds(base + r0, SUBLANES), :] + hh) * gate).astype(rec_ref.dtype)
            return hh[0:1, :]

        h = lax.fori_loop(0, n_tiles, bwd_tile, h)
    fin_ref[0, 1:2, :] = h


def _lru(xr, gb, h0, conv_w, conv_b, wf, wb, bf, bb, lam, seq_len):
    n = xr.shape[0]
    batch = n // seq_len
    const = lambda b: (0, 0)
    seq = pl.BlockSpec((seq_len, LRU_W), lambda b: (b, 0))
    st = pl.BlockSpec((1, 2, LRU_W), lambda b: (b, 0, 0))
    return pl.pallas_call(
        functools.partial(_lru_kernel, seq_len=seq_len),
        grid=(batch,),
        in_specs=[seq, seq, st,
                  pl.BlockSpec((CONV_W, LRU_W), const), pl.BlockSpec((1, LRU_W), const),
                  pl.BlockSpec((LRU_W, 2 * LRU_W), const), pl.BlockSpec((LRU_W, 2 * LRU_W), const),
                  pl.BlockSpec((1, 2 * LRU_W), const), pl.BlockSpec((1, 2 * LRU_W), const),
                  pl.BlockSpec((2, LRU_W), const)],
        out_specs=[seq, st],
        out_shape=[jax.ShapeDtypeStruct((n, LRU_W), BF16),
                   jax.ShapeDtypeStruct((batch, 2, LRU_W), F32)],
        scratch_shapes=[pltpu.VMEM((seq_len + 2 * SUBLANES, LRU_W), F32),
                        pltpu.VMEM((seq_len, LRU_W), F32),
                        pltpu.VMEM((seq_len, LRU_W), F32),
                        pltpu.VMEM((TC_LRU, LRU_W), F32),
                        pltpu.VMEM((TC_LRU, LRU_W), F32)],
        compiler_params=_cparams(("arbitrary",)),
        name="lru_%d" % seq_len,
    )(xr, gb, h0, conv_w, conv_b, wf, wb, bf, bb, lam)


def _post_kernel(attn_ref, rec_ref, x_ref, mod_ref, g2_ref, wo_ref, wr_hi_ref, wr_lo_ref, br_ref,
                 tri_ref, cnt_in_ref, x1_ref, h2_ref, route_ref, cnt_ref, cnt_s):
    @pl.when(pl.program_id(0) == 0)
    def _():
        cnt_s[...] = cnt_in_ref[...]

    m = mod_ref[0]
    u = _dot(attn_ref[...], wo_ref[:ATTN_W, :]) + _dot(rec_ref[...], wo_ref[ATTN_W:, :])
    x1 = x_ref[...] + m[2:3] * u
    x1_ref[...] = x1
    ms = jnp.mean(x1 * x1, axis=-1, keepdims=True)
    h2 = x1 * lax.rsqrt(ms + EPS) * g2_ref[...]
    h2 = h2 * (1.0 + m[4:5]) + m[3:4]
    hi, lo = _split_bf16(h2)
    h2_ref[...] = _to_row_tiles(h2)

    logits = (_dot(hi, wr_hi_ref[...]) + _dot(hi, wr_lo_ref[...]) + _dot(lo, wr_hi_ref[...])
              + br_ref[...])
    lane = lax.broadcasted_iota(jnp.int32, logits.shape, 1)
    neg = -jnp.inf
    big = jnp.int32(1 << 20)
    gmask = (lane >= N_EXPERTS) & (lane < N_EXPERTS + N_GROUPS)
    gl = jnp.where(gmask, logits, neg)
    gmax = jnp.max(gl, axis=-1, keepdims=True)
    gidx = jnp.min(jnp.where(gl == gmax, lane - N_EXPERTS, big), axis=-1, keepdims=True)
    p_sel = 1.0 / jnp.sum(jnp.where(gmask, jnp.exp(gl - gmax), 0.0), axis=-1, keepdims=True)

    emask = (lane < N_EXPERTS) & ((lane // EXPERTS_PER_GROUP) == gidx)
    el = jnp.where(emask, logits, neg)
    v1 = jnp.max(el, axis=-1, keepdims=True)
    i1 = jnp.min(jnp.where(el == v1, lane, big), axis=-1, keepdims=True)
    el2 = jnp.where(lane == i1, neg, el)
    v2 = jnp.max(el2, axis=-1, keepdims=True)
    i2 = jnp.min(jnp.where(el2 == v2, lane, big), axis=-1, keepdims=True)
    e2 = jnp.exp(v2 - v1)
    w1 = p_sel / (1.0 + e2)
    w2 = p_sel * e2 / (1.0 + e2)

    oh1 = lane == i1
    oh2 = lane == i2
    oh = jnp.where(oh1, 1.0, 0.0) + jnp.where(oh2, 1.0, 0.0)
    before = _dot(tri_ref[...], oh.astype(BF16)) + cnt_s[...]
    rank1 = jnp.sum(jnp.where(oh1, before, 0.0), axis=-1, keepdims=True)
    rank2 = jnp.sum(jnp.where(oh2, before, 0.0), axis=-1, keepdims=True)
    cnt = cnt_s[...] + jnp.sum(oh, axis=0, keepdims=True)
    cnt_s[...] = cnt
    cnt_ref[...] = cnt
    fields = (i1.astype(F32), i2.astype(F32), rank1, rank2, w1, w2)
    route = jnp.zeros(logits.shape, F32)
    for k, val in enumerate(fields):
        route = jnp.where(lane == k, val, route)
    route_ref[...] = route


def _post(attn, rec, x, mod3, mod_row, g2, w_out, wr_hi, wr_lo, br, tri, cnt_in, seq_len):
    n = x.shape[0]
    tm = TM_PRE
    tiles_per_seq = seq_len // tm
    const = lambda i: (0, 0)
    row = lambda w: pl.BlockSpec((tm, w), lambda i: (i, 0))
    return pl.pallas_call(
        _post_kernel,
        grid=(n // tm,),
        in_specs=[row(ATTN_W), row(LRU_W), row(D_MODEL),
                  pl.BlockSpec((1, 6, D_MODEL), lambda i: (mod_row(i // tiles_per_seq), 0, 0)),
                  pl.BlockSpec((1, D_MODEL), const),
                  pl.BlockSpec((D_MODEL, D_MODEL), const),
                  pl.BlockSpec((D_MODEL, LANES), const),
                  pl.BlockSpec((D_MODEL, LANES), const),
                  pl.BlockSpec((1, LANES), const),
                  pl.BlockSpec((tm, tm), const),
                  pl.BlockSpec((1, LANES), const)],
        out_specs=[row(D_MODEL), pl.BlockSpec((tm,) + ROW_TILE, lambda i: (i, 0, 0)), row(LANES),
                   pl.BlockSpec((1, LANES), const)],
        out_shape=[jax.ShapeDtypeStruct((n, D_MODEL), F32),
                   jax.ShapeDtypeStruct((n,) + ROW_TILE, F32),
                   jax.ShapeDtypeStruct((n, LANES), F32),
                   jax.ShapeDtypeStruct((1, LANES), F32)],
        scratch_shapes=[pltpu.VMEM((1, LANES), F32)],
        compiler_params=_cparams(("arbitrary",)),
        name="post",
    )(attn, rec, x, mod3, g2, w_out, wr_hi, wr_lo, br, tri, cnt_in)


def _row_copy(src_ref, src_row, dst_ref, dst_row, sem):
    return pltpu.make_async_copy(src_ref.at[pl.ds(src_row, 1)], dst_ref.at[pl.ds(dst_row, 1)], sem)


def _token_slots(idx_ref, seg_ref, n):
    base = 2 * TOP_K * n
    return (seg_ref[idx_ref[base]] + idx_ref[base + 2], seg_ref[idx_ref[base + 1]] + idx_ref[base + 3])


def _dispatch_kernel(idx_ref, seg_ref, zfill_ref, nu_ref, h2p_ref, h2s_ref, xs_ref, zero_s, sem,
                     *, n_prompt):
    i = pl.program_id(0)
    ch = CH_DISPATCH

    @pl.when(i == 0)
    def _():
        zero_s[...] = jnp.zeros_like(zero_s)

        def zero_tile(t):
            return pltpu.make_async_copy(
                zero_s, xs_ref.at[pl.ds(pl.multiple_of(t * TR_MOE, TR_MOE), TR_MOE)], sem)

        for e in range(N_EXPERTS):
            @pl.when(zfill_ref[e] >= 0)
            def _():
                zero_tile(zfill_ref[e]).start()
        n_all = xs_ref.shape[0] // TR_MOE
        lax.fori_loop(nu_ref[0], n_all, lambda t, c: (zero_tile(t).start(), c)[1], 0)
        for e in range(N_EXPERTS):
            @pl.when(zfill_ref[e] >= 0)
            def _():
                zero_tile(0).wait()
        lax.fori_loop(nu_ref[0], n_all, lambda t, c: (zero_tile(0).wait(), c)[1], 0)

    def scatter(src_ref):
        def body(j, carry):
            s1, s2 = _token_slots(idx_ref, seg_ref, i * ch + j)
            _row_copy(src_ref, j, xs_ref, s1, sem).start()
            _row_copy(src_ref, j, xs_ref, s2, sem).start()
            return carry
        lax.fori_loop(0, ch, body, 0, unroll=8)
        for _ in range(TOP_K):
            pltpu.make_async_copy(src_ref, xs_ref.at[pl.ds(0, ch)], sem).wait()

    @pl.when(i < n_prompt // ch)
    def _():
        scatter(h2p_ref)

    @pl.when(i >= n_prompt // ch)
    def _():
        scatter(h2s_ref)


def _dispatch(idx, seg_start, zfill, n_used, h2p, h2s, n_rows):
    n_prompt = h2p.shape[0]
    n = n_prompt + h2s.shape[0]
    ch = CH_DISPATCH
    npc = n_prompt // ch
    last_p = npc - 1
    return pl.pallas_call(
        functools.partial(_dispatch_kernel, n_prompt=n_prompt),
        grid_spec=pltpu.PrefetchScalarGridSpec(
            num_scalar_prefetch=4,
            grid=(n // ch,),
            in_specs=[pl.BlockSpec((ch,) + ROW_TILE, lambda i, *_: (jnp.minimum(i, last_p), 0, 0)),
                      pl.BlockSpec((ch,) + ROW_TILE, lambda i, *_: (jnp.maximum(i - npc, 0), 0, 0))],
            out_specs=pl.BlockSpec(memory_space=pl.ANY),
            scratch_shapes=[pltpu.VMEM((TR_MOE,) + ROW_TILE, F32), pltpu.SemaphoreType.DMA]),
        out_shape=jax.ShapeDtypeStruct((n_rows,) + ROW_TILE, F32),
        compiler_params=_cparams(("arbitrary",)),
        name="dispatch",
    )(idx, seg_start, zfill, n_used, h2p, h2s)


def _ffn_kernel(te_ref, nu_ref, xs_ref, wg_ref, wu_ref, wd_ref, ys_ref, wg_s, wu_s, wd_s):
    t = pl.program_id(0)

    @pl.when(t < nu_ref[0])
    def _():
        prev = te_ref[jnp.maximum(t - 1, 0)]

        @pl.when((t == 0) | (te_ref[t] != prev))
        def _():
            wg_s[...] = wg_ref[0].astype(BF16)
            wu_s[...] = wu_ref[0].astype(BF16)
            wd_s[...] = wd_ref[0].astype(BF16)

        x = _from_row_tiles(xs_ref[...]).astype(BF16)
        hg = _dot(x, wg_s[...])
        hu = _dot(x, wu_s[...])
        act = (hg * _sigmoid(hg)) * hu
        ys_ref[...] = _to_row_tiles(_dot(act.astype(BF16), wd_s[...]))

    @pl.when(t >= nu_ref[0])
    def _():
        ys_ref[...] = jnp.zeros_like(ys_ref)


def _ffn(tile_expert, n_used, xs, wg, wu, wd, n_tiles):
    tr = TR_MOE
    rows = lambda t, te, nu: (jnp.minimum(t, nu[0] - 1), 0, 0)
    wsel = lambda t, te, nu: (te[t], 0, 0)
    return pl.pallas_call(
        _ffn_kernel,
        grid_spec=pltpu.PrefetchScalarGridSpec(
            num_scalar_prefetch=2,
            grid=(n_tiles,),
            in_specs=[pl.BlockSpec((tr,) + ROW_TILE, rows),
                      pl.BlockSpec((1, D_MODEL, EXPERT_FF), wsel),
                      pl.BlockSpec((1, D_MODEL, EXPERT_FF), wsel),
                      pl.BlockSpec((1, EXPERT_FF, D_MODEL), wsel)],
            out_specs=pl.BlockSpec((tr,) + ROW_TILE, lambda t, te, nu: (t, 0, 0)),
            scratch_shapes=[pltpu.VMEM((D_MODEL, EXPERT_FF), BF16),
                            pltpu.VMEM((D_MODEL, EXPERT_FF), BF16),
                            pltpu.VMEM((EXPERT_FF, D_MODEL), BF16)]),
        out_shape=jax.ShapeDtypeStruct((n_tiles * tr,) + ROW_TILE, F32),
        compiler_params=_cparams(("arbitrary",)),
        name="ffn",
    )(tile_expert, n_used, xs, wg, wu, wd)


def _combine_kernel(idx_ref, seg_ref, ys_ref, x1_ref, route_ref, mod_ref, y_ref, b1_s, b2_s, sems):
    i = pl.program_id(0)
    tm = x1_ref.shape[0]

    def gather(step, slot):
        def body(j, carry):
            s1, s2 = _token_slots(idx_ref, seg_ref, step * tm + j)
            _row_copy(ys_ref, s1, b1_s.at[slot], j, sems.at[slot]).start()
            _row_copy(ys_ref, s2, b2_s.at[slot], j, sems.at[slot]).start()
            return carry
        lax.fori_loop(0, tm, body, 0, unroll=8)

    @pl.when(i == 0)
    def _():
        gather(0, 0)

    @pl.when(i + 1 < pl.num_programs(0))
    def _():
        gather(i + 1, (i + 1) % 2)

    slot = i % 2
    for buf in (b1_s, b2_s):
        pltpu.make_async_copy(ys_ref.at[pl.ds(0, tm)], buf.at[slot], sems.at[slot]).wait()
    route = route_ref[...]
    moe = route[:, 4:5] * _from_row_tiles(b1_s[slot]) + route[:, 5:6] * _from_row_tiles(b2_s[slot])
    y_ref[...] = x1_ref[...] + mod_ref[0][5:6] * moe


def _combine(idx, seg_start, ys, x1, route, mod3, mod_row, seq_len):
    n = x1.shape[0]
    tm = TM_PRE
    tiles_per_seq = seq_len // tm
    row = lambda w: pl.BlockSpec((tm, w), lambda i, *_: (i, 0))
    return pl.pallas_call(
        _combine_kernel,
        grid_spec=pltpu.PrefetchScalarGridSpec(
            num_scalar_prefetch=2,
            grid=(n // tm,),
            in_specs=[pl.BlockSpec(memory_space=pl.ANY), row(D_MODEL), row(LANES),
                      pl.BlockSpec((1, 6, D_MODEL),
                                   lambda i, *_: (mod_row(i // tiles_per_seq), 0, 0))],
            out_specs=row(D_MODEL),
            scratch_shapes=[pltpu.VMEM((2, tm) + ROW_TILE, F32), pltpu.VMEM((2, tm) + ROW_TILE, F32),
                            pltpu.SemaphoreType.DMA((2,))]),
        out_shape=jax.ShapeDtypeStruct((n, D_MODEL), F32),
        compiler_params=_cparams(("arbitrary",)),
        name="combine",
    )(idx, seg_start, ys, x1, route, mod3)


def _rope_tables(length):
    rows = length // GRID_W
    r, col = jnp.meshgrid(jnp.arange(rows), jnp.arange(GRID_W), indexing='ij')
    r = r.reshape(-1).astype(F32)
    col = col.reshape(-1).astype(F32)
    half = HEAD_DIM // 2
    inv = ROPE_THETA ** (-jnp.arange(0, half, 2, dtype=F32) / half)
    ang_r = r[:, None] * inv
    ang_c = col[:, None] * inv
    ang = jnp.concatenate([ang_r, ang_r, ang_c, ang_c], axis=-1)
    sign = jnp.where((jnp.arange(HEAD_DIM) // (HEAD_DIM // 4)) % 2 == 0, -1.0, 1.0).astype(F32)
    cos = jnp.tile(jnp.cos(ang), (1, LANES // HEAD_DIM))
    sin = jnp.tile(jnp.sin(ang) * sign, (1, LANES // HEAD_DIM))
    return cos, sin


def _block_diag(w):
    eye = jnp.eye(LRU_BLOCKS, dtype=w.dtype)
    return jnp.einsum('hij,hg->higj', w, eye).reshape(LRU_W, LRU_W)


def _expand_heads(kv, one):
    b, t, _ = kv.shape
    h0 = kv[..., :HEAD_DIM]
    h1 = kv[..., HEAD_DIM:]
    lane = jnp.arange(HEAD_DIM)
    fill_hi = jnp.broadcast_to(jnp.where(lane == ONES_LANE_EVEN - HEAD_DIM, one, 0.0).astype(F32), h0.shape)
    fill_lo = jnp.broadcast_to(jnp.where(lane == ONES_LANE_ODD, one, 0.0).astype(F32), h0.shape)
    out = jnp.concatenate([h0, fill_hi, fill_lo, h0, h1, fill_hi, fill_lo, h1], axis=-1)
    return out.reshape(b * t, 4 * LANES).astype(BF16)


def kernel(x_prompt, x_sample, cache_k, cache_v, state_lru, c, c_ctx, w_mod, b_mod, norm1, norm2,
           w_in, q_norm, k_norm, conv_w, conv_b, lru_wa, lru_ba, lru_wx, lru_bx, lru_lambda, w_out,
           router_grp_w, router_grp_b, router_exp_w, router_exp_b, exp_w_gate, exp_w_up, exp_w_down):
    batch, seq, _ = x_prompt.shape
    dec_batch, dec_seq, _ = x_sample.shape
    past = cache_k.shape[2]
    depth = w_mod.shape[0]
    assert depth == 1

    cvec = jnp.concatenate(
        [c_ctx[None, :], c, jnp.zeros((MOD_ROWS - 1 - dec_batch, D_MODEL), F32)], axis=0)
    mod3 = _modulation(cvec, w_mod[0], b_mod[0][None, :]).reshape(MOD_ROWS, 6, D_MODEL)

    w_in_b = w_in[0].astype(BF16)
    w_out_b = w_out[0].astype(BF16)
    head_id = jnp.arange(QK_W) // HEAD_DIM
    ones_qk = (head_id[:, None] == head_id[None, :]).astype(BF16)
    gqk = jnp.concatenate([jnp.tile(q_norm[0], N_HEADS), jnp.tile(k_norm[0], N_KV_HEADS)])[None, :]
    wf = jnp.concatenate([_block_diag(lru_wa[0, 0]), _block_diag(lru_wx[0, 0])], axis=1).astype(BF16)
    wb = jnp.concatenate([_block_diag(lru_wa[0, 1]), _block_diag(lru_wx[0, 1])], axis=1).astype(BF16)
    bf = jnp.concatenate([lru_ba[0, 0], lru_bx[0, 0]])[None, :]
    bb = jnp.concatenate([lru_ba[0, 1], lru_bx[0, 1]])[None, :]
    pad = LANES - N_EXPERTS - N_GROUPS
    wr = jnp.concatenate([router_exp_w[0], router_grp_w[0], jnp.zeros((D_MODEL, pad), F32)], axis=1)
    wr_hi = wr.astype(BF16)
    wr_lo = (wr - wr_hi.astype(F32)).astype(BF16)
    br = jnp.concatenate([router_exp_b[0], router_grp_b[0], jnp.zeros((pad,), F32)])[None, :]
    g1 = norm1[0][None, :]
    g2 = norm2[0][None, :]
    cw = conv_w[0]
    cb = conv_b[0][None, :]
    lam = lru_lambda[0]
    tri = (jnp.arange(TM_PRE)[:, None] > jnp.arange(TM_PRE)[None, :]).astype(BF16)

    def mixers(x, seq_len, mod_row, tables, extra_k, extra_v, h0, cnt_in):
        q, kx, vx, kf, vf, xr, gb = _pre(x, mod3, mod_row, g1, w_in_b, ones_qk, gqk, tables, seq_len)
        k_segs = [(kx, seq_len)] + extra_k
        v_segs = [(vx, seq_len)] + extra_v
        attn = _attention(q, k_segs, v_segs, seq_len)
        rec, fin = _lru(xr, gb, h0, cw, cb, wf, wb, bf, bb, lam, seq_len)
        x1, h2, route, cnt = _post(attn, rec, x, mod3, mod_row, g2, w_out_b, wr_hi, wr_lo, br,
                                   tri, cnt_in, seq_len)
        return x1, h2, route, cnt, kf, vf, fin

    mod_row_p = lambda b: 0
    mod_row_s = lambda b: b + 1
    xp = x_prompt.reshape(batch * seq, D_MODEL)
    x1p, h2p, route_p, cnt_p, kf, vf, fin = mixers(
        xp, seq, mod_row_p, None, [], [], jnp.zeros((batch, 2, LRU_W), F32),
        jnp.zeros((1, LANES), F32))
    xs = x_sample.reshape(dec_batch * dec_seq, D_MODEL)
    ck = _expand_heads(cache_k[:, 0].reshape(dec_batch, past, KV_W), 0.0)
    cv = _expand_heads(cache_v[:, 0].reshape(dec_batch, past, KV_W), 1.0)
    x1s, h2s, route_s, cnt_all, _, _, _ = mixers(
        xs, dec_seq, mod_row_s, _rope_tables(dec_seq), [(ck, past)], [(cv, past)],
        state_lru[:, 0], cnt_p)

    n_prompt = batch * seq
    n_tok = n_prompt + dec_batch * dec_seq
    n_tiles = (TOP_K * n_tok + N_EXPERTS * (TR_MOE - 1)) // TR_MOE
    cnt = cnt_all[0, :N_EXPERTS].astype(jnp.int32)
    ntile = (cnt + TR_MOE - 1) // TR_MOE
    tile_end = jnp.cumsum(ntile)
    seg_start = (tile_end - ntile) * TR_MOE
    n_used = tile_end[-1:]
    t_ids = jnp.minimum(jnp.arange(n_tiles, dtype=jnp.int32), n_used[0] - 1)
    tile_expert = jnp.sum((t_ids[:, None] >= tile_end[None, :]).astype(jnp.int32), axis=1)
    zfill = jnp.where(ntile > 0, tile_end - 1, -1)

    idx_p = route_p[:, :2 * TOP_K].astype(jnp.int32).reshape(-1)
    idx_s = route_s[:, :2 * TOP_K].astype(jnp.int32).reshape(-1)
    xsort = _dispatch(jnp.concatenate([idx_p, idx_s]), seg_start, zfill, n_used, h2p, h2s,
                      n_tiles * TR_MOE)
    ysort = _ffn(tile_expert, n_used, xsort, exp_w_gate[0], exp_w_up[0], exp_w_down[0], n_tiles)
    yp = _combine(idx_p, seg_start, ysort, x1p, route_p, mod3, mod_row_p, seq)
    ys = _combine(idx_s, seg_start, ysort, x1s, route_s, mod3, mod_row_s, dec_seq)

    return (yp.reshape(batch, seq, D_MODEL),
            ys.reshape(dec_batch, dec_seq, D_MODEL),
            kf.reshape(batch, 1, seq, N_KV_HEADS, HEAD_DIM),
            vf.reshape(batch, 1, seq, N_KV_HEADS, HEAD_DIM),
            fin.reshape(batch, 1, 2, LRU_W))
```

```python
import functools

import jax
import jax.numpy as jnp
from jax import lax
from jax.experimental import pallas as pl
from jax.experimental.pallas import tpu as pltpu

F32 = jnp.float32
BF16 = jnp.bfloat16

D_MODEL = 1024
GRID_W = 64
ATTN_W = 512
LRU_W = 512
HEAD_DIM = 64
N_HEADS = 8
N_KV_HEADS = 2
KV_W = N_KV_HEADS * HEAD_DIM
LRU_BLOCKS = 8
LRU_BLOCK_W = LRU_W // LRU_BLOCKS
CONV_W = 4
LRU_C = 8.0
IN_W = ATTN_W + 2 * KV_W + 2 * LRU_W
QK_W = ATTN_W + KV_W
N_GROUPS = 4
EXPERTS_PER_GROUP = 8
N_EXPERTS = N_GROUPS * EXPERTS_PER_GROUP
TOP_K = 2
EXPERT_FF = D_MODEL // 4
ROPE_THETA = 10000.0
EPS = 1e-6

LANES = 128
SUBLANES = 8
MOD_ROWS = 8
VMEM_LIMIT = 48 * 1024 * 1024

TM_PRE = 512
TQ_ATT = 512
TC_LRU = 256
TR_MOE = 256
CH_DISPATCH = 1024


def _cparams(sem):
    return pltpu.CompilerParams(dimension_semantics=sem, vmem_limit_bytes=VMEM_LIMIT)


def _dot(a, b):
    return jnp.dot(a, b, preferred_element_type=F32)


def _dot_nt(a, b):
    return lax.dot_general(a, b, (((1,), (1,)), ((), ())), preferred_element_type=F32)


ROW_TILE = (D_MODEL // LANES, LANES)
Q_SCALE = HEAD_DIM ** -0.5 * 1.4426950408889634
ONES_LANE_EVEN = HEAD_DIM
ONES_LANE_ODD = 0


def _to_row_tiles(x):
    cols = jnp.stack([x[:, c * LANES:(c + 1) * LANES] for c in range(D_MODEL // LANES)], axis=0)
    return jnp.swapaxes(cols, 0, 1)


def _from_row_tiles(x3):
    cols = jnp.swapaxes(x3, 0, 1)
    return jnp.concatenate([cols[c] for c in range(D_MODEL // LANES)], axis=1)


def _sigmoid(x):
    return 0.5 * jnp.tanh(0.5 * x) + 0.5


def _split_bf16(x):
    hi = x.astype(BF16)
    lo = (x - hi.astype(F32)).astype(BF16)
    return hi, lo


def _mod_kernel(c_ref, w_ref, b_ref, o_ref):
    c = c_ref[...]
    s = (c * jax.nn.sigmoid(c)).astype(BF16)
    o_ref[...] = _dot(s, w_ref[...].astype(BF16)) + b_ref[...]


def _modulation(cvec, w_mod, b_mod):
    n_out = w_mod.shape[1]
    tn = n_out // 4
    return pl.pallas_call(
        _mod_kernel,
        grid=(n_out // tn,),
        in_specs=[pl.BlockSpec((MOD_ROWS, D_MODEL), lambda j: (0, 0)),
                  pl.BlockSpec((D_MODEL, tn), lambda j: (0, j)),
                  pl.BlockSpec((1, tn), lambda j: (0, j))],
        out_specs=pl.BlockSpec((MOD_ROWS, tn), lambda j: (0, j)),
        out_shape=jax.ShapeDtypeStruct((MOD_ROWS, n_out), F32),
        compiler_params=_cparams(("arbitrary",)),
        name="modulation",
    )(cvec, w_mod, b_mod)


def _pre_kernel(*refs, rope):
    if rope:
        (x_ref, mod_ref, g1_ref, win_ref, ones_ref, gqk_ref, cos_ref, sin_ref,
         q_ref, kx_ref, vx_ref, kf_ref, vf_ref, xr_ref, gb_ref) = refs
    else:
        (x_ref, mod_ref, g1_ref, win_ref, ones_ref, gqk_ref,
         q_ref, kx_ref, vx_ref, kf_ref, vf_ref, xr_ref, gb_ref) = refs
    x = x_ref[...]
    m = mod_ref[0]
    ms = jnp.mean(x * x, axis=-1, keepdims=True)
    y = x * lax.rsqrt(ms + EPS) * g1_ref[...]
    h = y * (1.0 + m[1:2]) + m[0:1]
    z = _dot(h.astype(BF16), win_ref[...])

    qk = z[:, :QK_W]
    hi, lo = _split_bf16(qk * qk)
    ss = _dot(hi, ones_ref[...]) + _dot(lo, ones_ref[...])
    qk = qk * lax.rsqrt(ss * (1.0 / HEAD_DIM) + EPS) * gqk_ref[...]

    lane = lax.broadcasted_iota(jnp.int32, (x.shape[0], LANES), 1)
    cols = []
    for c in range(QK_W // LANES):
        xc = qk[:, c * LANES:(c + 1) * LANES]
        if rope:
            left = pltpu.roll(xc, LANES - HEAD_DIM // 4, 1)
            right = pltpu.roll(xc, HEAD_DIM // 4, 1)
            rot = jnp.where((lane // (HEAD_DIM // 4)) % 2 == 0, left, right)
            xc = xc * cos_ref[...] + rot * sin_ref[...]
        cols.append(xc)
    for c in range(ATTN_W // LANES):
        q_ref[:, c * LANES:(c + 1) * LANES] = (cols[c] * Q_SCALE).astype(BF16)

    lo_half = lane < HEAD_DIM
    for col, fref, xref, one in ((cols[ATTN_W // LANES], kf_ref, kx_ref, 0.0),
                                 (z[:, QK_W:QK_W + KV_W], vf_ref, vx_ref, 1.0)):
        fref[...] = col
        swapped = pltpu.roll(col, HEAD_DIM, 1)
        fill_hi = jnp.where(lane == ONES_LANE_EVEN, one, 0.0)
        fill_lo = jnp.where(lane == ONES_LANE_ODD, one, 0.0)
        xref[:, 0 * LANES:1 * LANES] = jnp.where(lo_half, col, fill_hi).astype(BF16)
        xref[:, 1 * LANES:2 * LANES] = jnp.where(lo_half, fill_lo, swapped).astype(BF16)
        xref[:, 2 * LANES:3 * LANES] = jnp.where(lo_half, swapped, fill_hi).astype(BF16)
        xref[:, 3 * LANES:4 * LANES] = jnp.where(lo_half, fill_lo, col).astype(BF16)

    xr_ref[...] = z[:, QK_W + KV_W:QK_W + KV_W + LRU_W]
    gb_ref[...] = z[:, QK_W + KV_W + LRU_W:]


def _pre(x, mod3, mod_row, g1, w_in, ones_qk, gqk, tables, seq_len):
    n = x.shape[0]
    tm = TM_PRE
    tiles_per_seq = seq_len // tm
    rope = tables is not None
    const = lambda i: (0, 0)
    in_specs = [pl.BlockSpec((tm, D_MODEL), lambda i: (i, 0)),
                pl.BlockSpec((1, 6, D_MODEL), lambda i: (mod_row(i * tm), 0, 0)),
                pl.BlockSpec((1, D_MODEL), const),
                pl.BlockSpec((D_MODEL, IN_W), const),
                pl.BlockSpec((QK_W, QK_W), const),
                pl.BlockSpec((1, QK_W), const)]
    args = [x, mod3, g1, w_in, ones_qk, gqk]
    if rope:
        in_specs += [pl.BlockSpec((tm, LANES), lambda i: (i % tiles_per_seq, 0))] * 2
        args += list(tables)
    row = lambda w: pl.BlockSpec((tm, w), lambda i: (i, 0))
    out_shape = [jax.ShapeDtypeStruct((n, ATTN_W), BF16),
                 jax.ShapeDtypeStruct((n, 4 * LANES), BF16),
                 jax.ShapeDtypeStruct((n, 4 * LANES), BF16),
                 jax.ShapeDtypeStruct((n, KV_W), F32),
                 jax.ShapeDtypeStruct((n, KV_W), F32),
                 jax.ShapeDtypeStruct((n, LRU_W), F32),
                 jax.ShapeDtypeStruct((n, LRU_W), F32)]
    out_specs = [row(ATTN_W), row(4 * LANES), row(4 * LANES), row(KV_W), row(KV_W),
                 row(LRU_W), row(LRU_W)]
    return pl.pallas_call(
        functools.partial(_pre_kernel, rope=rope),
        grid=(n // tm,),
        in_specs=in_specs, out_specs=out_specs, out_shape=out_shape,
        compiler_params=_cparams(("arbitrary",)),
        name="pre_rope" if rope else "pre",
    )(*args)


def _attn_kernel(*refs, n_seg):
    q_ref = refs[0]
    k_refs = refs[1:1 + n_seg]
    v_refs = refs[1 + n_seg:1 + 2 * n_seg]
    o_ref = refs[1 + 2 * n_seg]
    lane = lax.broadcasted_iota(jnp.int32, (q_ref.shape[0], LANES), 1)
    for c in range(ATTN_W // LANES):
        qc = q_ref[:, c * LANES:(c + 1) * LANES]
        g = c // 2
        accs = []
        for par in range(2):
            sl = slice((2 * g + par) * LANES, (2 * g + par + 1) * LANES)
            ss = [_dot_nt(qc, k[:, sl]) for k in k_refs]
            mx = functools.reduce(jnp.maximum, [jnp.max(s, axis=-1, keepdims=True) for s in ss])
            ps = [jnp.exp2((s - mx).astype(BF16)) for s in ss]
            accs.append(functools.reduce(lambda a, b: a + b,
                                         [_dot(p, v[:, sl]) for p, v in zip(ps, v_refs)]))
        even = accs[0] / accs[0][:, ONES_LANE_EVEN:ONES_LANE_EVEN + 1]
        odd = accs[1] / accs[1][:, ONES_LANE_ODD:ONES_LANE_ODD + 1]
        o_ref[:, c * LANES:(c + 1) * LANES] = jnp.where(lane < HEAD_DIM, even, odd).astype(BF16)


def _attention(q, k_segs, v_segs, seq_len):
    n = q.shape[0]
    tq = min(TQ_ATT, seq_len)
    nq = seq_len // tq
    n_seg = len(k_segs)
    in_specs = [pl.BlockSpec((tq, ATTN_W), lambda b, i: (b * nq + i, 0))]
    for arr, t in list(k_segs) + list(v_segs):
        in_specs.append(pl.BlockSpec((t, 4 * LANES), lambda b, i: (b, 0)))
    return pl.pallas_call(
        functools.partial(_attn_kernel, n_seg=n_seg),
        grid=(n // seq_len, nq),
        in_specs=in_specs,
        out_specs=pl.BlockSpec((tq, ATTN_W), lambda b, i: (b * nq + i, 0)),
        out_shape=jax.ShapeDtypeStruct((n, ATTN_W), BF16),
        compiler_params=_cparams(("arbitrary", "arbitrary")),
        name="attention_%dseg" % n_seg,
    )(q, *[a for a, _ in k_segs], *[a for a, _ in v_segs])


def _log_sigmoid(x):
    return jnp.minimum(x, 0.0) - jnp.log1p(jnp.exp(-jnp.abs(x)))


def _tile_scan(a, b, reverse):
    row = lax.broadcasted_iota(jnp.int32, a.shape, 0)
    d = 1
    while d < SUBLANES:
        if reverse:
            keep = row < SUBLANES - d
            shift = SUBLANES - d
        else:
            keep = row >= d
            shift = d
        a_sh = jnp.where(keep, pltpu.roll(a, shift, 0), 1.0)
        b_sh = jnp.where(keep, pltpu.roll(b, shift, 0), 0.0)
        b = a * b_sh + b
        a = a * a_sh
        d *= 2
    return a, b


def _lru_kernel(xr_ref, gb_ref, h0_ref, cw_ref, cb_ref, wf_ref, wb_ref, bf_ref, bb_ref, lam_ref,
                rec_ref, fin_ref, xpad_s, xc_s, hf_s, a_s, b_s, *, seq_len):
    tc = TC_LRU
    n_chunks = seq_len // tc
    n_tiles = tc // SUBLANES
    zpad = jnp.zeros((SUBLANES, LRU_W), F32)
    xpad_s[0:SUBLANES, :] = zpad
    xpad_s[SUBLANES:SUBLANES + seq_len, :] = xr_ref[...]
    xpad_s[SUBLANES + seq_len:2 * SUBLANES + seq_len, :] = zpad

    cl = LRU_C * _log_sigmoid(lam_ref[...])

    def gates(xcc, w_ref, bias_ref, cl_d):
        g = _dot(xcc.astype(BF16), w_ref[...]) + bias_ref[...]
        r = _sigmoid(g[:, :LRU_W])
        i = _sigmoid(g[:, LRU_W:])
        log_a = r * cl_d
        a = jnp.exp(log_a)
        a_s[...] = a
        b_s[...] = jnp.sqrt(1.0 - a * a) * i * xcc

    h = h0_ref[0, 0:1, :]
    for c in range(n_chunks):
        base = c * tc
        xcc = cb_ref[...] + functools.reduce(
            lambda u, v: u + v,
            [cw_ref[j:j + 1, :] * xpad_s[base + SUBLANES - 1 + j:base + SUBLANES - 1 + j + tc, :]
             for j in range(CONV_W)])
        xc_s[base:base + tc, :] = xcc
        gates(xcc, wf_ref, bf_ref, cl[0:1])

        def fwd_tile(t, hc, base=base):
            r0 = pl.multiple_of(t * SUBLANES, SUBLANES)
            ca, cb = _tile_scan(a_s[pl.ds(r0, SUBLANES), :], b_s[pl.ds(r0, SUBLANES), :], False)
            hh = ca * hc + cb
            hf_s[pl.ds(base + r0, SUBLANES), :] = hh
            return hh[SUBLANES - 1:SUBLANES, :]

        h = lax.fori_loop(0, n_tiles, fwd_tile, h)
    fin_ref[0, 0:1, :] = h

    h = h0_ref[0, 1:2, :]
    for c in reversed(range(n_chunks)):
        base = c * tc
        gates(xc_s[base:base + tc, :], wb_ref, bb_ref, cl[1:2])

        def bwd_tile(t, hc, base=base):
            r0 = pl.multiple_of((n_tiles - 1 - t) * SUBLANES, SUBLANES)
            ca, cb = _tile_scan(a_s[pl.ds(r0, SUBLANES), :], b_s[pl.ds(r0, SUBLANES), :], True)
            hh = ca * hc + cb
            gate = jax.nn.gelu(gb_ref[pl.ds(base + r0, SUBLANES), :], approximate=True)
            rec_ref[pl.ds(base + r0, SUBLANES), :] = (
                (hf_s[pl.ds(base + r0, SUBLANES), :] + hh) * gate).astype(rec_ref.dtype)
            return hh[0:1, :]

        h = lax.fori_loop(0, n_tiles, bwd_tile, h)
    fin_ref[0, 1:2, :] = h


def _lru(xr, gb, h0, conv_w, conv_b, wf, wb, bf, bb, lam, seq_len):
    n = xr.shape[0]
    batch = n // seq_len
    const = lambda b: (0, 0)
    seq = pl.BlockSpec((seq_len, LRU_W), lambda b: (b, 0))
    st = pl.BlockSpec((1, 2, LRU_W), lambda b: (b, 0, 0))
    return pl.pallas_call(
        functools.partial(_lru_kernel, seq_len=seq_len),
        grid=(batch,),
        in_specs=[seq, seq, st,
                  pl.BlockSpec((CONV_W, LRU_W), const), pl.BlockSpec((1, LRU_W), const),
                  pl.BlockSpec((LRU_W, 2 * LRU_W), const), pl.BlockSpec((LRU_W, 2 * LRU_W), const),
                  pl.BlockSpec((1, 2 * LRU_W), const), pl.BlockSpec((1, 2 * LRU_W), const),
                  pl.BlockSpec((2, LRU_W), const)],
        out_specs=[seq, st],
        out_shape=[jax.ShapeDtypeStruct((n, LRU_W), BF16),
                   jax.ShapeDtypeStruct((batch, 2, LRU_W), F32)],
        scratch_shapes=[pltpu.VMEM((seq_len + 2 * SUBLANES, LRU_W), F32),
                        pltpu.VMEM((seq_len, LRU_W), F32),
                        pltpu.VMEM((seq_len, LRU_W), F32),
                        pltpu.VMEM((TC_LRU, LRU_W), F32),
                        pltpu.VMEM((TC_LRU, LRU_W), F32)],
        compiler_params=_cparams(("arbitrary",)),
        name="lru_%d" % seq_len,
    )(xr, gb, h0, conv_w, conv_b, wf, wb, bf, bb, lam)


def _post_kernel(attn_ref, rec_ref, x_ref, mod_ref, g2_ref, wo_ref, wr_hi_ref, wr_lo_ref, br_ref,
                 tri_ref, cnt_in_ref, x1_ref, h2_ref, route_ref, cnt_ref, cnt_s):
    @pl.when(pl.program_id(0) == 0)
    def _():
        cnt_s[...] = cnt_in_ref[...]

    m = mod_ref[0]
    u = _dot(attn_ref[...], wo_ref[:ATTN_W, :]) + _dot(rec_ref[...], wo_ref[ATTN_W:, :])
    x1 = x_ref[...] + m[2:3] * u
    x1_ref[...] = x1
    ms = jnp.mean(x1 * x1, axis=-1, keepdims=True)
    h2 = x1 * lax.rsqrt(ms + EPS) * g2_ref[...]
    h2 = h2 * (1.0 + m[4:5]) + m[3:4]
    hi, lo = _split_bf16(h2)
    h2_ref[...] = _to_row_tiles(h2)

    logits = (_dot(hi, wr_hi_ref[...]) + _dot(hi, wr_lo_ref[...]) + _dot(lo, wr_hi_ref[...])
              + br_ref[...])
    lane = lax.broadcasted_iota(jnp.int32, logits.shape, 1)
    neg = -jnp.inf
    big = jnp.int32(1 << 20)
    gmask = (lane >= N_EXPERTS) & (lane < N_EXPERTS + N_GROUPS)
    gl = jnp.where(gmask, logits, neg)
    gmax = jnp.max(gl, axis=-1, keepdims=True)
    gidx = jnp.min(jnp.where(gl == gmax, lane - N_EXPERTS, big), axis=-1, keepdims=True)
    p_sel = 1.0 / jnp.sum(jnp.where(gmask, jnp.exp(gl - gmax), 0.0), axis=-1, keepdims=True)

    emask = (lane < N_EXPERTS) & ((lane // EXPERTS_PER_GROUP) == gidx)
    el = jnp.where(emask, logits, neg)
    v1 = jnp.max(el, axis=-1, keepdims=True)
    i1 = jnp.min(jnp.where(el == v1, lane, big), axis=-1, keepdims=True)
    el2 = jnp.where(lane == i1, neg, el)
    v2 = jnp.max(el2, axis=-1, keepdims=True)
    i2 = jnp.min(jnp.where(el2 == v2, lane, big), axis=-1, keepdims=True)
    e2 = jnp.exp(v2 - v1)
    w1 = p_sel / (1.0 + e2)
    w2 = p_sel * e2 / (1.0 + e2)

    oh1 = lane == i1
    oh2 = lane == i2
    oh = jnp.where(oh1, 1.0, 0.0) + jnp.where(oh2, 1.0, 0.0)
    before = _dot(tri_ref[...], oh.astype(BF16)) + cnt_s[...]
    rank1 = jnp.sum(jnp.where(oh1, before, 0.0), axis=-1, keepdims=True)
    rank2 = jnp.sum(jnp.where(oh2, before, 0.0), axis=-1, keepdims=True)
    cnt = cnt_s[...] + jnp.sum(oh, axis=0, keepdims=True)
    cnt_s[...] = cnt
    cnt_ref[...] = cnt
    fields = (i1.astype(F32), i2.astype(F32), rank1, rank2, w1, w2)
    route = jnp.zeros(logits.shape, F32)
    for k, val in enumerate(fields):
        route = jnp.where(lane == k, val, route)
    route_ref[...] = route


def _post(attn, rec, x, mod3, mod_row, g2, w_out, wr_hi, wr_lo, br, tri, cnt_in, seq_len):
    n = x.shape[0]
    tm = TM_PRE
    tiles_per_seq = seq_len // tm
    const = lambda i: (0, 0)
    row = lambda w: pl.BlockSpec((tm, w), lambda i: (i, 0))
    return pl.pallas_call(
        _post_kernel,
        grid=(n // tm,),
        in_specs=[row(ATTN_W), row(LRU_W), row(D_MODEL),
                  pl.BlockSpec((1, 6, D_MODEL), lambda i: (mod_row(i * tm), 0, 0)),
                  pl.BlockSpec((1, D_MODEL), const),
                  pl.BlockSpec((D_MODEL, D_MODEL), const),
                  pl.BlockSpec((D_MODEL, LANES), const),
                  pl.BlockSpec((D_MODEL, LANES), const),
                  pl.BlockSpec((1, LANES), const),
                  pl.BlockSpec((tm, tm), const),
                  pl.BlockSpec((1, LANES), const)],
        out_specs=[row(D_MODEL), pl.BlockSpec((tm,) + ROW_TILE, lambda i: (i, 0, 0)), row(LANES),
                   pl.BlockSpec((1, LANES), const)],
        out_shape=[jax.ShapeDtypeStruct((n, D_MODEL), F32),
                   jax.ShapeDtypeStruct((n,) + ROW_TILE, F32),
                   jax.ShapeDtypeStruct((n, LANES), F32),
                   jax.ShapeDtypeStruct((1, LANES), F32)],
        scratch_shapes=[pltpu.VMEM((1, LANES), F32)],
        compiler_params=_cparams(("arbitrary",)),
        name="post",
    )(attn, rec, x, mod3, g2, w_out, wr_hi, wr_lo, br, tri, cnt_in)


def _row_copy(src_ref, src_row, dst_ref, dst_row, sem):
    return pltpu.make_async_copy(src_ref.at[pl.ds(src_row, 1)], dst_ref.at[pl.ds(dst_row, 1)], sem)


def _token_slots(idx_ref, seg_ref, n):
    base = 2 * TOP_K * n
    return (seg_ref[idx_ref[base]] + idx_ref[base + 2], seg_ref[idx_ref[base + 1]] + idx_ref[base + 3])


def _dispatch_kernel(idx_ref, seg_ref, zfill_ref, nu_ref, h2p_ref, h2s_ref, xs_ref, zero_s, sem,
                     *, n_prompt):
    i = pl.program_id(0)
    ch = CH_DISPATCH

    @pl.when(i == 0)
    def _():
        zero_s[...] = jnp.zeros_like(zero_s)

        def zero_tile(t):
            return pltpu.make_async_copy(
                zero_s, xs_ref.at[pl.ds(pl.multiple_of(t * TR_MOE, TR_MOE), TR_MOE)], sem)

        for e in range(N_EXPERTS):
            @pl.when(zfill_ref[e] >= 0)
            def _():
                zero_tile(zfill_ref[e]).start()
        n_all = xs_ref.shape[0] // TR_MOE
        lax.fori_loop(nu_ref[0], n_all, lambda t, c: (zero_tile(t).start(), c)[1], 0)
        for e in range(N_EXPERTS):
            @pl.when(zfill_ref[e] >= 0)
            def _():
                zero_tile(0).wait()
        lax.fori_loop(nu_ref[0], n_all, lambda t, c: (zero_tile(0).wait(), c)[1], 0)

    def scatter(src_ref):
        def body(j, carry):
            s1, s2 = _token_slots(idx_ref, seg_ref, i * ch + j)
            _row_copy(src_ref, j, xs_ref, s1, sem).start()
            _row_copy(src_ref, j, xs_ref, s2, sem).start()
            return carry
        lax.fori_loop(0, ch, body, 0, unroll=8)
        for _ in range(TOP_K):
            pltpu.make_async_copy(src_ref, xs_ref.at[pl.ds(0, ch)], sem).wait()

    @pl.when(i < n_prompt // ch)
    def _():
        scatter(h2p_ref)

    @pl.when(i >= n_prompt // ch)
    def _():
        scatter(h2s_ref)


def _dispatch(idx, seg_start, zfill, n_used, h2p, h2s, n_rows):
    n_prompt = h2p.shape[0]
    n = n_prompt + h2s.shape[0]
    ch = CH_DISPATCH
    npc = n_prompt // ch
    last_p = npc - 1
    return pl.pallas_call(
        functools.partial(_dispatch_kernel, n_prompt=n_prompt),
        grid_spec=pltpu.PrefetchScalarGridSpec(
            num_scalar_prefetch=4,
            grid=(n // ch,),
            in_specs=[pl.BlockSpec((ch,) + ROW_TILE, lambda i, *_: (jnp.minimum(i, last_p), 0, 0)),
                      pl.BlockSpec((ch,) + ROW_TILE, lambda i, *_: (jnp.maximum(i - npc, 0), 0, 0))],
            out_specs=pl.BlockSpec(memory_space=pl.ANY),
            scratch_shapes=[pltpu.VMEM((TR_MOE,) + ROW_TILE, F32), pltpu.SemaphoreType.DMA]),
        out_shape=jax.ShapeDtypeStruct((n_rows,) + ROW_TILE, F32),
        compiler_params=_cparams(("arbitrary",)),
        name="dispatch",
    )(idx, seg_start, zfill, n_used, h2p, h2s)


def _ffn_kernel(tend_ref, xs_ref, wg_ref, wu_ref, wd_ref, ys_ref,
                xbuf, ybuf, wg_s, wu_s, wd_s, sem_in, sem_out):
    e = pl.program_id(0)
    n_used = tend_ref[N_EXPERTS - 1]
    t_first = jnp.where(e == 0, 0, tend_ref[jnp.maximum(e - 1, 0)])
    t_last = tend_ref[e]

    def tile_rows(t):
        return pl.ds(pl.multiple_of(t * TR_MOE, TR_MOE), TR_MOE)

    def fetch(t):
        return pltpu.make_async_copy(xs_ref.at[tile_rows(t)], xbuf.at[t % 2], sem_in.at[t % 2])

    def writeback(t):
        return pltpu.make_async_copy(ybuf.at[t % 2], ys_ref.at[tile_rows(t)], sem_out.at[t % 2])

    @pl.when(e == 0)
    def _():
        fetch(0).start()

    @pl.when(t_last > t_first)
    def _():
        wg_s[...] = wg_ref[0].astype(BF16)
        wu_s[...] = wu_ref[0].astype(BF16)
        wd_s[...] = wd_ref[0].astype(BF16)

    def tile(t, carry):
        fetch(t).wait()

        @pl.when(t + 1 < n_used)
        def _():
            fetch(t + 1).start()

        @pl.when(t >= 2)
        def _():
            writeback(t - 2).wait()

        x = _from_row_tiles(xbuf[t % 2]).astype(BF16)
        hg = _dot(x, wg_s[...])
        hu = _dot(x, wu_s[...])
        act = (hg * _sigmoid(hg)) * hu
        ybuf[t % 2] = _to_row_tiles(_dot(act.astype(BF16), wd_s[...]))
        writeback(t).start()
        return carry

    lax.fori_loop(t_first, t_last, tile, 0)

    @pl.when(e == N_EXPERTS - 1)
    def _():
        @pl.when(n_used >= 2)
        def _():
            writeback(n_used - 2).wait()
        writeback(n_used - 1).wait()
        n_all = ys_ref.shape[0] // TR_MOE
        ybuf[0] = jnp.zeros(ybuf.shape[1:], F32)

        def zero_tile(t):
            return pltpu.make_async_copy(ybuf.at[0], ys_ref.at[tile_rows(t)], sem_out.at[0])

        lax.fori_loop(n_used, n_all, lambda t, c: (zero_tile(t).start(), c)[1], 0)
        lax.fori_loop(n_used, n_all, lambda t, c: (zero_tile(t).wait(), c)[1], 0)


def _ffn(tile_end, xs, wg, wu, wd, n_tiles):
    tr = TR_MOE
    wsel = lambda e, tend: (e, 0, 0)
    return pl.pallas_call(
        _ffn_kernel,
        grid_spec=pltpu.PrefetchScalarGridSpec(
            num_scalar_prefetch=1,
            grid=(N_EXPERTS,),
            in_specs=[pl.BlockSpec(memory_space=pl.ANY),
                      pl.BlockSpec((1, D_MODEL, EXPERT_FF), wsel),
                      pl.BlockSpec((1, D_MODEL, EXPERT_FF), wsel),
                      pl.BlockSpec((1, EXPERT_FF, D_MODEL), wsel)],
            out_specs=pl.BlockSpec(memory_space=pl.ANY),
            scratch_shapes=[pltpu.VMEM((2, tr) + ROW_TILE, F32),
                            pltpu.VMEM((2, tr) + ROW_TILE, F32),
                            pltpu.VMEM((D_MODEL, EXPERT_FF), BF16),
                            pltpu.VMEM((D_MODEL, EXPERT_FF), BF16),
                            pltpu.VMEM((EXPERT_FF, D_MODEL), BF16),
                            pltpu.SemaphoreType.DMA((2,)),
                            pltpu.SemaphoreType.DMA((2,))]),
        out_shape=jax.ShapeDtypeStruct((n_tiles * tr,) + ROW_TILE, F32),
        compiler_params=_cparams(("arbitrary",)),
        name="ffn",
    )(tile_end, xs, wg, wu, wd)


def _combine_kernel(idx_ref, seg_ref, ys_ref, x1_ref, route_ref, mod_ref, y_ref, b1_s, b2_s, sems):
    i = pl.program_id(0)
    tm = x1_ref.shape[0]

    def gather(step, slot):
        def body(j, carry):
            s1, s2 = _token_slots(idx_ref, seg_ref, step * tm + j)
            _row_copy(ys_ref, s1, b1_s.at[slot], j, sems.at[slot]).start()
            _row_copy(ys_ref, s2, b2_s.at[slot], j, sems.at[slot]).start()
            return carry
        lax.fori_loop(0, tm, body, 0, unroll=8)

    @pl.when(i == 0)
    def _():
        gather(0, 0)

    @pl.when(i + 1 < pl.num_programs(0))
    def _():
        gather(i + 1, (i + 1) % 2)

    slot = i % 2
    for buf in (b1_s, b2_s):
        pltpu.make_async_copy(ys_ref.at[pl.ds(0, tm)], buf.at[slot], sems.at[slot]).wait()
    route = route_ref[...]
    moe = route[:, 4:5] * _from_row_tiles(b1_s[slot]) + route[:, 5:6] * _from_row_tiles(b2_s[slot])
    y_ref[...] = x1_ref[...] + mod_ref[0][5:6] * moe


def _combine(idx, seg_start, ys, x1, route, mod3, mod_row, seq_len):
    n = x1.shape[0]
    tm = TM_PRE
    tiles_per_seq = seq_len // tm
    row = lambda w: pl.BlockSpec((tm, w), lambda i, *_: (i, 0))
    return pl.pallas_call(
        _combine_kernel,
        grid_spec=pltpu.PrefetchScalarGridSpec(
            num_scalar_prefetch=2,
            grid=(n // tm,),
            in_specs=[pl.BlockSpec(memory_space=pl.ANY), row(D_MODEL), row(LANES),
                      pl.BlockSpec((1, 6, D_MODEL),
                                   lambda i, *_: (mod_row(i * tm), 0, 0))],
            out_specs=row(D_MODEL),
            scratch_shapes=[pltpu.VMEM((2, tm) + ROW_TILE, F32), pltpu.VMEM((2, tm) + ROW_TILE, F32),
                            pltpu.SemaphoreType.DMA((2,))]),
        out_shape=jax.ShapeDtypeStruct((n, D_MODEL), F32),
        compiler_params=_cparams(("arbitrary",)),
        name="combine",
    )(idx, seg_start, ys, x1, route, mod3)


def _rope_tables(length):
    rows = length // GRID_W
    r, col = jnp.meshgrid(jnp.arange(rows), jnp.arange(GRID_W), indexing='ij')
    r = r.reshape(-1).astype(F32)
    col = col.reshape(-1).astype(F32)
    half = HEAD_DIM // 2
    inv = ROPE_THETA ** (-jnp.arange(0, half, 2, dtype=F32) / half)
    ang_r = r[:, None] * inv
    ang_c = col[:, None] * inv
    ang = jnp.concatenate([ang_r, ang_r, ang_c, ang_c], axis=-1)
    sign = jnp.where((jnp.arange(HEAD_DIM) // (HEAD_DIM // 4)) % 2 == 0, -1.0, 1.0).astype(F32)
    cos = jnp.tile(jnp.cos(ang), (1, LANES // HEAD_DIM))
    sin = jnp.tile(jnp.sin(ang) * sign, (1, LANES // HEAD_DIM))
    return cos, sin


def _block_diag(w):
    eye = jnp.eye(LRU_BLOCKS, dtype=w.dtype)
    return jnp.einsum('hij,hg->higj', w, eye).reshape(LRU_W, LRU_W)


def _expand_heads(kv, one):
    b, t, _ = kv.shape
    h0 = kv[..., :HEAD_DIM]
    h1 = kv[..., HEAD_DIM:]
    lane = jnp.arange(HEAD_DIM)
    fill_hi = jnp.broadcast_to(jnp.where(lane == ONES_LANE_EVEN - HEAD_DIM, one, 0.0).astype(F32), h0.shape)
    fill_lo = jnp.broadcast_to(jnp.where(lane == ONES_LANE_ODD, one, 0.0).astype(F32), h0.shape)
    out = jnp.concatenate([h0, fill_hi, fill_lo, h0, h1, fill_hi, fill_lo, h1], axis=-1)
    return out.reshape(b * t, 4 * LANES).astype(BF16)


def kernel(x_prompt, x_sample, cache_k, cache_v, state_lru, c, c_ctx, w_mod, b_mod, norm1, norm2,
           w_in, q_norm, k_norm, conv_w, conv_b, lru_wa, lru_ba, lru_wx, lru_bx, lru_lambda, w_out,
           router_grp_w, router_grp_b, router_exp_w, router_exp_b, exp_w_gate, exp_w_up, exp_w_down):
    batch, seq, _ = x_prompt.shape
    dec_batch, dec_seq, _ = x_sample.shape
    past = cache_k.shape[2]
    depth = w_mod.shape[0]
    assert depth == 1

    cvec = jnp.concatenate(
        [c_ctx[None, :], c, jnp.zeros((MOD_ROWS - 1 - dec_batch, D_MODEL), F32)], axis=0)
    mod3 = _modulation(cvec, w_mod[0], b_mod[0][None, :]).reshape(MOD_ROWS, 6, D_MODEL)

    w_in_b = w_in[0].astype(BF16)
    w_out_b = w_out[0].astype(BF16)
    head_id = jnp.arange(QK_W) // HEAD_DIM
    ones_qk = (head_id[:, None] == head_id[None, :]).astype(BF16)
    gqk = jnp.concatenate([jnp.tile(q_norm[0], N_HEADS), jnp.tile(k_norm[0], N_KV_HEADS)])[None, :]
    wf = jnp.concatenate([_block_diag(lru_wa[0, 0]), _block_diag(lru_wx[0, 0])], axis=1).astype(BF16)
    wb = jnp.concatenate([_block_diag(lru_wa[0, 1]), _block_diag(lru_wx[0, 1])], axis=1).astype(BF16)
    bf = jnp.concatenate([lru_ba[0, 0], lru_bx[0, 0]])[None, :]
    bb = jnp.concatenate([lru_ba[0, 1], lru_bx[0, 1]])[None, :]
    pad = LANES - N_EXPERTS - N_GROUPS
    wr = jnp.concatenate([router_exp_w[0], router_grp_w[0], jnp.zeros((D_MODEL, pad), F32)], axis=1)
    wr_hi = wr.astype(BF16)
    wr_lo = (wr - wr_hi.astype(F32)).astype(BF16)
    br = jnp.concatenate([router_exp_b[0], router_grp_b[0], jnp.zeros((pad,), F32)])[None, :]
    g1 = norm1[0][None, :]
    g2 = norm2[0][None, :]
    cw = conv_w[0]
    cb = conv_b[0][None, :]
    lam = lru_lambda[0]
    tri = (jnp.arange(TM_PRE)[:, None] > jnp.arange(TM_PRE)[None, :]).astype(BF16)

    def mixers(x, seq_len, mod_row, tables, extra_k, extra_v, h0, cnt_in):
        q, kx, vx, kf, vf, xr, gb = _pre(x, mod3, mod_row, g1, w_in_b, ones_qk, gqk, tables, seq_len)
        k_segs = [(kx, seq_len)] + extra_k
        v_segs = [(vx, seq_len)] + extra_v
        attn = _attention(q, k_segs, v_segs, seq_len)
        rec, fin = _lru(xr, gb, h0, cw, cb, wf, wb, bf, bb, lam, seq_len)
        x1, h2, route, cnt = _post(attn, rec, x, mod3, mod_row, g2, w_out_b, wr_hi, wr_lo, br,
                                   tri, cnt_in, seq_len)
        return x1, h2, route, cnt, kf, vf, fin

    mod_row_p = lambda tok: 0
    mod_row_s = lambda tok: tok // dec_seq + 1
    xp = x_prompt.reshape(batch * seq, D_MODEL)
    x1p, h2p, route_p, cnt_p, kf, vf, fin = mixers(
        xp, seq, mod_row_p, None, [], [], jnp.zeros((batch, 2, LRU_W), F32),
        jnp.zeros((1, LANES), F32))
    xs = x_sample.reshape(dec_batch * dec_seq, D_MODEL)
    ck = _expand_heads(cache_k[:, 0].reshape(dec_batch, past, KV_W), 0.0)
    cv = _expand_heads(cache_v[:, 0].reshape(dec_batch, past, KV_W), 1.0)
    x1s, h2s, route_s, cnt_all, _, _, _ = mixers(
        xs, dec_seq, mod_row_s, _rope_tables(dec_seq), [(ck, past)], [(cv, past)],
        state_lru[:, 0], cnt_p)

    n_prompt = batch * seq
    n_tok = n_prompt + dec_batch * dec_seq
    n_tiles = (TOP_K * n_tok + N_EXPERTS * (TR_MOE - 1)) // TR_MOE
    cnt = cnt_all[0, :N_EXPERTS].astype(jnp.int32)
    ntile = (cnt + TR_MOE - 1) // TR_MOE
    tile_end = jnp.cumsum(ntile)
    seg_start = (tile_end - ntile) * TR_MOE
    n_used = tile_end[-1:]
    zfill = jnp.where(ntile > 0, tile_end - 1, -1)

    idx_p = route_p[:, :2 * TOP_K].astype(jnp.int32).reshape(-1)
    idx_s = route_s[:, :2 * TOP_K].astype(jnp.int32).reshape(-1)
    xsort = _dispatch(jnp.concatenate([idx_p, idx_s]), seg_start, zfill, n_used, h2p, h2s,
                      n_tiles * TR_MOE)
    ysort = _ffn(tile_end, xsort, exp_w_gate[0], exp_w_up[0], exp_w_down[0], n_tiles)
    yp = _combine(idx_p, seg_start, ysort, x1p, route_p, mod3, mod_row_p, seq)
    ys = _combine(idx_s, seg_start, ysort, x1s, route_s, mod3, mod_row_s, dec_seq)

    return (yp.reshape(batch, seq, D_MODEL),
            ys.reshape(dec_batch, dec_seq, D_MODEL),
            kf.reshape(batch, 1, seq, N_KV_HEADS, HEAD_DIM),
            vf.reshape(batch, 1, seq, N_KV_HEADS, HEAD_DIM),
            fin.reshape(batch, 1, 2, LRU_W))
```

```python
import functools

import jax
import jax.numpy as jnp
from jax import lax
from jax.experimental import pallas as pl
from jax.experimental.pallas import tpu as pltpu

F32 = jnp.float32
BF16 = jnp.bfloat16

D_MODEL = 1024
GRID_W = 64
ATTN_W = 512
LRU_W = 512
HEAD_DIM = 64
N_HEADS = 8
N_KV_HEADS = 2
KV_W = N_KV_HEADS * HEAD_DIM
LRU_BLOCKS = 8
LRU_BLOCK_W = LRU_W // LRU_BLOCKS
CONV_W = 4
LRU_C = 8.0
IN_W = ATTN_W + 2 * KV_W + 2 * LRU_W
QK_W = ATTN_W + KV_W
N_GROUPS = 4
EXPERTS_PER_GROUP = 8
N_EXPERTS = N_GROUPS * EXPERTS_PER_GROUP
TOP_K = 2
EXPERT_FF = D_MODEL // 4
ROPE_THETA = 10000.0
EPS = 1e-6

LANES = 128
SUBLANES = 8
MOD_ROWS = 8
VMEM_LIMIT = 48 * 1024 * 1024

TM_PRE = 512
TQ_ATT = 512
TC_LRU = 256
TR_MOE = 256
CH_DISPATCH = 1024
FFN_IN_SLOTS = 4
FFN_OUT_SLOTS = 3


def _cparams(sem):
    return pltpu.CompilerParams(dimension_semantics=sem, vmem_limit_bytes=VMEM_LIMIT)


def _dot(a, b):
    return jnp.dot(a, b, preferred_element_type=F32)


def _dot_nt(a, b):
    return lax.dot_general(a, b, (((1,), (1,)), ((), ())), preferred_element_type=F32)


ROW_TILE = (D_MODEL // LANES, LANES)
Q_SCALE = HEAD_DIM ** -0.5 * 1.4426950408889634
ONES_LANE_EVEN = HEAD_DIM
ONES_LANE_ODD = 0


def _to_row_tiles(x):
    cols = jnp.stack([x[:, c * LANES:(c + 1) * LANES] for c in range(D_MODEL // LANES)], axis=0)
    return jnp.swapaxes(cols, 0, 1)


def _from_row_tiles(x3):
    cols = jnp.swapaxes(x3, 0, 1)
    return jnp.concatenate([cols[c] for c in range(D_MODEL // LANES)], axis=1)


def _sigmoid(x):
    return 0.5 * jnp.tanh(0.5 * x) + 0.5


def _split_bf16(x):
    hi = x.astype(BF16)
    lo = (x - hi.astype(F32)).astype(BF16)
    return hi, lo


def _mod_kernel(c_ref, w_ref, b_ref, o_ref):
    c = c_ref[...]
    s = (c * jax.nn.sigmoid(c)).astype(BF16)
    o_ref[...] = _dot(s, w_ref[...].astype(BF16)) + b_ref[...]


def _modulation(cvec, w_mod, b_mod):
    n_out = w_mod.shape[1]
    tn = n_out // 4
    return pl.pallas_call(
        _mod_kernel,
        grid=(n_out // tn,),
        in_specs=[pl.BlockSpec((MOD_ROWS, D_MODEL), lambda j: (0, 0)),
                  pl.BlockSpec((D_MODEL, tn), lambda j: (0, j)),
                  pl.BlockSpec((1, tn), lambda j: (0, j))],
        out_specs=pl.BlockSpec((MOD_ROWS, tn), lambda j: (0, j)),
        out_shape=jax.ShapeDtypeStruct((MOD_ROWS, n_out), F32),
        compiler_params=_cparams(("arbitrary",)),
        name="modulation",
    )(cvec, w_mod, b_mod)


def _pre_kernel(*refs, rope):
    if rope:
        (x_ref, mod_ref, g1_ref, win_ref, ones_ref, gqk_ref, cos_ref, sin_ref,
         q_ref, kx_ref, vx_ref, kf_ref, vf_ref, xr_ref, gb_ref) = refs
    else:
        (x_ref, mod_ref, g1_ref, win_ref, ones_ref, gqk_ref,
         q_ref, kx_ref, vx_ref, kf_ref, vf_ref, xr_ref, gb_ref) = refs
    x = x_ref[...]
    m = mod_ref[0]
    ms = jnp.mean(x * x, axis=-1, keepdims=True)
    y = x * lax.rsqrt(ms + EPS) * g1_ref[...]
    h = y * (1.0 + m[1:2]) + m[0:1]
    z = _dot(h.astype(BF16), win_ref[...])

    qk = z[:, :QK_W]
    hi, lo = _split_bf16(qk * qk)
    ss = _dot(hi, ones_ref[...]) + _dot(lo, ones_ref[...])
    qk = qk * lax.rsqrt(ss * (1.0 / HEAD_DIM) + EPS) * gqk_ref[...]

    lane = lax.broadcasted_iota(jnp.int32, (x.shape[0], LANES), 1)
    cols = []
    for c in range(QK_W // LANES):
        xc = qk[:, c * LANES:(c + 1) * LANES]
        if rope:
            left = pltpu.roll(xc, LANES - HEAD_DIM // 4, 1)
            right = pltpu.roll(xc, HEAD_DIM // 4, 1)
            rot = jnp.where((lane // (HEAD_DIM // 4)) % 2 == 0, left, right)
            xc = xc * cos_ref[...] + rot * sin_ref[...]
        cols.append(xc)
    for c in range(ATTN_W // LANES):
        q_ref[:, c * LANES:(c + 1) * LANES] = (cols[c] * Q_SCALE).astype(BF16)

    lo_half = lane < HEAD_DIM
    for col, fref, xref, one in ((cols[ATTN_W // LANES], kf_ref, kx_ref, 0.0),
                                 (z[:, QK_W:QK_W + KV_W], vf_ref, vx_ref, 1.0)):
        fref[...] = col
        swapped = pltpu.roll(col, HEAD_DIM, 1)
        fill_hi = jnp.where(lane == ONES_LANE_EVEN, one, 0.0)
        fill_lo = jnp.where(lane == ONES_LANE_ODD, one, 0.0)
        xref[:, 0 * LANES:1 * LANES] = jnp.where(lo_half, col, fill_hi).astype(BF16)
        xref[:, 1 * LANES:2 * LANES] = jnp.where(lo_half, fill_lo, swapped).astype(BF16)
        xref[:, 2 * LANES:3 * LANES] = jnp.where(lo_half, swapped, fill_hi).astype(BF16)
        xref[:, 3 * LANES:4 * LANES] = jnp.where(lo_half, fill_lo, col).astype(BF16)

    xr_ref[...] = z[:, QK_W + KV_W:QK_W + KV_W + LRU_W]
    gb_ref[...] = z[:, QK_W + KV_W + LRU_W:]


def _pre(x, mod3, mod_row, g1, w_in, ones_qk, gqk, tables, seq_len):
    n = x.shape[0]
    tm = TM_PRE
    tiles_per_seq = seq_len // tm
    rope = tables is not None
    const = lambda i: (0, 0)
    in_specs = [pl.BlockSpec((tm, D_MODEL), lambda i: (i, 0)),
                pl.BlockSpec((1, 6, D_MODEL), lambda i: (mod_row(i * tm), 0, 0)),
                pl.BlockSpec((1, D_MODEL), const),
                pl.BlockSpec((D_MODEL, IN_W), const),
                pl.BlockSpec((QK_W, QK_W), const),
                pl.BlockSpec((1, QK_W), const)]
    args = [x, mod3, g1, w_in, ones_qk, gqk]
    if rope:
        in_specs += [pl.BlockSpec((tm, LANES), lambda i: (i % tiles_per_seq, 0))] * 2
        args += list(tables)
    row = lambda w: pl.BlockSpec((tm, w), lambda i: (i, 0))
    out_shape = [jax.ShapeDtypeStruct((n, ATTN_W), BF16),
                 jax.ShapeDtypeStruct((n, 4 * LANES), BF16),
                 jax.ShapeDtypeStruct((n, 4 * LANES), BF16),
                 jax.ShapeDtypeStruct((n, KV_W), F32),
                 jax.ShapeDtypeStruct((n, KV_W), F32),
                 jax.ShapeDtypeStruct((n, LRU_W), F32),
                 jax.ShapeDtypeStruct((n, LRU_W), F32)]
    out_specs = [row(ATTN_W), row(4 * LANES), row(4 * LANES), row(KV_W), row(KV_W),
                 row(LRU_W), row(LRU_W)]
    return pl.pallas_call(
        functools.partial(_pre_kernel, rope=rope),
        grid=(n // tm,),
        in_specs=in_specs, out_specs=out_specs, out_shape=out_shape,
        compiler_params=_cparams(("arbitrary",)),
        name="pre_rope" if rope else "pre",
    )(*args)


def _attn_kernel(*refs, n_seg):
    q_ref = refs[0]
    k_refs = refs[1:1 + n_seg]
    v_refs = refs[1 + n_seg:1 + 2 * n_seg]
    o_ref = refs[1 + 2 * n_seg]
    lane = lax.broadcasted_iota(jnp.int32, (q_ref.shape[0], LANES), 1)
    for c in range(ATTN_W // LANES):
        qc = q_ref[:, c * LANES:(c + 1) * LANES]
        g = c // 2
        accs = []
        for par in range(2):
            sl = slice((2 * g + par) * LANES, (2 * g + par + 1) * LANES)
            ss = [_dot_nt(qc, k[:, sl]) for k in k_refs]
            mx = functools.reduce(jnp.maximum, [jnp.max(s, axis=-1, keepdims=True) for s in ss])
            ps = [jnp.exp2((s - mx).astype(BF16)) for s in ss]
            accs.append(functools.reduce(lambda a, b: a + b,
                                         [_dot(p, v[:, sl]) for p, v in zip(ps, v_refs)]))
        even = accs[0] / accs[0][:, ONES_LANE_EVEN:ONES_LANE_EVEN + 1]
        odd = accs[1] / accs[1][:, ONES_LANE_ODD:ONES_LANE_ODD + 1]
        o_ref[:, c * LANES:(c + 1) * LANES] = jnp.where(lane < HEAD_DIM, even, odd).astype(BF16)


def _attention(q, k_segs, v_segs, seq_len):
    n = q.shape[0]
    tq = min(TQ_ATT, seq_len)
    nq = seq_len // tq
    n_seg = len(k_segs)
    in_specs = [pl.BlockSpec((tq, ATTN_W), lambda b, i: (b * nq + i, 0))]
    for arr, t in list(k_segs) + list(v_segs):
        in_specs.append(pl.BlockSpec((t, 4 * LANES), lambda b, i: (b, 0)))
    return pl.pallas_call(
        functools.partial(_attn_kernel, n_seg=n_seg),
        grid=(n // seq_len, nq),
        in_specs=in_specs,
        out_specs=pl.BlockSpec((tq, ATTN_W), lambda b, i: (b * nq + i, 0)),
        out_shape=jax.ShapeDtypeStruct((n, ATTN_W), BF16),
        compiler_params=_cparams(("arbitrary", "arbitrary")),
        name="attention_%dseg" % n_seg,
    )(q, *[a for a, _ in k_segs], *[a for a, _ in v_segs])


def _log_sigmoid(x):
    return jnp.minimum(x, 0.0) - jnp.log1p(jnp.exp(-jnp.abs(x)))


def _tile_scan(a, b, reverse):
    row = lax.broadcasted_iota(jnp.int32, a.shape, 0)
    d = 1
    while d < SUBLANES:
        if reverse:
            keep = row < SUBLANES - d
            shift = SUBLANES - d
        else:
            keep = row >= d
            shift = d
        a_sh = jnp.where(keep, pltpu.roll(a, shift, 0), 1.0)
        b_sh = jnp.where(keep, pltpu.roll(b, shift, 0), 0.0)
        b = a * b_sh + b
        a = a * a_sh
        d *= 2
    return a, b


def _lru_kernel(xr_ref, gb_ref, h0_ref, cw_ref, cb_ref, wf_ref, wb_ref, bf_ref, bb_ref, lam_ref,
                rec_ref, fin_ref, xpad_s, xc_s, hf_s, a_s, b_s, *, seq_len):
    tc = TC_LRU
    n_chunks = seq_len // tc
    n_tiles = tc // SUBLANES
    zpad = jnp.zeros((SUBLANES, LRU_W), F32)
    xpad_s[0:SUBLANES, :] = zpad
    xpad_s[SUBLANES:SUBLANES + seq_len, :] = xr_ref[...]
    xpad_s[SUBLANES + seq_len:2 * SUBLANES + seq_len, :] = zpad

    cl = LRU_C * _log_sigmoid(lam_ref[...])

    def gates(xcc, w_ref, bias_ref, cl_d):
        g = _dot(xcc.astype(BF16), w_ref[...]) + bias_ref[...]
        r = _sigmoid(g[:, :LRU_W])
        i = _sigmoid(g[:, LRU_W:])
        log_a = r * cl_d
        a = jnp.exp(log_a)
        a_s[...] = a
        b_s[...] = jnp.sqrt(1.0 - a * a) * i * xcc

    h = h0_ref[0, 0:1, :]
    for c in range(n_chunks):
        base = c * tc
        xcc = cb_ref[...] + functools.reduce(
            lambda u, v: u + v,
            [cw_ref[j:j + 1, :] * xpad_s[base + SUBLANES - 1 + j:base + SUBLANES - 1 + j + tc, :]
             for j in range(CONV_W)])
        xc_s[base:base + tc, :] = xcc
        gates(xcc, wf_ref, bf_ref, cl[0:1])

        def fwd_tile(t, hc, base=base):
            r0 = pl.multiple_of(t * SUBLANES, SUBLANES)
            ca, cb = _tile_scan(a_s[pl.ds(r0, SUBLANES), :], b_s[pl.ds(r0, SUBLANES), :], False)
            hh = ca * hc + cb
            hf_s[pl.ds(base + r0, SUBLANES), :] = hh
            return hh[SUBLANES - 1:SUBLANES, :]

        h = lax.fori_loop(0, n_tiles, fwd_tile, h)
    fin_ref[0, 0:1, :] = h

    h = h0_ref[0, 1:2, :]
    for c in reversed(range(n_chunks)):
        base = c * tc
        gates(xc_s[base:base + tc, :], wb_ref, bb_ref, cl[1:2])

        def bwd_tile(t, hc, base=base):
            r0 = pl.multiple_of((n_tiles - 1 - t) * SUBLANES, SUBLANES)
            ca, cb = _tile_scan(a_s[pl.ds(r0, SUBLANES), :], b_s[pl.ds(r0, SUBLANES), :], True)
            hh = ca * hc + cb
            gate = jax.nn.gelu(gb_ref[pl.ds(base + r0, SUBLANES), :], approximate=True)
            rec_ref[pl.ds(base + r0, SUBLANES), :] = (
                (hf_s[pl.ds(base + r0, SUBLANES), :] + hh) * gate).astype(rec_ref.dtype)
            return hh[0:1, :]

        h = lax.fori_loop(0, n_tiles, bwd_tile, h)
    fin_ref[0, 1:2, :] = h


def _lru(xr, gb, h0, conv_w, conv_b, wf, wb, bf, bb, lam, seq_len):
    n = xr.shape[0]
    batch = n // seq_len
    const = lambda b: (0, 0)
    seq = pl.BlockSpec((seq_len, LRU_W), lambda b: (b, 0))
    st = pl.BlockSpec((1, 2, LRU_W), lambda b: (b, 0, 0))
    return pl.pallas_call(
        functools.partial(_lru_kernel, seq_len=seq_len),
        grid=(batch,),
        in_specs=[seq, seq, st,
                  pl.BlockSpec((CONV_W, LRU_W), const), pl.BlockSpec((1, LRU_W), const),
                  pl.BlockSpec((LRU_W, 2 * LRU_W), const), pl.BlockSpec((LRU_W, 2 * LRU_W), const),
                  pl.BlockSpec((1, 2 * LRU_W), const), pl.BlockSpec((1, 2 * LRU_W), const),
                  pl.BlockSpec((2, LRU_W), const)],
        out_specs=[seq, st],
        out_shape=[jax.ShapeDtypeStruct((n, LRU_W), BF16),
                   jax.ShapeDtypeStruct((batch, 2, LRU_W), F32)],
        scratch_shapes=[pltpu.VMEM((seq_len + 2 * SUBLANES, LRU_W), F32),
                        pltpu.VMEM((seq_len, LRU_W), F32),
                        pltpu.VMEM((seq_len, LRU_W), F32),
                        pltpu.VMEM((TC_LRU, LRU_W), F32),
                        pltpu.VMEM((TC_LRU, LRU_W), F32)],
        compiler_params=_cparams(("arbitrary",)),
        name="lru_%d" % seq_len,
    )(xr, gb, h0, conv_w, conv_b, wf, wb, bf, bb, lam)


def _post_kernel(attn_ref, rec_ref, x_ref, mod_ref, g2_ref, wo_ref, wr_hi_ref, wr_lo_ref, br_ref,
                 tri_ref, cnt_in_ref, x1_ref, h2_ref, route_ref, cnt_ref, cnt_s):
    @pl.when(pl.program_id(0) == 0)
    def _():
        cnt_s[...] = cnt_in_ref[...]

    m = mod_ref[0]
    u = _dot(attn_ref[...], wo_ref[:ATTN_W, :]) + _dot(rec_ref[...], wo_ref[ATTN_W:, :])
    x1 = x_ref[...] + m[2:3] * u
    x1_ref[...] = x1
    ms = jnp.mean(x1 * x1, axis=-1, keepdims=True)
    h2 = x1 * lax.rsqrt(ms + EPS) * g2_ref[...]
    h2 = h2 * (1.0 + m[4:5]) + m[3:4]
    hi, lo = _split_bf16(h2)
    h2_ref[...] = _to_row_tiles(h2)

    logits = (_dot(hi, wr_hi_ref[...]) + _dot(hi, wr_lo_ref[...]) + _dot(lo, wr_hi_ref[...])
              + br_ref[...])
    lane = lax.broadcasted_iota(jnp.int32, logits.shape, 1)
    neg = -jnp.inf
    big = jnp.int32(1 << 20)
    gmask = (lane >= N_EXPERTS) & (lane < N_EXPERTS + N_GROUPS)
    gl = jnp.where(gmask, logits, neg)
    gmax = jnp.max(gl, axis=-1, keepdims=True)
    gidx = jnp.min(jnp.where(gl == gmax, lane - N_EXPERTS, big), axis=-1, keepdims=True)
    p_sel = 1.0 / jnp.sum(jnp.where(gmask, jnp.exp(gl - gmax), 0.0), axis=-1, keepdims=True)

    emask = (lane < N_EXPERTS) & ((lane // EXPERTS_PER_GROUP) == gidx)
    el = jnp.where(emask, logits, neg)
    v1 = jnp.max(el, axis=-1, keepdims=True)
    i1 = jnp.min(jnp.where(el == v1, lane, big), axis=-1, keepdims=True)
    el2 = jnp.where(lane == i1, neg, el)
    v2 = jnp.max(el2, axis=-1, keepdims=True)
    i2 = jnp.min(jnp.where(el2 == v2, lane, big), axis=-1, keepdims=True)
    e2 = jnp.exp(v2 - v1)
    w1 = p_sel / (1.0 + e2)
    w2 = p_sel * e2 / (1.0 + e2)

    oh1 = lane == i1
    oh2 = lane == i2
    oh = jnp.where(oh1, 1.0, 0.0) + jnp.where(oh2, 1.0, 0.0)
    before = _dot(tri_ref[...], oh.astype(BF16)) + cnt_s[...]
    rank1 = jnp.sum(jnp.where(oh1, before, 0.0), axis=-1, keepdims=True)
    rank2 = jnp.sum(jnp.where(oh2, before, 0.0), axis=-1, keepdims=True)
    cnt = cnt_s[...] + jnp.sum(oh, axis=0, keepdims=True)
    cnt_s[...] = cnt
    cnt_ref[...] = cnt
    fields = (i1.astype(F32), i2.astype(F32), rank1, rank2, w1, w2)
    route = jnp.zeros(logits.shape, F32)
    for k, val in enumerate(fields):
        route = jnp.where(lane == k, val, route)
    route_ref[...] = route


def _post(attn, rec, x, mod3, mod_row, g2, w_out, wr_hi, wr_lo, br, tri, cnt_in, seq_len):
    n = x.shape[0]
    tm = TM_PRE
    tiles_per_seq = seq_len // tm
    const = lambda i: (0, 0)
    row = lambda w: pl.BlockSpec((tm, w), lambda i: (i, 0))
    return pl.pallas_call(
        _post_kernel,
        grid=(n // tm,),
        in_specs=[row(ATTN_W), row(LRU_W), row(D_MODEL),
                  pl.BlockSpec((1, 6, D_MODEL), lambda i: (mod_row(i * tm), 0, 0)),
                  pl.BlockSpec((1, D_MODEL), const),
                  pl.BlockSpec((D_MODEL, D_MODEL), const),
                  pl.BlockSpec((D_MODEL, LANES), const),
                  pl.BlockSpec((D_MODEL, LANES), const),
                  pl.BlockSpec((1, LANES), const),
                  pl.BlockSpec((tm, tm), const),
                  pl.BlockSpec((1, LANES), const)],
        out_specs=[row(D_MODEL), pl.BlockSpec((tm,) + ROW_TILE, lambda i: (i, 0, 0)), row(LANES),
                   pl.BlockSpec((1, LANES), const)],
        out_shape=[jax.ShapeDtypeStruct((n, D_MODEL), F32),
                   jax.ShapeDtypeStruct((n,) + ROW_TILE, F32),
                   jax.ShapeDtypeStruct((n, LANES), F32),
                   jax.ShapeDtypeStruct((1, LANES), F32)],
        scratch_shapes=[pltpu.VMEM((1, LANES), F32)],
        compiler_params=_cparams(("arbitrary",)),
        name="post",
    )(attn, rec, x, mod3, g2, w_out, wr_hi, wr_lo, br, tri, cnt_in)


def _row_copy(src_ref, src_row, dst_ref, dst_row, sem):
    return pltpu.make_async_copy(src_ref.at[pl.ds(src_row, 1)], dst_ref.at[pl.ds(dst_row, 1)], sem)


def _token_slots(idx_ref, seg_ref, n):
    base = 2 * TOP_K * n
    return (seg_ref[idx_ref[base]] + idx_ref[base + 2], seg_ref[idx_ref[base + 1]] + idx_ref[base + 3])


def _dispatch_kernel(idx_ref, seg_ref, zfill_ref, nu_ref, h2p_ref, h2s_ref, xs_ref, zero_s, sem,
                     *, n_prompt):
    i = pl.program_id(0)
    ch = CH_DISPATCH

    @pl.when(i == 0)
    def _():
        zero_s[...] = jnp.zeros_like(zero_s)

        def zero_tile(t):
            return pltpu.make_async_copy(
                zero_s, xs_ref.at[pl.ds(pl.multiple_of(t * TR_MOE, TR_MOE), TR_MOE)], sem)

        for e in range(N_EXPERTS):
            @pl.when(zfill_ref[e] >= 0)
            def _():
                zero_tile(zfill_ref[e]).start()
        n_all = xs_ref.shape[0] // TR_MOE
        lax.fori_loop(nu_ref[0], n_all, lambda t, c: (zero_tile(t).start(), c)[1], 0)
        for e in range(N_EXPERTS):
            @pl.when(zfill_ref[e] >= 0)
            def _():
                zero_tile(0).wait()
        lax.fori_loop(nu_ref[0], n_all, lambda t, c: (zero_tile(0).wait(), c)[1], 0)

    def scatter(src_ref):
        def body(j, carry):
            s1, s2 = _token_slots(idx_ref, seg_ref, i * ch + j)
            _row_copy(src_ref, j, xs_ref, s1, sem).start()
            _row_copy(src_ref, j, xs_ref, s2, sem).start()
            return carry
        lax.fori_loop(0, ch, body, 0, unroll=8)
        for _ in range(TOP_K):
            pltpu.make_async_copy(src_ref, xs_ref.at[pl.ds(0, ch)], sem).wait()

    @pl.when(i < n_prompt // ch)
    def _():
        scatter(h2p_ref)

    @pl.when(i >= n_prompt // ch)
    def _():
        scatter(h2s_ref)


def _dispatch(idx, seg_start, zfill, n_used, h2p, h2s, n_rows):
    n_prompt = h2p.shape[0]
    n = n_prompt + h2s.shape[0]
    ch = CH_DISPATCH
    npc = n_prompt // ch
    last_p = npc - 1
    return pl.pallas_call(
        functools.partial(_dispatch_kernel, n_prompt=n_prompt),
        grid_spec=pltpu.PrefetchScalarGridSpec(
            num_scalar_prefetch=4,
            grid=(n // ch,),
            in_specs=[pl.BlockSpec((ch,) + ROW_TILE, lambda i, *_: (jnp.minimum(i, last_p), 0, 0)),
                      pl.BlockSpec((ch,) + ROW_TILE, lambda i, *_: (jnp.maximum(i - npc, 0), 0, 0))],
            out_specs=pl.BlockSpec(memory_space=pl.ANY),
            scratch_shapes=[pltpu.VMEM((TR_MOE,) + ROW_TILE, F32), pltpu.SemaphoreType.DMA]),
        out_shape=jax.ShapeDtypeStruct((n_rows,) + ROW_TILE, F32),
        compiler_params=_cparams(("arbitrary",)),
        name="dispatch",
    )(idx, seg_start, zfill, n_used, h2p, h2s)


def _ffn_kernel(tend_ref, xs_ref, wg_ref, wu_ref, wd_ref, ys_ref,
                xbuf, ybuf, wg_s, wu_s, wd_s, sem_in, sem_out):
    e = pl.program_id(0)
    n_used = tend_ref[N_EXPERTS - 1]
    t_first = jnp.where(e == 0, 0, tend_ref[jnp.maximum(e - 1, 0)])
    t_last = tend_ref[e]
    n_in = xbuf.shape[0]
    n_out = ybuf.shape[0]

    def tile_rows(t):
        return pl.ds(pl.multiple_of(t * TR_MOE, TR_MOE), TR_MOE)

    def fetch(t):
        return pltpu.make_async_copy(xs_ref.at[tile_rows(t)], xbuf.at[t % n_in], sem_in.at[t % n_in])

    def writeback(t):
        return pltpu.make_async_copy(ybuf.at[t % n_out], ys_ref.at[tile_rows(t)],
                                     sem_out.at[t % n_out])

    @pl.when(e == 0)
    def _():
        for t in range(n_in - 1):
            @pl.when(t < n_used)
            def _():
                fetch(t).start()

    @pl.when(t_last > t_first)
    def _():
        wg_s[...] = wg_ref[0].astype(BF16)
        wu_s[...] = wu_ref[0].astype(BF16)
        wd_s[...] = wd_ref[0].astype(BF16)

    def tile(t, carry):
        fetch(t).wait()

        @pl.when(t + n_in - 1 < n_used)
        def _():
            fetch(t + n_in - 1).start()

        @pl.when(t >= n_out)
        def _():
            writeback(t - n_out).wait()

        x = _from_row_tiles(xbuf[t % n_in]).astype(BF16)
        hg = _dot(x, wg_s[...])
        hu = _dot(x, wu_s[...])
        act = (hg * _sigmoid(hg)) * hu
        ybuf[t % n_out] = _to_row_tiles(_dot(act.astype(BF16), wd_s[...]))
        writeback(t).start()
        return carry

    lax.fori_loop(t_first, t_last, tile, 0)

    @pl.when(e == N_EXPERTS - 1)
    def _():
        for back in range(n_out, 0, -1):
            @pl.when(n_used >= back)
            def _():
                writeback(n_used - back).wait()
        n_all = ys_ref.shape[0] // TR_MOE
        ybuf[0] = jnp.zeros(ybuf.shape[1:], F32)

        def zero_tile(t):
            return pltpu.make_async_copy(ybuf.at[0], ys_ref.at[tile_rows(t)], sem_out.at[0])

        lax.fori_loop(n_used, n_all, lambda t, c: (zero_tile(t).start(), c)[1], 0)
        lax.fori_loop(n_used, n_all, lambda t, c: (zero_tile(t).wait(), c)[1], 0)


def _ffn(tile_end, xs, wg, wu, wd, n_tiles):
    tr = TR_MOE
    wsel = lambda e, tend: (e, 0, 0)
    return pl.pallas_call(
        _ffn_kernel,
        grid_spec=pltpu.PrefetchScalarGridSpec(
            num_scalar_prefetch=1,
            grid=(N_EXPERTS,),
            in_specs=[pl.BlockSpec(memory_space=pl.ANY),
                      pl.BlockSpec((1, D_MODEL, EXPERT_FF), wsel),
                      pl.BlockSpec((1, D_MODEL, EXPERT_FF), wsel),
                      pl.BlockSpec((1, EXPERT_FF, D_MODEL), wsel)],
            out_specs=pl.BlockSpec(memory_space=pl.ANY),
            scratch_shapes=[pltpu.VMEM((FFN_IN_SLOTS, tr) + ROW_TILE, F32),
                            pltpu.VMEM((FFN_OUT_SLOTS, tr) + ROW_TILE, F32),
                            pltpu.VMEM((D_MODEL, EXPERT_FF), BF16),
                            pltpu.VMEM((D_MODEL, EXPERT_FF), BF16),
                            pltpu.VMEM((EXPERT_FF, D_MODEL), BF16),
                            pltpu.SemaphoreType.DMA((FFN_IN_SLOTS,)),
                            pltpu.SemaphoreType.DMA((FFN_OUT_SLOTS,))]),
        out_shape=jax.ShapeDtypeStruct((n_tiles * tr,) + ROW_TILE, F32),
        compiler_params=_cparams(("arbitrary",)),
        name="ffn",
    )(tile_end, xs, wg, wu, wd)


def _combine_kernel(idx_ref, seg_ref, ys_ref, x1_ref, route_ref, mod_ref, y_ref, b1_s, b2_s, sems):
    i = pl.program_id(0)
    tm = x1_ref.shape[0]

    def gather(step, slot):
        def body(j, carry):
            s1, s2 = _token_slots(idx_ref, seg_ref, step * tm + j)
            _row_copy(ys_ref, s1, b1_s.at[slot], j, sems.at[slot]).start()
            _row_copy(ys_ref, s2, b2_s.at[slot], j, sems.at[slot]).start()
            return carry
        lax.fori_loop(0, tm, body, 0, unroll=8)

    @pl.when(i == 0)
    def _():
        gather(0, 0)

    @pl.when(i + 1 < pl.num_programs(0))
    def _():
        gather(i + 1, (i + 1) % 2)

    slot = i % 2
    for buf in (b1_s, b2_s):
        pltpu.make_async_copy(ys_ref.at[pl.ds(0, tm)], buf.at[slot], sems.at[slot]).wait()
    route = route_ref[...]
    moe = route[:, 4:5] * _from_row_tiles(b1_s[slot]) + route[:, 5:6] * _from_row_tiles(b2_s[slot])
    y_ref[...] = x1_ref[...] + mod_ref[0][5:6] * moe


def _combine(idx, seg_start, ys, x1, route, mod3, mod_row, seq_len):
    n = x1.shape[0]
    tm = TM_PRE
    tiles_per_seq = seq_len // tm
    row = lambda w: pl.BlockSpec((tm, w), lambda i, *_: (i, 0))
    return pl.pallas_call(
        _combine_kernel,
        grid_spec=pltpu.PrefetchScalarGridSpec(
            num_scalar_prefetch=2,
            grid=(n // tm,),
            in_specs=[pl.BlockSpec(memory_space=pl.ANY), row(D_MODEL), row(LANES),
                      pl.BlockSpec((1, 6, D_MODEL),
                                   lambda i, *_: (mod_row(i * tm), 0, 0))],
            out_specs=row(D_MODEL),
            scratch_shapes=[pltpu.VMEM((2, tm) + ROW_TILE, F32), pltpu.VMEM((2, tm) + ROW_TILE, F32),
                            pltpu.SemaphoreType.DMA((2,))]),
        out_shape=jax.ShapeDtypeStruct((n, D_MODEL), F32),
        compiler_params=_cparams(("arbitrary",)),
        name="combine",
    )(idx, seg_start, ys, x1, route, mod3)


def _rope_tables(length):
    rows = length // GRID_W
    r, col = jnp.meshgrid(jnp.arange(rows), jnp.arange(GRID_W), indexing='ij')
    r = r.reshape(-1).astype(F32)
    col = col.reshape(-1).astype(F32)
    half = HEAD_DIM // 2
    inv = ROPE_THETA ** (-jnp.arange(0, half, 2, dtype=F32) / half)
    ang_r = r[:, None] * inv
    ang_c = col[:, None] * inv
    ang = jnp.concatenate([ang_r, ang_r, ang_c, ang_c], axis=-1)
    sign = jnp.where((jnp.arange(HEAD_DIM) // (HEAD_DIM // 4)) % 2 == 0, -1.0, 1.0).astype(F32)
    cos = jnp.tile(jnp.cos(ang), (1, LANES // HEAD_DIM))
    sin = jnp.tile(jnp.sin(ang) * sign, (1, LANES // HEAD_DIM))
    return cos, sin


def _block_diag(w):
    eye = jnp.eye(LRU_BLOCKS, dtype=w.dtype)
    return jnp.einsum('hij,hg->higj', w, eye).reshape(LRU_W, LRU_W)


def _expand_heads(kv, one):
    b, t, _ = kv.shape
    h0 = kv[..., :HEAD_DIM]
    h1 = kv[..., HEAD_DIM:]
    lane = jnp.arange(HEAD_DIM)
    fill_hi = jnp.broadcast_to(jnp.where(lane == ONES_LANE_EVEN - HEAD_DIM, one, 0.0).astype(F32), h0.shape)
    fill_lo = jnp.broadcast_to(jnp.where(lane == ONES_LANE_ODD, one, 0.0).astype(F32), h0.shape)
    out = jnp.concatenate([h0, fill_hi, fill_lo, h0, h1, fill_hi, fill_lo, h1], axis=-1)
    return out.reshape(b * t, 4 * LANES).astype(BF16)


def kernel(x_prompt, x_sample, cache_k, cache_v, state_lru, c, c_ctx, w_mod, b_mod, norm1, norm2,
           w_in, q_norm, k_norm, conv_w, conv_b, lru_wa, lru_ba, lru_wx, lru_bx, lru_lambda, w_out,
           router_grp_w, router_grp_b, router_exp_w, router_exp_b, exp_w_gate, exp_w_up, exp_w_down):
    batch, seq, _ = x_prompt.shape
    dec_batch, dec_seq, _ = x_sample.shape
    past = cache_k.shape[2]
    depth = w_mod.shape[0]
    assert depth == 1

    cvec = jnp.concatenate(
        [c_ctx[None, :], c, jnp.zeros((MOD_ROWS - 1 - dec_batch, D_MODEL), F32)], axis=0)
    mod3 = _modulation(cvec, w_mod[0], b_mod[0][None, :]).reshape(MOD_ROWS, 6, D_MODEL)

    w_in_b = w_in[0].astype(BF16)
    w_out_b = w_out[0].astype(BF16)
    head_id = jnp.arange(QK_W) // HEAD_DIM
    ones_qk = (head_id[:, None] == head_id[None, :]).astype(BF16)
    gqk = jnp.concatenate([jnp.tile(q_norm[0], N_HEADS), jnp.tile(k_norm[0], N_KV_HEADS)])[None, :]
    wf = jnp.concatenate([_block_diag(lru_wa[0, 0]), _block_diag(lru_wx[0, 0])], axis=1).astype(BF16)
    wb = jnp.concatenate([_block_diag(lru_wa[0, 1]), _block_diag(lru_wx[0, 1])], axis=1).astype(BF16)
    bf = jnp.concatenate([lru_ba[0, 0], lru_bx[0, 0]])[None, :]
    bb = jnp.concatenate([lru_ba[0, 1], lru_bx[0, 1]])[None, :]
    pad = LANES - N_EXPERTS - N_GROUPS
    wr = jnp.concatenate([router_exp_w[0], router_grp_w[0], jnp.zeros((D_MODEL, pad), F32)], axis=1)
    wr_hi = wr.astype(BF16)
    wr_lo = (wr - wr_hi.astype(F32)).astype(BF16)
    br = jnp.concatenate([router_exp_b[0], router_grp_b[0], jnp.zeros((pad,), F32)])[None, :]
    g1 = norm1[0][None, :]
    g2 = norm2[0][None, :]
    cw = conv_w[0]
    cb = conv_b[0][None, :]
    lam = lru_lambda[0]
    tri = (jnp.arange(TM_PRE)[:, None] > jnp.arange(TM_PRE)[None, :]).astype(BF16)

    def mixers(x, seq_len, mod_row, tables, extra_k, extra_v, h0, cnt_in):
        q, kx, vx, kf, vf, xr, gb = _pre(x, mod3, mod_row, g1, w_in_b, ones_qk, gqk, tables, seq_len)
        k_segs = [(kx, seq_len)] + extra_k
        v_segs = [(vx, seq_len)] + extra_v
        attn = _attention(q, k_segs, v_segs, seq_len)
        rec, fin = _lru(xr, gb, h0, cw, cb, wf, wb, bf, bb, lam, seq_len)
        x1, h2, route, cnt = _post(attn, rec, x, mod3, mod_row, g2, w_out_b, wr_hi, wr_lo, br,
                                   tri, cnt_in, seq_len)
        return x1, h2, route, cnt, kf, vf, fin

    mod_row_p = lambda tok: 0
    mod_row_s = lambda tok: tok // dec_seq + 1
    xp = x_prompt.reshape(batch * seq, D_MODEL)
    x1p, h2p, route_p, cnt_p, kf, vf, fin = mixers(
        xp, seq, mod_row_p, None, [], [], jnp.zeros((batch, 2, LRU_W), F32),
        jnp.zeros((1, LANES), F32))
    xs = x_sample.reshape(dec_batch * dec_seq, D_MODEL)
    ck = _expand_heads(cache_k[:, 0].reshape(dec_batch, past, KV_W), 0.0)
    cv = _expand_heads(cache_v[:, 0].reshape(dec_batch, past, KV_W), 1.0)
    x1s, h2s, route_s, cnt_all, _, _, _ = mixers(
        xs, dec_seq, mod_row_s, _rope_tables(dec_seq), [(ck, past)], [(cv, past)],
        state_lru[:, 0], cnt_p)

    n_prompt = batch * seq
    n_tok = n_prompt + dec_batch * dec_seq
    n_tiles = (TOP_K * n_tok + N_EXPERTS * (TR_MOE - 1)) // TR_MOE
    cnt = cnt_all[0, :N_EXPERTS].astype(jnp.int32)
    ntile = (cnt + TR_MOE - 1) // TR_MOE
    tile_end = jnp.cumsum(ntile)
    seg_start = (tile_end - ntile) * TR_MOE
    n_used = tile_end[-1:]
    zfill = jnp.where(ntile > 0, tile_end - 1, -1)

    idx_p = route_p[:, :2 * TOP_K].astype(jnp.int32).reshape(-1)
    idx_s = route_s[:, :2 * TOP_K].astype(jnp.int32).reshape(-1)
    xsort = _dispatch(jnp.concatenate([idx_p, idx_s]), seg_start, zfill, n_used, h2p, h2s,
                      n_tiles * TR_MOE)
    ysort = _ffn(tile_end, xsort, exp_w_gate[0], exp_w_up[0], exp_w_down[0], n_tiles)
    yp = _combine(idx_p, seg_start, ysort, x1p, route_p, mod3, mod_row_p, seq)
    ys = _combine(idx_s, seg_start, ysort, x1s, route_s, mod3, mod_row_s, dec_seq)

    return (yp.reshape(batch, seq, D_MODEL),
            ys.reshape(dec_batch, dec_seq, D_MODEL),
            kf.reshape(batch, 1, seq, N_KV_HEADS, HEAD_DIM),
            vf.reshape(batch, 1, seq, N_KV_HEADS, HEAD_DIM),
            fin.reshape(batch, 1, 2, LRU_W))
```

```python
import functools

import jax
import jax.numpy as jnp
from jax import lax
from jax.experimental import pallas as pl
from jax.experimental.pallas import tpu as pltpu

F32 = jnp.float32
BF16 = jnp.bfloat16

D_MODEL = 1024
GRID_W = 64
ATTN_W = 512
LRU_W = 512
HEAD_DIM = 64
N_HEADS = 8
N_KV_HEADS = 2
KV_W = N_KV_HEADS * HEAD_DIM
LRU_BLOCKS = 8
LRU_BLOCK_W = LRU_W // LRU_BLOCKS
CONV_W = 4
LRU_C = 8.0
IN_W = ATTN_W + 2 * KV_W + 2 * LRU_W
QK_W = ATTN_W + KV_W
N_GROUPS = 4
EXPERTS_PER_GROUP = 8
N_EXPERTS = N_GROUPS * EXPERTS_PER_GROUP
TOP_K = 2
EXPERT_FF = D_MODEL // 4
ROPE_THETA = 10000.0
EPS = 1e-6

LANES = 128
SUBLANES = 8
MOD_ROWS = 8
VMEM_LIMIT = 48 * 1024 * 1024

TM_PRE = 512
TQ_ATT = 1024
TC_LRU = 256
TR_MOE = 256
CH_DISPATCH = 1024
FFN_IN_SLOTS = 4
FFN_OUT_SLOTS = 3


def _cparams(sem):
    return pltpu.CompilerParams(dimension_semantics=sem, vmem_limit_bytes=VMEM_LIMIT)


def _dot(a, b):
    return jnp.dot(a, b, preferred_element_type=F32)


def _dot_nt(a, b):
    return lax.dot_general(a, b, (((1,), (1,)), ((), ())), preferred_element_type=F32)


ROW_TILE = (D_MODEL // LANES, LANES)
Q_SCALE = HEAD_DIM ** -0.5 * 1.4426950408889634
ONES_LANE_EVEN = HEAD_DIM
ONES_LANE_ODD = 0


def _to_row_tiles(x):
    cols = jnp.stack([x[:, c * LANES:(c + 1) * LANES] for c in range(D_MODEL // LANES)], axis=0)
    return jnp.swapaxes(cols, 0, 1)


def _from_row_tiles(x3):
    cols = jnp.swapaxes(x3, 0, 1)
    return jnp.concatenate([cols[c] for c in range(D_MODEL // LANES)], axis=1)


def _sigmoid(x):
    return 0.5 * jnp.tanh(0.5 * x) + 0.5


def _split_bf16(x):
    hi = x.astype(BF16)
    lo = (x - hi.astype(F32)).astype(BF16)
    return hi, lo


def _mod_kernel(c_ref, w_ref, b_ref, o_ref):
    c = c_ref[...]
    s = (c * jax.nn.sigmoid(c)).astype(BF16)
    o_ref[...] = _dot(s, w_ref[...].astype(BF16)) + b_ref[...]


def _modulation(cvec, w_mod, b_mod):
    n_out = w_mod.shape[1]
    tn = n_out // 4
    return pl.pallas_call(
        _mod_kernel,
        grid=(n_out // tn,),
        in_specs=[pl.BlockSpec((MOD_ROWS, D_MODEL), lambda j: (0, 0)),
                  pl.BlockSpec((D_MODEL, tn), lambda j: (0, j)),
                  pl.BlockSpec((1, tn), lambda j: (0, j))],
        out_specs=pl.BlockSpec((MOD_ROWS, tn), lambda j: (0, j)),
        out_shape=jax.ShapeDtypeStruct((MOD_ROWS, n_out), F32),
        compiler_params=_cparams(("arbitrary",)),
        name="modulation",
    )(cvec, w_mod, b_mod)


def _pre_kernel(*refs, rope):
    if rope:
        (x_ref, mod_ref, g1_ref, win_ref, heads_ref, heads_t_ref, gqk_ref, cos_ref, sin_ref,
         q_ref, kx_ref, vx_ref, kf_ref, vf_ref, xr_ref, gb_ref) = refs
    else:
        (x_ref, mod_ref, g1_ref, win_ref, heads_ref, heads_t_ref, gqk_ref,
         q_ref, kx_ref, vx_ref, kf_ref, vf_ref, xr_ref, gb_ref) = refs
    x = x_ref[...]
    m = mod_ref[0]
    ms = jnp.mean(x * x, axis=-1, keepdims=True)
    y = x * lax.rsqrt(ms + EPS) * g1_ref[...]
    h = y * (1.0 + m[1:2]) + m[0:1]
    z = _dot(h.astype(BF16), win_ref[...])

    qk = z[:, :QK_W]
    ss = _dot((qk * qk).astype(BF16), heads_ref[...])
    hi, lo = _split_bf16(lax.rsqrt(ss * (1.0 / HEAD_DIM) + EPS))
    qk = qk * (_dot(hi, heads_t_ref[...]) + _dot(lo, heads_t_ref[...])) * gqk_ref[...]

    lane = lax.broadcasted_iota(jnp.int32, (x.shape[0], LANES), 1)
    cols = []
    for c in range(QK_W // LANES):
        xc = qk[:, c * LANES:(c + 1) * LANES]
        if rope:
            left = pltpu.roll(xc, LANES - HEAD_DIM // 4, 1)
            right = pltpu.roll(xc, HEAD_DIM // 4, 1)
            rot = jnp.where((lane // (HEAD_DIM // 4)) % 2 == 0, left, right)
            xc = xc * cos_ref[...] + rot * sin_ref[...]
        cols.append(xc)
    for c in range(ATTN_W // LANES):
        q_ref[:, c * LANES:(c + 1) * LANES] = (cols[c] * Q_SCALE).astype(BF16)

    lo_half = lane < HEAD_DIM
    for col, fref, xref, one in ((cols[ATTN_W // LANES], kf_ref, kx_ref, 0.0),
                                 (z[:, QK_W:QK_W + KV_W], vf_ref, vx_ref, 1.0)):
        fref[...] = col
        swapped = pltpu.roll(col, HEAD_DIM, 1)
        fill_hi = jnp.where(lane == ONES_LANE_EVEN, one, 0.0)
        fill_lo = jnp.where(lane == ONES_LANE_ODD, one, 0.0)
        xref[:, 0 * LANES:1 * LANES] = jnp.where(lo_half, col, fill_hi).astype(BF16)
        xref[:, 1 * LANES:2 * LANES] = jnp.where(lo_half, fill_lo, swapped).astype(BF16)
        xref[:, 2 * LANES:3 * LANES] = jnp.where(lo_half, swapped, fill_hi).astype(BF16)
        xref[:, 3 * LANES:4 * LANES] = jnp.where(lo_half, fill_lo, col).astype(BF16)

    xr_ref[...] = z[:, QK_W + KV_W:QK_W + KV_W + LRU_W]
    gb_ref[...] = z[:, QK_W + KV_W + LRU_W:]


def _pre(x, mod3, mod_row, g1, w_in, heads, gqk, tables, seq_len):
    n = x.shape[0]
    tm = TM_PRE
    tiles_per_seq = seq_len // tm
    rope = tables is not None
    const = lambda i: (0, 0)
    in_specs = [pl.BlockSpec((tm, D_MODEL), lambda i: (i, 0)),
                pl.BlockSpec((1, 6, D_MODEL), lambda i: (mod_row(i * tm), 0, 0)),
                pl.BlockSpec((1, D_MODEL), const),
                pl.BlockSpec((D_MODEL, IN_W), const),
                pl.BlockSpec((QK_W, LANES), const),
                pl.BlockSpec((LANES, QK_W), const),
                pl.BlockSpec((1, QK_W), const)]
    args = [x, mod3, g1, w_in, heads, heads.T, gqk]
    if rope:
        in_specs += [pl.BlockSpec((tm, LANES), lambda i: (i % tiles_per_seq, 0))] * 2
        args += list(tables)
    row = lambda w: pl.BlockSpec((tm, w), lambda i: (i, 0))
    out_shape = [jax.ShapeDtypeStruct((n, ATTN_W), BF16),
                 jax.ShapeDtypeStruct((n, 4 * LANES), BF16),
                 jax.ShapeDtypeStruct((n, 4 * LANES), BF16),
                 jax.ShapeDtypeStruct((n, KV_W), F32),
                 jax.ShapeDtypeStruct((n, KV_W), F32),
                 jax.ShapeDtypeStruct((n, LRU_W), F32),
                 jax.ShapeDtypeStruct((n, LRU_W), F32)]
    out_specs = [row(ATTN_W), row(4 * LANES), row(4 * LANES), row(KV_W), row(KV_W),
                 row(LRU_W), row(LRU_W)]
    return pl.pallas_call(
        functools.partial(_pre_kernel, rope=rope),
        grid=(n // tm,),
        in_specs=in_specs, out_specs=out_specs, out_shape=out_shape,
        compiler_params=_cparams(("arbitrary",)),
        name="pre_rope" if rope else "pre",
    )(*args)


def _attn_kernel(*refs, n_seg):
    q_ref = refs[0]
    k_refs = refs[1:1 + n_seg]
    v_refs = refs[1 + n_seg:1 + 2 * n_seg]
    o_ref = refs[1 + 2 * n_seg]
    lane = lax.broadcasted_iota(jnp.int32, (q_ref.shape[0], LANES), 1)
    for c in range(ATTN_W // LANES):
        qc = q_ref[:, c * LANES:(c + 1) * LANES]
        g = c // 2
        accs = []
        for par in range(2):
            sl = slice((2 * g + par) * LANES, (2 * g + par + 1) * LANES)
            ss = [_dot_nt(qc, k[:, sl]) for k in k_refs]
            mx = functools.reduce(jnp.maximum, [jnp.max(s, axis=-1, keepdims=True) for s in ss])
            ps = [jnp.exp2((s - mx).astype(BF16)) for s in ss]
            accs.append(functools.reduce(lambda a, b: a + b,
                                         [_dot(p, v[:, sl]) for p, v in zip(ps, v_refs)]))
        even = accs[0] / accs[0][:, ONES_LANE_EVEN:ONES_LANE_EVEN + 1]
        odd = accs[1] / accs[1][:, ONES_LANE_ODD:ONES_LANE_ODD + 1]
        o_ref[:, c * LANES:(c + 1) * LANES] = jnp.where(lane < HEAD_DIM, even, odd).astype(BF16)


def _attention(q, k_segs, v_segs, seq_len):
    n = q.shape[0]
    tq = min(TQ_ATT, seq_len)
    nq = seq_len // tq
    n_seg = len(k_segs)
    in_specs = [pl.BlockSpec((tq, ATTN_W), lambda b, i: (b * nq + i, 0))]
    for arr, t in list(k_segs) + list(v_segs):
        in_specs.append(pl.BlockSpec((t, 4 * LANES), lambda b, i: (b, 0)))
    return pl.pallas_call(
        functools.partial(_attn_kernel, n_seg=n_seg),
        grid=(n // seq_len, nq),
        in_specs=in_specs,
        out_specs=pl.BlockSpec((tq, ATTN_W), lambda b, i: (b * nq + i, 0)),
        out_shape=jax.ShapeDtypeStruct((n, ATTN_W), BF16),
        compiler_params=_cparams(("arbitrary", "arbitrary")),
        name="attention_%dseg" % n_seg,
    )(q, *[a for a, _ in k_segs], *[a for a, _ in v_segs])


def _log_sigmoid(x):
    return jnp.minimum(x, 0.0) - jnp.log1p(jnp.exp(-jnp.abs(x)))


def _tile_scan(a, b, reverse):
    row = lax.broadcasted_iota(jnp.int32, a.shape, 0)
    d = 1
    while d < SUBLANES:
        if reverse:
            keep = row < SUBLANES - d
            shift = SUBLANES - d
        else:
            keep = row >= d
            shift = d
        a_sh = jnp.where(keep, pltpu.roll(a, shift, 0), 1.0)
        b_sh = jnp.where(keep, pltpu.roll(b, shift, 0), 0.0)
        b = a * b_sh + b
        a = a * a_sh
        d *= 2
    return a, b


def _lru_kernel(xr_ref, gb_ref, h0_ref, cw_ref, cb_ref, wf_ref, wb_ref, bf_ref, bb_ref, lam_ref,
                rec_ref, fin_ref, xpad_s, xc_s, hf_s, a_s, b_s, *, seq_len):
    tc = TC_LRU
    n_chunks = seq_len // tc
    n_tiles = tc // SUBLANES
    zpad = jnp.zeros((SUBLANES, LRU_W), F32)
    xpad_s[0:SUBLANES, :] = zpad
    xpad_s[SUBLANES:SUBLANES + seq_len, :] = xr_ref[...]
    xpad_s[SUBLANES + seq_len:2 * SUBLANES + seq_len, :] = zpad

    cl = LRU_C * _log_sigmoid(lam_ref[...])

    def gates(xcc, w_ref, bias_ref, cl_d):
        g = _dot(xcc.astype(BF16), w_ref[...]) + bias_ref[...]
        r = _sigmoid(g[:, :LRU_W])
        i = _sigmoid(g[:, LRU_W:])
        log_a = r * cl_d
        a = jnp.exp(log_a)
        a_s[...] = a
        b_s[...] = jnp.sqrt(1.0 - a * a) * i * xcc

    h = h0_ref[0, 0:1, :]
    for c in range(n_chunks):
        base = c * tc
        xcc = cb_ref[...] + functools.reduce(
            lambda u, v: u + v,
            [cw_ref[j:j + 1, :] * xpad_s[base + SUBLANES - 1 + j:base + SUBLANES - 1 + j + tc, :]
             for j in range(CONV_W)])
        xc_s[base:base + tc, :] = xcc
        gates(xcc, wf_ref, bf_ref, cl[0:1])

        def fwd_tile(t, hc, base=base):
            r0 = pl.multiple_of(t * SUBLANES, SUBLANES)
            ca, cb = _tile_scan(a_s[pl.ds(r0, SUBLANES), :], b_s[pl.ds(r0, SUBLANES), :], False)
            hh = ca * hc + cb
            hf_s[pl.ds(base + r0, SUBLANES), :] = hh
            return hh[SUBLANES - 1:SUBLANES, :]

        h = lax.fori_loop(0, n_tiles, fwd_tile, h)
    fin_ref[0, 0:1, :] = h

    h = h0_ref[0, 1:2, :]
    for c in reversed(range(n_chunks)):
        base = c * tc
        gates(xc_s[base:base + tc, :], wb_ref, bb_ref, cl[1:2])

        def bwd_tile(t, hc, base=base):
            r0 = pl.multiple_of((n_tiles - 1 - t) * SUBLANES, SUBLANES)
            ca, cb = _tile_scan(a_s[pl.ds(r0, SUBLANES), :], b_s[pl.ds(r0, SUBLANES), :], True)
            hh = ca * hc + cb
            gate = jax.nn.gelu(gb_ref[pl.ds(base + r0, SUBLANES), :], approximate=True)
            rec_ref[pl.ds(base + r0, SUBLANES), :] = (
                (hf_s[pl.ds(base + r0, SUBLANES), :] + hh) * gate).astype(rec_ref.dtype)
            return hh[0:1, :]

        h = lax.fori_loop(0, n_tiles, bwd_tile, h)
    fin_ref[0, 1:2, :] = h


def _lru(xr, gb, h0, conv_w, conv_b, wf, wb, bf, bb, lam, seq_len):
    n = xr.shape[0]
    batch = n // seq_len
    const = lambda b: (0, 0)
    seq = pl.BlockSpec((seq_len, LRU_W), lambda b: (b, 0))
    st = pl.BlockSpec((1, 2, LRU_W), lambda b: (b, 0, 0))
    return pl.pallas_call(
        functools.partial(_lru_kernel, seq_len=seq_len),
        grid=(batch,),
        in_specs=[seq, seq, st,
                  pl.BlockSpec((CONV_W, LRU_W), const), pl.BlockSpec((1, LRU_W), const),
                  pl.BlockSpec((LRU_W, 2 * LRU_W), const), pl.BlockSpec((LRU_W, 2 * LRU_W), const),
                  pl.BlockSpec((1, 2 * LRU_W), const), pl.BlockSpec((1, 2 * LRU_W), const),
                  pl.BlockSpec((2, LRU_W), const)],
        out_specs=[seq, st],
        out_shape=[jax.ShapeDtypeStruct((n, LRU_W), BF16),
                   jax.ShapeDtypeStruct((batch, 2, LRU_W), F32)],
        scratch_shapes=[pltpu.VMEM((seq_len + 2 * SUBLANES, LRU_W), F32),
                        pltpu.VMEM((seq_len, LRU_W), F32),
                        pltpu.VMEM((seq_len, LRU_W), F32),
                        pltpu.VMEM((TC_LRU, LRU_W), F32),
                        pltpu.VMEM((TC_LRU, LRU_W), F32)],
        compiler_params=_cparams(("arbitrary",)),
        name="lru_%d" % seq_len,
    )(xr, gb, h0, conv_w, conv_b, wf, wb, bf, bb, lam)


def _post_kernel(attn_ref, rec_ref, x_ref, mod_ref, g2_ref, wo_ref, wr2_ref, br_ref,
                 tri_ref, cnt_in_ref, x1_ref, h2_ref, route_ref, cnt_ref, cnt_s):
    @pl.when(pl.program_id(0) == 0)
    def _():
        cnt_s[...] = cnt_in_ref[...]

    m = mod_ref[0]
    u = _dot(attn_ref[...], wo_ref[:ATTN_W, :]) + _dot(rec_ref[...], wo_ref[ATTN_W:, :])
    x1 = x_ref[...] + m[2:3] * u
    x1_ref[...] = x1
    ms = jnp.mean(x1 * x1, axis=-1, keepdims=True)
    h2 = x1 * lax.rsqrt(ms + EPS) * g2_ref[...]
    h2 = h2 * (1.0 + m[4:5]) + m[3:4]
    hi, lo = _split_bf16(h2)
    h2_ref[...] = _to_row_tiles(h2)

    hw = _dot(hi, wr2_ref[...])
    logits = hw[:, :LANES] + hw[:, LANES:] + _dot(lo, wr2_ref[:, :LANES]) + br_ref[...]
    lane_i = lax.broadcasted_iota(jnp.int32, logits.shape, 1)
    lane = lane_i.astype(F32)
    lane_group = (lane_i // EXPERTS_PER_GROUP).astype(F32)
    neg = -jnp.inf
    big = float(1 << 20)
    gmask = (lane_i >= N_EXPERTS) & (lane_i < N_EXPERTS + N_GROUPS)
    gl = jnp.where(gmask, logits, neg)
    gmax = jnp.max(gl, axis=-1, keepdims=True)
    gidx = jnp.min(jnp.where(gl == gmax, lane - N_EXPERTS, big), axis=-1, keepdims=True)
    p_sel = 1.0 / jnp.sum(jnp.where(gmask, jnp.exp(gl - gmax), 0.0), axis=-1, keepdims=True)

    emask = (lane_i < N_EXPERTS) & (lane_group == gidx)
    el = jnp.where(emask, logits, neg)
    v1 = jnp.max(el, axis=-1, keepdims=True)
    i1 = jnp.min(jnp.where(el == v1, lane, big), axis=-1, keepdims=True)
    el2 = jnp.where(lane == i1, neg, el)
    v2 = jnp.max(el2, axis=-1, keepdims=True)
    i2 = jnp.min(jnp.where(el2 == v2, lane, big), axis=-1, keepdims=True)
    e2 = jnp.exp(v2 - v1)
    w1 = p_sel / (1.0 + e2)
    w2 = p_sel * e2 / (1.0 + e2)

    oh1 = lane == i1
    oh2 = lane == i2
    oh = jnp.where(oh1, 1.0, 0.0) + jnp.where(oh2, 1.0, 0.0)
    before = _dot(tri_ref[...], oh.astype(BF16)) + cnt_s[...]
    rank1 = jnp.sum(jnp.where(oh1, before, 0.0), axis=-1, keepdims=True)
    rank2 = jnp.sum(jnp.where(oh2, before, 0.0), axis=-1, keepdims=True)
    cnt = cnt_s[...] + jnp.sum(oh, axis=0, keepdims=True)
    cnt_s[...] = cnt
    cnt_ref[...] = cnt
    fields = (i1, i2, rank1, rank2, w1, w2)
    route = jnp.zeros(logits.shape, F32)
    for k, val in enumerate(fields):
        route = jnp.where(lane_i == k, val, route)
    route_ref[...] = route


def _post(attn, rec, x, mod3, mod_row, g2, w_out, wr2, br, tri, cnt_in):
    n = x.shape[0]
    tm = TM_PRE
    const = lambda i: (0, 0)
    row = lambda w: pl.BlockSpec((tm, w), lambda i: (i, 0))
    return pl.pallas_call(
        _post_kernel,
        grid=(n // tm,),
        in_specs=[row(ATTN_W), row(LRU_W), row(D_MODEL),
                  pl.BlockSpec((1, 6, D_MODEL), lambda i: (mod_row(i * tm), 0, 0)),
                  pl.BlockSpec((1, D_MODEL), const),
                  pl.BlockSpec((D_MODEL, D_MODEL), const),
                  pl.BlockSpec((D_MODEL, 2 * LANES), const),
                  pl.BlockSpec((1, LANES), const),
                  pl.BlockSpec((tm, tm), const),
                  pl.BlockSpec((1, LANES), const)],
        out_specs=[row(D_MODEL), pl.BlockSpec((tm,) + ROW_TILE, lambda i: (i, 0, 0)), row(LANES),
                   pl.BlockSpec((1, LANES), const)],
        out_shape=[jax.ShapeDtypeStruct((n, D_MODEL), F32),
                   jax.ShapeDtypeStruct((n,) + ROW_TILE, F32),
                   jax.ShapeDtypeStruct((n, LANES), F32),
                   jax.ShapeDtypeStruct((1, LANES), F32)],
        scratch_shapes=[pltpu.VMEM((1, LANES), F32)],
        compiler_params=_cparams(("arbitrary",)),
        name="post",
    )(attn, rec, x, mod3, g2, w_out, wr2, br, tri, cnt_in)


def _row_copy(src_ref, src_row, dst_ref, dst_row, sem):
    return pltpu.make_async_copy(src_ref.at[pl.ds(src_row, 1)], dst_ref.at[pl.ds(dst_row, 1)], sem)


def _token_slots(idx_ref, seg_ref, n):
    base = 2 * TOP_K * n
    return (seg_ref[idx_ref[base]] + idx_ref[base + 2], seg_ref[idx_ref[base + 1]] + idx_ref[base + 3])


def _dispatch_kernel(idx_ref, seg_ref, zfill_ref, nu_ref, h2p_ref, h2s_ref, xs_ref, zero_s, sem,
                     *, n_prompt):
    i = pl.program_id(0)
    ch = CH_DISPATCH

    @pl.when(i == 0)
    def _():
        zero_s[...] = jnp.zeros_like(zero_s)

        def zero_tile(t):
            return pltpu.make_async_copy(
                zero_s, xs_ref.at[pl.ds(pl.multiple_of(t * TR_MOE, TR_MOE), TR_MOE)], sem)

        for e in range(N_EXPERTS):
            @pl.when(zfill_ref[e] >= 0)
            def _():
                zero_tile(zfill_ref[e]).start()
        n_all = xs_ref.shape[0] // TR_MOE
        lax.fori_loop(nu_ref[0], n_all, lambda t, c: (zero_tile(t).start(), c)[1], 0)
        for e in range(N_EXPERTS):
            @pl.when(zfill_ref[e] >= 0)
            def _():
                zero_tile(0).wait()
        lax.fori_loop(nu_ref[0], n_all, lambda t, c: (zero_tile(0).wait(), c)[1], 0)

    def scatter(src_ref):
        def body(j, carry):
            s1, s2 = _token_slots(idx_ref, seg_ref, i * ch + j)
            _row_copy(src_ref, j, xs_ref, s1, sem).start()
            _row_copy(src_ref, j, xs_ref, s2, sem).start()
            return carry
        lax.fori_loop(0, ch, body, 0, unroll=8)
        for _ in range(TOP_K):
            pltpu.make_async_copy(src_ref, xs_ref.at[pl.ds(0, ch)], sem).wait()

    @pl.when(i < n_prompt // ch)
    def _():
        scatter(h2p_ref)

    @pl.when(i >= n_prompt // ch)
    def _():
        scatter(h2s_ref)


def _dispatch(idx, seg_start, zfill, n_used, h2p, h2s, n_rows):
    n_prompt = h2p.shape[0]
    n = n_prompt + h2s.shape[0]
    ch = CH_DISPATCH
    npc = n_prompt // ch
    last_p = npc - 1
    return pl.pallas_call(
        functools.partial(_dispatch_kernel, n_prompt=n_prompt),
        grid_spec=pltpu.PrefetchScalarGridSpec(
            num_scalar_prefetch=4,
            grid=(n // ch,),
            in_specs=[pl.BlockSpec((ch,) + ROW_TILE, lambda i, *_: (jnp.minimum(i, last_p), 0, 0)),
                      pl.BlockSpec((ch,) + ROW_TILE, lambda i, *_: (jnp.maximum(i - npc, 0), 0, 0))],
            out_specs=pl.BlockSpec(memory_space=pl.ANY),
            scratch_shapes=[pltpu.VMEM((TR_MOE,) + ROW_TILE, F32), pltpu.SemaphoreType.DMA]),
        out_shape=jax.ShapeDtypeStruct((n_rows,) + ROW_TILE, F32),
        compiler_params=_cparams(("arbitrary",)),
        name="dispatch",
    )(idx, seg_start, zfill, n_used, h2p, h2s)


def _ffn_kernel(tend_ref, xs_ref, wg_ref, wu_ref, wd_ref, ys_ref,
                xbuf, ybuf, wg_s, wu_s, wd_s, sem_in, sem_out):
    e = pl.program_id(0)
    n_used = tend_ref[N_EXPERTS - 1]
    t_first = jnp.where(e == 0, 0, tend_ref[jnp.maximum(e - 1, 0)])
    t_last = tend_ref[e]
    n_in = xbuf.shape[0]
    n_out = ybuf.shape[0]

    def tile_rows(t):
        return pl.ds(pl.multiple_of(t * TR_MOE, TR_MOE), TR_MOE)

    def fetch(t):
        return pltpu.make_async_copy(xs_ref.at[tile_rows(t)], xbuf.at[t % n_in], sem_in.at[t % n_in])

    def writeback(t):
        return pltpu.make_async_copy(ybuf.at[t % n_out], ys_ref.at[tile_rows(t)],
                                     sem_out.at[t % n_out])

    @pl.when(e == 0)
    def _():
        for t in range(n_in - 1):
            @pl.when(t < n_used)
            def _():
                fetch(t).start()

    @pl.when(t_last > t_first)
    def _():
        wg_s[...] = wg_ref[0].astype(BF16)
        wu_s[...] = wu_ref[0].astype(BF16)
        wd_s[...] = wd_ref[0].astype(BF16)

    def tile(t, carry):
        fetch(t).wait()

        @pl.when(t + n_in - 1 < n_used)
        def _():
            fetch(t + n_in - 1).start()

        @pl.when(t >= n_out)
        def _():
            writeback(t - n_out).wait()

        x = _from_row_tiles(xbuf[t % n_in]).astype(BF16)
        hg = _dot(x, wg_s[...])
        hu = _dot(x, wu_s[...])
        act = (hg * _sigmoid(hg)) * hu
        ybuf[t % n_out] = _to_row_tiles(_dot(act.astype(BF16), wd_s[...]))
        writeback(t).start()
        return carry

    lax.fori_loop(t_first, t_last, tile, 0)

    @pl.when(e == N_EXPERTS - 1)
    def _():
        for back in range(n_out, 0, -1):
            @pl.when(n_used >= back)
            def _():
                writeback(n_used - back).wait()
        n_all = ys_ref.shape[0] // TR_MOE
        ybuf[0] = jnp.zeros(ybuf.shape[1:], F32)

        def zero_tile(t):
            return pltpu.make_async_copy(ybuf.at[0], ys_ref.at[tile_rows(t)], sem_out.at[0])

        lax.fori_loop(n_used, n_all, lambda t, c: (zero_tile(t).start(), c)[1], 0)
        lax.fori_loop(n_used, n_all, lambda t, c: (zero_tile(t).wait(), c)[1], 0)


def _ffn(tile_end, xs, wg, wu, wd, n_tiles):
    tr = TR_MOE
    wsel = lambda e, tend: (e, 0, 0)
    return pl.pallas_call(
        _ffn_kernel,
        grid_spec=pltpu.PrefetchScalarGridSpec(
            num_scalar_prefetch=1,
            grid=(N_EXPERTS,),
            in_specs=[pl.BlockSpec(memory_space=pl.ANY),
                      pl.BlockSpec((1, D_MODEL, EXPERT_FF), wsel),
                      pl.BlockSpec((1, D_MODEL, EXPERT_FF), wsel),
                      pl.BlockSpec((1, EXPERT_FF, D_MODEL), wsel)],
            out_specs=pl.BlockSpec(memory_space=pl.ANY),
            scratch_shapes=[pltpu.VMEM((FFN_IN_SLOTS, tr) + ROW_TILE, F32),
                            pltpu.VMEM((FFN_OUT_SLOTS, tr) + ROW_TILE, F32),
                            pltpu.VMEM((D_MODEL, EXPERT_FF), BF16),
                            pltpu.VMEM((D_MODEL, EXPERT_FF), BF16),
                            pltpu.VMEM((EXPERT_FF, D_MODEL), BF16),
                            pltpu.SemaphoreType.DMA((FFN_IN_SLOTS,)),
                            pltpu.SemaphoreType.DMA((FFN_OUT_SLOTS,))]),
        out_shape=jax.ShapeDtypeStruct((n_tiles * tr,) + ROW_TILE, F32),
        compiler_params=_cparams(("arbitrary",)),
        name="ffn",
    )(tile_end, xs, wg, wu, wd)


def _combine_kernel(idx_ref, seg_ref, ys_ref, x1_ref, route_ref, mod_ref, y_ref, b1_s, b2_s, sems):
    i = pl.program_id(0)
    tm = x1_ref.shape[0]

    def gather(step, slot):
        def body(j, carry):
            s1, s2 = _token_slots(idx_ref, seg_ref, step * tm + j)
            _row_copy(ys_ref, s1, b1_s.at[slot], j, sems.at[slot]).start()
            _row_copy(ys_ref, s2, b2_s.at[slot], j, sems.at[slot]).start()
            return carry
        lax.fori_loop(0, tm, body, 0, unroll=8)

    @pl.when(i == 0)
    def _():
        gather(0, 0)

    @pl.when(i + 1 < pl.num_programs(0))
    def _():
        gather(i + 1, (i + 1) % 2)

    slot = i % 2
    for buf in (b1_s, b2_s):
        pltpu.make_async_copy(ys_ref.at[pl.ds(0, tm)], buf.at[slot], sems.at[slot]).wait()
    route = route_ref[...]
    moe = route[:, 4:5] * _from_row_tiles(b1_s[slot]) + route[:, 5:6] * _from_row_tiles(b2_s[slot])
    y_ref[...] = x1_ref[...] + mod_ref[0][5:6] * moe


def _combine(idx, seg_start, ys, x1, route, mod3, mod_row, seq_len):
    n = x1.shape[0]
    tm = TM_PRE
    tiles_per_seq = seq_len // tm
    row = lambda w: pl.BlockSpec((tm, w), lambda i, *_: (i, 0))
    return pl.pallas_call(
        _combine_kernel,
        grid_spec=pltpu.PrefetchScalarGridSpec(
            num_scalar_prefetch=2,
            grid=(n // tm,),
            in_specs=[pl.BlockSpec(memory_space=pl.ANY), row(D_MODEL), row(LANES),
                      pl.BlockSpec((1, 6, D_MODEL),
                                   lambda i, *_: (mod_row(i * tm), 0, 0))],
            out_specs=row(D_MODEL),
            scratch_shapes=[pltpu.VMEM((2, tm) + ROW_TILE, F32), pltpu.VMEM((2, tm) + ROW_TILE, F32),
                            pltpu.SemaphoreType.DMA((2,))]),
        out_shape=jax.ShapeDtypeStruct((n, D_MODEL), F32),
        compiler_params=_cparams(("arbitrary",)),
        name="combine",
    )(idx, seg_start, ys, x1, route, mod3)


def _rope_tables(length):
    rows = length // GRID_W
    r, col = jnp.meshgrid(jnp.arange(rows), jnp.arange(GRID_W), indexing='ij')
    r = r.reshape(-1).astype(F32)
    col = col.reshape(-1).astype(F32)
    half = HEAD_DIM // 2
    inv = ROPE_THETA ** (-jnp.arange(0, half, 2, dtype=F32) / half)
    ang_r = r[:, None] * inv
    ang_c = col[:, None] * inv
    ang = jnp.concatenate([ang_r, ang_r, ang_c, ang_c], axis=-1)
    sign = jnp.where((jnp.arange(HEAD_DIM) // (HEAD_DIM // 4)) % 2 == 0, -1.0, 1.0).astype(F32)
    cos = jnp.tile(jnp.cos(ang), (1, LANES // HEAD_DIM))
    sin = jnp.tile(jnp.sin(ang) * sign, (1, LANES // HEAD_DIM))
    return cos, sin


def _block_diag(w):
    eye = jnp.eye(LRU_BLOCKS, dtype=w.dtype)
    return jnp.einsum('hij,hg->higj', w, eye).reshape(LRU_W, LRU_W)


def _expand_heads(kv, one):
    b, t, _ = kv.shape
    h0 = kv[..., :HEAD_DIM]
    h1 = kv[..., HEAD_DIM:]
    lane = jnp.arange(HEAD_DIM)
    fill_hi = jnp.broadcast_to(jnp.where(lane == ONES_LANE_EVEN - HEAD_DIM, one, 0.0).astype(F32), h0.shape)
    fill_lo = jnp.broadcast_to(jnp.where(lane == ONES_LANE_ODD, one, 0.0).astype(F32), h0.shape)
    out = jnp.concatenate([h0, fill_hi, fill_lo, h0, h1, fill_hi, fill_lo, h1], axis=-1)
    return out.reshape(b * t, 4 * LANES).astype(BF16)


def kernel(x_prompt, x_sample, cache_k, cache_v, state_lru, c, c_ctx, w_mod, b_mod, norm1, norm2,
           w_in, q_norm, k_norm, conv_w, conv_b, lru_wa, lru_ba, lru_wx, lru_bx, lru_lambda, w_out,
           router_grp_w, router_grp_b, router_exp_w, router_exp_b, exp_w_gate, exp_w_up, exp_w_down):
    batch, seq, _ = x_prompt.shape
    dec_batch, dec_seq, _ = x_sample.shape
    past = cache_k.shape[2]
    depth = w_mod.shape[0]
    assert depth == 1

    cvec = jnp.concatenate(
        [c_ctx[None, :], c, jnp.zeros((MOD_ROWS - 1 - dec_batch, D_MODEL), F32)], axis=0)
    mod3 = _modulation(cvec, w_mod[0], b_mod[0][None, :]).reshape(MOD_ROWS, 6, D_MODEL)

    w_in_b = w_in[0].astype(BF16)
    w_out_b = w_out[0].astype(BF16)
    head_id = jnp.arange(QK_W) // HEAD_DIM
    heads = (head_id[:, None] == jnp.arange(LANES)[None, :]).astype(BF16)
    gqk = jnp.concatenate([jnp.tile(q_norm[0], N_HEADS), jnp.tile(k_norm[0], N_KV_HEADS)])[None, :]
    wf = jnp.concatenate([_block_diag(lru_wa[0, 0]), _block_diag(lru_wx[0, 0])], axis=1).astype(BF16)
    wb = jnp.concatenate([_block_diag(lru_wa[0, 1]), _block_diag(lru_wx[0, 1])], axis=1).astype(BF16)
    bf = jnp.concatenate([lru_ba[0, 0], lru_bx[0, 0]])[None, :]
    bb = jnp.concatenate([lru_ba[0, 1], lru_bx[0, 1]])[None, :]
    pad = LANES - N_EXPERTS - N_GROUPS
    wr = jnp.concatenate([router_exp_w[0], router_grp_w[0], jnp.zeros((D_MODEL, pad), F32)], axis=1)
    wr_hi = wr.astype(BF16)
    wr2 = jnp.concatenate([wr_hi, (wr - wr_hi.astype(F32)).astype(BF16)], axis=1)
    br = jnp.concatenate([router_exp_b[0], router_grp_b[0], jnp.zeros((pad,), F32)])[None, :]
    g1 = norm1[0][None, :]
    g2 = norm2[0][None, :]
    cw = conv_w[0]
    cb = conv_b[0][None, :]
    lam = lru_lambda[0]
    tri = (jnp.arange(TM_PRE)[:, None] > jnp.arange(TM_PRE)[None, :]).astype(BF16)

    def mixers(x, seq_len, mod_row, tables, extra_k, extra_v, h0, cnt_in):
        q, kx, vx, kf, vf, xr, gb = _pre(x, mod3, mod_row, g1, w_in_b, heads, gqk, tables, seq_len)
        k_segs = [(kx, seq_len)] + extra_k
        v_segs = [(vx, seq_len)] + extra_v
        attn = _attention(q, k_segs, v_segs, seq_len)
        rec, fin = _lru(xr, gb, h0, cw, cb, wf, wb, bf, bb, lam, seq_len)
        x1, h2, route, cnt = _post(attn, rec, x, mod3, mod_row, g2, w_out_b, wr2, br, tri, cnt_in)
        return x1, h2, route, cnt, kf, vf, fin

    mod_row_p = lambda tok: 0
    mod_row_s = lambda tok: tok // dec_seq + 1
    xp = x_prompt.reshape(batch * seq, D_MODEL)
    x1p, h2p, route_p, cnt_p, kf, vf, fin = mixers(
        xp, seq, mod_row_p, None, [], [], jnp.zeros((batch, 2, LRU_W), F32),
        jnp.zeros((1, LANES), F32))
    xs = x_sample.reshape(dec_batch * dec_seq, D_MODEL)
    ck = _expand_heads(cache_k[:, 0].reshape(dec_batch, past, KV_W), 0.0)
    cv = _expand_heads(cache_v[:, 0].reshape(dec_batch, past, KV_W), 1.0)
    x1s, h2s, route_s, cnt_all, _, _, _ = mixers(
        xs, dec_seq, mod_row_s, _rope_tables(dec_seq), [(ck, past)], [(cv, past)],
        state_lru[:, 0], cnt_p)

    n_prompt = batch * seq
    n_tok = n_prompt + dec_batch * dec_seq
    n_tiles = (TOP_K * n_tok + N_EXPERTS * (TR_MOE - 1)) // TR_MOE
    cnt = cnt_all[0, :N_EXPERTS].astype(jnp.int32)
    ntile = (cnt + TR_MOE - 1) // TR_MOE
    tile_end = jnp.cumsum(ntile)
    seg_start = (tile_end - ntile) * TR_MOE
    n_used = tile_end[-1:]
    zfill = jnp.where(ntile > 0, tile_end - 1, -1)

    idx_p = route_p[:, :2 * TOP_K].astype(jnp.int32).reshape(-1)
    idx_s = route_s[:, :2 * TOP_K].astype(jnp.int32).reshape(-1)
    xsort = _dispatch(jnp.concatenate([idx_p, idx_s]), seg_start, zfill, n_used, h2p, h2s,
                      n_tiles * TR_MOE)
    ysort = _ffn(tile_end, xsort, exp_w_gate[0], exp_w_up[0], exp_w_down[0], n_tiles)
    yp = _combine(idx_p, seg_start, ysort, x1p, route_p, mod3, mod_row_p, seq)
    ys = _combine(idx_s, seg_start, ysort, x1s, route_s, mod3, mod_row_s, dec_seq)

    return (yp.reshape(batch, seq, D_MODEL),
            ys.reshape(dec_batch, dec_seq, D_MODEL),
            kf.reshape(batch, 1, seq, N_KV_HEADS, HEAD_DIM),
            vf.reshape(batch, 1, seq, N_KV_HEADS, HEAD_DIM),
            fin.reshape(batch, 1, 2, LRU_W))
```

```python
import functools

import jax
import jax.numpy as jnp
from jax import lax
from jax.experimental import pallas as pl
from jax.experimental.pallas import tpu as pltpu

F32 = jnp.float32
BF16 = jnp.bfloat16

D_MODEL = 1024
GRID_W = 64
ATTN_W = 512
LRU_W = 512
HEAD_DIM = 64
N_HEADS = 8
N_KV_HEADS = 2
KV_W = N_KV_HEADS * HEAD_DIM
LRU_BLOCKS = 8
LRU_BLOCK_W = LRU_W // LRU_BLOCKS
CONV_W = 4
LRU_C = 8.0
IN_W = ATTN_W + 2 * KV_W + 2 * LRU_W
QK_W = ATTN_W + KV_W
N_GROUPS = 4
EXPERTS_PER_GROUP = 8
N_EXPERTS = N_GROUPS * EXPERTS_PER_GROUP
TOP_K = 2
EXPERT_FF = D_MODEL // 4
ROPE_THETA = 10000.0
EPS = 1e-6

LANES = 128
SUBLANES = 8
MOD_ROWS = 8
VMEM_LIMIT = 48 * 1024 * 1024

TM_PRE = 512
TQ_ATT = 1024
TC_LRU = 256
TR_MOE = 256
CH_DISPATCH = 1024
FFN_IN_SLOTS = 4
FFN_OUT_SLOTS = 3


def _cparams(sem):
    return pltpu.CompilerParams(dimension_semantics=sem, vmem_limit_bytes=VMEM_LIMIT)


def _dot(a, b):
    return jnp.dot(a, b, preferred_element_type=F32)


def _dot_nt(a, b):
    return lax.dot_general(a, b, (((1,), (1,)), ((), ())), preferred_element_type=F32)


ROW_TILE = (D_MODEL // LANES, LANES)
LOG2_E = 1.4426950408889634
Q_SCALE = HEAD_DIM ** -0.5 * LOG2_E
ONES_LANE_EVEN = HEAD_DIM
ONES_LANE_ODD = 0


def _to_row_tiles(x):
    cols = jnp.stack([x[:, c * LANES:(c + 1) * LANES] for c in range(D_MODEL // LANES)], axis=0)
    return jnp.swapaxes(cols, 0, 1)


def _from_row_tiles(x3):
    cols = jnp.swapaxes(x3, 0, 1)
    return jnp.concatenate([cols[c] for c in range(D_MODEL // LANES)], axis=1)


def _sigmoid(x):
    return 0.5 * jnp.tanh(0.5 * x) + 0.5


def _split_bf16(x):
    hi = x.astype(BF16)
    lo = (x - hi.astype(F32)).astype(BF16)
    return hi, lo


def _mod_kernel(c_ref, w_ref, b_ref, o_ref):
    c = c_ref[...]
    s = (c * jax.nn.sigmoid(c)).astype(BF16)
    o_ref[...] = _dot(s, w_ref[...].astype(BF16)) + b_ref[...]


def _modulation(cvec, w_mod, b_mod):
    n_out = w_mod.shape[1]
    tn = n_out // 4
    return pl.pallas_call(
        _mod_kernel,
        grid=(n_out // tn,),
        in_specs=[pl.BlockSpec((MOD_ROWS, D_MODEL), lambda j: (0, 0)),
                  pl.BlockSpec((D_MODEL, tn), lambda j: (0, j)),
                  pl.BlockSpec((1, tn), lambda j: (0, j))],
        out_specs=pl.BlockSpec((MOD_ROWS, tn), lambda j: (0, j)),
        out_shape=jax.ShapeDtypeStruct((MOD_ROWS, n_out), F32),
        compiler_params=_cparams(("arbitrary",)),
        name="modulation",
    )(cvec, w_mod, b_mod)


def _split_heads(col):
    return jnp.swapaxes(jnp.stack([col[:, h * HEAD_DIM:(h + 1) * HEAD_DIM]
                                   for h in range(N_KV_HEADS)], axis=0), 0, 1)


def _store_expanded(xref, col, one):
    lane = lax.broadcasted_iota(jnp.int32, col.shape, 1)
    lo_half = lane < HEAD_DIM
    swapped = pltpu.roll(col, HEAD_DIM, 1)
    fill_hi = jnp.where(lane == ONES_LANE_EVEN, one, 0.0)
    fill_lo = jnp.where(lane == ONES_LANE_ODD, one, 0.0)
    xref[:, 0 * LANES:1 * LANES] = jnp.where(lo_half, col, fill_hi).astype(BF16)
    xref[:, 1 * LANES:2 * LANES] = jnp.where(lo_half, fill_lo, swapped).astype(BF16)
    xref[:, 2 * LANES:3 * LANES] = jnp.where(lo_half, swapped, fill_hi).astype(BF16)
    xref[:, 3 * LANES:4 * LANES] = jnp.where(lo_half, fill_lo, col).astype(BF16)


def _pre_kernel(*refs, rope):
    if rope:
        (x_ref, mod_ref, g1_ref, win_ref, heads_ref, heads_t_ref, gqk_ref, cos_ref, sin_ref,
         q_ref, kx_ref, vx_ref, xr_ref, gb_ref, win_s) = refs
    else:
        (x_ref, mod_ref, g1_ref, win_ref, heads_ref, heads_t_ref, gqk_ref,
         q_ref, kx_ref, vx_ref, xr_ref, gb_ref, kf_ref, vf_ref, win_s) = refs

    @pl.when(pl.program_id(0) == 0)
    def _():
        win_s[...] = win_ref[...].astype(BF16)

    x = x_ref[...]
    m = mod_ref[0]
    ms = jnp.mean(x * x, axis=-1, keepdims=True)
    y = x * lax.rsqrt(ms + EPS) * g1_ref[...]
    h = y * (1.0 + m[1:2]) + m[0:1]
    z = _dot(h.astype(BF16), win_s[...])

    qk = z[:, :QK_W]
    ss = _dot((qk * qk).astype(BF16), heads_ref[...])
    hi, lo = _split_bf16(lax.rsqrt(ss * (1.0 / HEAD_DIM) + EPS))
    qk = qk * (_dot(hi, heads_t_ref[...]) + _dot(lo, heads_t_ref[...])) * gqk_ref[...]

    lane = lax.broadcasted_iota(jnp.int32, (x.shape[0], LANES), 1)
    cols = []
    for c in range(QK_W // LANES):
        xc = qk[:, c * LANES:(c + 1) * LANES]
        if rope:
            left = pltpu.roll(xc, LANES - HEAD_DIM // 4, 1)
            right = pltpu.roll(xc, HEAD_DIM // 4, 1)
            rot = jnp.where((lane // (HEAD_DIM // 4)) % 2 == 0, left, right)
            xc = xc * cos_ref[...] + rot * sin_ref[...]
        cols.append(xc)
    for c in range(ATTN_W // LANES):
        q_ref[:, c * LANES:(c + 1) * LANES] = (cols[c] * Q_SCALE).astype(BF16)

    k_col = cols[ATTN_W // LANES]
    v_col = z[:, QK_W:QK_W + KV_W]
    _store_expanded(kx_ref, k_col, 0.0)
    _store_expanded(vx_ref, v_col, 1.0)
    if not rope:
        kf_ref[...] = _split_heads(k_col)
        vf_ref[...] = _split_heads(v_col)

    xr_ref[...] = z[:, QK_W + KV_W:QK_W + KV_W + LRU_W]
    gb_ref[...] = z[:, QK_W + KV_W + LRU_W:]


def _pre(x, mod3, mod_row, g1, w_in, heads, gqk, tables, seq_len):
    n = x.shape[0]
    tm = TM_PRE
    tiles_per_seq = seq_len // tm
    rope = tables is not None
    const = lambda i: (0, 0)
    in_specs = [pl.BlockSpec((tm, D_MODEL), lambda i: (i, 0)),
                pl.BlockSpec((1, 6, D_MODEL), lambda i: (mod_row(i * tm), 0, 0)),
                pl.BlockSpec((1, D_MODEL), const),
                pl.BlockSpec((D_MODEL, IN_W), const),
                pl.BlockSpec((QK_W, LANES), const),
                pl.BlockSpec((LANES, QK_W), const),
                pl.BlockSpec((1, QK_W), const)]
    args = [x, mod3, g1, w_in, heads, heads.T, gqk]
    if rope:
        in_specs += [pl.BlockSpec((tm, LANES), lambda i: (i % tiles_per_seq, 0))] * 2
        args += list(tables)
    row = lambda w: pl.BlockSpec((tm, w), lambda i: (i, 0))
    out_shape = [jax.ShapeDtypeStruct((n, ATTN_W), BF16),
                 jax.ShapeDtypeStruct((n, 4 * LANES), BF16),
                 jax.ShapeDtypeStruct((n, 4 * LANES), BF16),
                 jax.ShapeDtypeStruct((n, LRU_W), F32),
                 jax.ShapeDtypeStruct((n, LRU_W), F32)]
    out_specs = [row(ATTN_W), row(4 * LANES), row(4 * LANES), row(LRU_W), row(LRU_W)]
    if not rope:
        cache = pl.BlockSpec((tm, N_KV_HEADS, HEAD_DIM), lambda i: (i, 0, 0))
        out_shape += [jax.ShapeDtypeStruct((n, N_KV_HEADS, HEAD_DIM), F32)] * 2
        out_specs += [cache, cache]
    return pl.pallas_call(
        functools.partial(_pre_kernel, rope=rope),
        grid=(n // tm,),
        in_specs=in_specs, out_specs=out_specs, out_shape=out_shape,
        scratch_shapes=[pltpu.VMEM((D_MODEL, IN_W), BF16)],
        compiler_params=_cparams(("arbitrary",)),
        name="pre_rope" if rope else "pre",
    )(*args)


def _attn_kernel(*refs, n_seg):
    q_ref = refs[0]
    k_refs = refs[1:1 + n_seg]
    v_refs = refs[1 + n_seg:1 + 2 * n_seg]
    o_ref = refs[1 + 2 * n_seg]
    lane = lax.broadcasted_iota(jnp.int32, (q_ref.shape[0], LANES), 1)
    for c in range(ATTN_W // LANES):
        qc = q_ref[:, c * LANES:(c + 1) * LANES]
        g = c // 2
        accs = []
        for par in range(2):
            sl = slice((2 * g + par) * LANES, (2 * g + par + 1) * LANES)
            ss = [_dot_nt(qc, k[:, sl]) for k in k_refs]
            mx = functools.reduce(jnp.maximum, [jnp.max(s, axis=-1, keepdims=True) for s in ss])
            ps = [jnp.exp2((s - mx).astype(BF16)) for s in ss]
            accs.append(functools.reduce(lambda a, b: a + b,
                                         [_dot(p, v[:, sl]) for p, v in zip(ps, v_refs)]))
        even = accs[0] / accs[0][:, ONES_LANE_EVEN:ONES_LANE_EVEN + 1]
        odd = accs[1] / accs[1][:, ONES_LANE_ODD:ONES_LANE_ODD + 1]
        o_ref[:, c * LANES:(c + 1) * LANES] = jnp.where(lane < HEAD_DIM, even, odd).astype(BF16)


def _attention(q, k_segs, v_segs, seq_len):
    n = q.shape[0]
    tq = min(TQ_ATT, seq_len)
    nq = seq_len // tq
    n_seg = len(k_segs)
    in_specs = [pl.BlockSpec((tq, ATTN_W), lambda b, i: (b * nq + i, 0))]
    for arr, t in list(k_segs) + list(v_segs):
        in_specs.append(pl.BlockSpec((t, 4 * LANES), lambda b, i: (b, 0)))
    return pl.pallas_call(
        functools.partial(_attn_kernel, n_seg=n_seg),
        grid=(n // seq_len, nq),
        in_specs=in_specs,
        out_specs=pl.BlockSpec((tq, ATTN_W), lambda b, i: (b * nq + i, 0)),
        out_shape=jax.ShapeDtypeStruct((n, ATTN_W), BF16),
        compiler_params=_cparams(("arbitrary", "arbitrary")),
        name="attention_%dseg" % n_seg,
    )(q, *[a for a, _ in k_segs], *[a for a, _ in v_segs])


def _log_sigmoid(x):
    return jnp.minimum(x, 0.0) - jnp.log1p(jnp.exp(-jnp.abs(x)))


def _tile_scan(a, b, reverse):
    row = lax.broadcasted_iota(jnp.int32, a.shape, 0)
    d = 1
    while d < SUBLANES:
        if reverse:
            keep = row < SUBLANES - d
            shift = SUBLANES - d
        else:
            keep = row >= d
            shift = d
        a_sh = jnp.where(keep, pltpu.roll(a, shift, 0), 1.0)
        b_sh = jnp.where(keep, pltpu.roll(b, shift, 0), 0.0)
        b = a * b_sh + b
        a = a * a_sh
        d *= 2
    return a, b


def _lru_kernel(xr_ref, gb_ref, h0_ref, cw_ref, cb_ref, wf_ref, wb_ref, bf_ref, bb_ref, lam_ref,
                rec_ref, fin_ref, xpad_s, xc_s, hf_s, a_s, b_s, *, seq_len):
    tc = TC_LRU
    n_chunks = seq_len // tc
    n_tiles = tc // SUBLANES
    zpad = jnp.zeros((SUBLANES, LRU_W), F32)
    xpad_s[0:SUBLANES, :] = zpad
    xpad_s[SUBLANES:SUBLANES + seq_len, :] = xr_ref[...]
    xpad_s[SUBLANES + seq_len:2 * SUBLANES + seq_len, :] = zpad

    half_cl = (0.5 * LRU_C * LOG2_E) * _log_sigmoid(lam_ref[...])

    def gates(xcc, w_ref, bias_ref, half_cl_d):
        t = jnp.tanh(_dot(xcc.astype(BF16), w_ref[...]) + bias_ref[...])
        a = jnp.exp2(t[:, :LRU_W] * half_cl_d + half_cl_d)
        half_x = 0.5 * xcc
        a_s[...] = a
        b_s[...] = jnp.sqrt(1.0 - a * a) * (t[:, LRU_W:] * half_x + half_x)

    h = h0_ref[0, 0:1, :]
    for c in range(n_chunks):
        base = c * tc
        xcc = cb_ref[...] + functools.reduce(
            lambda u, v: u + v,
            [cw_ref[j:j + 1, :] * xpad_s[base + SUBLANES - 1 + j:base + SUBLANES - 1 + j + tc, :]
             for j in range(CONV_W)])
        xc_s[base:base + tc, :] = xcc
        gates(xcc, wf_ref, bf_ref, half_cl[0:1])

        def fwd_tile(t, hc, base=base):
            r0 = pl.multiple_of(t * SUBLANES, SUBLANES)
            ca, cb = _tile_scan(a_s[pl.ds(r0, SUBLANES), :], b_s[pl.ds(r0, SUBLANES), :], False)
            hh = ca * hc + cb
            hf_s[pl.ds(base + r0, SUBLANES), :] = hh
            return hh[SUBLANES - 1:SUBLANES, :]

        h = lax.fori_loop(0, n_tiles, fwd_tile, h)
    fin_ref[0, 0:1, :] = h

    h = h0_ref[0, 1:2, :]
    for c in reversed(range(n_chunks)):
        base = c * tc
        gates(xc_s[base:base + tc, :], wb_ref, bb_ref, half_cl[1:2])

        def bwd_tile(t, hc, base=base):
            r0 = pl.multiple_of((n_tiles - 1 - t) * SUBLANES, SUBLANES)
            ca, cb = _tile_scan(a_s[pl.ds(r0, SUBLANES), :], b_s[pl.ds(r0, SUBLANES), :], True)
            hh = ca * hc + cb
            gate = jax.nn.gelu(gb_ref[pl.ds(base + r0, SUBLANES), :], approximate=True)
            rec_ref[pl.ds(base + r0, SUBLANES), :] = (
                (hf_s[pl.ds(base + r0, SUBLANES), :] + hh) * gate).astype(rec_ref.dtype)
            return hh[0:1, :]

        h = lax.fori_loop(0, n_tiles, bwd_tile, h)
    fin_ref[0, 1:2, :] = h


def _lru(xr, gb, h0, conv_w, conv_b, wf, wb, bf, bb, lam, seq_len):
    n = xr.shape[0]
    batch = n // seq_len
    const = lambda b: (0, 0)
    seq = pl.BlockSpec((seq_len, LRU_W), lambda b: (b, 0))
    st = pl.BlockSpec((1, 2, LRU_W), lambda b: (b, 0, 0))
    return pl.pallas_call(
        functools.partial(_lru_kernel, seq_len=seq_len),
        grid=(batch,),
        in_specs=[seq, seq, st,
                  pl.BlockSpec((CONV_W, LRU_W), const), pl.BlockSpec((1, LRU_W), const),
                  pl.BlockSpec((LRU_W, 2 * LRU_W), const), pl.BlockSpec((LRU_W, 2 * LRU_W), const),
                  pl.BlockSpec((1, 2 * LRU_W), const), pl.BlockSpec((1, 2 * LRU_W), const),
                  pl.BlockSpec((2, LRU_W), const)],
        out_specs=[seq, st],
        out_shape=[jax.ShapeDtypeStruct((n, LRU_W), BF16),
                   jax.ShapeDtypeStruct((batch, 2, LRU_W), F32)],
        scratch_shapes=[pltpu.VMEM((seq_len + 2 * SUBLANES, LRU_W), F32),
                        pltpu.VMEM((seq_len, LRU_W), F32),
                        pltpu.VMEM((seq_len, LRU_W), F32),
                        pltpu.VMEM((TC_LRU, LRU_W), F32),
                        pltpu.VMEM((TC_LRU, LRU_W), F32)],
        compiler_params=_cparams(("arbitrary",)),
        name="lru_%d" % seq_len,
    )(xr, gb, h0, conv_w, conv_b, wf, wb, bf, bb, lam)


def _post_kernel(attn_ref, rec_ref, x_ref, mod_ref, g2_ref, wo_ref, wr2_ref, br_ref,
                 tri_ref, cnt_in_ref, x1_ref, h2_ref, route_ref, cnt_ref, cnt_s, wo_s):
    @pl.when(pl.program_id(0) == 0)
    def _():
        cnt_s[...] = cnt_in_ref[...]
        wo_s[...] = wo_ref[...].astype(BF16)

    m = mod_ref[0]
    u = _dot(attn_ref[...], wo_s[:ATTN_W, :]) + _dot(rec_ref[...], wo_s[ATTN_W:, :])
    x1 = x_ref[...] + m[2:3] * u
    x1_ref[...] = x1
    ms = jnp.mean(x1 * x1, axis=-1, keepdims=True)
    h2 = x1 * lax.rsqrt(ms + EPS) * g2_ref[...]
    h2 = h2 * (1.0 + m[4:5]) + m[3:4]
    hi, lo = _split_bf16(h2)
    h2_ref[...] = _to_row_tiles(h2)

    hw = _dot(hi, wr2_ref[...])
    logits = hw[:, :LANES] + hw[:, LANES:] + _dot(lo, wr2_ref[:, :LANES]) + br_ref[...]
    lane_i = lax.broadcasted_iota(jnp.int32, logits.shape, 1)
    lane = lane_i.astype(F32)
    lane_group = (lane_i // EXPERTS_PER_GROUP).astype(F32)
    neg = -jnp.inf
    big = float(1 << 20)
    gmask = (lane_i >= N_EXPERTS) & (lane_i < N_EXPERTS + N_GROUPS)
    gl = jnp.where(gmask, logits, neg)
    gmax = jnp.max(gl, axis=-1, keepdims=True)
    gidx = jnp.min(jnp.where(gl == gmax, lane - N_EXPERTS, big), axis=-1, keepdims=True)
    p_sel = 1.0 / jnp.sum(jnp.where(gmask, jnp.exp(gl - gmax), 0.0), axis=-1, keepdims=True)

    emask = (lane_i < N_EXPERTS) & (lane_group == gidx)
    el = jnp.where(emask, logits, neg)
    v1 = jnp.max(el, axis=-1, keepdims=True)
    i1 = jnp.min(jnp.where(el == v1, lane, big), axis=-1, keepdims=True)
    el2 = jnp.where(lane == i1, neg, el)
    v2 = jnp.max(el2, axis=-1, keepdims=True)
    i2 = jnp.min(jnp.where(el2 == v2, lane, big), axis=-1, keepdims=True)
    e2 = jnp.exp(v2 - v1)
    w1 = p_sel / (1.0 + e2)
    w2 = p_sel * e2 / (1.0 + e2)

    oh1 = lane == i1
    oh2 = lane == i2
    oh = jnp.where(oh1, 1.0, 0.0) + jnp.where(oh2, 1.0, 0.0)
    before = _dot(tri_ref[...], oh.astype(BF16)) + cnt_s[...]
    rank1 = jnp.sum(jnp.where(oh1, before, 0.0), axis=-1, keepdims=True)
    rank2 = jnp.sum(jnp.where(oh2, before, 0.0), axis=-1, keepdims=True)
    cnt = cnt_s[...] + jnp.sum(oh, axis=0, keepdims=True)
    cnt_s[...] = cnt
    cnt_ref[...] = cnt
    fields = (i1, i2, rank1, rank2, w1, w2)
    route = jnp.zeros(logits.shape, F32)
    for k, val in enumerate(fields):
        route = jnp.where(lane_i == k, val, route)
    route_ref[...] = route


def _post(attn, rec, x, mod3, mod_row, g2, w_out, wr2, br, tri, cnt_in):
    n = x.shape[0]
    tm = TM_PRE
    const = lambda i: (0, 0)
    row = lambda w: pl.BlockSpec((tm, w), lambda i: (i, 0))
    return pl.pallas_call(
        _post_kernel,
        grid=(n // tm,),
        in_specs=[row(ATTN_W), row(LRU_W), row(D_MODEL),
                  pl.BlockSpec((1, 6, D_MODEL), lambda i: (mod_row(i * tm), 0, 0)),
                  pl.BlockSpec((1, D_MODEL), const),
                  pl.BlockSpec((D_MODEL, D_MODEL), const),
                  pl.BlockSpec((D_MODEL, 2 * LANES), const),
                  pl.BlockSpec((1, LANES), const),
                  pl.BlockSpec((tm, tm), const),
                  pl.BlockSpec((1, LANES), const)],
        out_specs=[row(D_MODEL), pl.BlockSpec((tm,) + ROW_TILE, lambda i: (i, 0, 0)), row(LANES),
                   pl.BlockSpec((1, LANES), const)],
        out_shape=[jax.ShapeDtypeStruct((n, D_MODEL), F32),
                   jax.ShapeDtypeStruct((n,) + ROW_TILE, F32),
                   jax.ShapeDtypeStruct((n, LANES), F32),
                   jax.ShapeDtypeStruct((1, LANES), F32)],
        scratch_shapes=[pltpu.VMEM((1, LANES), F32), pltpu.VMEM((D_MODEL, D_MODEL), BF16)],
        compiler_params=_cparams(("arbitrary",)),
        name="post",
    )(attn, rec, x, mod3, g2, w_out, wr2, br, tri, cnt_in)


def _row_copy(src_ref, src_row, dst_ref, dst_row, sem):
    return pltpu.make_async_copy(src_ref.at[pl.ds(src_row, 1)], dst_ref.at[pl.ds(dst_row, 1)], sem)


def _token_slots(idx_ref, seg_ref, n):
    base = 2 * TOP_K * n
    return (seg_ref[idx_ref[base]] + idx_ref[base + 2], seg_ref[idx_ref[base + 1]] + idx_ref[base + 3])


def _dispatch_kernel(idx_ref, seg_ref, zfill_ref, nu_ref, h2p_ref, h2s_ref, xs_ref, zero_s, sem,
                     *, n_prompt):
    i = pl.program_id(0)
    ch = CH_DISPATCH

    @pl.when(i == 0)
    def _():
        zero_s[...] = jnp.zeros_like(zero_s)

        def zero_tile(t):
            return pltpu.make_async_copy(
                zero_s, xs_ref.at[pl.ds(pl.multiple_of(t * TR_MOE, TR_MOE), TR_MOE)], sem)

        for e in range(N_EXPERTS):
            @pl.when(zfill_ref[e] >= 0)
            def _():
                zero_tile(zfill_ref[e]).start()
        n_all = xs_ref.shape[0] // TR_MOE
        lax.fori_loop(nu_ref[0], n_all, lambda t, c: (zero_tile(t).start(), c)[1], 0)
        for e in range(N_EXPERTS):
            @pl.when(zfill_ref[e] >= 0)
            def _():
                zero_tile(0).wait()
        lax.fori_loop(nu_ref[0], n_all, lambda t, c: (zero_tile(0).wait(), c)[1], 0)

    def scatter(src_ref):
        def body(j, carry):
            s1, s2 = _token_slots(idx_ref, seg_ref, i * ch + j)
            _row_copy(src_ref, j, xs_ref, s1, sem).start()
            _row_copy(src_ref, j, xs_ref, s2, sem).start()
            return carry
        lax.fori_loop(0, ch, body, 0, unroll=8)
        for _ in range(TOP_K):
            pltpu.make_async_copy(src_ref, xs_ref.at[pl.ds(0, ch)], sem).wait()

    @pl.when(i < n_prompt // ch)
    def _():
        scatter(h2p_ref)

    @pl.when(i >= n_prompt // ch)
    def _():
        scatter(h2s_ref)


def _dispatch(idx, seg_start, zfill, n_used, h2p, h2s, n_rows):
    n_prompt = h2p.shape[0]
    n = n_prompt + h2s.shape[0]
    ch = CH_DISPATCH
    npc = n_prompt // ch
    last_p = npc - 1
    return pl.pallas_call(
        functools.partial(_dispatch_kernel, n_prompt=n_prompt),
        grid_spec=pltpu.PrefetchScalarGridSpec(
            num_scalar_prefetch=4,
            grid=(n // ch,),
            in_specs=[pl.BlockSpec((ch,) + ROW_TILE, lambda i, *_: (jnp.minimum(i, last_p), 0, 0)),
                      pl.BlockSpec((ch,) + ROW_TILE, lambda i, *_: (jnp.maximum(i - npc, 0), 0, 0))],
            out_specs=pl.BlockSpec(memory_space=pl.ANY),
            scratch_shapes=[pltpu.VMEM((TR_MOE,) + ROW_TILE, F32), pltpu.SemaphoreType.DMA]),
        out_shape=jax.ShapeDtypeStruct((n_rows,) + ROW_TILE, F32),
        compiler_params=_cparams(("arbitrary",)),
        name="dispatch",
    )(idx, seg_start, zfill, n_used, h2p, h2s)


def _ffn_kernel(tend_ref, xs_ref, wg_ref, wu_ref, wd_ref, ys_ref,
                xbuf, ybuf, wg_s, wu_s, wd_s, sem_in, sem_out):
    e = pl.program_id(0)
    n_used = tend_ref[N_EXPERTS - 1]
    t_first = jnp.where(e == 0, 0, tend_ref[jnp.maximum(e - 1, 0)])
    t_last = tend_ref[e]
    n_in = xbuf.shape[0]
    n_out = ybuf.shape[0]

    def tile_rows(t):
        return pl.ds(pl.multiple_of(t * TR_MOE, TR_MOE), TR_MOE)

    def fetch(t):
        return pltpu.make_async_copy(xs_ref.at[tile_rows(t)], xbuf.at[t % n_in], sem_in.at[t % n_in])

    def writeback(t):
        return pltpu.make_async_copy(ybuf.at[t % n_out], ys_ref.at[tile_rows(t)],
                                     sem_out.at[t % n_out])

    @pl.when(e == 0)
    def _():
        for t in range(n_in - 1):
            @pl.when(t < n_used)
            def _():
                fetch(t).start()

    @pl.when(t_last > t_first)
    def _():
        wg_s[...] = wg_ref[0].astype(BF16)
        wu_s[...] = wu_ref[0].astype(BF16)
        wd_s[...] = wd_ref[0].astype(BF16)

    def tile(t, carry):
        fetch(t).wait()

        @pl.when(t + n_in - 1 < n_used)
        def _():
            fetch(t + n_in - 1).start()

        @pl.when(t >= n_out)
        def _():
            writeback(t - n_out).wait()

        x = _from_row_tiles(xbuf[t % n_in]).astype(BF16)
        hg = _dot(x, wg_s[...])
        hu = _dot(x, wu_s[...])
        act = (hg * _sigmoid(hg)) * hu
        ybuf[t % n_out] = _to_row_tiles(_dot(act.astype(BF16), wd_s[...]))
        writeback(t).start()
        return carry

    lax.fori_loop(t_first, t_last, tile, 0)

    @pl.when(e == N_EXPERTS - 1)
    def _():
        for back in range(n_out, 0, -1):
            @pl.when(n_used >= back)
            def _():
                writeback(n_used - back).wait()
        n_all = ys_ref.shape[0] // TR_MOE
        ybuf[0] = jnp.zeros(ybuf.shape[1:], F32)

        def zero_tile(t):
            return pltpu.make_async_copy(ybuf.at[0], ys_ref.at[tile_rows(t)], sem_out.at[0])

        lax.fori_loop(n_used, n_all, lambda t, c: (zero_tile(t).start(), c)[1], 0)
        lax.fori_loop(n_used, n_all, lambda t, c: (zero_tile(t).wait(), c)[1], 0)


def _ffn(tile_end, xs, wg, wu, wd, n_tiles):
    tr = TR_MOE
    wsel = lambda e, tend: (e, 0, 0)
    return pl.pallas_call(
        _ffn_kernel,
        grid_spec=pltpu.PrefetchScalarGridSpec(
            num_scalar_prefetch=1,
            grid=(N_EXPERTS,),
            in_specs=[pl.BlockSpec(memory_space=pl.ANY),
                      pl.BlockSpec((1, D_MODEL, EXPERT_FF), wsel),
                      pl.BlockSpec((1, D_MODEL, EXPERT_FF), wsel),
                      pl.BlockSpec((1, EXPERT_FF, D_MODEL), wsel)],
            out_specs=pl.BlockSpec(memory_space=pl.ANY),
            scratch_shapes=[pltpu.VMEM((FFN_IN_SLOTS, tr) + ROW_TILE, F32),
                            pltpu.VMEM((FFN_OUT_SLOTS, tr) + ROW_TILE, F32),
                            pltpu.VMEM((D_MODEL, EXPERT_FF), BF16),
                            pltpu.VMEM((D_MODEL, EXPERT_FF), BF16),
                            pltpu.VMEM((EXPERT_FF, D_MODEL), BF16),
                            pltpu.SemaphoreType.DMA((FFN_IN_SLOTS,)),
                            pltpu.SemaphoreType.DMA((FFN_OUT_SLOTS,))]),
        out_shape=jax.ShapeDtypeStruct((n_tiles * tr,) + ROW_TILE, F32),
        compiler_params=_cparams(("arbitrary",)),
        name="ffn",
    )(tile_end, xs, wg, wu, wd)


def _combine_kernel(idx_ref, seg_ref, ys_ref, x1_ref, route_ref, mod_ref, y_ref, b1_s, b2_s, sems):
    i = pl.program_id(0)
    tm = x1_ref.shape[0]

    def gather(step, slot):
        def body(j, carry):
            s1, s2 = _token_slots(idx_ref, seg_ref, step * tm + j)
            _row_copy(ys_ref, s1, b1_s.at[slot], j, sems.at[slot]).start()
            _row_copy(ys_ref, s2, b2_s.at[slot], j, sems.at[slot]).start()
            return carry
        lax.fori_loop(0, tm, body, 0, unroll=8)

    @pl.when(i == 0)
    def _():
        gather(0, 0)

    @pl.when(i + 1 < pl.num_programs(0))
    def _():
        gather(i + 1, (i + 1) % 2)

    slot = i % 2
    for buf in (b1_s, b2_s):
        pltpu.make_async_copy(ys_ref.at[pl.ds(0, tm)], buf.at[slot], sems.at[slot]).wait()
    route = route_ref[...]
    moe = route[:, 4:5] * _from_row_tiles(b1_s[slot]) + route[:, 5:6] * _from_row_tiles(b2_s[slot])
    y_ref[...] = x1_ref[...] + mod_ref[0][5:6] * moe


def _combine(idx, seg_start, ys, x1, route, mod3, mod_row, seq_len):
    n = x1.shape[0]
    tm = TM_PRE
    tiles_per_seq = seq_len // tm
    row = lambda w: pl.BlockSpec((tm, w), lambda i, *_: (i, 0))
    return pl.pallas_call(
        _combine_kernel,
        grid_spec=pltpu.PrefetchScalarGridSpec(
            num_scalar_prefetch=2,
            grid=(n // tm,),
            in_specs=[pl.BlockSpec(memory_space=pl.ANY), row(D_MODEL), row(LANES),
                      pl.BlockSpec((1, 6, D_MODEL),
                                   lambda i, *_: (mod_row(i * tm), 0, 0))],
            out_specs=row(D_MODEL),
            scratch_shapes=[pltpu.VMEM((2, tm) + ROW_TILE, F32), pltpu.VMEM((2, tm) + ROW_TILE, F32),
                            pltpu.SemaphoreType.DMA((2,))]),
        out_shape=jax.ShapeDtypeStruct((n, D_MODEL), F32),
        compiler_params=_cparams(("arbitrary",)),
        name="combine",
    )(idx, seg_start, ys, x1, route, mod3)


def _rope_tables(length):
    rows = length // GRID_W
    r, col = jnp.meshgrid(jnp.arange(rows), jnp.arange(GRID_W), indexing='ij')
    r = r.reshape(-1).astype(F32)
    col = col.reshape(-1).astype(F32)
    half = HEAD_DIM // 2
    inv = ROPE_THETA ** (-jnp.arange(0, half, 2, dtype=F32) / half)
    ang_r = r[:, None] * inv
    ang_c = col[:, None] * inv
    ang = jnp.concatenate([ang_r, ang_r, ang_c, ang_c], axis=-1)
    sign = jnp.where((jnp.arange(HEAD_DIM) // (HEAD_DIM // 4)) % 2 == 0, -1.0, 1.0).astype(F32)
    cos = jnp.tile(jnp.cos(ang), (1, LANES // HEAD_DIM))
    sin = jnp.tile(jnp.sin(ang) * sign, (1, LANES // HEAD_DIM))
    return cos, sin


def _block_diag(w):
    eye = jnp.eye(LRU_BLOCKS, dtype=w.dtype)
    return jnp.einsum('hij,hg->higj', w, eye).reshape(LRU_W, LRU_W)


def _expand_cache_kernel(k_ref, v_ref, kx_ref, vx_ref):
    for ref, xref, one in ((k_ref, kx_ref, 0.0), (v_ref, vx_ref, 1.0)):
        heads = jnp.swapaxes(ref[0, 0], 0, 1)
        col = jnp.concatenate([heads[h] for h in range(N_KV_HEADS)], axis=1)
        _store_expanded(xref, col, one)


def _expand_cache(cache_k, cache_v):
    b, _, t, _, _ = cache_k.shape
    src = pl.BlockSpec((1, 1, t, N_KV_HEADS, HEAD_DIM), lambda i: (i, 0, 0, 0, 0))
    dst = pl.BlockSpec((t, 4 * LANES), lambda i: (i, 0))
    return pl.pallas_call(
        _expand_cache_kernel,
        grid=(b,),
        in_specs=[src, src], out_specs=[dst, dst],
        out_shape=[jax.ShapeDtypeStruct((b * t, 4 * LANES), BF16)] * 2,
        compiler_params=_cparams(("arbitrary",)),
        name="expand_cache",
    )(cache_k, cache_v)


def kernel(x_prompt, x_sample, cache_k, cache_v, state_lru, c, c_ctx, w_mod, b_mod, norm1, norm2,
           w_in, q_norm, k_norm, conv_w, conv_b, lru_wa, lru_ba, lru_wx, lru_bx, lru_lambda, w_out,
           router_grp_w, router_grp_b, router_exp_w, router_exp_b, exp_w_gate, exp_w_up, exp_w_down):
    batch, seq, _ = x_prompt.shape
    dec_batch, dec_seq, _ = x_sample.shape
    past = cache_k.shape[2]
    depth = w_mod.shape[0]
    assert depth == 1

    cvec = jnp.concatenate(
        [c_ctx[None, :], c, jnp.zeros((MOD_ROWS - 1 - dec_batch, D_MODEL), F32)], axis=0)
    mod3 = _modulation(cvec, w_mod[0], b_mod[0][None, :]).reshape(MOD_ROWS, 6, D_MODEL)

    head_id = jnp.arange(QK_W) // HEAD_DIM
    heads = (head_id[:, None] == jnp.arange(LANES)[None, :]).astype(BF16)
    gqk = jnp.concatenate([jnp.tile(q_norm[0], N_HEADS), jnp.tile(k_norm[0], N_KV_HEADS)])[None, :]
    wf = (0.5 * jnp.concatenate([_block_diag(lru_wa[0, 0]), _block_diag(lru_wx[0, 0])], axis=1)).astype(BF16)
    wb = (0.5 * jnp.concatenate([_block_diag(lru_wa[0, 1]), _block_diag(lru_wx[0, 1])], axis=1)).astype(BF16)
    bf = 0.5 * jnp.concatenate([lru_ba[0, 0], lru_bx[0, 0]])[None, :]
    bb = 0.5 * jnp.concatenate([lru_ba[0, 1], lru_bx[0, 1]])[None, :]
    pad = LANES - N_EXPERTS - N_GROUPS
    wr = jnp.concatenate([router_exp_w[0], router_grp_w[0], jnp.zeros((D_MODEL, pad), F32)], axis=1)
    wr_hi = wr.astype(BF16)
    wr2 = jnp.concatenate([wr_hi, (wr - wr_hi.astype(F32)).astype(BF16)], axis=1)
    br = jnp.concatenate([router_exp_b[0], router_grp_b[0], jnp.zeros((pad,), F32)])[None, :]
    g1 = norm1[0][None, :]
    g2 = norm2[0][None, :]
    cw = conv_w[0]
    cb = conv_b[0][None, :]
    lam = lru_lambda[0]
    tri = (jnp.arange(TM_PRE)[:, None] > jnp.arange(TM_PRE)[None, :]).astype(BF16)

    def mixers(x, seq_len, mod_row, tables, extra_k, extra_v, h0, cnt_in):
        q, kx, vx, xr, gb, *cache = _pre(x, mod3, mod_row, g1, w_in[0], heads, gqk, tables, seq_len)
        k_segs = [(kx, seq_len)] + extra_k
        v_segs = [(vx, seq_len)] + extra_v
        attn = _attention(q, k_segs, v_segs, seq_len)
        rec, fin = _lru(xr, gb, h0, cw, cb, wf, wb, bf, bb, lam, seq_len)
        x1, h2, route, cnt = _post(attn, rec, x, mod3, mod_row, g2, w_out[0], wr2, br, tri, cnt_in)
        return x1, h2, route, cnt, cache, fin

    mod_row_p = lambda tok: 0
    mod_row_s = lambda tok: tok // dec_seq + 1
    xp = x_prompt.reshape(batch * seq, D_MODEL)
    x1p, h2p, route_p, cnt_p, (kf, vf), fin = mixers(
        xp, seq, mod_row_p, None, [], [], jnp.zeros((batch, 2, LRU_W), F32),
        jnp.zeros((1, LANES), F32))
    xs = x_sample.reshape(dec_batch * dec_seq, D_MODEL)
    ck, cv = _expand_cache(cache_k, cache_v)
    x1s, h2s, route_s, cnt_all, _, _ = mixers(
        xs, dec_seq, mod_row_s, _rope_tables(dec_seq), [(ck, past)], [(cv, past)],
        state_lru[:, 0], cnt_p)

    n_prompt = batch * seq
    n_tok = n_prompt + dec_batch * dec_seq
    n_tiles = (TOP_K * n_tok + N_EXPERTS * (TR_MOE - 1)) // TR_MOE
    cnt = cnt_all[0, :N_EXPERTS].astype(jnp.int32)
    ntile = (cnt + TR_MOE - 1) // TR_MOE
    tile_end = jnp.cumsum(ntile)
    seg_start = (tile_end - ntile) * TR_MOE
    n_used = tile_end[-1:]
    zfill = jnp.where(ntile > 0, tile_end - 1, -1)

    idx_p = route_p[:, :2 * TOP_K].astype(jnp.int32).reshape(-1)
    idx_s = route_s[:, :2 * TOP_K].astype(jnp.int32).reshape(-1)
    xsort = _dispatch(jnp.concatenate([idx_p, idx_s]), seg_start, zfill, n_used, h2p, h2s,
                      n_tiles * TR_MOE)
    ysort = _ffn(tile_end, xsort, exp_w_gate[0], exp_w_up[0], exp_w_down[0], n_tiles)
    yp = _combine(idx_p, seg_start, ysort, x1p, route_p, mod3, mod_row_p, seq)
    ys = _combine(idx_s, seg_start, ysort, x1s, route_s, mod3, mod_row_s, dec_seq)

    return (yp.reshape(batch, seq, D_MODEL),
            ys.reshape(dec_batch, dec_seq, D_MODEL),
            kf.reshape(batch, 1, seq, N_KV_HEADS, HEAD_DIM),
            vf.reshape(batch, 1, seq, N_KV_HEADS, HEAD_DIM),
            fin.reshape(batch, 1, 2, LRU_W))
```

```python
import functools

import jax
import jax.numpy as jnp
from jax import lax
from jax.experimental import pallas as pl
from jax.experimental.pallas import tpu as pltpu

F32 = jnp.float32
BF16 = jnp.bfloat16

D_MODEL = 1024
GRID_W = 64
ATTN_W = 512
LRU_W = 512
HEAD_DIM = 64
N_HEADS = 8
N_KV_HEADS = 2
KV_W = N_KV_HEADS * HEAD_DIM
LRU_BLOCKS = 8
LRU_BLOCK_W = LRU_W // LRU_BLOCKS
CONV_W = 4
LRU_C = 8.0
IN_W = ATTN_W + 2 * KV_W + 2 * LRU_W
QK_W = ATTN_W + KV_W
N_GROUPS = 4
EXPERTS_PER_GROUP = 8
N_EXPERTS = N_GROUPS * EXPERTS_PER_GROUP
TOP_K = 2
EXPERT_FF = D_MODEL // 4
ROPE_THETA = 10000.0
EPS = 1e-6

LANES = 128
SUBLANES = 8
MOD_ROWS = 8
VMEM_LIMIT = 48 * 1024 * 1024

TM_PRE = 512
TQ_ATT = 1024
TC_LRU = 256
TR_MOE = 256
CH_DISPATCH = 1024
FFN_IN_SLOTS = 4
FFN_OUT_SLOTS = 3


def _cparams(sem):
    return pltpu.CompilerParams(dimension_semantics=sem, vmem_limit_bytes=VMEM_LIMIT)


def _dot(a, b):
    return jnp.dot(a, b, preferred_element_type=F32)


def _dot_nt(a, b):
    return lax.dot_general(a, b, (((1,), (1,)), ((), ())), preferred_element_type=F32)


ROW_TILE = (D_MODEL // LANES, LANES)
LOG2_E = 1.4426950408889634
Q_SCALE = HEAD_DIM ** -0.5 * LOG2_E
ONES_LANE_EVEN = HEAD_DIM
ONES_LANE_ODD = 0


def _to_row_tiles(x):
    cols = jnp.stack([x[:, c * LANES:(c + 1) * LANES] for c in range(D_MODEL // LANES)], axis=0)
    return jnp.swapaxes(cols, 0, 1)


def _from_row_tiles(x3):
    cols = jnp.swapaxes(x3, 0, 1)
    return jnp.concatenate([cols[c] for c in range(D_MODEL // LANES)], axis=1)


def _sigmoid(x):
    return 0.5 * jnp.tanh(0.5 * x) + 0.5


def _split_bf16(x):
    hi = x.astype(BF16)
    lo = (x - hi.astype(F32)).astype(BF16)
    return hi, lo


def _mod_kernel(c_ref, w_ref, b_ref, o_ref):
    c = c_ref[...]
    s = (c * jax.nn.sigmoid(c)).astype(BF16)
    o_ref[...] = _dot(s, w_ref[...].astype(BF16)) + b_ref[...]


def _modulation(cvec, w_mod, b_mod):
    n_out = w_mod.shape[1]
    tn = n_out // 4
    return pl.pallas_call(
        _mod_kernel,
        grid=(n_out // tn,),
        in_specs=[pl.BlockSpec((MOD_ROWS, D_MODEL), lambda j: (0, 0)),
                  pl.BlockSpec((D_MODEL, tn), lambda j: (0, j)),
                  pl.BlockSpec((1, tn), lambda j: (0, j))],
        out_specs=pl.BlockSpec((MOD_ROWS, tn), lambda j: (0, j)),
        out_shape=jax.ShapeDtypeStruct((MOD_ROWS, n_out), F32),
        compiler_params=_cparams(("arbitrary",)),
        name="modulation",
    )(cvec, w_mod, b_mod)


def _split_heads(col):
    return jnp.swapaxes(jnp.stack([col[:, h * HEAD_DIM:(h + 1) * HEAD_DIM]
                                   for h in range(N_KV_HEADS)], axis=0), 0, 1)


def _store_expanded(xref, col, one):
    lane = lax.broadcasted_iota(jnp.int32, col.shape, 1)
    lo_half = lane < HEAD_DIM
    swapped = pltpu.roll(col, HEAD_DIM, 1)
    fill_hi = jnp.where(lane == ONES_LANE_EVEN, one, 0.0)
    fill_lo = jnp.where(lane == ONES_LANE_ODD, one, 0.0)
    xref[:, 0 * LANES:1 * LANES] = jnp.where(lo_half, col, fill_hi).astype(BF16)
    xref[:, 1 * LANES:2 * LANES] = jnp.where(lo_half, fill_lo, swapped).astype(BF16)
    xref[:, 2 * LANES:3 * LANES] = jnp.where(lo_half, swapped, fill_hi).astype(BF16)
    xref[:, 3 * LANES:4 * LANES] = jnp.where(lo_half, fill_lo, col).astype(BF16)


def _pre_kernel(*refs, rope):
    if rope:
        (x_ref, mod_ref, g1_ref, win_ref, heads_ref, heads_t_ref, gqk_ref, cos_ref, sin_ref,
         q_ref, kx_ref, vx_ref, xr_ref, gb_ref, win_s) = refs
    else:
        (x_ref, mod_ref, g1_ref, win_ref, heads_ref, heads_t_ref, gqk_ref,
         q_ref, kx_ref, vx_ref, xr_ref, gb_ref, kf_ref, vf_ref, win_s) = refs

    @pl.when(pl.program_id(0) == 0)
    def _():
        win_s[...] = win_ref[...].astype(BF16)

    x = x_ref[...]
    m = mod_ref[0]
    ms = jnp.mean(x * x, axis=-1, keepdims=True)
    y = x * lax.rsqrt(ms + EPS) * g1_ref[...]
    h = y * (1.0 + m[1:2]) + m[0:1]
    z = _dot(h.astype(BF16), win_s[...])

    qk = z[:, :QK_W]
    ss = _dot((qk * qk).astype(BF16), heads_ref[...])
    hi, lo = _split_bf16(lax.rsqrt(ss * (1.0 / HEAD_DIM) + EPS))
    qk = qk * (_dot(hi, heads_t_ref[...]) + _dot(lo, heads_t_ref[...])) * gqk_ref[...]

    lane = lax.broadcasted_iota(jnp.int32, (x.shape[0], LANES), 1)
    cols = []
    for c in range(QK_W // LANES):
        xc = qk[:, c * LANES:(c + 1) * LANES]
        if rope:
            left = pltpu.roll(xc, LANES - HEAD_DIM // 4, 1)
            right = pltpu.roll(xc, HEAD_DIM // 4, 1)
            rot = jnp.where((lane // (HEAD_DIM // 4)) % 2 == 0, left, right)
            xc = xc * cos_ref[...] + rot * sin_ref[...]
        cols.append(xc)
    for c in range(ATTN_W // LANES):
        q_ref[:, c * LANES:(c + 1) * LANES] = (cols[c] * Q_SCALE).astype(BF16)

    k_col = cols[ATTN_W // LANES]
    v_col = z[:, QK_W:QK_W + KV_W]
    _store_expanded(kx_ref, k_col, 0.0)
    _store_expanded(vx_ref, v_col, 1.0)
    if not rope:
        kf_ref[...] = _split_heads(k_col)
        vf_ref[...] = _split_heads(v_col)

    xr_ref[...] = z[:, QK_W + KV_W:QK_W + KV_W + LRU_W]
    gb_ref[...] = z[:, QK_W + KV_W + LRU_W:]


def _pre(x, mod3, mod_row, g1, w_in, heads, gqk, tables, seq_len):
    n = x.shape[0]
    tm = TM_PRE
    tiles_per_seq = seq_len // tm
    rope = tables is not None
    const = lambda i: (0, 0)
    in_specs = [pl.BlockSpec((tm, D_MODEL), lambda i: (i, 0)),
                pl.BlockSpec((1, 6, D_MODEL), lambda i: (mod_row(i * tm), 0, 0)),
                pl.BlockSpec((1, D_MODEL), const),
                pl.BlockSpec((D_MODEL, IN_W), const),
                pl.BlockSpec((QK_W, LANES), const),
                pl.BlockSpec((LANES, QK_W), const),
                pl.BlockSpec((1, QK_W), const)]
    args = [x, mod3, g1, w_in, heads, heads.T, gqk]
    if rope:
        in_specs += [pl.BlockSpec((tm, LANES), lambda i: (i % tiles_per_seq, 0))] * 2
        args += list(tables)
    row = lambda w: pl.BlockSpec((tm, w), lambda i: (i, 0))
    out_shape = [jax.ShapeDtypeStruct((n, ATTN_W), BF16),
                 jax.ShapeDtypeStruct((n, 4 * LANES), BF16),
                 jax.ShapeDtypeStruct((n, 4 * LANES), BF16),
                 jax.ShapeDtypeStruct((n, LRU_W), F32),
                 jax.ShapeDtypeStruct((n, LRU_W), F32)]
    out_specs = [row(ATTN_W), row(4 * LANES), row(4 * LANES), row(LRU_W), row(LRU_W)]
    if not rope:
        cache = pl.BlockSpec((tm, N_KV_HEADS, HEAD_DIM), lambda i: (i, 0, 0))
        out_shape += [jax.ShapeDtypeStruct((n, N_KV_HEADS, HEAD_DIM), F32)] * 2
        out_specs += [cache, cache]
    return pl.pallas_call(
        functools.partial(_pre_kernel, rope=rope),
        grid=(n // tm,),
        in_specs=in_specs, out_specs=out_specs, out_shape=out_shape,
        scratch_shapes=[pltpu.VMEM((D_MODEL, IN_W), BF16)],
        compiler_params=_cparams(("arbitrary",)),
        name="pre_rope" if rope else "pre",
    )(*args)


def _attn_kernel(*refs, n_seg):
    q_ref = refs[0]
    k_refs = refs[1:1 + n_seg]
    v_refs = refs[1 + n_seg:1 + 2 * n_seg]
    o_ref = refs[1 + 2 * n_seg]
    lane = lax.broadcasted_iota(jnp.int32, (q_ref.shape[0], LANES), 1)
    for c in range(ATTN_W // LANES):
        qc = q_ref[:, c * LANES:(c + 1) * LANES]
        g = c // 2
        accs = []
        for par in range(2):
            sl = slice((2 * g + par) * LANES, (2 * g + par + 1) * LANES)
            ss = [_dot_nt(qc, k[:, sl]) for k in k_refs]
            mx = functools.reduce(jnp.maximum, [jnp.max(s, axis=-1, keepdims=True) for s in ss])
            ps = [jnp.exp2((s - mx).astype(BF16)) for s in ss]
            accs.append(functools.reduce(lambda a, b: a + b,
                                         [_dot(p, v[:, sl]) for p, v in zip(ps, v_refs)]))
        even = accs[0] / accs[0][:, ONES_LANE_EVEN:ONES_LANE_EVEN + 1]
        odd = accs[1] / accs[1][:, ONES_LANE_ODD:ONES_LANE_ODD + 1]
        o_ref[:, c * LANES:(c + 1) * LANES] = jnp.where(lane < HEAD_DIM, even, odd).astype(BF16)


def _attention(q, k_segs, v_segs, seq_len):
    n = q.shape[0]
    tq = min(TQ_ATT, seq_len)
    nq = seq_len // tq
    n_seg = len(k_segs)
    in_specs = [pl.BlockSpec((tq, ATTN_W), lambda b, i: (b * nq + i, 0))]
    for arr, t in list(k_segs) + list(v_segs):
        in_specs.append(pl.BlockSpec((t, 4 * LANES), lambda b, i: (b, 0)))
    return pl.pallas_call(
        functools.partial(_attn_kernel, n_seg=n_seg),
        grid=(n // seq_len, nq),
        in_specs=in_specs,
        out_specs=pl.BlockSpec((tq, ATTN_W), lambda b, i: (b * nq + i, 0)),
        out_shape=jax.ShapeDtypeStruct((n, ATTN_W), BF16),
        compiler_params=_cparams(("arbitrary", "arbitrary")),
        name="attention_%dseg" % n_seg,
    )(q, *[a for a, _ in k_segs], *[a for a, _ in v_segs])


def _log_sigmoid(x):
    return jnp.minimum(x, 0.0) - jnp.log1p(jnp.exp(-jnp.abs(x)))


def _tile_scan(a, b, reverse):
    row = lax.broadcasted_iota(jnp.int32, a.shape, 0)
    d = 1
    while d < SUBLANES:
        if reverse:
            keep = row < SUBLANES - d
            shift = SUBLANES - d
        else:
            keep = row >= d
            shift = d
        a_sh = jnp.where(keep, pltpu.roll(a, shift, 0), 1.0)
        b_sh = jnp.where(keep, pltpu.roll(b, shift, 0), 0.0)
        b = a * b_sh + b
        a = a * a_sh
        d *= 2
    return a, b


def _lru_kernel(xr_ref, gb_ref, h0_ref, cw_ref, cb_ref, wf_ref, wb_ref, bf_ref, bb_ref, lam_ref,
                rec_ref, fin_ref, xpad_s, xc_s, hf_s, a_s, b_s, *, seq_len):
    tc = TC_LRU
    n_chunks = seq_len // tc
    n_tiles = tc // SUBLANES
    zpad = jnp.zeros((SUBLANES, LRU_W), F32)
    xpad_s[0:SUBLANES, :] = zpad
    xpad_s[SUBLANES:SUBLANES + seq_len, :] = xr_ref[...]
    xpad_s[SUBLANES + seq_len:2 * SUBLANES + seq_len, :] = zpad

    half_cl = (0.5 * LRU_C * LOG2_E) * _log_sigmoid(lam_ref[...])

    def gates(xcc, w_ref, bias_ref, half_cl_d):
        t = jnp.tanh(_dot(xcc.astype(BF16), w_ref[...]) + bias_ref[...])
        a = jnp.exp2(t[:, :LRU_W] * half_cl_d + half_cl_d)
        half_x = 0.5 * xcc
        a_s[...] = a
        b_s[...] = jnp.sqrt(1.0 - a * a) * (t[:, LRU_W:] * half_x + half_x)

    h = h0_ref[0, 0:1, :]
    for c in range(n_chunks):
        base = c * tc
        xcc = cb_ref[...] + functools.reduce(
            lambda u, v: u + v,
            [cw_ref[j:j + 1, :] * xpad_s[base + SUBLANES - 1 + j:base + SUBLANES - 1 + j + tc, :]
             for j in range(CONV_W)])
        xc_s[base:base + tc, :] = xcc
        gates(xcc, wf_ref, bf_ref, half_cl[0:1])

        def fwd_tile(t, hc, base=base):
            r0 = pl.multiple_of(t * SUBLANES, SUBLANES)
            ca, cb = _tile_scan(a_s[pl.ds(r0, SUBLANES), :], b_s[pl.ds(r0, SUBLANES), :], False)
            hh = ca * hc + cb
            hf_s[pl.ds(base + r0, SUBLANES), :] = hh
            return hh[SUBLANES - 1:SUBLANES, :]

        h = lax.fori_loop(0, n_tiles, fwd_tile, h, unroll=4)
    fin_ref[0, 0:1, :] = h

    h = h0_ref[0, 1:2, :]
    for c in reversed(range(n_chunks)):
        base = c * tc
        gates(xc_s[base:base + tc, :], wb_ref, bb_ref, half_cl[1:2])

        def bwd_tile(t, hc, base=base):
            r0 = pl.multiple_of((n_tiles - 1 - t) * SUBLANES, SUBLANES)
            ca, cb = _tile_scan(a_s[pl.ds(r0, SUBLANES), :], b_s[pl.ds(r0, SUBLANES), :], True)
            hh = ca * hc + cb
            gate = jax.nn.gelu(gb_ref[pl.ds(base + r0, SUBLANES), :], approximate=True)
            rec_ref[pl.ds(base + r0, SUBLANES), :] = (
                (hf_s[pl.ds(base + r0, SUBLANES), :] + hh) * gate).astype(rec_ref.dtype)
            return hh[0:1, :]

        h = lax.fori_loop(0, n_tiles, bwd_tile, h, unroll=4)
    fin_ref[0, 1:2, :] = h


def _lru(xr, gb, h0, conv_w, conv_b, wf, wb, bf, bb, lam, seq_len):
    n = xr.shape[0]
    batch = n // seq_len
    const = lambda b: (0, 0)
    seq = pl.BlockSpec((seq_len, LRU_W), lambda b: (b, 0))
    st = pl.BlockSpec((1, 2, LRU_W), lambda b: (b, 0, 0))
    return pl.pallas_call(
        functools.partial(_lru_kernel, seq_len=seq_len),
        grid=(batch,),
        in_specs=[seq, seq, st,
                  pl.BlockSpec((CONV_W, LRU_W), const), pl.BlockSpec((1, LRU_W), const),
                  pl.BlockSpec((LRU_W, 2 * LRU_W), const), pl.BlockSpec((LRU_W, 2 * LRU_W), const),
                  pl.BlockSpec((1, 2 * LRU_W), const), pl.BlockSpec((1, 2 * LRU_W), const),
                  pl.BlockSpec((2, LRU_W), const)],
        out_specs=[seq, st],
        out_shape=[jax.ShapeDtypeStruct((n, LRU_W), BF16),
                   jax.ShapeDtypeStruct((batch, 2, LRU_W), F32)],
        scratch_shapes=[pltpu.VMEM((seq_len + 2 * SUBLANES, LRU_W), F32),
                        pltpu.VMEM((seq_len, LRU_W), F32),
                        pltpu.VMEM((seq_len, LRU_W), F32),
                        pltpu.VMEM((TC_LRU, LRU_W), F32),
                        pltpu.VMEM((TC_LRU, LRU_W), F32)],
        compiler_params=_cparams(("arbitrary",)),
        name="lru_%d" % seq_len,
    )(xr, gb, h0, conv_w, conv_b, wf, wb, bf, bb, lam)


def _post_kernel(attn_ref, rec_ref, x_ref, mod_ref, g2_ref, wo_ref, wr2_ref, br_ref,
                 tri_ref, cnt_in_ref, x1_ref, h2_ref, route_ref, cnt_ref, cnt_s, wo_s):
    @pl.when(pl.program_id(0) == 0)
    def _():
        cnt_s[...] = cnt_in_ref[...]
        wo_s[...] = wo_ref[...].astype(BF16)

    m = mod_ref[0]
    u = _dot(attn_ref[...], wo_s[:ATTN_W, :]) + _dot(rec_ref[...], wo_s[ATTN_W:, :])
    x1 = x_ref[...] + m[2:3] * u
    x1_ref[...] = x1
    ms = jnp.mean(x1 * x1, axis=-1, keepdims=True)
    h2 = x1 * lax.rsqrt(ms + EPS) * g2_ref[...]
    h2 = h2 * (1.0 + m[4:5]) + m[3:4]
    hi, lo = _split_bf16(h2)
    h2_ref[...] = _to_row_tiles(h2)

    hw = _dot(hi, wr2_ref[...])
    logits = hw[:, :LANES] + hw[:, LANES:] + _dot(lo, wr2_ref[:, :LANES]) + br_ref[...]
    lane_i = lax.broadcasted_iota(jnp.int32, logits.shape, 1)
    lane = lane_i.astype(F32)
    lane_group = (lane_i // EXPERTS_PER_GROUP).astype(F32)
    neg = -jnp.inf
    big = float(1 << 20)
    gmask = (lane_i >= N_EXPERTS) & (lane_i < N_EXPERTS + N_GROUPS)
    gl = jnp.where(gmask, logits, neg)
    gmax = jnp.max(gl, axis=-1, keepdims=True)
    gidx = jnp.min(jnp.where(gl == gmax, lane - N_EXPERTS, big), axis=-1, keepdims=True)
    p_sel = 1.0 / jnp.sum(jnp.where(gmask, jnp.exp(gl - gmax), 0.0), axis=-1, keepdims=True)

    emask = (lane_i < N_EXPERTS) & (lane_group == gidx)
    el = jnp.where(emask, logits, neg)
    v1 = jnp.max(el, axis=-1, keepdims=True)
    i1 = jnp.min(jnp.where(el == v1, lane, big), axis=-1, keepdims=True)
    el2 = jnp.where(lane == i1, neg, el)
    v2 = jnp.max(el2, axis=-1, keepdims=True)
    i2 = jnp.min(jnp.where(el2 == v2, lane, big), axis=-1, keepdims=True)
    e2 = jnp.exp(v2 - v1)
    w1 = p_sel / (1.0 + e2)
    w2 = p_sel * e2 / (1.0 + e2)

    oh1 = lane == i1
    oh2 = lane == i2
    oh = jnp.where(oh1, 1.0, 0.0) + jnp.where(oh2, 1.0, 0.0)
    before = _dot(tri_ref[...], oh.astype(BF16)) + cnt_s[...]
    rank1 = jnp.sum(jnp.where(oh1, before, 0.0), axis=-1, keepdims=True)
    rank2 = jnp.sum(jnp.where(oh2, before, 0.0), axis=-1, keepdims=True)
    cnt = cnt_s[...] + jnp.sum(oh, axis=0, keepdims=True)
    cnt_s[...] = cnt
    cnt_ref[...] = cnt
    fields = (i1, i2, rank1, rank2, w1, w2)
    route = jnp.zeros(logits.shape, F32)
    for k, val in enumerate(fields):
        route = jnp.where(lane_i == k, val, route)
    route_ref[...] = route


def _post(attn, rec, x, mod3, mod_row, g2, w_out, wr2, br, tri, cnt_in):
    n = x.shape[0]
    tm = TM_PRE
    const = lambda i: (0, 0)
    row = lambda w: pl.BlockSpec((tm, w), lambda i: (i, 0))
    return pl.pallas_call(
        _post_kernel,
        grid=(n // tm,),
        in_specs=[row(ATTN_W), row(LRU_W), row(D_MODEL),
                  pl.BlockSpec((1, 6, D_MODEL), lambda i: (mod_row(i * tm), 0, 0)),
                  pl.BlockSpec((1, D_MODEL), const),
                  pl.BlockSpec((D_MODEL, D_MODEL), const),
                  pl.BlockSpec((D_MODEL, 2 * LANES), const),
                  pl.BlockSpec((1, LANES), const),
                  pl.BlockSpec((tm, tm), const),
                  pl.BlockSpec((1, LANES), const)],
        out_specs=[row(D_MODEL), pl.BlockSpec((tm,) + ROW_TILE, lambda i: (i, 0, 0)), row(LANES),
                   pl.BlockSpec((1, LANES), const)],
        out_shape=[jax.ShapeDtypeStruct((n, D_MODEL), F32),
                   jax.ShapeDtypeStruct((n,) + ROW_TILE, F32),
                   jax.ShapeDtypeStruct((n, LANES), F32),
                   jax.ShapeDtypeStruct((1, LANES), F32)],
        scratch_shapes=[pltpu.VMEM((1, LANES), F32), pltpu.VMEM((D_MODEL, D_MODEL), BF16)],
        compiler_params=_cparams(("arbitrary",)),
        name="post",
    )(attn, rec, x, mod3, g2, w_out, wr2, br, tri, cnt_in)


def _row_copy(src_ref, src_row, dst_ref, dst_row, sem):
    return pltpu.make_async_copy(src_ref.at[pl.ds(src_row, 1)], dst_ref.at[pl.ds(dst_row, 1)], sem)


def _token_slots(idx_ref, seg_ref, n):
    base = 2 * TOP_K * n
    return (seg_ref[idx_ref[base]] + idx_ref[base + 2], seg_ref[idx_ref[base + 1]] + idx_ref[base + 3])


def _dispatch_kernel(idx_ref, seg_ref, pad_start_ref, pad_rows_ref, nu_ref, h2p_ref, h2s_ref, xs_ref,
                     zero_s, sem, zsem, *, n_prompt):
    i = pl.program_id(0)
    ch = CH_DISPATCH

    def zero_fill(op):
        for e in range(N_EXPERTS):
            rows = pad_rows_ref[e]
            for b in range(TR_MOE.bit_length() - 1):
                size = 1 << b
                off = (rows >> (b + 1)) << (b + 1)

                @pl.when(((rows >> b) & 1) == 1)
                def _():
                    op(pltpu.make_async_copy(zero_s.at[pl.ds(0, size)],
                                             xs_ref.at[pl.ds(pad_start_ref[e] + off, size)], zsem))

        def zero_tile(t, carry):
            op(pltpu.make_async_copy(
                zero_s, xs_ref.at[pl.ds(pl.multiple_of(t * TR_MOE, TR_MOE), TR_MOE)], zsem))
            return carry
        lax.fori_loop(nu_ref[0], xs_ref.shape[0] // TR_MOE, zero_tile, 0)

    @pl.when(i == 0)
    def _():
        zero_s[...] = jnp.zeros_like(zero_s)
        zero_fill(lambda cp: cp.start())

    @pl.when(i == pl.num_programs(0) - 1)
    def _():
        zero_fill(lambda cp: cp.wait())

    def scatter(src_ref):
        def body(j, carry):
            s1, s2 = _token_slots(idx_ref, seg_ref, i * ch + j)
            _row_copy(src_ref, j, xs_ref, s1, sem).start()
            _row_copy(src_ref, j, xs_ref, s2, sem).start()
            return carry
        lax.fori_loop(0, ch, body, 0, unroll=8)
        for _ in range(TOP_K):
            pltpu.make_async_copy(src_ref, xs_ref.at[pl.ds(0, ch)], sem).wait()

    @pl.when(i < n_prompt // ch)
    def _():
        scatter(h2p_ref)

    @pl.when(i >= n_prompt // ch)
    def _():
        scatter(h2s_ref)


def _dispatch(idx, seg_start, pad_start, pad_rows, n_used, h2p, h2s, n_rows):
    n_prompt = h2p.shape[0]
    n = n_prompt + h2s.shape[0]
    ch = CH_DISPATCH
    npc = n_prompt // ch
    last_p = npc - 1
    return pl.pallas_call(
        functools.partial(_dispatch_kernel, n_prompt=n_prompt),
        grid_spec=pltpu.PrefetchScalarGridSpec(
            num_scalar_prefetch=5,
            grid=(n // ch,),
            in_specs=[pl.BlockSpec((ch,) + ROW_TILE, lambda i, *_: (jnp.minimum(i, last_p), 0, 0)),
                      pl.BlockSpec((ch,) + ROW_TILE, lambda i, *_: (jnp.maximum(i - npc, 0), 0, 0))],
            out_specs=pl.BlockSpec(memory_space=pl.ANY),
            scratch_shapes=[pltpu.VMEM((TR_MOE,) + ROW_TILE, F32), pltpu.SemaphoreType.DMA,
                            pltpu.SemaphoreType.DMA]),
        out_shape=jax.ShapeDtypeStruct((n_rows,) + ROW_TILE, F32),
        compiler_params=_cparams(("arbitrary",)),
        name="dispatch",
    )(idx, seg_start, pad_start, pad_rows, n_used, h2p, h2s)


def _ffn_kernel(tend_ref, xs_ref, wg_ref, wu_ref, wd_ref, ys_ref,
                xbuf, ybuf, wg_s, wu_s, wd_s, sem_in, sem_out):
    e = pl.program_id(0)
    n_used = tend_ref[N_EXPERTS - 1]
    t_first = jnp.where(e == 0, 0, tend_ref[jnp.maximum(e - 1, 0)])
    t_last = tend_ref[e]
    n_in = xbuf.shape[0]
    n_out = ybuf.shape[0]

    def tile_rows(t):
        return pl.ds(pl.multiple_of(t * TR_MOE, TR_MOE), TR_MOE)

    def fetch(t):
        return pltpu.make_async_copy(xs_ref.at[tile_rows(t)], xbuf.at[t % n_in], sem_in.at[t % n_in])

    def writeback(t):
        return pltpu.make_async_copy(ybuf.at[t % n_out], ys_ref.at[tile_rows(t)],
                                     sem_out.at[t % n_out])

    @pl.when(e == 0)
    def _():
        for t in range(n_in - 1):
            @pl.when(t < n_used)
            def _():
                fetch(t).start()

    @pl.when(t_last > t_first)
    def _():
        wg_s[...] = wg_ref[0].astype(BF16)
        wu_s[...] = wu_ref[0].astype(BF16)
        wd_s[...] = wd_ref[0].astype(BF16)

    def tile(t, carry):
        fetch(t).wait()

        @pl.when(t + n_in - 1 < n_used)
        def _():
            fetch(t + n_in - 1).start()

        @pl.when(t >= n_out)
        def _():
            writeback(t - n_out).wait()

        x = _from_row_tiles(xbuf[t % n_in]).astype(BF16)
        hg = _dot(x, wg_s[...])
        hu = _dot(x, wu_s[...])
        act = (hg * _sigmoid(hg)) * hu
        ybuf[t % n_out] = _to_row_tiles(_dot(act.astype(BF16), wd_s[...]))
        writeback(t).start()
        return carry

    lax.fori_loop(t_first, t_last, tile, 0)

    @pl.when(e == N_EXPERTS - 1)
    def _():
        for back in range(n_out, 0, -1):
            @pl.when(n_used >= back)
            def _():
                writeback(n_used - back).wait()
        n_all = ys_ref.shape[0] // TR_MOE
        ybuf[0] = jnp.zeros(ybuf.shape[1:], F32)

        def zero_tile(t):
            return pltpu.make_async_copy(ybuf.at[0], ys_ref.at[tile_rows(t)], sem_out.at[0])

        lax.fori_loop(n_used, n_all, lambda t, c: (zero_tile(t).start(), c)[1], 0)
        lax.fori_loop(n_used, n_all, lambda t, c: (zero_tile(t).wait(), c)[1], 0)


def _ffn(tile_end, xs, wg, wu, wd, n_tiles):
    tr = TR_MOE
    wsel = lambda e, tend: (e, 0, 0)
    return pl.pallas_call(
        _ffn_kernel,
        grid_spec=pltpu.PrefetchScalarGridSpec(
            num_scalar_prefetch=1,
            grid=(N_EXPERTS,),
            in_specs=[pl.BlockSpec(memory_space=pl.ANY),
                      pl.BlockSpec((1, D_MODEL, EXPERT_FF), wsel),
                      pl.BlockSpec((1, D_MODEL, EXPERT_FF), wsel),
                      pl.BlockSpec((1, EXPERT_FF, D_MODEL), wsel)],
            out_specs=pl.BlockSpec(memory_space=pl.ANY),
            scratch_shapes=[pltpu.VMEM((FFN_IN_SLOTS, tr) + ROW_TILE, F32),
                            pltpu.VMEM((FFN_OUT_SLOTS, tr) + ROW_TILE, F32),
                            pltpu.VMEM((D_MODEL, EXPERT_FF), BF16),
                            pltpu.VMEM((D_MODEL, EXPERT_FF), BF16),
                            pltpu.VMEM((EXPERT_FF, D_MODEL), BF16),
                            pltpu.SemaphoreType.DMA((FFN_IN_SLOTS,)),
                            pltpu.SemaphoreType.DMA((FFN_OUT_SLOTS,))]),
        out_shape=jax.ShapeDtypeStruct((n_tiles * tr,) + ROW_TILE, F32),
        compiler_params=_cparams(("arbitrary",)),
        name="ffn",
    )(tile_end, xs, wg, wu, wd)


def _combine_kernel(idx_ref, seg_ref, ys_ref, x1_ref, route_ref, mod_ref, y_ref, b1_s, b2_s, sems):
    i = pl.program_id(0)
    tm = x1_ref.shape[0]

    def gather(step, slot):
        def body(j, carry):
            s1, s2 = _token_slots(idx_ref, seg_ref, step * tm + j)
            _row_copy(ys_ref, s1, b1_s.at[slot], j, sems.at[slot]).start()
            _row_copy(ys_ref, s2, b2_s.at[slot], j, sems.at[slot]).start()
            return carry
        lax.fori_loop(0, tm, body, 0, unroll=8)

    @pl.when(i == 0)
    def _():
        gather(0, 0)

    @pl.when(i + 1 < pl.num_programs(0))
    def _():
        gather(i + 1, (i + 1) % 2)

    slot = i % 2
    for buf in (b1_s, b2_s):
        pltpu.make_async_copy(ys_ref.at[pl.ds(0, tm)], buf.at[slot], sems.at[slot]).wait()
    route = route_ref[...]
    moe = route[:, 4:5] * _from_row_tiles(b1_s[slot]) + route[:, 5:6] * _from_row_tiles(b2_s[slot])
    y_ref[...] = x1_ref[...] + mod_ref[0][5:6] * moe


def _combine(idx, seg_start, ys, x1, route, mod3, mod_row, seq_len):
    n = x1.shape[0]
    tm = TM_PRE
    tiles_per_seq = seq_len // tm
    row = lambda w: pl.BlockSpec((tm, w), lambda i, *_: (i, 0))
    return pl.pallas_call(
        _combine_kernel,
        grid_spec=pltpu.PrefetchScalarGridSpec(
            num_scalar_prefetch=2,
            grid=(n // tm,),
            in_specs=[pl.BlockSpec(memory_space=pl.ANY), row(D_MODEL), row(LANES),
                      pl.BlockSpec((1, 6, D_MODEL),
                                   lambda i, *_: (mod_row(i * tm), 0, 0))],
            out_specs=row(D_MODEL),
            scratch_shapes=[pltpu.VMEM((2, tm) + ROW_TILE, F32), pltpu.VMEM((2, tm) + ROW_TILE, F32),
                            pltpu.SemaphoreType.DMA((2,))]),
        out_shape=jax.ShapeDtypeStruct((n, D_MODEL), F32),
        compiler_params=_cparams(("arbitrary",)),
        name="combine",
    )(idx, seg_start, ys, x1, route, mod3)


def _rope_tables(length):
    rows = length // GRID_W
    r, col = jnp.meshgrid(jnp.arange(rows), jnp.arange(GRID_W), indexing='ij')
    r = r.reshape(-1).astype(F32)
    col = col.reshape(-1).astype(F32)
    half = HEAD_DIM // 2
    inv = ROPE_THETA ** (-jnp.arange(0, half, 2, dtype=F32) / half)
    ang_r = r[:, None] * inv
    ang_c = col[:, None] * inv
    ang = jnp.concatenate([ang_r, ang_r, ang_c, ang_c], axis=-1)
    sign = jnp.where((jnp.arange(HEAD_DIM) // (HEAD_DIM // 4)) % 2 == 0, -1.0, 1.0).astype(F32)
    cos = jnp.tile(jnp.cos(ang), (1, LANES // HEAD_DIM))
    sin = jnp.tile(jnp.sin(ang) * sign, (1, LANES // HEAD_DIM))
    return cos, sin


def _block_diag(w):
    eye = jnp.eye(LRU_BLOCKS, dtype=w.dtype)
    return jnp.einsum('hij,hg->higj', w, eye).reshape(LRU_W, LRU_W)


def _expand_cache_kernel(k_ref, v_ref, kx_ref, vx_ref):
    for ref, xref, one in ((k_ref, kx_ref, 0.0), (v_ref, vx_ref, 1.0)):
        heads = jnp.swapaxes(ref[0, 0], 0, 1)
        col = jnp.concatenate([heads[h] for h in range(N_KV_HEADS)], axis=1)
        _store_expanded(xref, col, one)


def _expand_cache(cache_k, cache_v):
    b, _, t, _, _ = cache_k.shape
    src = pl.BlockSpec((1, 1, t, N_KV_HEADS, HEAD_DIM), lambda i: (i, 0, 0, 0, 0))
    dst = pl.BlockSpec((t, 4 * LANES), lambda i: (i, 0))
    return pl.pallas_call(
        _expand_cache_kernel,
        grid=(b,),
        in_specs=[src, src], out_specs=[dst, dst],
        out_shape=[jax.ShapeDtypeStruct((b * t, 4 * LANES), BF16)] * 2,
        compiler_params=_cparams(("arbitrary",)),
        name="expand_cache",
    )(cache_k, cache_v)


def kernel(x_prompt, x_sample, cache_k, cache_v, state_lru, c, c_ctx, w_mod, b_mod, norm1, norm2,
           w_in, q_norm, k_norm, conv_w, conv_b, lru_wa, lru_ba, lru_wx, lru_bx, lru_lambda, w_out,
           router_grp_w, router_grp_b, router_exp_w, router_exp_b, exp_w_gate, exp_w_up, exp_w_down):
    batch, seq, _ = x_prompt.shape
    dec_batch, dec_seq, _ = x_sample.shape
    past = cache_k.shape[2]
    depth = w_mod.shape[0]
    assert depth == 1

    cvec = jnp.concatenate(
        [c_ctx[None, :], c, jnp.zeros((MOD_ROWS - 1 - dec_batch, D_MODEL), F32)], axis=0)
    mod3 = _modulation(cvec, w_mod[0], b_mod[0][None, :]).reshape(MOD_ROWS, 6, D_MODEL)

    head_id = jnp.arange(QK_W) // HEAD_DIM
    heads = (head_id[:, None] == jnp.arange(LANES)[None, :]).astype(BF16)
    gqk = jnp.concatenate([jnp.tile(q_norm[0], N_HEADS), jnp.tile(k_norm[0], N_KV_HEADS)])[None, :]
    wf = (0.5 * jnp.concatenate([_block_diag(lru_wa[0, 0]), _block_diag(lru_wx[0, 0])], axis=1)).astype(BF16)
    wb = (0.5 * jnp.concatenate([_block_diag(lru_wa[0, 1]), _block_diag(lru_wx[0, 1])], axis=1)).astype(BF16)
    bf = 0.5 * jnp.concatenate([lru_ba[0, 0], lru_bx[0, 0]])[None, :]
    bb = 0.5 * jnp.concatenate([lru_ba[0, 1], lru_bx[0, 1]])[None, :]
    pad = LANES - N_EXPERTS - N_GROUPS
    wr = jnp.concatenate([router_exp_w[0], router_grp_w[0], jnp.zeros((D_MODEL, pad), F32)], axis=1)
    wr_hi = wr.astype(BF16)
    wr2 = jnp.concatenate([wr_hi, (wr - wr_hi.astype(F32)).astype(BF16)], axis=1)
    br = jnp.concatenate([router_exp_b[0], router_grp_b[0], jnp.zeros((pad,), F32)])[None, :]
    g1 = norm1[0][None, :]
    g2 = norm2[0][None, :]
    cw = conv_w[0]
    cb = conv_b[0][None, :]
    lam = lru_lambda[0]
    tri = (jnp.arange(TM_PRE)[:, None] > jnp.arange(TM_PRE)[None, :]).astype(BF16)

    def mixers(x, seq_len, mod_row, tables, extra_k, extra_v, h0, cnt_in):
        q, kx, vx, xr, gb, *cache = _pre(x, mod3, mod_row, g1, w_in[0], heads, gqk, tables, seq_len)
        k_segs = [(kx, seq_len)] + extra_k
        v_segs = [(vx, seq_len)] + extra_v
        attn = _attention(q, k_segs, v_segs, seq_len)
        rec, fin = _lru(xr, gb, h0, cw, cb, wf, wb, bf, bb, lam, seq_len)
        x1, h2, route, cnt = _post(attn, rec, x, mod3, mod_row, g2, w_out[0], wr2, br, tri, cnt_in)
        return x1, h2, route, cnt, cache, fin

    mod_row_p = lambda tok: 0
    mod_row_s = lambda tok: tok // dec_seq + 1
    xp = x_prompt.reshape(batch * seq, D_MODEL)
    x1p, h2p, route_p, cnt_p, (kf, vf), fin = mixers(
        xp, seq, mod_row_p, None, [], [], jnp.zeros((batch, 2, LRU_W), F32),
        jnp.zeros((1, LANES), F32))
    xs = x_sample.reshape(dec_batch * dec_seq, D_MODEL)
    ck, cv = _expand_cache(cache_k, cache_v)
    x1s, h2s, route_s, cnt_all, _, _ = mixers(
        xs, dec_seq, mod_row_s, _rope_tables(dec_seq), [(ck, past)], [(cv, past)],
        state_lru[:, 0], cnt_p)

    n_prompt = batch * seq
    n_tok = n_prompt + dec_batch * dec_seq
    n_tiles = (TOP_K * n_tok + N_EXPERTS * (TR_MOE - 1)) // TR_MOE
    cnt = cnt_all[0, :N_EXPERTS].astype(jnp.int32)
    ntile = (cnt + TR_MOE - 1) // TR_MOE
    tile_end = jnp.cumsum(ntile)
    seg_start = (tile_end - ntile) * TR_MOE
    n_used = tile_end[-1:]
    pad_start = seg_start + cnt
    pad_rows = tile_end * TR_MOE - pad_start

    idx_p = route_p[:, :2 * TOP_K].astype(jnp.int32).reshape(-1)
    idx_s = route_s[:, :2 * TOP_K].astype(jnp.int32).reshape(-1)
    xsort = _dispatch(jnp.concatenate([idx_p, idx_s]), seg_start, pad_start, pad_rows, n_used, h2p, h2s,
                      n_tiles * TR_MOE)
    ysort = _ffn(tile_end, xsort, exp_w_gate[0], exp_w_up[0], exp_w_down[0], n_tiles)
    yp = _combine(idx_p, seg_start, ysort, x1p, route_p, mod3, mod_row_p, seq)
    ys = _combine(idx_s, seg_start, ysort, x1s, route_s, mod3, mod_row_s, dec_seq)

    return (yp.reshape(batch, seq, D_MODEL),
            ys.reshape(dec_batch, dec_seq, D_MODEL),
            kf.reshape(batch, 1, seq, N_KV_HEADS, HEAD_DIM),
            vf.reshape(batch, 1, seq, N_KV_HEADS, HEAD_DIM),
            fin.reshape(batch, 1, 2, LRU_W))
```

```python
import functools

import jax
import jax.numpy as jnp
from jax import lax
from jax.experimental import pallas as pl
from jax.experimental.pallas import tpu as pltpu

F32 = jnp.float32
BF16 = jnp.bfloat16

D_MODEL = 1024
GRID_W = 64
ATTN_W = 512
LRU_W = 512
HEAD_DIM = 64
N_HEADS = 8
N_KV_HEADS = 2
KV_W = N_KV_HEADS * HEAD_DIM
LRU_BLOCKS = 8
LRU_BLOCK_W = LRU_W // LRU_BLOCKS
CONV_W = 4
LRU_C = 8.0
IN_W = ATTN_W + 2 * KV_W + 2 * LRU_W
QK_W = ATTN_W + KV_W
N_GROUPS = 4
EXPERTS_PER_GROUP = 8
N_EXPERTS = N_GROUPS * EXPERTS_PER_GROUP
TOP_K = 2
EXPERT_FF = D_MODEL // 4
ROPE_THETA = 10000.0
EPS = 1e-6

LANES = 128
SUBLANES = 8
MOD_ROWS = 8
VMEM_LIMIT = 48 * 1024 * 1024

TM_PRE = 512
TQ_ATT = 1024
TC_LRU = 256
TR_MOE = 256
CH_DISPATCH = 1024
FFN_IN_SLOTS = 4
FFN_OUT_SLOTS = 3


def _cparams(sem):
    return pltpu.CompilerParams(dimension_semantics=sem, vmem_limit_bytes=VMEM_LIMIT)


def _dot(a, b):
    return jnp.dot(a, b, preferred_element_type=F32)


def _dot_nt(a, b):
    return lax.dot_general(a, b, (((1,), (1,)), ((), ())), preferred_element_type=F32)


ROW_TILE = (D_MODEL // LANES, LANES)
LOG2_E = 1.4426950408889634
Q_SCALE = HEAD_DIM ** -0.5 * LOG2_E
ONES_LANE_EVEN = HEAD_DIM
ONES_LANE_ODD = 0


def _to_row_tiles(x):
    cols = jnp.stack([x[:, c * LANES:(c + 1) * LANES] for c in range(D_MODEL // LANES)], axis=0)
    return jnp.swapaxes(cols, 0, 1)


def _from_row_tiles(x3):
    cols = jnp.swapaxes(x3, 0, 1)
    return jnp.concatenate([cols[c] for c in range(D_MODEL // LANES)], axis=1)


def _sigmoid(x):
    return 0.5 * jnp.tanh(0.5 * x) + 0.5


def _split_bf16(x):
    hi = x.astype(BF16)
    lo = (x - hi.astype(F32)).astype(BF16)
    return hi, lo


def _mod_kernel(c_ref, w_ref, b_ref, o_ref):
    c = c_ref[...]
    s = (c * jax.nn.sigmoid(c)).astype(BF16)
    o_ref[...] = _dot(s, w_ref[...].astype(BF16)) + b_ref[...]


def _modulation(cvec, w_mod, b_mod):
    n_out = w_mod.shape[1]
    tn = n_out // 4
    return pl.pallas_call(
        _mod_kernel,
        grid=(n_out // tn,),
        in_specs=[pl.BlockSpec((MOD_ROWS, D_MODEL), lambda j: (0, 0)),
                  pl.BlockSpec((D_MODEL, tn), lambda j: (0, j)),
                  pl.BlockSpec((1, tn), lambda j: (0, j))],
        out_specs=pl.BlockSpec((MOD_ROWS, tn), lambda j: (0, j)),
        out_shape=jax.ShapeDtypeStruct((MOD_ROWS, n_out), F32),
        compiler_params=_cparams(("arbitrary",)),
        name="modulation",
    )(cvec, w_mod, b_mod)


def _split_heads(col):
    return jnp.swapaxes(jnp.stack([col[:, h * HEAD_DIM:(h + 1) * HEAD_DIM]
                                   for h in range(N_KV_HEADS)], axis=0), 0, 1)


def _store_expanded(xref, col, one):
    lane = lax.broadcasted_iota(jnp.int32, col.shape, 1)
    lo_half = lane < HEAD_DIM
    swapped = pltpu.roll(col, HEAD_DIM, 1)
    fill_hi = jnp.where(lane == ONES_LANE_EVEN, one, 0.0)
    fill_lo = jnp.where(lane == ONES_LANE_ODD, one, 0.0)
    xref[:, 0 * LANES:1 * LANES] = jnp.where(lo_half, col, fill_hi).astype(BF16)
    xref[:, 1 * LANES:2 * LANES] = jnp.where(lo_half, fill_lo, swapped).astype(BF16)
    xref[:, 2 * LANES:3 * LANES] = jnp.where(lo_half, swapped, fill_hi).astype(BF16)
    xref[:, 3 * LANES:4 * LANES] = jnp.where(lo_half, fill_lo, col).astype(BF16)


def _pre_kernel(*refs, rope):
    if rope:
        (x_ref, mod_ref, g1_ref, win_ref, heads_ref, heads_t_ref, gqk_ref, cos_ref, sin_ref,
         q_ref, kx_ref, vx_ref, xr_ref, gb_ref, win_s) = refs
    else:
        (x_ref, mod_ref, g1_ref, win_ref, heads_ref, heads_t_ref, gqk_ref,
         q_ref, kx_ref, vx_ref, xr_ref, gb_ref, kf_ref, vf_ref, win_s) = refs

    @pl.when(pl.program_id(0) == 0)
    def _():
        win_s[...] = win_ref[...].astype(BF16)

    x = x_ref[...]
    m = mod_ref[0]
    ms = jnp.mean(x * x, axis=-1, keepdims=True)
    y = x * lax.rsqrt(ms + EPS) * g1_ref[...]
    h = y * (1.0 + m[1:2]) + m[0:1]
    z = _dot(h.astype(BF16), win_s[...])

    qk = z[:, :QK_W]
    ss = _dot((qk * qk).astype(BF16), heads_ref[...])
    hi, lo = _split_bf16(lax.rsqrt(ss * (1.0 / HEAD_DIM) + EPS))
    qk = qk * (_dot(hi, heads_t_ref[...]) + _dot(lo, heads_t_ref[...])) * gqk_ref[...]

    lane = lax.broadcasted_iota(jnp.int32, (x.shape[0], LANES), 1)
    cols = []
    for c in range(QK_W // LANES):
        xc = qk[:, c * LANES:(c + 1) * LANES]
        if rope:
            left = pltpu.roll(xc, LANES - HEAD_DIM // 4, 1)
            right = pltpu.roll(xc, HEAD_DIM // 4, 1)
            rot = jnp.where((lane // (HEAD_DIM // 4)) % 2 == 0, left, right)
            xc = xc * cos_ref[...] + rot * sin_ref[...]
        cols.append(xc)
    for c in range(ATTN_W // LANES):
        q_ref[:, c * LANES:(c + 1) * LANES] = (cols[c] * Q_SCALE).astype(BF16)

    k_col = cols[ATTN_W // LANES]
    v_col = z[:, QK_W:QK_W + KV_W]
    _store_expanded(kx_ref, k_col, 0.0)
    _store_expanded(vx_ref, v_col, 1.0)
    if not rope:
        kf_ref[...] = _split_heads(k_col)
        vf_ref[...] = _split_heads(v_col)

    xr_ref[...] = z[:, QK_W + KV_W:QK_W + KV_W + LRU_W]
    gb_ref[...] = z[:, QK_W + KV_W + LRU_W:]


def _pre(x, mod3, mod_row, g1, w_in, heads, gqk, tables, seq_len):
    n = x.shape[0]
    tm = TM_PRE
    tiles_per_seq = seq_len // tm
    rope = tables is not None
    const = lambda i: (0, 0)
    in_specs = [pl.BlockSpec((tm, D_MODEL), lambda i: (i, 0)),
                pl.BlockSpec((1, 6, D_MODEL), lambda i: (mod_row(i * tm), 0, 0)),
                pl.BlockSpec((1, D_MODEL), const),
                pl.BlockSpec((D_MODEL, IN_W), const),
                pl.BlockSpec((QK_W, LANES), const),
                pl.BlockSpec((LANES, QK_W), const),
                pl.BlockSpec((1, QK_W), const)]
    args = [x, mod3, g1, w_in, heads, heads.T, gqk]
    if rope:
        in_specs += [pl.BlockSpec((tm, LANES), lambda i: (i % tiles_per_seq, 0))] * 2
        args += list(tables)
    row = lambda w: pl.BlockSpec((tm, w), lambda i: (i, 0))
    out_shape = [jax.ShapeDtypeStruct((n, ATTN_W), BF16),
                 jax.ShapeDtypeStruct((n, 4 * LANES), BF16),
                 jax.ShapeDtypeStruct((n, 4 * LANES), BF16),
                 jax.ShapeDtypeStruct((n, LRU_W), F32),
                 jax.ShapeDtypeStruct((n, LRU_W), F32)]
    out_specs = [row(ATTN_W), row(4 * LANES), row(4 * LANES), row(LRU_W), row(LRU_W)]
    if not rope:
        cache = pl.BlockSpec((tm, N_KV_HEADS, HEAD_DIM), lambda i: (i, 0, 0))
        out_shape += [jax.ShapeDtypeStruct((n, N_KV_HEADS, HEAD_DIM), F32)] * 2
        out_specs += [cache, cache]
    return pl.pallas_call(
        functools.partial(_pre_kernel, rope=rope),
        grid=(n // tm,),
        in_specs=in_specs, out_specs=out_specs, out_shape=out_shape,
        scratch_shapes=[pltpu.VMEM((D_MODEL, IN_W), BF16)],
        compiler_params=_cparams(("arbitrary",)),
        name="pre_rope" if rope else "pre",
    )(*args)


def _attn_kernel(*refs, n_seg):
    q_ref = refs[0]
    k_refs = refs[1:1 + n_seg]
    v_refs = refs[1 + n_seg:1 + 2 * n_seg]
    o_ref = refs[1 + 2 * n_seg]
    lane = lax.broadcasted_iota(jnp.int32, (q_ref.shape[0], LANES), 1)
    for c in range(ATTN_W // LANES):
        qc = q_ref[:, c * LANES:(c + 1) * LANES]
        g = c // 2
        accs = []
        for par in range(2):
            sl = slice((2 * g + par) * LANES, (2 * g + par + 1) * LANES)
            ss = [_dot_nt(qc, k[:, sl]) for k in k_refs]
            mx = functools.reduce(jnp.maximum, [jnp.max(s, axis=-1, keepdims=True) for s in ss])
            ps = [jnp.exp2((s - mx).astype(BF16)) for s in ss]
            accs.append(functools.reduce(lambda a, b: a + b,
                                         [_dot(p, v[:, sl]) for p, v in zip(ps, v_refs)]))
        even = accs[0] / accs[0][:, ONES_LANE_EVEN:ONES_LANE_EVEN + 1]
        odd = accs[1] / accs[1][:, ONES_LANE_ODD:ONES_LANE_ODD + 1]
        o_ref[:, c * LANES:(c + 1) * LANES] = jnp.where(lane < HEAD_DIM, even, odd).astype(BF16)


def _attention(q, k_segs, v_segs, seq_len):
    n = q.shape[0]
    tq = min(TQ_ATT, seq_len)
    nq = seq_len // tq
    n_seg = len(k_segs)
    in_specs = [pl.BlockSpec((tq, ATTN_W), lambda b, i: (b * nq + i, 0))]
    for arr, t in list(k_segs) + list(v_segs):
        in_specs.append(pl.BlockSpec((t, 4 * LANES), lambda b, i: (b, 0)))
    return pl.pallas_call(
        functools.partial(_attn_kernel, n_seg=n_seg),
        grid=(n // seq_len, nq),
        in_specs=in_specs,
        out_specs=pl.BlockSpec((tq, ATTN_W), lambda b, i: (b * nq + i, 0)),
        out_shape=jax.ShapeDtypeStruct((n, ATTN_W), BF16),
        compiler_params=_cparams(("arbitrary", "arbitrary")),
        name="attention_%dseg" % n_seg,
    )(q, *[a for a, _ in k_segs], *[a for a, _ in v_segs])


def _log_sigmoid(x):
    return jnp.minimum(x, 0.0) - jnp.log1p(jnp.exp(-jnp.abs(x)))


def _tile_scan(a, b, reverse):
    row = lax.broadcasted_iota(jnp.int32, a.shape, 0)
    d = 1
    while d < SUBLANES:
        if reverse:
            keep = row < SUBLANES - d
            shift = SUBLANES - d
        else:
            keep = row >= d
            shift = d
        a_sh = jnp.where(keep, pltpu.roll(a, shift, 0), 1.0)
        b_sh = jnp.where(keep, pltpu.roll(b, shift, 0), 0.0)
        b = a * b_sh + b
        a = a * a_sh
        d *= 2
    return a, b


def _lru_kernel(xr_ref, gb_ref, h0_ref, cw_ref, cb_ref, wf_ref, wb_ref, bf_ref, bb_ref, lam_ref,
                rec_ref, fin_ref, xpad_s, xc_s, hf_s, a_s, b_s, *, seq_len):
    tc = TC_LRU
    n_chunks = seq_len // tc
    n_tiles = tc // SUBLANES
    zpad = jnp.zeros((SUBLANES, LRU_W), F32)
    xpad_s[0:SUBLANES, :] = zpad
    xpad_s[SUBLANES:SUBLANES + seq_len, :] = xr_ref[...]
    xpad_s[SUBLANES + seq_len:2 * SUBLANES + seq_len, :] = zpad

    half_cl = (0.5 * LRU_C * LOG2_E) * _log_sigmoid(lam_ref[...])

    def gates(xcc, w_ref, bias_ref, half_cl_d):
        t = jnp.tanh(_dot(xcc.astype(BF16), w_ref[...]) + bias_ref[...])
        a = jnp.exp2(t[:, :LRU_W] * half_cl_d + half_cl_d)
        half_x = 0.5 * xcc
        a_s[...] = a
        b_s[...] = jnp.sqrt(1.0 - a * a) * (t[:, LRU_W:] * half_x + half_x)

    h = h0_ref[0, 0:1, :]
    for c in range(n_chunks):
        base = c * tc
        xcc = cb_ref[...] + functools.reduce(
            lambda u, v: u + v,
            [cw_ref[j:j + 1, :] * xpad_s[base + SUBLANES - 1 + j:base + SUBLANES - 1 + j + tc, :]
             for j in range(CONV_W)])
        xc_s[base:base + tc, :] = xcc
        gates(xcc, wf_ref, bf_ref, half_cl[0:1])

        def fwd_tile(t, hc, base=base):
            r0 = pl.multiple_of(t * SUBLANES, SUBLANES)
            ca, cb = _tile_scan(a_s[pl.ds(r0, SUBLANES), :], b_s[pl.ds(r0, SUBLANES), :], False)
            hh = ca * hc + cb
            hf_s[pl.ds(base + r0, SUBLANES), :] = hh
            return hh[SUBLANES - 1:SUBLANES, :]

        h = lax.fori_loop(0, n_tiles, fwd_tile, h, unroll=4)
    fin_ref[0, 0:1, :] = h

    h = h0_ref[0, 1:2, :]
    for c in reversed(range(n_chunks)):
        base = c * tc
        gates(xc_s[base:base + tc, :], wb_ref, bb_ref, half_cl[1:2])

        def bwd_tile(t, hc, base=base):
            r0 = pl.multiple_of((n_tiles - 1 - t) * SUBLANES, SUBLANES)
            ca, cb = _tile_scan(a_s[pl.ds(r0, SUBLANES), :], b_s[pl.ds(r0, SUBLANES), :], True)
            hh = ca * hc + cb
            gate = jax.nn.gelu(gb_ref[pl.ds(base + r0, SUBLANES), :], approximate=True)
            rec_ref[pl.ds(base + r0, SUBLANES), :] = (
                (hf_s[pl.ds(base + r0, SUBLANES), :] + hh) * gate).astype(rec_ref.dtype)
            return hh[0:1, :]

        h = lax.fori_loop(0, n_tiles, bwd_tile, h, unroll=4)
    fin_ref[0, 1:2, :] = h


def _lru(xr, gb, h0, conv_w, conv_b, wf, wb, bf, bb, lam, seq_len):
    n = xr.shape[0]
    batch = n // seq_len
    const = lambda b: (0, 0)
    seq = pl.BlockSpec((seq_len, LRU_W), lambda b: (b, 0))
    st = pl.BlockSpec((1, 2, LRU_W), lambda b: (b, 0, 0))
    return pl.pallas_call(
        functools.partial(_lru_kernel, seq_len=seq_len),
        grid=(batch,),
        in_specs=[seq, seq, st,
                  pl.BlockSpec((CONV_W, LRU_W), const), pl.BlockSpec((1, LRU_W), const),
                  pl.BlockSpec((LRU_W, 2 * LRU_W), const), pl.BlockSpec((LRU_W, 2 * LRU_W), const),
                  pl.BlockSpec((1, 2 * LRU_W), const), pl.BlockSpec((1, 2 * LRU_W), const),
                  pl.BlockSpec((2, LRU_W), const)],
        out_specs=[seq, st],
        out_shape=[jax.ShapeDtypeStruct((n, LRU_W), BF16),
                   jax.ShapeDtypeStruct((batch, 2, LRU_W), F32)],
        scratch_shapes=[pltpu.VMEM((seq_len + 2 * SUBLANES, LRU_W), F32),
                        pltpu.VMEM((seq_len, LRU_W), F32),
                        pltpu.VMEM((seq_len, LRU_W), F32),
                        pltpu.VMEM((TC_LRU, LRU_W), F32),
                        pltpu.VMEM((TC_LRU, LRU_W), F32)],
        compiler_params=_cparams(("arbitrary",)),
        name="lru_%d" % seq_len,
    )(xr, gb, h0, conv_w, conv_b, wf, wb, bf, bb, lam)


def _post_kernel(attn_ref, rec_ref, x_ref, mod_ref, g2_ref, wo_ref, wr2_ref, br_ref,
                 tri_ref, cnt_in_ref, x1_ref, h2_ref, route_ref, cnt_ref, cnt_s, wo_s):
    @pl.when(pl.program_id(0) == 0)
    def _():
        cnt_s[...] = cnt_in_ref[...]
        wo_s[...] = wo_ref[...].astype(BF16)

    m = mod_ref[0]
    u = _dot(attn_ref[...], wo_s[:ATTN_W, :]) + _dot(rec_ref[...], wo_s[ATTN_W:, :])
    x1 = x_ref[...] + m[2:3] * u
    x1_ref[...] = x1
    ms = jnp.mean(x1 * x1, axis=-1, keepdims=True)
    h2 = x1 * lax.rsqrt(ms + EPS) * g2_ref[...]
    h2 = h2 * (1.0 + m[4:5]) + m[3:4]
    hi, lo = _split_bf16(h2)
    h2_ref[...] = _to_row_tiles(h2)

    hw = _dot(hi, wr2_ref[...])
    logits = hw[:, :LANES] + hw[:, LANES:] + _dot(lo, wr2_ref[:, :LANES]) + br_ref[...]
    lane_i = lax.broadcasted_iota(jnp.int32, logits.shape, 1)
    lane = lane_i.astype(F32)
    lane_group = (lane_i // EXPERTS_PER_GROUP).astype(F32)
    neg = -jnp.inf
    big = float(1 << 20)
    gmask = (lane_i >= N_EXPERTS) & (lane_i < N_EXPERTS + N_GROUPS)
    gl = jnp.where(gmask, logits, neg)
    gmax = jnp.max(gl, axis=-1, keepdims=True)
    gidx = jnp.min(jnp.where(gl == gmax, lane - N_EXPERTS, big), axis=-1, keepdims=True)
    p_sel = 1.0 / jnp.sum(jnp.where(gmask, jnp.exp(gl - gmax), 0.0), axis=-1, keepdims=True)

    emask = (lane_i < N_EXPERTS) & (lane_group == gidx)
    el = jnp.where(emask, logits, neg)
    v1 = jnp.max(el, axis=-1, keepdims=True)
    i1 = jnp.min(jnp.where(el == v1, lane, big), axis=-1, keepdims=True)
    el2 = jnp.where(lane == i1, neg, el)
    v2 = jnp.max(el2, axis=-1, keepdims=True)
    i2 = jnp.min(jnp.where(el2 == v2, lane, big), axis=-1, keepdims=True)
    e2 = jnp.exp(v2 - v1)
    w1 = p_sel / (1.0 + e2)
    w2 = p_sel * e2 / (1.0 + e2)

    oh1 = lane == i1
    oh2 = lane == i2
    oh = jnp.where(oh1, 1.0, 0.0) + jnp.where(oh2, 1.0, 0.0)
    before = _dot(tri_ref[...], oh.astype(BF16)) + cnt_s[...]
    rank1 = jnp.sum(jnp.where(oh1, before, 0.0), axis=-1, keepdims=True)
    rank2 = jnp.sum(jnp.where(oh2, before, 0.0), axis=-1, keepdims=True)
    cnt = cnt_s[...] + jnp.sum(oh, axis=0, keepdims=True)
    cnt_s[...] = cnt
    cnt_ref[...] = cnt
    fields = (i1, i2, rank1, rank2, w1, w2)
    route = jnp.zeros(logits.shape, F32)
    for k, val in enumerate(fields):
        route = jnp.where(lane_i == k, val, route)
    route_ref[...] = route


def _post(attn, rec, x, mod3, mod_row, g2, w_out, wr2, br, tri, cnt_in):
    n = x.shape[0]
    tm = TM_PRE
    const = lambda i: (0, 0)
    row = lambda w: pl.BlockSpec((tm, w), lambda i: (i, 0))
    return pl.pallas_call(
        _post_kernel,
        grid=(n // tm,),
        in_specs=[row(ATTN_W), row(LRU_W), row(D_MODEL),
                  pl.BlockSpec((1, 6, D_MODEL), lambda i: (mod_row(i * tm), 0, 0)),
                  pl.BlockSpec((1, D_MODEL), const),
                  pl.BlockSpec((D_MODEL, D_MODEL), const),
                  pl.BlockSpec((D_MODEL, 2 * LANES), const),
                  pl.BlockSpec((1, LANES), const),
                  pl.BlockSpec((tm, tm), const),
                  pl.BlockSpec((1, LANES), const)],
        out_specs=[row(D_MODEL), pl.BlockSpec((tm,) + ROW_TILE, lambda i: (i, 0, 0)), row(LANES),
                   pl.BlockSpec((1, LANES), const)],
        out_shape=[jax.ShapeDtypeStruct((n, D_MODEL), F32),
                   jax.ShapeDtypeStruct((n,) + ROW_TILE, F32),
                   jax.ShapeDtypeStruct((n, LANES), F32),
                   jax.ShapeDtypeStruct((1, LANES), F32)],
        scratch_shapes=[pltpu.VMEM((1, LANES), F32), pltpu.VMEM((D_MODEL, D_MODEL), BF16)],
        compiler_params=_cparams(("arbitrary",)),
        name="post",
    )(attn, rec, x, mod3, g2, w_out, wr2, br, tri, cnt_in)


def _row_copy(src_ref, src_row, dst_ref, dst_row, sem):
    return pltpu.make_async_copy(src_ref.at[pl.ds(src_row, 1)], dst_ref.at[pl.ds(dst_row, 1)], sem)


def _plan_kernel(route_ref, seg_ref, slots_ref):
    route = route_ref[...]
    lane_i = lax.broadcasted_iota(jnp.int32, route.shape, 1)
    lane = lane_i.astype(F32)
    seg = seg_ref[...]
    out = jnp.zeros(route.shape, F32)
    for k in range(TOP_K):
        start = jnp.sum(jnp.where(lane == route[:, k:k + 1], seg, 0.0), axis=-1, keepdims=True)
        out = jnp.where(lane_i == k, start + route[:, TOP_K + k:TOP_K + k + 1], out)
    slots_ref[...] = jnp.transpose(out)[:SUBLANES, :].astype(jnp.int32)


def _plan(route, seg_row):
    n = route.shape[0]
    tm = TM_PRE
    return pl.pallas_call(
        _plan_kernel,
        grid=(n // tm,),
        in_specs=[pl.BlockSpec((tm, LANES), lambda i: (i, 0)),
                  pl.BlockSpec((1, LANES), lambda i: (0, 0))],
        out_specs=pl.BlockSpec((SUBLANES, tm), lambda i: (0, i)),
        out_shape=jax.ShapeDtypeStruct((SUBLANES, n), jnp.int32),
        compiler_params=_cparams(("arbitrary",)),
        name="plan",
    )(route, seg_row)


def _dispatch_kernel(slots_p_ref, slots_s_ref, pad_start_ref, pad_rows_ref, nu_ref, h2p_ref, h2s_ref,
                     xs_ref, zero_s, sem, zsem, *, n_prompt):
    i = pl.program_id(0)
    ch = CH_DISPATCH

    def zero_fill(op):
        for e in range(N_EXPERTS):
            rows = pad_rows_ref[e]
            for b in range(TR_MOE.bit_length() - 1):
                size = 1 << b
                off = (rows >> (b + 1)) << (b + 1)

                @pl.when(((rows >> b) & 1) == 1)
                def _():
                    op(pltpu.make_async_copy(zero_s.at[pl.ds(0, size)],
                                             xs_ref.at[pl.ds(pad_start_ref[e] + off, size)], zsem))

        def zero_tile(t, carry):
            op(pltpu.make_async_copy(
                zero_s, xs_ref.at[pl.ds(pl.multiple_of(t * TR_MOE, TR_MOE), TR_MOE)], zsem))
            return carry
        lax.fori_loop(nu_ref[0], xs_ref.shape[0] // TR_MOE, zero_tile, 0)

    @pl.when(i == 0)
    def _():
        zero_s[...] = jnp.zeros_like(zero_s)
        zero_fill(lambda cp: cp.start())

    @pl.when(i == pl.num_programs(0) - 1)
    def _():
        zero_fill(lambda cp: cp.wait())

    def scatter(src_ref, slots_ref, first_token):
        def body(j, carry):
            for k in range(TOP_K):
                _row_copy(src_ref, j, xs_ref, slots_ref[k, i * ch + j - first_token], sem).start()
            return carry
        lax.fori_loop(0, ch, body, 0, unroll=8)
        for _ in range(TOP_K):
            pltpu.make_async_copy(src_ref, xs_ref.at[pl.ds(0, ch)], sem).wait()

    @pl.when(i < n_prompt // ch)
    def _():
        scatter(h2p_ref, slots_p_ref, 0)

    @pl.when(i >= n_prompt // ch)
    def _():
        scatter(h2s_ref, slots_s_ref, n_prompt)


def _dispatch(slots_p, slots_s, pad_start, pad_rows, n_used, h2p, h2s, n_rows):
    n_prompt = h2p.shape[0]
    n = n_prompt + h2s.shape[0]
    ch = CH_DISPATCH
    npc = n_prompt // ch
    last_p = npc - 1
    return pl.pallas_call(
        functools.partial(_dispatch_kernel, n_prompt=n_prompt),
        grid_spec=pltpu.PrefetchScalarGridSpec(
            num_scalar_prefetch=5,
            grid=(n // ch,),
            in_specs=[pl.BlockSpec((ch,) + ROW_TILE, lambda i, *_: (jnp.minimum(i, last_p), 0, 0)),
                      pl.BlockSpec((ch,) + ROW_TILE, lambda i, *_: (jnp.maximum(i - npc, 0), 0, 0))],
            out_specs=pl.BlockSpec(memory_space=pl.ANY),
            scratch_shapes=[pltpu.VMEM((TR_MOE,) + ROW_TILE, F32), pltpu.SemaphoreType.DMA,
                            pltpu.SemaphoreType.DMA]),
        out_shape=jax.ShapeDtypeStruct((n_rows,) + ROW_TILE, F32),
        compiler_params=_cparams(("arbitrary",)),
        name="dispatch",
    )(slots_p, slots_s, pad_start, pad_rows, n_used, h2p, h2s)


def _ffn_kernel(tend_ref, xs_ref, wg_ref, wu_ref, wd_ref, ys_ref,
                xbuf, ybuf, wg_s, wu_s, wd_s, sem_in, sem_out):
    e = pl.program_id(0)
    n_used = tend_ref[N_EXPERTS - 1]
    t_first = jnp.where(e == 0, 0, tend_ref[jnp.maximum(e - 1, 0)])
    t_last = tend_ref[e]
    n_in = xbuf.shape[0]
    n_out = ybuf.shape[0]

    def tile_rows(t):
        return pl.ds(pl.multiple_of(t * TR_MOE, TR_MOE), TR_MOE)

    def fetch(t):
        return pltpu.make_async_copy(xs_ref.at[tile_rows(t)], xbuf.at[t % n_in], sem_in.at[t % n_in])

    def writeback(t):
        return pltpu.make_async_copy(ybuf.at[t % n_out], ys_ref.at[tile_rows(t)],
                                     sem_out.at[t % n_out])

    @pl.when(e == 0)
    def _():
        for t in range(n_in - 1):
            @pl.when(t < n_used)
            def _():
                fetch(t).start()

    @pl.when(t_last > t_first)
    def _():
        wg_s[...] = wg_ref[0].astype(BF16)
        wu_s[...] = wu_ref[0].astype(BF16)
        wd_s[...] = wd_ref[0].astype(BF16)

    def tile(t, carry):
        fetch(t).wait()

        @pl.when(t + n_in - 1 < n_used)
        def _():
            fetch(t + n_in - 1).start()

        @pl.when(t >= n_out)
        def _():
            writeback(t - n_out).wait()

        x = _from_row_tiles(xbuf[t % n_in]).astype(BF16)
        hg = _dot(x, wg_s[...])
        hu = _dot(x, wu_s[...])
        act = (hg * _sigmoid(hg)) * hu
        ybuf[t % n_out] = _to_row_tiles(_dot(act.astype(BF16), wd_s[...]))
        writeback(t).start()
        return carry

    lax.fori_loop(t_first, t_last, tile, 0)

    @pl.when(e == N_EXPERTS - 1)
    def _():
        for back in range(n_out, 0, -1):
            @pl.when(n_used >= back)
            def _():
                writeback(n_used - back).wait()
        n_all = ys_ref.shape[0] // TR_MOE
        ybuf[0] = jnp.zeros(ybuf.shape[1:], F32)

        def zero_tile(t):
            return pltpu.make_async_copy(ybuf.at[0], ys_ref.at[tile_rows(t)], sem_out.at[0])

        lax.fori_loop(n_used, n_all, lambda t, c: (zero_tile(t).start(), c)[1], 0)
        lax.fori_loop(n_used, n_all, lambda t, c: (zero_tile(t).wait(), c)[1], 0)


def _ffn(tile_end, xs, wg, wu, wd, n_tiles):
    tr = TR_MOE
    wsel = lambda e, tend: (e, 0, 0)
    return pl.pallas_call(
        _ffn_kernel,
        grid_spec=pltpu.PrefetchScalarGridSpec(
            num_scalar_prefetch=1,
            grid=(N_EXPERTS,),
            in_specs=[pl.BlockSpec(memory_space=pl.ANY),
                      pl.BlockSpec((1, D_MODEL, EXPERT_FF), wsel),
                      pl.BlockSpec((1, D_MODEL, EXPERT_FF), wsel),
                      pl.BlockSpec((1, EXPERT_FF, D_MODEL), wsel)],
            out_specs=pl.BlockSpec(memory_space=pl.ANY),
            scratch_shapes=[pltpu.VMEM((FFN_IN_SLOTS, tr) + ROW_TILE, F32),
                            pltpu.VMEM((FFN_OUT_SLOTS, tr) + ROW_TILE, F32),
                            pltpu.VMEM((D_MODEL, EXPERT_FF), BF16),
                            pltpu.VMEM((D_MODEL, EXPERT_FF), BF16),
                            pltpu.VMEM((EXPERT_FF, D_MODEL), BF16),
                            pltpu.SemaphoreType.DMA((FFN_IN_SLOTS,)),
                            pltpu.SemaphoreType.DMA((FFN_OUT_SLOTS,))]),
        out_shape=jax.ShapeDtypeStruct((n_tiles * tr,) + ROW_TILE, F32),
        compiler_params=_cparams(("arbitrary",)),
        name="ffn",
    )(tile_end, xs, wg, wu, wd)


def _combine_kernel(slots_ref, ys_ref, x1_ref, route_ref, mod_ref, y_ref, b1_s, b2_s, sems):
    i = pl.program_id(0)
    tm = x1_ref.shape[0]

    def gather(step, slot):
        def body(j, carry):
            _row_copy(ys_ref, slots_ref[0, step * tm + j], b1_s.at[slot], j, sems.at[slot]).start()
            _row_copy(ys_ref, slots_ref[1, step * tm + j], b2_s.at[slot], j, sems.at[slot]).start()
            return carry
        lax.fori_loop(0, tm, body, 0, unroll=8)

    @pl.when(i == 0)
    def _():
        gather(0, 0)

    @pl.when(i + 1 < pl.num_programs(0))
    def _():
        gather(i + 1, (i + 1) % 2)

    slot = i % 2
    for buf in (b1_s, b2_s):
        pltpu.make_async_copy(ys_ref.at[pl.ds(0, tm)], buf.at[slot], sems.at[slot]).wait()
    route = route_ref[...]
    moe = route[:, 4:5] * _from_row_tiles(b1_s[slot]) + route[:, 5:6] * _from_row_tiles(b2_s[slot])
    y_ref[...] = x1_ref[...] + mod_ref[0][5:6] * moe


def _combine(slots, ys, x1, route, mod3, mod_row):
    n = x1.shape[0]
    tm = TM_PRE
    row = lambda w: pl.BlockSpec((tm, w), lambda i, *_: (i, 0))
    return pl.pallas_call(
        _combine_kernel,
        grid_spec=pltpu.PrefetchScalarGridSpec(
            num_scalar_prefetch=1,
            grid=(n // tm,),
            in_specs=[pl.BlockSpec(memory_space=pl.ANY), row(D_MODEL), row(LANES),
                      pl.BlockSpec((1, 6, D_MODEL),
                                   lambda i, *_: (mod_row(i * tm), 0, 0))],
            out_specs=row(D_MODEL),
            scratch_shapes=[pltpu.VMEM((2, tm) + ROW_TILE, F32), pltpu.VMEM((2, tm) + ROW_TILE, F32),
                            pltpu.SemaphoreType.DMA((2,))]),
        out_shape=jax.ShapeDtypeStruct((n, D_MODEL), F32),
        compiler_params=_cparams(("arbitrary",)),
        name="combine",
    )(slots, ys, x1, route, mod3)


def _rope_tables(length):
    rows = length // GRID_W
    r, col = jnp.meshgrid(jnp.arange(rows), jnp.arange(GRID_W), indexing='ij')
    r = r.reshape(-1).astype(F32)
    col = col.reshape(-1).astype(F32)
    half = HEAD_DIM // 2
    inv = ROPE_THETA ** (-jnp.arange(0, half, 2, dtype=F32) / half)
    ang_r = r[:, None] * inv
    ang_c = col[:, None] * inv
    ang = jnp.concatenate([ang_r, ang_r, ang_c, ang_c], axis=-1)
    sign = jnp.where((jnp.arange(HEAD_DIM) // (HEAD_DIM // 4)) % 2 == 0, -1.0, 1.0).astype(F32)
    cos = jnp.tile(jnp.cos(ang), (1, LANES // HEAD_DIM))
    sin = jnp.tile(jnp.sin(ang) * sign, (1, LANES // HEAD_DIM))
    return cos, sin


def _block_diag(w):
    eye = jnp.eye(LRU_BLOCKS, dtype=w.dtype)
    return jnp.einsum('hij,hg->higj', w, eye).reshape(LRU_W, LRU_W)


def _expand_cache_kernel(k_ref, v_ref, kx_ref, vx_ref):
    for ref, xref, one in ((k_ref, kx_ref, 0.0), (v_ref, vx_ref, 1.0)):
        heads = jnp.swapaxes(ref[0, 0], 0, 1)
        col = jnp.concatenate([heads[h] for h in range(N_KV_HEADS)], axis=1)
        _store_expanded(xref, col, one)


def _expand_cache(cache_k, cache_v):
    b, _, t, _, _ = cache_k.shape
    src = pl.BlockSpec((1, 1, t, N_KV_HEADS, HEAD_DIM), lambda i: (i, 0, 0, 0, 0))
    dst = pl.BlockSpec((t, 4 * LANES), lambda i: (i, 0))
    return pl.pallas_call(
        _expand_cache_kernel,
        grid=(b,),
        in_specs=[src, src], out_specs=[dst, dst],
        out_shape=[jax.ShapeDtypeStruct((b * t, 4 * LANES), BF16)] * 2,
        compiler_params=_cparams(("arbitrary",)),
        name="expand_cache",
    )(cache_k, cache_v)


def kernel(x_prompt, x_sample, cache_k, cache_v, state_lru, c, c_ctx, w_mod, b_mod, norm1, norm2,
           w_in, q_norm, k_norm, conv_w, conv_b, lru_wa, lru_ba, lru_wx, lru_bx, lru_lambda, w_out,
           router_grp_w, router_grp_b, router_exp_w, router_exp_b, exp_w_gate, exp_w_up, exp_w_down):
    batch, seq, _ = x_prompt.shape
    dec_batch, dec_seq, _ = x_sample.shape
    past = cache_k.shape[2]
    depth = w_mod.shape[0]
    assert depth == 1

    cvec = jnp.concatenate(
        [c_ctx[None, :], c, jnp.zeros((MOD_ROWS - 1 - dec_batch, D_MODEL), F32)], axis=0)
    mod3 = _modulation(cvec, w_mod[0], b_mod[0][None, :]).reshape(MOD_ROWS, 6, D_MODEL)

    head_id = jnp.arange(QK_W) // HEAD_DIM
    heads = (head_id[:, None] == jnp.arange(LANES)[None, :]).astype(BF16)
    gqk = jnp.concatenate([jnp.tile(q_norm[0], N_HEADS), jnp.tile(k_norm[0], N_KV_HEADS)])[None, :]
    wf = (0.5 * jnp.concatenate([_block_diag(lru_wa[0, 0]), _block_diag(lru_wx[0, 0])], axis=1)).astype(BF16)
    wb = (0.5 * jnp.concatenate([_block_diag(lru_wa[0, 1]), _block_diag(lru_wx[0, 1])], axis=1)).astype(BF16)
    bf = 0.5 * jnp.concatenate([lru_ba[0, 0], lru_bx[0, 0]])[None, :]
    bb = 0.5 * jnp.concatenate([lru_ba[0, 1], lru_bx[0, 1]])[None, :]
    pad = LANES - N_EXPERTS - N_GROUPS
    wr = jnp.concatenate([router_exp_w[0], router_grp_w[0], jnp.zeros((D_MODEL, pad), F32)], axis=1)
    wr_hi = wr.astype(BF16)
    wr2 = jnp.concatenate([wr_hi, (wr - wr_hi.astype(F32)).astype(BF16)], axis=1)
    br = jnp.concatenate([router_exp_b[0], router_grp_b[0], jnp.zeros((pad,), F32)])[None, :]
    g1 = norm1[0][None, :]
    g2 = norm2[0][None, :]
    cw = conv_w[0]
    cb = conv_b[0][None, :]
    lam = lru_lambda[0]
    tri = (jnp.arange(TM_PRE)[:, None] > jnp.arange(TM_PRE)[None, :]).astype(BF16)

    def mixers(x, seq_len, mod_row, tables, extra_k, extra_v, h0, cnt_in):
        q, kx, vx, xr, gb, *cache = _pre(x, mod3, mod_row, g1, w_in[0], heads, gqk, tables, seq_len)
        k_segs = [(kx, seq_len)] + extra_k
        v_segs = [(vx, seq_len)] + extra_v
        attn = _attention(q, k_segs, v_segs, seq_len)
        rec, fin = _lru(xr, gb, h0, cw, cb, wf, wb, bf, bb, lam, seq_len)
        x1, h2, route, cnt = _post(attn, rec, x, mod3, mod_row, g2, w_out[0], wr2, br, tri, cnt_in)
        return x1, h2, route, cnt, cache, fin

    mod_row_p = lambda tok: 0
    mod_row_s = lambda tok: tok // dec_seq + 1
    xp = x_prompt.reshape(batch * seq, D_MODEL)
    x1p, h2p, route_p, cnt_p, (kf, vf), fin = mixers(
        xp, seq, mod_row_p, None, [], [], jnp.zeros((batch, 2, LRU_W), F32),
        jnp.zeros((1, LANES), F32))
    xs = x_sample.reshape(dec_batch * dec_seq, D_MODEL)
    ck, cv = _expand_cache(cache_k, cache_v)
    x1s, h2s, route_s, cnt_all, _, _ = mixers(
        xs, dec_seq, mod_row_s, _rope_tables(dec_seq), [(ck, past)], [(cv, past)],
        state_lru[:, 0], cnt_p)

    n_prompt = batch * seq
    n_tok = n_prompt + dec_batch * dec_seq
    n_tiles = (TOP_K * n_tok + N_EXPERTS * (TR_MOE - 1)) // TR_MOE
    cnt = cnt_all[0, :N_EXPERTS].astype(jnp.int32)
    ntile = (cnt + TR_MOE - 1) // TR_MOE
    tile_end = jnp.cumsum(ntile)
    seg_start = (tile_end - ntile) * TR_MOE
    n_used = tile_end[-1:]
    pad_start = seg_start + cnt
    pad_rows = tile_end * TR_MOE - pad_start

    seg_row = jnp.pad(seg_start.astype(F32), (0, LANES - N_EXPERTS))[None, :]
    slots_p = _plan(route_p, seg_row)
    slots_s = _plan(route_s, seg_row)
    xsort = _dispatch(slots_p, slots_s, pad_start, pad_rows, n_used, h2p, h2s,
                      n_tiles * TR_MOE)
    ysort = _ffn(tile_end, xsort, exp_w_gate[0], exp_w_up[0], exp_w_down[0], n_tiles)
    yp = _combine(slots_p, ysort, x1p, route_p, mod3, mod_row_p)
    ys = _combine(slots_s, ysort, x1s, route_s, mod3, mod_row_s)

    return (yp.reshape(batch, seq, D_MODEL),
            ys.reshape(dec_batch, dec_seq, D_MODEL),
            kf.reshape(batch, 1, seq, N_KV_HEADS, HEAD_DIM),
            vf.reshape(batch, 1, seq, N_KV_HEADS, HEAD_DIM),
            fin.reshape(batch, 1, 2, LRU_W))
```

```python
import functools

import jax
import jax.numpy as jnp
from jax import lax
from jax.experimental import pallas as pl
from jax.experimental.pallas import tpu as pltpu

F32 = jnp.float32
BF16 = jnp.bfloat16

D_MODEL = 1024
GRID_W = 64
ATTN_W = 512
LRU_W = 512
HEAD_DIM = 64
N_HEADS = 8
N_KV_HEADS = 2
KV_W = N_KV_HEADS * HEAD_DIM
LRU_BLOCKS = 8
LRU_BLOCK_W = LRU_W // LRU_BLOCKS
CONV_W = 4
LRU_C = 8.0
IN_W = ATTN_W + 2 * KV_W + 2 * LRU_W
QK_W = ATTN_W + KV_W
N_GROUPS = 4
EXPERTS_PER_GROUP = 8
N_EXPERTS = N_GROUPS * EXPERTS_PER_GROUP
TOP_K = 2
EXPERT_FF = D_MODEL // 4
ROPE_THETA = 10000.0
EPS = 1e-6

LANES = 128
SUBLANES = 8
MOD_ROWS = 8
VMEM_LIMIT = 48 * 1024 * 1024

TM_PRE = 512
TQ_ATT = 1024
TC_LRU = 256
TR_MOE = 256
CH_DISPATCH = 1024
FFN_IN_SLOTS = 4
FFN_OUT_SLOTS = 3


def _cparams(sem):
    return pltpu.CompilerParams(dimension_semantics=sem, vmem_limit_bytes=VMEM_LIMIT)


def _dot(a, b):
    return jnp.dot(a, b, preferred_element_type=F32)


def _dot_nt(a, b):
    return lax.dot_general(a, b, (((1,), (1,)), ((), ())), preferred_element_type=F32)


ROW_TILE = (D_MODEL // LANES, LANES)
LOG2_E = 1.4426950408889634
Q_SCALE = HEAD_DIM ** -0.5 * LOG2_E
ONES_LANE_EVEN = HEAD_DIM
ONES_LANE_ODD = 0


def _to_row_tiles(x):
    cols = jnp.stack([x[:, c * LANES:(c + 1) * LANES] for c in range(D_MODEL // LANES)], axis=0)
    return jnp.swapaxes(cols, 0, 1)


def _from_row_tiles(x3):
    cols = jnp.swapaxes(x3, 0, 1)
    return jnp.concatenate([cols[c] for c in range(D_MODEL // LANES)], axis=1)


def _sigmoid(x):
    return 0.5 * jnp.tanh(0.5 * x) + 0.5


def _split_bf16(x):
    hi = x.astype(BF16)
    lo = (x - hi.astype(F32)).astype(BF16)
    return hi, lo


def _mod_kernel(c_ref, w_ref, b_ref, o_ref):
    c = c_ref[...]
    s = (c * jax.nn.sigmoid(c)).astype(BF16)
    o_ref[...] = _dot(s, w_ref[...].astype(BF16)) + b_ref[...]


def _modulation(cvec, w_mod, b_mod):
    n_out = w_mod.shape[1]
    tn = n_out // 4
    return pl.pallas_call(
        _mod_kernel,
        grid=(n_out // tn,),
        in_specs=[pl.BlockSpec((MOD_ROWS, D_MODEL), lambda j: (0, 0)),
                  pl.BlockSpec((D_MODEL, tn), lambda j: (0, j)),
                  pl.BlockSpec((1, tn), lambda j: (0, j))],
        out_specs=pl.BlockSpec((MOD_ROWS, tn), lambda j: (0, j)),
        out_shape=jax.ShapeDtypeStruct((MOD_ROWS, n_out), F32),
        compiler_params=_cparams(("arbitrary",)),
        name="modulation",
    )(cvec, w_mod, b_mod)


def _split_heads(col):
    return jnp.swapaxes(jnp.stack([col[:, h * HEAD_DIM:(h + 1) * HEAD_DIM]
                                   for h in range(N_KV_HEADS)], axis=0), 0, 1)


def _store_expanded(xref, col, one):
    lane = lax.broadcasted_iota(jnp.int32, col.shape, 1)
    lo_half = lane < HEAD_DIM
    swapped = pltpu.roll(col, HEAD_DIM, 1)
    fill_hi = jnp.where(lane == ONES_LANE_EVEN, one, 0.0)
    fill_lo = jnp.where(lane == ONES_LANE_ODD, one, 0.0)
    xref[:, 0 * LANES:1 * LANES] = jnp.where(lo_half, col, fill_hi).astype(BF16)
    xref[:, 1 * LANES:2 * LANES] = jnp.where(lo_half, fill_lo, swapped).astype(BF16)
    xref[:, 2 * LANES:3 * LANES] = jnp.where(lo_half, swapped, fill_hi).astype(BF16)
    xref[:, 3 * LANES:4 * LANES] = jnp.where(lo_half, fill_lo, col).astype(BF16)


def _pre_kernel(*refs, rope):
    if rope:
        (x_ref, mod_ref, g1_ref, win_ref, heads_ref, heads_t_ref, gqk_ref, cos_ref, sin_ref,
         q_ref, kx_ref, vx_ref, xr_ref, gb_ref, win_s) = refs
    else:
        (x_ref, mod_ref, g1_ref, win_ref, heads_ref, heads_t_ref, gqk_ref,
         q_ref, kx_ref, vx_ref, xr_ref, gb_ref, kf_ref, vf_ref, win_s) = refs

    @pl.when(pl.program_id(0) == 0)
    def _():
        win_s[...] = win_ref[...].astype(BF16)

    x = x_ref[...]
    m = mod_ref[0]
    ms = jnp.mean(x * x, axis=-1, keepdims=True)
    y = x * lax.rsqrt(ms + EPS) * g1_ref[...]
    h = y * (1.0 + m[1:2]) + m[0:1]
    z = _dot(h.astype(BF16), win_s[...])

    qk = z[:, :QK_W]
    ss = _dot((qk * qk).astype(BF16), heads_ref[...])
    hi, lo = _split_bf16(lax.rsqrt(ss * (1.0 / HEAD_DIM) + EPS))
    qk = qk * (_dot(hi, heads_t_ref[...]) + _dot(lo, heads_t_ref[...])) * gqk_ref[...]

    lane = lax.broadcasted_iota(jnp.int32, (x.shape[0], LANES), 1)
    cols = []
    for c in range(QK_W // LANES):
        xc = qk[:, c * LANES:(c + 1) * LANES]
        if rope:
            left = pltpu.roll(xc, LANES - HEAD_DIM // 4, 1)
            right = pltpu.roll(xc, HEAD_DIM // 4, 1)
            rot = jnp.where((lane // (HEAD_DIM // 4)) % 2 == 0, left, right)
            xc = xc * cos_ref[...] + rot * sin_ref[...]
        cols.append(xc)
    for c in range(ATTN_W // LANES):
        q_ref[:, c * LANES:(c + 1) * LANES] = (cols[c] * Q_SCALE).astype(BF16)

    k_col = cols[ATTN_W // LANES]
    v_col = z[:, QK_W:QK_W + KV_W]
    _store_expanded(kx_ref, k_col, 0.0)
    _store_expanded(vx_ref, v_col, 1.0)
    if not rope:
        kf_ref[...] = _split_heads(k_col)
        vf_ref[...] = _split_heads(v_col)

    xr_ref[...] = z[:, QK_W + KV_W:QK_W + KV_W + LRU_W]
    gb_ref[...] = z[:, QK_W + KV_W + LRU_W:]


def _pre(x, mod3, mod_row, g1, w_in, heads, gqk, tables, seq_len):
    n = x.shape[0]
    tm = TM_PRE
    tiles_per_seq = seq_len // tm
    rope = tables is not None
    const = lambda i: (0, 0)
    in_specs = [pl.BlockSpec((tm, D_MODEL), lambda i: (i, 0)),
                pl.BlockSpec((1, 6, D_MODEL), lambda i: (mod_row(i * tm), 0, 0)),
                pl.BlockSpec((1, D_MODEL), const),
                pl.BlockSpec((D_MODEL, IN_W), const),
                pl.BlockSpec((QK_W, LANES), const),
                pl.BlockSpec((LANES, QK_W), const),
                pl.BlockSpec((1, QK_W), const)]
    args = [x, mod3, g1, w_in, heads, heads.T, gqk]
    if rope:
        in_specs += [pl.BlockSpec((tm, LANES), lambda i: (i % tiles_per_seq, 0))] * 2
        args += list(tables)
    row = lambda w: pl.BlockSpec((tm, w), lambda i: (i, 0))
    out_shape = [jax.ShapeDtypeStruct((n, ATTN_W), BF16),
                 jax.ShapeDtypeStruct((n, 4 * LANES), BF16),
                 jax.ShapeDtypeStruct((n, 4 * LANES), BF16),
                 jax.ShapeDtypeStruct((n, LRU_W), F32),
                 jax.ShapeDtypeStruct((n, LRU_W), F32)]
    out_specs = [row(ATTN_W), row(4 * LANES), row(4 * LANES), row(LRU_W), row(LRU_W)]
    if not rope:
        cache = pl.BlockSpec((tm, N_KV_HEADS, HEAD_DIM), lambda i: (i, 0, 0))
        out_shape += [jax.ShapeDtypeStruct((n, N_KV_HEADS, HEAD_DIM), F32)] * 2
        out_specs += [cache, cache]
    return pl.pallas_call(
        functools.partial(_pre_kernel, rope=rope),
        grid=(n // tm,),
        in_specs=in_specs, out_specs=out_specs, out_shape=out_shape,
        scratch_shapes=[pltpu.VMEM((D_MODEL, IN_W), BF16)],
        compiler_params=_cparams(("arbitrary",)),
        name="pre_rope" if rope else "pre",
    )(*args)


def _attn_kernel(*refs, n_seg):
    q_ref = refs[0]
    k_refs = refs[1:1 + n_seg]
    v_refs = refs[1 + n_seg:1 + 2 * n_seg]
    o_ref = refs[1 + 2 * n_seg]
    lane = lax.broadcasted_iota(jnp.int32, (q_ref.shape[0], LANES), 1)
    for c in range(ATTN_W // LANES):
        qc = q_ref[:, c * LANES:(c + 1) * LANES]
        g = c // 2
        accs = []
        for par in range(2):
            sl = slice((2 * g + par) * LANES, (2 * g + par + 1) * LANES)
            ss = [_dot_nt(qc, k[:, sl]) for k in k_refs]
            mx = functools.reduce(jnp.maximum, [jnp.max(s, axis=-1, keepdims=True) for s in ss])
            ps = [jnp.exp2((s - mx).astype(BF16)) for s in ss]
            accs.append(functools.reduce(lambda a, b: a + b,
                                         [_dot(p, v[:, sl]) for p, v in zip(ps, v_refs)]))
        even = accs[0] / accs[0][:, ONES_LANE_EVEN:ONES_LANE_EVEN + 1]
        odd = accs[1] / accs[1][:, ONES_LANE_ODD:ONES_LANE_ODD + 1]
        o_ref[:, c * LANES:(c + 1) * LANES] = jnp.where(lane < HEAD_DIM, even, odd).astype(BF16)


def _attention(q, k_segs, v_segs, seq_len):
    n = q.shape[0]
    tq = min(TQ_ATT, seq_len)
    nq = seq_len // tq
    n_seg = len(k_segs)
    in_specs = [pl.BlockSpec((tq, ATTN_W), lambda b, i: (b * nq + i, 0))]
    for arr, t in list(k_segs) + list(v_segs):
        in_specs.append(pl.BlockSpec((t, 4 * LANES), lambda b, i: (b, 0)))
    return pl.pallas_call(
        functools.partial(_attn_kernel, n_seg=n_seg),
        grid=(n // seq_len, nq),
        in_specs=in_specs,
        out_specs=pl.BlockSpec((tq, ATTN_W), lambda b, i: (b * nq + i, 0)),
        out_shape=jax.ShapeDtypeStruct((n, ATTN_W), BF16),
        compiler_params=_cparams(("arbitrary", "arbitrary")),
        name="attention_%dseg" % n_seg,
    )(q, *[a for a, _ in k_segs], *[a for a, _ in v_segs])


def _log_sigmoid(x):
    return jnp.minimum(x, 0.0) - jnp.log1p(jnp.exp(-jnp.abs(x)))


def _tile_scan(a, b, reverse):
    row = lax.broadcasted_iota(jnp.int32, a.shape, 0)
    d = 1
    while d < SUBLANES:
        if reverse:
            keep = row < SUBLANES - d
            shift = SUBLANES - d
        else:
            keep = row >= d
            shift = d
        a_sh = jnp.where(keep, pltpu.roll(a, shift, 0), 1.0)
        b_sh = jnp.where(keep, pltpu.roll(b, shift, 0), 0.0)
        b = a * b_sh + b
        a = a * a_sh
        d *= 2
    return a, b


def _lru_kernel(xr_ref, gb_ref, h0_ref, cw_ref, cb_ref, wf_ref, wb_ref, bf_ref, bb_ref, lam_ref,
                rec_ref, fin_ref, xpad_s, xc_s, hf_s, a_s, b_s, *, seq_len):
    tc = TC_LRU
    n_chunks = seq_len // tc
    n_tiles = tc // SUBLANES
    zpad = jnp.zeros((SUBLANES, LRU_W), F32)
    xpad_s[0:SUBLANES, :] = zpad
    xpad_s[SUBLANES:SUBLANES + seq_len, :] = xr_ref[...]
    xpad_s[SUBLANES + seq_len:2 * SUBLANES + seq_len, :] = zpad

    half_cl = (0.5 * LRU_C * LOG2_E) * _log_sigmoid(lam_ref[...])

    def gates(xcc, w_ref, bias_ref, half_cl_d):
        t = jnp.tanh(_dot(xcc.astype(BF16), w_ref[...]) + bias_ref[...])
        a = jnp.exp2(t[:, :LRU_W] * half_cl_d + half_cl_d)
        half_x = 0.5 * xcc
        a_s[...] = a
        b_s[...] = jnp.sqrt(1.0 - a * a) * (t[:, LRU_W:] * half_x + half_x)

    h = h0_ref[0, 0:1, :]
    for c in range(n_chunks):
        base = c * tc
        xcc = cb_ref[...] + functools.reduce(
            lambda u, v: u + v,
            [cw_ref[j:j + 1, :] * xpad_s[base + SUBLANES - 1 + j:base + SUBLANES - 1 + j + tc, :]
             for j in range(CONV_W)])
        xc_s[base:base + tc, :] = xcc
        gates(xcc, wf_ref, bf_ref, half_cl[0:1])

        def fwd_tile(t, hc, base=base):
            r0 = pl.multiple_of(t * SUBLANES, SUBLANES)
            ca, cb = _tile_scan(a_s[pl.ds(r0, SUBLANES), :], b_s[pl.ds(r0, SUBLANES), :], False)
            hh = ca * hc + cb
            hf_s[pl.ds(base + r0, SUBLANES), :] = hh
            return hh[SUBLANES - 1:SUBLANES, :]

        h = lax.fori_loop(0, n_tiles, fwd_tile, h, unroll=4)
    fin_ref[0, 0:1, :] = h

    h = h0_ref[0, 1:2, :]
    for c in reversed(range(n_chunks)):
        base = c * tc
        gates(xc_s[base:base + tc, :], wb_ref, bb_ref, half_cl[1:2])

        def bwd_tile(t, hc, base=base):
            r0 = pl.multiple_of((n_tiles - 1 - t) * SUBLANES, SUBLANES)
            ca, cb = _tile_scan(a_s[pl.ds(r0, SUBLANES), :], b_s[pl.ds(r0, SUBLANES), :], True)
            hh = ca * hc + cb
            gate = jax.nn.gelu(gb_ref[pl.ds(base + r0, SUBLANES), :], approximate=True)
            rec_ref[pl.ds(base + r0, SUBLANES), :] = (
                (hf_s[pl.ds(base + r0, SUBLANES), :] + hh) * gate).astype(rec_ref.dtype)
            return hh[0:1, :]

        h = lax.fori_loop(0, n_tiles, bwd_tile, h, unroll=4)
    fin_ref[0, 1:2, :] = h


def _lru(xr, gb, h0, conv_w, conv_b, wf, wb, bf, bb, lam, seq_len):
    n = xr.shape[0]
    batch = n // seq_len
    const = lambda b: (0, 0)
    seq = pl.BlockSpec((seq_len, LRU_W), lambda b: (b, 0))
    st = pl.BlockSpec((1, 2, LRU_W), lambda b: (b, 0, 0))
    return pl.pallas_call(
        functools.partial(_lru_kernel, seq_len=seq_len),
        grid=(batch,),
        in_specs=[seq, seq, st,
                  pl.BlockSpec((CONV_W, LRU_W), const), pl.BlockSpec((1, LRU_W), const),
                  pl.BlockSpec((LRU_W, 2 * LRU_W), const), pl.BlockSpec((LRU_W, 2 * LRU_W), const),
                  pl.BlockSpec((1, 2 * LRU_W), const), pl.BlockSpec((1, 2 * LRU_W), const),
                  pl.BlockSpec((2, LRU_W), const)],
        out_specs=[seq, st],
        out_shape=[jax.ShapeDtypeStruct((n, LRU_W), BF16),
                   jax.ShapeDtypeStruct((batch, 2, LRU_W), F32)],
        scratch_shapes=[pltpu.VMEM((seq_len + 2 * SUBLANES, LRU_W), F32),
                        pltpu.VMEM((seq_len, LRU_W), F32),
                        pltpu.VMEM((seq_len, LRU_W), F32),
                        pltpu.VMEM((TC_LRU, LRU_W), F32),
                        pltpu.VMEM((TC_LRU, LRU_W), F32)],
        compiler_params=_cparams(("arbitrary",)),
        name="lru_%d" % seq_len,
    )(xr, gb, h0, conv_w, conv_b, wf, wb, bf, bb, lam)


def _post_kernel(attn_ref, rec_ref, x_ref, mod_ref, g2_ref, wo_ref, wr2_ref, br_ref,
                 tri_ref, cnt_in_ref, x1_ref, h2_ref, route_ref, cnt_ref, cnt_s, wo_s):
    @pl.when(pl.program_id(0) == 0)
    def _():
        cnt_s[...] = cnt_in_ref[...]
        wo_s[...] = wo_ref[...].astype(BF16)

    m = mod_ref[0]
    u = _dot(attn_ref[...], wo_s[:ATTN_W, :]) + _dot(rec_ref[...], wo_s[ATTN_W:, :])
    x1 = x_ref[...] + m[2:3] * u
    x1_ref[...] = x1
    ms = jnp.mean(x1 * x1, axis=-1, keepdims=True)
    h2 = x1 * lax.rsqrt(ms + EPS) * g2_ref[...]
    h2 = h2 * (1.0 + m[4:5]) + m[3:4]
    hi, lo = _split_bf16(h2)
    h2_ref[...] = _to_row_tiles(h2)

    hw = _dot(hi, wr2_ref[...])
    logits = hw[:, :LANES] + hw[:, LANES:] + _dot(lo, wr2_ref[:, :LANES]) + br_ref[...]
    lane_i = lax.broadcasted_iota(jnp.int32, logits.shape, 1)
    lane = lane_i.astype(F32)
    lane_group = (lane_i // EXPERTS_PER_GROUP).astype(F32)
    neg = -jnp.inf
    big = float(1 << 20)
    gmask = (lane_i >= N_EXPERTS) & (lane_i < N_EXPERTS + N_GROUPS)
    gl = jnp.where(gmask, logits, neg)
    gmax = jnp.max(gl, axis=-1, keepdims=True)
    gidx = jnp.min(jnp.where(gl == gmax, lane - N_EXPERTS, big), axis=-1, keepdims=True)
    p_sel = 1.0 / jnp.sum(jnp.where(gmask, jnp.exp(gl - gmax), 0.0), axis=-1, keepdims=True)

    emask = (lane_i < N_EXPERTS) & (lane_group == gidx)
    el = jnp.where(emask, logits, neg)
    v1 = jnp.max(el, axis=-1, keepdims=True)
    i1 = jnp.min(jnp.where(el == v1, lane, big), axis=-1, keepdims=True)
    el2 = jnp.where(lane == i1, neg, el)
    v2 = jnp.max(el2, axis=-1, keepdims=True)
    i2 = jnp.min(jnp.where(el2 == v2, lane, big), axis=-1, keepdims=True)
    e2 = jnp.exp(v2 - v1)
    w1 = p_sel / (1.0 + e2)
    w2 = p_sel * e2 / (1.0 + e2)

    oh1 = lane == i1
    oh2 = lane == i2
    oh = jnp.where(oh1, 1.0, 0.0) + jnp.where(oh2, 1.0, 0.0)
    before = _dot(tri_ref[...], oh.astype(BF16)) + cnt_s[...]
    rank1 = jnp.sum(jnp.where(oh1, before, 0.0), axis=-1, keepdims=True)
    rank2 = jnp.sum(jnp.where(oh2, before, 0.0), axis=-1, keepdims=True)
    cnt = cnt_s[...] + jnp.sum(oh, axis=0, keepdims=True)
    cnt_s[...] = cnt
    cnt_ref[...] = cnt
    fields = (i1, i2, rank1, rank2, w1, w2)
    route = jnp.zeros(logits.shape, F32)
    for k, val in enumerate(fields):
        route = jnp.where(lane_i == k, val, route)
    route_ref[...] = route


def _post(attn, rec, x, mod3, mod_row, g2, w_out, wr2, br, tri, cnt_in):
    n = x.shape[0]
    tm = TM_PRE
    const = lambda i: (0, 0)
    row = lambda w: pl.BlockSpec((tm, w), lambda i: (i, 0))
    return pl.pallas_call(
        _post_kernel,
        grid=(n // tm,),
        in_specs=[row(ATTN_W), row(LRU_W), row(D_MODEL),
                  pl.BlockSpec((1, 6, D_MODEL), lambda i: (mod_row(i * tm), 0, 0)),
                  pl.BlockSpec((1, D_MODEL), const),
                  pl.BlockSpec((D_MODEL, D_MODEL), const),
                  pl.BlockSpec((D_MODEL, 2 * LANES), const),
                  pl.BlockSpec((1, LANES), const),
                  pl.BlockSpec((tm, tm), const),
                  pl.BlockSpec((1, LANES), const)],
        out_specs=[row(D_MODEL), pl.BlockSpec((tm,) + ROW_TILE, lambda i: (i, 0, 0)), row(LANES),
                   pl.BlockSpec((1, LANES), const)],
        out_shape=[jax.ShapeDtypeStruct((n, D_MODEL), F32),
                   jax.ShapeDtypeStruct((n,) + ROW_TILE, F32),
                   jax.ShapeDtypeStruct((n, LANES), F32),
                   jax.ShapeDtypeStruct((1, LANES), F32)],
        scratch_shapes=[pltpu.VMEM((1, LANES), F32), pltpu.VMEM((D_MODEL, D_MODEL), BF16)],
        compiler_params=_cparams(("arbitrary",)),
        name="post",
    )(attn, rec, x, mod3, g2, w_out, wr2, br, tri, cnt_in)


def _row_copy(src_ref, src_row, dst_ref, dst_row, sem):
    return pltpu.make_async_copy(src_ref.at[pl.ds(src_row, 1)], dst_ref.at[pl.ds(dst_row, 1)], sem)


def _plan_kernel(route_ref, seg_ref, slots_ref):
    route = route_ref[...]
    lane_i = lax.broadcasted_iota(jnp.int32, route.shape, 1)
    lane = lane_i.astype(F32)
    seg = seg_ref[...]
    out = jnp.zeros(route.shape, F32)
    for k in range(TOP_K):
        start = jnp.sum(jnp.where(lane == route[:, k:k + 1], seg, 0.0), axis=-1, keepdims=True)
        out = jnp.where(lane_i == k, start + route[:, TOP_K + k:TOP_K + k + 1], out)
    slots_ref[...] = jnp.transpose(out)[:SUBLANES, :].astype(jnp.int32)


def _plan(route, seg_row):
    n = route.shape[0]
    tm = TM_PRE
    return pl.pallas_call(
        _plan_kernel,
        grid=(n // tm,),
        in_specs=[pl.BlockSpec((tm, LANES), lambda i: (i, 0)),
                  pl.BlockSpec((1, LANES), lambda i: (0, 0))],
        out_specs=pl.BlockSpec((SUBLANES, tm), lambda i: (0, i)),
        out_shape=jax.ShapeDtypeStruct((SUBLANES, n), jnp.int32),
        compiler_params=_cparams(("arbitrary",)),
        name="plan",
    )(route, seg_row)


def _dispatch_kernel(slots_p_ref, slots_s_ref, pad_start_ref, pad_rows_ref, nu_ref, h2p_ref, h2s_ref,
                     xs_ref, zero_s, sem, zsem, *, n_prompt):
    i = pl.program_id(0)
    ch = CH_DISPATCH

    def zero_fill(op):
        for e in range(N_EXPERTS):
            rows = pad_rows_ref[e]
            for b in range(TR_MOE.bit_length() - 1):
                size = 1 << b
                off = (rows >> (b + 1)) << (b + 1)

                @pl.when(((rows >> b) & 1) == 1)
                def _():
                    op(pltpu.make_async_copy(zero_s.at[pl.ds(0, size)],
                                             xs_ref.at[pl.ds(pad_start_ref[e] + off, size)], zsem))

        def zero_tile(t, carry):
            op(pltpu.make_async_copy(
                zero_s, xs_ref.at[pl.ds(pl.multiple_of(t * TR_MOE, TR_MOE), TR_MOE)], zsem))
            return carry
        lax.fori_loop(nu_ref[0], xs_ref.shape[0] // TR_MOE, zero_tile, 0)

    @pl.when(i == 0)
    def _():
        zero_s[...] = jnp.zeros_like(zero_s)
        zero_fill(lambda cp: cp.start())

    @pl.when(i == pl.num_programs(0) - 1)
    def _():
        zero_fill(lambda cp: cp.wait())

    def scatter(src_ref, slots_ref, first_token):
        def body(j, carry):
            for k in range(TOP_K):
                _row_copy(src_ref, j, xs_ref, slots_ref[k, i * ch + j - first_token], sem).start(
                    priority=k)
            return carry
        lax.fori_loop(0, ch, body, 0, unroll=8)
        for _ in range(TOP_K):
            pltpu.make_async_copy(src_ref, xs_ref.at[pl.ds(0, ch)], sem).wait()

    @pl.when(i < n_prompt // ch)
    def _():
        scatter(h2p_ref, slots_p_ref, 0)

    @pl.when(i >= n_prompt // ch)
    def _():
        scatter(h2s_ref, slots_s_ref, n_prompt)


def _dispatch(slots_p, slots_s, pad_start, pad_rows, n_used, h2p, h2s, n_rows):
    n_prompt = h2p.shape[0]
    n = n_prompt + h2s.shape[0]
    ch = CH_DISPATCH
    npc = n_prompt // ch
    last_p = npc - 1
    return pl.pallas_call(
        functools.partial(_dispatch_kernel, n_prompt=n_prompt),
        grid_spec=pltpu.PrefetchScalarGridSpec(
            num_scalar_prefetch=5,
            grid=(n // ch,),
            in_specs=[pl.BlockSpec((ch,) + ROW_TILE, lambda i, *_: (jnp.minimum(i, last_p), 0, 0)),
                      pl.BlockSpec((ch,) + ROW_TILE, lambda i, *_: (jnp.maximum(i - npc, 0), 0, 0))],
            out_specs=pl.BlockSpec(memory_space=pl.ANY),
            scratch_shapes=[pltpu.VMEM((TR_MOE,) + ROW_TILE, F32), pltpu.SemaphoreType.DMA,
                            pltpu.SemaphoreType.DMA]),
        out_shape=jax.ShapeDtypeStruct((n_rows,) + ROW_TILE, F32),
        compiler_params=_cparams(("arbitrary",)),
        name="dispatch",
    )(slots_p, slots_s, pad_start, pad_rows, n_used, h2p, h2s)


def _ffn_kernel(tend_ref, xs_ref, wg_ref, wu_ref, wd_ref, ys_ref,
                xbuf, ybuf, wg_s, wu_s, wd_s, sem_in, sem_out):
    e = pl.program_id(0)
    n_used = tend_ref[N_EXPERTS - 1]
    t_first = jnp.where(e == 0, 0, tend_ref[jnp.maximum(e - 1, 0)])
    t_last = tend_ref[e]
    n_in = xbuf.shape[0]
    n_out = ybuf.shape[0]

    def tile_rows(t):
        return pl.ds(pl.multiple_of(t * TR_MOE, TR_MOE), TR_MOE)

    def fetch(t):
        return pltpu.make_async_copy(xs_ref.at[tile_rows(t)], xbuf.at[t % n_in], sem_in.at[t % n_in])

    def writeback(t):
        return pltpu.make_async_copy(ybuf.at[t % n_out], ys_ref.at[tile_rows(t)],
                                     sem_out.at[t % n_out])

    @pl.when(e == 0)
    def _():
        for t in range(n_in - 1):
            @pl.when(t < n_used)
            def _():
                fetch(t).start()

    @pl.when(t_last > t_first)
    def _():
        wg_s[...] = wg_ref[0].astype(BF16)
        wu_s[...] = wu_ref[0].astype(BF16)
        wd_s[...] = wd_ref[0].astype(BF16)

    def tile(t, carry):
        fetch(t).wait()

        @pl.when(t + n_in - 1 < n_used)
        def _():
            fetch(t + n_in - 1).start()

        @pl.when(t >= n_out)
        def _():
            writeback(t - n_out).wait()

        x = _from_row_tiles(xbuf[t % n_in]).astype(BF16)
        hg = _dot(x, wg_s[...])
        hu = _dot(x, wu_s[...])
        act = (hg * _sigmoid(hg)) * hu
        ybuf[t % n_out] = _to_row_tiles(_dot(act.astype(BF16), wd_s[...]))
        writeback(t).start()
        return carry

    lax.fori_loop(t_first, t_last, tile, 0)

    @pl.when(e == N_EXPERTS - 1)
    def _():
        for back in range(n_out, 0, -1):
            @pl.when(n_used >= back)
            def _():
                writeback(n_used - back).wait()
        n_all = ys_ref.shape[0] // TR_MOE
        ybuf[0] = jnp.zeros(ybuf.shape[1:], F32)

        def zero_tile(t):
            return pltpu.make_async_copy(ybuf.at[0], ys_ref.at[tile_rows(t)], sem_out.at[0])

        lax.fori_loop(n_used, n_all, lambda t, c: (zero_tile(t).start(), c)[1], 0)
        lax.fori_loop(n_used, n_all, lambda t, c: (zero_tile(t).wait(), c)[1], 0)


def _ffn(tile_end, xs, wg, wu, wd, n_tiles):
    tr = TR_MOE
    wsel = lambda e, tend: (e, 0, 0)
    return pl.pallas_call(
        _ffn_kernel,
        grid_spec=pltpu.PrefetchScalarGridSpec(
            num_scalar_prefetch=1,
            grid=(N_EXPERTS,),
            in_specs=[pl.BlockSpec(memory_space=pl.ANY),
                      pl.BlockSpec((1, D_MODEL, EXPERT_FF), wsel),
                      pl.BlockSpec((1, D_MODEL, EXPERT_FF), wsel),
                      pl.BlockSpec((1, EXPERT_FF, D_MODEL), wsel)],
            out_specs=pl.BlockSpec(memory_space=pl.ANY),
            scratch_shapes=[pltpu.VMEM((FFN_IN_SLOTS, tr) + ROW_TILE, F32),
                            pltpu.VMEM((FFN_OUT_SLOTS, tr) + ROW_TILE, F32),
                            pltpu.VMEM((D_MODEL, EXPERT_FF), BF16),
                            pltpu.VMEM((D_MODEL, EXPERT_FF), BF16),
                            pltpu.VMEM((EXPERT_FF, D_MODEL), BF16),
                            pltpu.SemaphoreType.DMA((FFN_IN_SLOTS,)),
                            pltpu.SemaphoreType.DMA((FFN_OUT_SLOTS,))]),
        out_shape=jax.ShapeDtypeStruct((n_tiles * tr,) + ROW_TILE, F32),
        compiler_params=_cparams(("arbitrary",)),
        name="ffn",
    )(tile_end, xs, wg, wu, wd)


def _combine_kernel(slots_ref, ys_ref, x1_ref, route_ref, mod_ref, y_ref, b1_s, b2_s, sems):
    i = pl.program_id(0)
    tm = x1_ref.shape[0]

    def gather(step, slot):
        def body(j, carry):
            _row_copy(ys_ref, slots_ref[0, step * tm + j], b1_s.at[slot], j, sems.at[slot]).start(
                priority=0)
            _row_copy(ys_ref, slots_ref[1, step * tm + j], b2_s.at[slot], j, sems.at[slot]).start(
                priority=1)
            return carry
        lax.fori_loop(0, tm, body, 0, unroll=8)

    @pl.when(i == 0)
    def _():
        gather(0, 0)

    @pl.when(i + 1 < pl.num_programs(0))
    def _():
        gather(i + 1, (i + 1) % 2)

    slot = i % 2
    for buf in (b1_s, b2_s):
        pltpu.make_async_copy(ys_ref.at[pl.ds(0, tm)], buf.at[slot], sems.at[slot]).wait()
    route = route_ref[...]
    moe = route[:, 4:5] * _from_row_tiles(b1_s[slot]) + route[:, 5:6] * _from_row_tiles(b2_s[slot])
    y_ref[...] = x1_ref[...] + mod_ref[0][5:6] * moe


def _combine(slots, ys, x1, route, mod3, mod_row):
    n = x1.shape[0]
    tm = TM_PRE
    row = lambda w: pl.BlockSpec((tm, w), lambda i, *_: (i, 0))
    return pl.pallas_call(
        _combine_kernel,
        grid_spec=pltpu.PrefetchScalarGridSpec(
            num_scalar_prefetch=1,
            grid=(n // tm,),
            in_specs=[pl.BlockSpec(memory_space=pl.ANY), row(D_MODEL), row(LANES),
                      pl.BlockSpec((1, 6, D_MODEL),
                                   lambda i, *_: (mod_row(i * tm), 0, 0))],
            out_specs=row(D_MODEL),
            scratch_shapes=[pltpu.VMEM((2, tm) + ROW_TILE, F32), pltpu.VMEM((2, tm) + ROW_TILE, F32),
                            pltpu.SemaphoreType.DMA((2,))]),
        out_shape=jax.ShapeDtypeStruct((n, D_MODEL), F32),
        compiler_params=_cparams(("arbitrary",)),
        name="combine",
    )(slots, ys, x1, route, mod3)


def _rope_tables(length):
    rows = length // GRID_W
    r, col = jnp.meshgrid(jnp.arange(rows), jnp.arange(GRID_W), indexing='ij')
    r = r.reshape(-1).astype(F32)
    col = col.reshape(-1).astype(F32)
    half = HEAD_DIM // 2
    inv = ROPE_THETA ** (-jnp.arange(0, half, 2, dtype=F32) / half)
    ang_r = r[:, None] * inv
    ang_c = col[:, None] * inv
    ang = jnp.concatenate([ang_r, ang_r, ang_c, ang_c], axis=-1)
    sign = jnp.where((jnp.arange(HEAD_DIM) // (HEAD_DIM // 4)) % 2 == 0, -1.0, 1.0).astype(F32)
    cos = jnp.tile(jnp.cos(ang), (1, LANES // HEAD_DIM))
    sin = jnp.tile(jnp.sin(ang) * sign, (1, LANES // HEAD_DIM))
    return cos, sin


def _block_diag(w):
    eye = jnp.eye(LRU_BLOCKS, dtype=w.dtype)
    return jnp.einsum('hij,hg->higj', w, eye).reshape(LRU_W, LRU_W)


def _expand_cache_kernel(k_ref, v_ref, kx_ref, vx_ref):
    for ref, xref, one in ((k_ref, kx_ref, 0.0), (v_ref, vx_ref, 1.0)):
        heads = jnp.swapaxes(ref[0, 0], 0, 1)
        col = jnp.concatenate([heads[h] for h in range(N_KV_HEADS)], axis=1)
        _store_expanded(xref, col, one)


def _expand_cache(cache_k, cache_v):
    b, _, t, _, _ = cache_k.shape
    src = pl.BlockSpec((1, 1, t, N_KV_HEADS, HEAD_DIM), lambda i: (i, 0, 0, 0, 0))
    dst = pl.BlockSpec((t, 4 * LANES), lambda i: (i, 0))
    return pl.pallas_call(
        _expand_cache_kernel,
        grid=(b,),
        in_specs=[src, src], out_specs=[dst, dst],
        out_shape=[jax.ShapeDtypeStruct((b * t, 4 * LANES), BF16)] * 2,
        compiler_params=_cparams(("arbitrary",)),
        name="expand_cache",
    )(cache_k, cache_v)


def kernel(x_prompt, x_sample, cache_k, cache_v, state_lru, c, c_ctx, w_mod, b_mod, norm1, norm2,
           w_in, q_norm, k_norm, conv_w, conv_b, lru_wa, lru_ba, lru_wx, lru_bx, lru_lambda, w_out,
           router_grp_w, router_grp_b, router_exp_w, router_exp_b, exp_w_gate, exp_w_up, exp_w_down):
    batch, seq, _ = x_prompt.shape
    dec_batch, dec_seq, _ = x_sample.shape
    past = cache_k.shape[2]
    depth = w_mod.shape[0]
    assert depth == 1

    cvec = jnp.concatenate(
        [c_ctx[None, :], c, jnp.zeros((MOD_ROWS - 1 - dec_batch, D_MODEL), F32)], axis=0)
    mod3 = _modulation(cvec, w_mod[0], b_mod[0][None, :]).reshape(MOD_ROWS, 6, D_MODEL)

    head_id = jnp.arange(QK_W) // HEAD_DIM
    heads = (head_id[:, None] == jnp.arange(LANES)[None, :]).astype(BF16)
    gqk = jnp.concatenate([jnp.tile(q_norm[0], N_HEADS), jnp.tile(k_norm[0], N_KV_HEADS)])[None, :]
    wf = (0.5 * jnp.concatenate([_block_diag(lru_wa[0, 0]), _block_diag(lru_wx[0, 0])], axis=1)).astype(BF16)
    wb = (0.5 * jnp.concatenate([_block_diag(lru_wa[0, 1]), _block_diag(lru_wx[0, 1])], axis=1)).astype(BF16)
    bf = 0.5 * jnp.concatenate([lru_ba[0, 0], lru_bx[0, 0]])[None, :]
    bb = 0.5 * jnp.concatenate([lru_ba[0, 1], lru_bx[0, 1]])[None, :]
    pad = LANES - N_EXPERTS - N_GROUPS
    wr = jnp.concatenate([router_exp_w[0], router_grp_w[0], jnp.zeros((D_MODEL, pad), F32)], axis=1)
    wr_hi = wr.astype(BF16)
    wr2 = jnp.concatenate([wr_hi, (wr - wr_hi.astype(F32)).astype(BF16)], axis=1)
    br = jnp.concatenate([router_exp_b[0], router_grp_b[0], jnp.zeros((pad,), F32)])[None, :]
    g1 = norm1[0][None, :]
    g2 = norm2[0][None, :]
    cw = conv_w[0]
    cb = conv_b[0][None, :]
    lam = lru_lambda[0]
    tri = (jnp.arange(TM_PRE)[:, None] > jnp.arange(TM_PRE)[None, :]).astype(BF16)

    def mixers(x, seq_len, mod_row, tables, extra_k, extra_v, h0, cnt_in):
        q, kx, vx, xr, gb, *cache = _pre(x, mod3, mod_row, g1, w_in[0], heads, gqk, tables, seq_len)
        k_segs = [(kx, seq_len)] + extra_k
        v_segs = [(vx, seq_len)] + extra_v
        attn = _attention(q, k_segs, v_segs, seq_len)
        rec, fin = _lru(xr, gb, h0, cw, cb, wf, wb, bf, bb, lam, seq_len)
        x1, h2, route, cnt = _post(attn, rec, x, mod3, mod_row, g2, w_out[0], wr2, br, tri, cnt_in)
        return x1, h2, route, cnt, cache, fin

    mod_row_p = lambda tok: 0
    mod_row_s = lambda tok: tok // dec_seq + 1
    xp = x_prompt.reshape(batch * seq, D_MODEL)
    x1p, h2p, route_p, cnt_p, (kf, vf), fin = mixers(
        xp, seq, mod_row_p, None, [], [], jnp.zeros((batch, 2, LRU_W), F32),
        jnp.zeros((1, LANES), F32))
    xs = x_sample.reshape(dec_batch * dec_seq, D_MODEL)
    ck, cv = _expand_cache(cache_k, cache_v)
    x1s, h2s, route_s, cnt_all, _, _ = mixers(
        xs, dec_seq, mod_row_s, _rope_tables(dec_seq), [(ck, past)], [(cv, past)],
        state_lru[:, 0], cnt_p)

    n_prompt = batch * seq
    n_tok = n_prompt + dec_batch * dec_seq
    n_tiles = (TOP_K * n_tok + N_EXPERTS * (TR_MOE - 1)) // TR_MOE
    cnt = cnt_all[0, :N_EXPERTS].astype(jnp.int32)
    ntile = (cnt + TR_MOE - 1) // TR_MOE
    tile_end = jnp.cumsum(ntile)
    seg_start = (tile_end - ntile) * TR_MOE
    n_used = tile_end[-1:]
    pad_start = seg_start + cnt
    pad_rows = tile_end * TR_MOE - pad_start

    seg_row = jnp.pad(seg_start.astype(F32), (0, LANES - N_EXPERTS))[None, :]
    slots_p = _plan(route_p, seg_row)
    slots_s = _plan(route_s, seg_row)
    xsort = _dispatch(slots_p, slots_s, pad_start, pad_rows, n_used, h2p, h2s,
                      n_tiles * TR_MOE)
    ysort = _ffn(tile_end, xsort, exp_w_gate[0], exp_w_up[0], exp_w_down[0], n_tiles)
    yp = _combine(slots_p, ysort, x1p, route_p, mod3, mod_row_p)
    ys = _combine(slots_s, ysort, x1s, route_s, mod3, mod_row_s)

    return (yp.reshape(batch, seq, D_MODEL),
            ys.reshape(dec_batch, dec_seq, D_MODEL),
            kf.reshape(batch, 1, seq, N_KV_HEADS, HEAD_DIM),
            vf.reshape(batch, 1, seq, N_KV_HEADS, HEAD_DIM),
            fin.reshape(batch, 1, 2, LRU_W))
```

```python
import functools

import jax
import jax.numpy as jnp
from jax import lax
from jax.experimental import pallas as pl
from jax.experimental.pallas import tpu as pltpu

F32 = jnp.float32
BF16 = jnp.bfloat16

D_MODEL = 1024
GRID_W = 64
ATTN_W = 512
LRU_W = 512
HEAD_DIM = 64
N_HEADS = 8
N_KV_HEADS = 2
KV_W = N_KV_HEADS * HEAD_DIM
LRU_BLOCKS = 8
LRU_BLOCK_W = LRU_W // LRU_BLOCKS
CONV_W = 4
LRU_C = 8.0
IN_W = ATTN_W + 2 * KV_W + 2 * LRU_W
QK_W = ATTN_W + KV_W
N_GROUPS = 4
EXPERTS_PER_GROUP = 8
N_EXPERTS = N_GROUPS * EXPERTS_PER_GROUP
TOP_K = 2
EXPERT_FF = D_MODEL // 4
ROPE_THETA = 10000.0
EPS = 1e-6

LANES = 128
SUBLANES = 8
MOD_ROWS = 8
VMEM_LIMIT = 48 * 1024 * 1024

TM_PRE = 512
TQ_ATT = 1024
TC_LRU = 256
TR_MOE = 256
CH_DISPATCH = 1024
FFN_IN_SLOTS = 4
FFN_OUT_SLOTS = 3


def _cparams(sem):
    return pltpu.CompilerParams(dimension_semantics=sem, vmem_limit_bytes=VMEM_LIMIT)


def _dot(a, b):
    return jnp.dot(a, b, preferred_element_type=F32)


def _dot_nt(a, b):
    return lax.dot_general(a, b, (((1,), (1,)), ((), ())), preferred_element_type=F32)


ROW_TILE = (D_MODEL // LANES, LANES)
LOG2_E = 1.4426950408889634
Q_SCALE = HEAD_DIM ** -0.5 * LOG2_E
ONES_LANE_EVEN = HEAD_DIM
ONES_LANE_ODD = 0


def _to_row_tiles(x):
    cols = jnp.stack([x[:, c * LANES:(c + 1) * LANES] for c in range(D_MODEL // LANES)], axis=0)
    return jnp.swapaxes(cols, 0, 1)


def _from_row_tiles(x3):
    cols = jnp.swapaxes(x3, 0, 1)
    return jnp.concatenate([cols[c] for c in range(D_MODEL // LANES)], axis=1)


def _sigmoid(x):
    return 0.5 * jnp.tanh(0.5 * x) + 0.5


def _split_bf16(x):
    hi = x.astype(BF16)
    lo = (x - hi.astype(F32)).astype(BF16)
    return hi, lo


def _mod_kernel(c_ref, w_ref, b_ref, o_ref):
    c = c_ref[...]
    s = (c * jax.nn.sigmoid(c)).astype(BF16)
    o_ref[...] = _dot(s, w_ref[...].astype(BF16)) + b_ref[...]


def _modulation(cvec, w_mod, b_mod):
    n_out = w_mod.shape[1]
    tn = n_out // 4
    return pl.pallas_call(
        _mod_kernel,
        grid=(n_out // tn,),
        in_specs=[pl.BlockSpec((MOD_ROWS, D_MODEL), lambda j: (0, 0)),
                  pl.BlockSpec((D_MODEL, tn), lambda j: (0, j)),
                  pl.BlockSpec((1, tn), lambda j: (0, j))],
        out_specs=pl.BlockSpec((MOD_ROWS, tn), lambda j: (0, j)),
        out_shape=jax.ShapeDtypeStruct((MOD_ROWS, n_out), F32),
        compiler_params=_cparams(("arbitrary",)),
        name="modulation",
    )(cvec, w_mod, b_mod)


def _split_heads(col):
    return jnp.swapaxes(jnp.stack([col[:, h * HEAD_DIM:(h + 1) * HEAD_DIM]
                                   for h in range(N_KV_HEADS)], axis=0), 0, 1)


def _store_expanded(xref, col, one):
    lane = lax.broadcasted_iota(jnp.int32, col.shape, 1)
    lo_half = lane < HEAD_DIM
    swapped = pltpu.roll(col, HEAD_DIM, 1)
    fill_hi = jnp.where(lane == ONES_LANE_EVEN, one, 0.0)
    fill_lo = jnp.where(lane == ONES_LANE_ODD, one, 0.0)
    xref[:, 0 * LANES:1 * LANES] = jnp.where(lo_half, col, fill_hi).astype(BF16)
    xref[:, 1 * LANES:2 * LANES] = jnp.where(lo_half, fill_lo, swapped).astype(BF16)
    xref[:, 2 * LANES:3 * LANES] = jnp.where(lo_half, swapped, fill_hi).astype(BF16)
    xref[:, 3 * LANES:4 * LANES] = jnp.where(lo_half, fill_lo, col).astype(BF16)


def _pre_kernel(*refs, rope):
    if rope:
        (x_ref, mod_ref, g1_ref, win_ref, heads_ref, heads_t_ref, gqk_ref, cos_ref, sin_ref,
         q_ref, kx_ref, vx_ref, xr_ref, gb_ref, win_s) = refs
    else:
        (x_ref, mod_ref, g1_ref, win_ref, heads_ref, heads_t_ref, gqk_ref,
         q_ref, kx_ref, vx_ref, xr_ref, gb_ref, kf_ref, vf_ref, win_s) = refs

    @pl.when(pl.program_id(0) == 0)
    def _():
        win_s[...] = win_ref[...].astype(BF16)

    x = x_ref[...]
    m = mod_ref[0]
    ms = jnp.mean(x * x, axis=-1, keepdims=True)
    y = x * lax.rsqrt(ms + EPS) * g1_ref[...]
    h = y * (1.0 + m[1:2]) + m[0:1]
    z = _dot(h.astype(BF16), win_s[...])

    qk = z[:, :QK_W]
    ss = _dot((qk * qk).astype(BF16), heads_ref[...])
    hi, lo = _split_bf16(lax.rsqrt(ss * (1.0 / HEAD_DIM) + EPS))
    qk = qk * (_dot(hi, heads_t_ref[...]) + _dot(lo, heads_t_ref[...])) * gqk_ref[...]

    lane = lax.broadcasted_iota(jnp.int32, (x.shape[0], LANES), 1)
    cols = []
    for c in range(QK_W // LANES):
        xc = qk[:, c * LANES:(c + 1) * LANES]
        if rope:
            left = pltpu.roll(xc, LANES - HEAD_DIM // 4, 1)
            right = pltpu.roll(xc, HEAD_DIM // 4, 1)
            rot = jnp.where((lane // (HEAD_DIM // 4)) % 2 == 0, left, right)
            xc = xc * cos_ref[...] + rot * sin_ref[...]
        cols.append(xc)
    for c in range(ATTN_W // LANES):
        q_ref[:, c * LANES:(c + 1) * LANES] = (cols[c] * Q_SCALE).astype(BF16)

    k_col = cols[ATTN_W // LANES]
    v_col = z[:, QK_W:QK_W + KV_W]
    _store_expanded(kx_ref, k_col, 0.0)
    _store_expanded(vx_ref, v_col, 1.0)
    if not rope:
        kf_ref[...] = _split_heads(k_col)
        vf_ref[...] = _split_heads(v_col)

    xr_ref[...] = z[:, QK_W + KV_W:QK_W + KV_W + LRU_W]
    gb_ref[...] = z[:, QK_W + KV_W + LRU_W:]


def _pre(x, mod3, mod_row, g1, w_in, heads, gqk, tables, seq_len):
    n = x.shape[0]
    tm = TM_PRE
    tiles_per_seq = seq_len // tm
    rope = tables is not None
    const = lambda i: (0, 0)
    in_specs = [pl.BlockSpec((tm, D_MODEL), lambda i: (i, 0)),
                pl.BlockSpec((1, 6, D_MODEL), lambda i: (mod_row(i * tm), 0, 0)),
                pl.BlockSpec((1, D_MODEL), const),
                pl.BlockSpec((D_MODEL, IN_W), const),
                pl.BlockSpec((QK_W, LANES), const),
                pl.BlockSpec((LANES, QK_W), const),
                pl.BlockSpec((1, QK_W), const)]
    args = [x, mod3, g1, w_in, heads, heads.T, gqk]
    if rope:
        in_specs += [pl.BlockSpec((tm, LANES), lambda i: (i % tiles_per_seq, 0))] * 2
        args += list(tables)
    row = lambda w: pl.BlockSpec((tm, w), lambda i: (i, 0))
    out_shape = [jax.ShapeDtypeStruct((n, ATTN_W), BF16),
                 jax.ShapeDtypeStruct((n, 4 * LANES), BF16),
                 jax.ShapeDtypeStruct((n, 4 * LANES), BF16),
                 jax.ShapeDtypeStruct((n, LRU_W), F32),
                 jax.ShapeDtypeStruct((n, LRU_W), F32)]
    out_specs = [row(ATTN_W), row(4 * LANES), row(4 * LANES), row(LRU_W), row(LRU_W)]
    if not rope:
        cache = pl.BlockSpec((tm, N_KV_HEADS, HEAD_DIM), lambda i: (i, 0, 0))
        out_shape += [jax.ShapeDtypeStruct((n, N_KV_HEADS, HEAD_DIM), F32)] * 2
        out_specs += [cache, cache]
    return pl.pallas_call(
        functools.partial(_pre_kernel, rope=rope),
        grid=(n // tm,),
        in_specs=in_specs, out_specs=out_specs, out_shape=out_shape,
        scratch_shapes=[pltpu.VMEM((D_MODEL, IN_W), BF16)],
        compiler_params=_cparams(("arbitrary",)),
        name="pre_rope" if rope else "pre",
    )(*args)


def _attn_kernel(*refs, n_seg):
    q_ref = refs[0]
    k_refs = refs[1:1 + n_seg]
    v_refs = refs[1 + n_seg:1 + 2 * n_seg]
    o_ref = refs[1 + 2 * n_seg]
    lane = lax.broadcasted_iota(jnp.int32, (q_ref.shape[0], LANES), 1)
    for c in range(ATTN_W // LANES):
        qc = q_ref[:, c * LANES:(c + 1) * LANES]
        g = c // 2
        accs = []
        for par in range(2):
            sl = slice((2 * g + par) * LANES, (2 * g + par + 1) * LANES)
            ss = [_dot_nt(qc, k[:, sl]) for k in k_refs]
            mx = functools.reduce(jnp.maximum, [jnp.max(s, axis=-1, keepdims=True) for s in ss])
            ps = [jnp.exp2((s - mx).astype(BF16)) for s in ss]
            accs.append(functools.reduce(lambda a, b: a + b,
                                         [_dot(p, v[:, sl]) for p, v in zip(ps, v_refs)]))
        even = accs[0] / accs[0][:, ONES_LANE_EVEN:ONES_LANE_EVEN + 1]
        odd = accs[1] / accs[1][:, ONES_LANE_ODD:ONES_LANE_ODD + 1]
        o_ref[:, c * LANES:(c + 1) * LANES] = jnp.where(lane < HEAD_DIM, even, odd).astype(BF16)


def _attention(q, k_segs, v_segs, seq_len):
    n = q.shape[0]
    tq = min(TQ_ATT, seq_len)
    nq = seq_len // tq
    n_seg = len(k_segs)
    in_specs = [pl.BlockSpec((tq, ATTN_W), lambda b, i: (b * nq + i, 0))]
    for arr, t in list(k_segs) + list(v_segs):
        in_specs.append(pl.BlockSpec((t, 4 * LANES), lambda b, i: (b, 0)))
    return pl.pallas_call(
        functools.partial(_attn_kernel, n_seg=n_seg),
        grid=(n // seq_len, nq),
        in_specs=in_specs,
        out_specs=pl.BlockSpec((tq, ATTN_W), lambda b, i: (b * nq + i, 0)),
        out_shape=jax.ShapeDtypeStruct((n, ATTN_W), BF16),
        compiler_params=_cparams(("arbitrary", "arbitrary")),
        name="attention_%dseg" % n_seg,
    )(q, *[a for a, _ in k_segs], *[a for a, _ in v_segs])


def _log_sigmoid(x):
    return jnp.minimum(x, 0.0) - jnp.log1p(jnp.exp(-jnp.abs(x)))


def _tile_scan(a, b, reverse):
    row = lax.broadcasted_iota(jnp.int32, a.shape, 0)
    d = 1
    while d < SUBLANES:
        if reverse:
            keep = row < SUBLANES - d
            shift = SUBLANES - d
        else:
            keep = row >= d
            shift = d
        a_sh = jnp.where(keep, pltpu.roll(a, shift, 0), 1.0)
        b_sh = jnp.where(keep, pltpu.roll(b, shift, 0), 0.0)
        b = a * b_sh + b
        a = a * a_sh
        d *= 2
    return a, b


def _lru_kernel(xr_ref, gb_ref, h0_ref, cw_ref, cb_ref, wf_ref, wb_ref, bf_ref, bb_ref, lam_ref,
                rec_ref, fin_ref, xpad_s, xc_s, hf_s, a_s, b_s, *, seq_len):
    tc = TC_LRU
    n_chunks = seq_len // tc
    n_tiles = tc // SUBLANES
    zpad = jnp.zeros((SUBLANES, LRU_W), F32)
    xpad_s[0:SUBLANES, :] = zpad
    xpad_s[SUBLANES:SUBLANES + seq_len, :] = xr_ref[...]
    xpad_s[SUBLANES + seq_len:2 * SUBLANES + seq_len, :] = zpad

    half_cl = (0.5 * LRU_C * LOG2_E) * _log_sigmoid(lam_ref[...])

    def gates(xcc, w_ref, bias_ref, half_cl_d):
        t = jnp.tanh(_dot(xcc.astype(BF16), w_ref[...]) + bias_ref[...])
        a = jnp.exp2(t[:, :LRU_W] * half_cl_d + half_cl_d)
        half_x = 0.5 * xcc
        a_s[...] = a
        u = 1.0 - a * a
        b_s[...] = jnp.where(u > 0.0, u * lax.rsqrt(u), 0.0) * (t[:, LRU_W:] * half_x + half_x)

    h = h0_ref[0, 0:1, :]
    for c in range(n_chunks):
        base = c * tc
        xcc = cb_ref[...] + functools.reduce(
            lambda u, v: u + v,
            [cw_ref[j:j + 1, :] * xpad_s[base + SUBLANES - 1 + j:base + SUBLANES - 1 + j + tc, :]
             for j in range(CONV_W)])
        xc_s[base:base + tc, :] = xcc
        gates(xcc, wf_ref, bf_ref, half_cl[0:1])

        def fwd_tile(t, hc, base=base):
            r0 = pl.multiple_of(t * SUBLANES, SUBLANES)
            ca, cb = _tile_scan(a_s[pl.ds(r0, SUBLANES), :], b_s[pl.ds(r0, SUBLANES), :], False)
            hh = ca * hc + cb
            hf_s[pl.ds(base + r0, SUBLANES), :] = hh
            return hh[SUBLANES - 1:SUBLANES, :]

        h = lax.fori_loop(0, n_tiles, fwd_tile, h, unroll=4)
    fin_ref[0, 0:1, :] = h

    h = h0_ref[0, 1:2, :]
    for c in reversed(range(n_chunks)):
        base = c * tc
        gates(xc_s[base:base + tc, :], wb_ref, bb_ref, half_cl[1:2])

        def bwd_tile(t, hc, base=base):
            r0 = pl.multiple_of((n_tiles - 1 - t) * SUBLANES, SUBLANES)
            ca, cb = _tile_scan(a_s[pl.ds(r0, SUBLANES), :], b_s[pl.ds(r0, SUBLANES), :], True)
            hh = ca * hc + cb
            gate = jax.nn.gelu(gb_ref[pl.ds(base + r0, SUBLANES), :], approximate=True)
            rec_ref[pl.ds(base + r0, SUBLANES), :] = (
                (hf_s[pl.ds(base + r0, SUBLANES), :] + hh) * gate).astype(rec_ref.dtype)
            return hh[0:1, :]

        h = lax.fori_loop(0, n_tiles, bwd_tile, h, unroll=4)
    fin_ref[0, 1:2, :] = h


def _lru(xr, gb, h0, conv_w, conv_b, wf, wb, bf, bb, lam, seq_len):
    n = xr.shape[0]
    batch = n // seq_len
    const = lambda b: (0, 0)
    seq = pl.BlockSpec((seq_len, LRU_W), lambda b: (b, 0))
    st = pl.BlockSpec((1, 2, LRU_W), lambda b: (b, 0, 0))
    return pl.pallas_call(
        functools.partial(_lru_kernel, seq_len=seq_len),
        grid=(batch,),
        in_specs=[seq, seq, st,
                  pl.BlockSpec((CONV_W, LRU_W), const), pl.BlockSpec((1, LRU_W), const),
                  pl.BlockSpec((LRU_W, 2 * LRU_W), const), pl.BlockSpec((LRU_W, 2 * LRU_W), const),
                  pl.BlockSpec((1, 2 * LRU_W), const), pl.BlockSpec((1, 2 * LRU_W), const),
                  pl.BlockSpec((2, LRU_W), const)],
        out_specs=[seq, st],
        out_shape=[jax.ShapeDtypeStruct((n, LRU_W), BF16),
                   jax.ShapeDtypeStruct((batch, 2, LRU_W), F32)],
        scratch_shapes=[pltpu.VMEM((seq_len + 2 * SUBLANES, LRU_W), F32),
                        pltpu.VMEM((seq_len, LRU_W), F32),
                        pltpu.VMEM((seq_len, LRU_W), F32),
                        pltpu.VMEM((TC_LRU, LRU_W), F32),
                        pltpu.VMEM((TC_LRU, LRU_W), F32)],
        compiler_params=_cparams(("arbitrary",)),
        name="lru_%d" % seq_len,
    )(xr, gb, h0, conv_w, conv_b, wf, wb, bf, bb, lam)


def _post_kernel(attn_ref, rec_ref, x_ref, mod_ref, g2_ref, wo_ref, wr2_ref, br_ref,
                 tri_ref, cnt_in_ref, x1_ref, h2_ref, route_ref, cnt_ref, route_t_ref, cnt_s, wo_s):
    @pl.when(pl.program_id(0) == 0)
    def _():
        cnt_s[...] = cnt_in_ref[...]
        wo_s[...] = wo_ref[...].astype(BF16)

    m = mod_ref[0]
    u = _dot(attn_ref[...], wo_s[:ATTN_W, :]) + _dot(rec_ref[...], wo_s[ATTN_W:, :])
    x1 = x_ref[...] + m[2:3] * u
    x1_ref[...] = x1
    ms = jnp.mean(x1 * x1, axis=-1, keepdims=True)
    h2 = x1 * lax.rsqrt(ms + EPS) * g2_ref[...]
    h2 = h2 * (1.0 + m[4:5]) + m[3:4]
    hi, lo = _split_bf16(h2)
    h2_ref[...] = _to_row_tiles(h2)

    hw = _dot(hi, wr2_ref[...])
    logits = hw[:, :LANES] + hw[:, LANES:] + _dot(lo, wr2_ref[:, :LANES]) + br_ref[...]
    lane_i = lax.broadcasted_iota(jnp.int32, logits.shape, 1)
    lane = lane_i.astype(F32)
    lane_group = (lane_i // EXPERTS_PER_GROUP).astype(F32)
    neg = -jnp.inf
    big = float(1 << 20)
    gmask = (lane_i >= N_EXPERTS) & (lane_i < N_EXPERTS + N_GROUPS)
    gl = jnp.where(gmask, logits, neg)
    gmax = jnp.max(gl, axis=-1, keepdims=True)
    gidx = jnp.min(jnp.where(gl == gmax, lane - N_EXPERTS, big), axis=-1, keepdims=True)
    p_sel = 1.0 / jnp.sum(jnp.where(gmask, jnp.exp(gl - gmax), 0.0), axis=-1, keepdims=True)

    emask = (lane_i < N_EXPERTS) & (lane_group == gidx)
    el = jnp.where(emask, logits, neg)
    v1 = jnp.max(el, axis=-1, keepdims=True)
    i1 = jnp.min(jnp.where(el == v1, lane, big), axis=-1, keepdims=True)
    el2 = jnp.where(lane == i1, neg, el)
    v2 = jnp.max(el2, axis=-1, keepdims=True)
    i2 = jnp.min(jnp.where(el2 == v2, lane, big), axis=-1, keepdims=True)
    e2 = jnp.exp(v2 - v1)
    w1 = p_sel / (1.0 + e2)
    w2 = p_sel * e2 / (1.0 + e2)

    oh1 = lane == i1
    oh2 = lane == i2
    oh = jnp.where(oh1, 1.0, 0.0) + jnp.where(oh2, 1.0, 0.0)
    before = _dot(tri_ref[...], oh.astype(BF16)) + cnt_s[...]
    rank1 = jnp.sum(jnp.where(oh1, before, 0.0), axis=-1, keepdims=True)
    rank2 = jnp.sum(jnp.where(oh2, before, 0.0), axis=-1, keepdims=True)
    cnt = cnt_s[...] + jnp.sum(oh, axis=0, keepdims=True)
    cnt_s[...] = cnt
    cnt_ref[...] = cnt
    fields = (i1, i2, rank1, rank2, w1, w2)
    route = jnp.zeros(logits.shape, F32)
    for k, val in enumerate(fields):
        route = jnp.where(lane_i == k, val, route)
    route_ref[...] = route
    route_t_ref[...] = jnp.transpose(route)[:SUBLANES, :]


def _post(attn, rec, x, mod3, mod_row, g2, w_out, wr2, br, tri, cnt_in):
    n = x.shape[0]
    tm = TM_PRE
    const = lambda i: (0, 0)
    row = lambda w: pl.BlockSpec((tm, w), lambda i: (i, 0))
    return pl.pallas_call(
        _post_kernel,
        grid=(n // tm,),
        in_specs=[row(ATTN_W), row(LRU_W), row(D_MODEL),
                  pl.BlockSpec((1, 6, D_MODEL), lambda i: (mod_row(i * tm), 0, 0)),
                  pl.BlockSpec((1, D_MODEL), const),
                  pl.BlockSpec((D_MODEL, D_MODEL), const),
                  pl.BlockSpec((D_MODEL, 2 * LANES), const),
                  pl.BlockSpec((1, LANES), const),
                  pl.BlockSpec((tm, tm), const),
                  pl.BlockSpec((1, LANES), const)],
        out_specs=[row(D_MODEL), pl.BlockSpec((tm,) + ROW_TILE, lambda i: (i, 0, 0)), row(LANES),
                   pl.BlockSpec((1, LANES), const),
                   pl.BlockSpec((SUBLANES, tm), lambda i: (0, i))],
        out_shape=[jax.ShapeDtypeStruct((n, D_MODEL), F32),
                   jax.ShapeDtypeStruct((n,) + ROW_TILE, F32),
                   jax.ShapeDtypeStruct((n, LANES), F32),
                   jax.ShapeDtypeStruct((1, LANES), F32),
                   jax.ShapeDtypeStruct((SUBLANES, n), F32)],
        scratch_shapes=[pltpu.VMEM((1, LANES), F32), pltpu.VMEM((D_MODEL, D_MODEL), BF16)],
        compiler_params=_cparams(("arbitrary",)),
        name="post",
    )(attn, rec, x, mod3, g2, w_out, wr2, br, tri, cnt_in)


def _row_copy(src_ref, src_row, dst_ref, dst_row, sem):
    return pltpu.make_async_copy(src_ref.at[pl.ds(src_row, 1)], dst_ref.at[pl.ds(dst_row, 1)], sem)


def _plan_kernel(route_t_ref, seg_ref, slots_ref):
    route_t = route_t_ref[...]
    t = route_t.shape[1]
    seg = jnp.concatenate([seg_ref[...]] * (t // LANES), axis=1)
    expert = lax.broadcasted_iota(jnp.int32, seg.shape, 0).astype(F32)
    rows = []
    for k in range(TOP_K):
        start = jnp.sum(jnp.where(expert == route_t[k:k + 1, :], seg, 0.0), axis=0, keepdims=True)
        rows.append(start + route_t[TOP_K + k:TOP_K + k + 1, :])
    rows.append(jnp.zeros((SUBLANES - TOP_K, t), F32))
    slots_ref[...] = jnp.concatenate(rows, axis=0).astype(jnp.int32)


def _plan(route_t, seg_rows):
    n = route_t.shape[1]
    tm = TM_PRE
    return pl.pallas_call(
        _plan_kernel,
        grid=(n // tm,),
        in_specs=[pl.BlockSpec((SUBLANES, tm), lambda i: (0, i)),
                  pl.BlockSpec((N_EXPERTS, LANES), lambda i: (0, 0))],
        out_specs=pl.BlockSpec((SUBLANES, tm), lambda i: (0, i)),
        out_shape=jax.ShapeDtypeStruct((SUBLANES, n), jnp.int32),
        compiler_params=_cparams(("arbitrary",)),
        name="plan",
    )(route_t, seg_rows)


def _dispatch_kernel(slots_p_ref, slots_s_ref, pad_start_ref, pad_rows_ref, nu_ref, h2p_ref, h2s_ref,
                     xs_ref, zero_s, sem, zsem, *, n_prompt):
    i = pl.program_id(0)
    ch = CH_DISPATCH

    def zero_fill(op):
        for e in range(N_EXPERTS):
            rows = pad_rows_ref[e]
            for b in range(TR_MOE.bit_length() - 1):
                size = 1 << b
                off = (rows >> (b + 1)) << (b + 1)

                @pl.when(((rows >> b) & 1) == 1)
                def _():
                    op(pltpu.make_async_copy(zero_s.at[pl.ds(0, size)],
                                             xs_ref.at[pl.ds(pad_start_ref[e] + off, size)], zsem))

        def zero_tile(t, carry):
            op(pltpu.make_async_copy(
                zero_s, xs_ref.at[pl.ds(pl.multiple_of(t * TR_MOE, TR_MOE), TR_MOE)], zsem))
            return carry
        lax.fori_loop(nu_ref[0], xs_ref.shape[0] // TR_MOE, zero_tile, 0)

    @pl.when(i == 0)
    def _():
        zero_s[...] = jnp.zeros_like(zero_s)
        zero_fill(lambda cp: cp.start())

    @pl.when(i == pl.num_programs(0) - 1)
    def _():
        zero_fill(lambda cp: cp.wait())

    def scatter(src_ref, slots_ref, first_token):
        def body(j, carry):
            for k in range(TOP_K):
                _row_copy(src_ref, j, xs_ref, slots_ref[k, i * ch + j - first_token], sem).start(
                    priority=k)
            return carry
        lax.fori_loop(0, ch, body, 0, unroll=8)
        for _ in range(TOP_K):
            pltpu.make_async_copy(src_ref, xs_ref.at[pl.ds(0, ch)], sem).wait()

    @pl.when(i < n_prompt // ch)
    def _():
        scatter(h2p_ref, slots_p_ref, 0)

    @pl.when(i >= n_prompt // ch)
    def _():
        scatter(h2s_ref, slots_s_ref, n_prompt)


def _dispatch(slots_p, slots_s, pad_start, pad_rows, n_used, h2p, h2s, n_rows):
    n_prompt = h2p.shape[0]
    n = n_prompt + h2s.shape[0]
    ch = CH_DISPATCH
    npc = n_prompt // ch
    last_p = npc - 1
    return pl.pallas_call(
        functools.partial(_dispatch_kernel, n_prompt=n_prompt),
        grid_spec=pltpu.PrefetchScalarGridSpec(
            num_scalar_prefetch=5,
            grid=(n // ch,),
            in_specs=[pl.BlockSpec((ch,) + ROW_TILE, lambda i, *_: (jnp.minimum(i, last_p), 0, 0)),
                      pl.BlockSpec((ch,) + ROW_TILE, lambda i, *_: (jnp.maximum(i - npc, 0), 0, 0))],
            out_specs=pl.BlockSpec(memory_space=pl.ANY),
            scratch_shapes=[pltpu.VMEM((TR_MOE,) + ROW_TILE, F32), pltpu.SemaphoreType.DMA,
                            pltpu.SemaphoreType.DMA]),
        out_shape=jax.ShapeDtypeStruct((n_rows,) + ROW_TILE, F32),
        compiler_params=_cparams(("arbitrary",)),
        name="dispatch",
    )(slots_p, slots_s, pad_start, pad_rows, n_used, h2p, h2s)


def _ffn_kernel(tend_ref, xs_ref, wg_ref, wu_ref, wd_ref, ys_ref,
                xbuf, ybuf, wg_s, wu_s, wd_s, sem_in, sem_out):
    e = pl.program_id(0)
    n_used = tend_ref[N_EXPERTS - 1]
    t_first = jnp.where(e == 0, 0, tend_ref[jnp.maximum(e - 1, 0)])
    t_last = tend_ref[e]
    n_in = xbuf.shape[0]
    n_out = ybuf.shape[0]

    def tile_rows(t):
        return pl.ds(pl.multiple_of(t * TR_MOE, TR_MOE), TR_MOE)

    def fetch(t):
        return pltpu.make_async_copy(xs_ref.at[tile_rows(t)], xbuf.at[t % n_in], sem_in.at[t % n_in])

    def writeback(t):
        return pltpu.make_async_copy(ybuf.at[t % n_out], ys_ref.at[tile_rows(t)],
                                     sem_out.at[t % n_out])

    @pl.when(e == 0)
    def _():
        for t in range(n_in - 1):
            @pl.when(t < n_used)
            def _():
                fetch(t).start()

    @pl.when(t_last > t_first)
    def _():
        wg_s[...] = wg_ref[0].astype(BF16)
        wu_s[...] = wu_ref[0].astype(BF16)
        wd_s[...] = wd_ref[0].astype(BF16)

    def tile(t, carry):
        fetch(t).wait()

        @pl.when(t + n_in - 1 < n_used)
        def _():
            fetch(t + n_in - 1).start()

        @pl.when(t >= n_out)
        def _():
            writeback(t - n_out).wait()

        x = _from_row_tiles(xbuf[t % n_in]).astype(BF16)
        hg = _dot(x, wg_s[...])
        hu = _dot(x, wu_s[...])
        act = (hg * _sigmoid(hg)) * hu
        ybuf[t % n_out] = _to_row_tiles(_dot(act.astype(BF16), wd_s[...]))
        writeback(t).start()
        return carry

    lax.fori_loop(t_first, t_last, tile, 0)

    @pl.when(e == N_EXPERTS - 1)
    def _():
        for back in range(n_out, 0, -1):
            @pl.when(n_used >= back)
            def _():
                writeback(n_used - back).wait()
        n_all = ys_ref.shape[0] // TR_MOE
        ybuf[0] = jnp.zeros(ybuf.shape[1:], F32)

        def zero_tile(t):
            return pltpu.make_async_copy(ybuf.at[0], ys_ref.at[tile_rows(t)], sem_out.at[0])

        lax.fori_loop(n_used, n_all, lambda t, c: (zero_tile(t).start(), c)[1], 0)
        lax.fori_loop(n_used, n_all, lambda t, c: (zero_tile(t).wait(), c)[1], 0)


def _ffn(tile_end, xs, wg, wu, wd, n_tiles):
    tr = TR_MOE
    wsel = lambda e, tend: (e, 0, 0)
    return pl.pallas_call(
        _ffn_kernel,
        grid_spec=pltpu.PrefetchScalarGridSpec(
            num_scalar_prefetch=1,
            grid=(N_EXPERTS,),
            in_specs=[pl.BlockSpec(memory_space=pl.ANY),
                      pl.BlockSpec((1, D_MODEL, EXPERT_FF), wsel),
                      pl.BlockSpec((1, D_MODEL, EXPERT_FF), wsel),
                      pl.BlockSpec((1, EXPERT_FF, D_MODEL), wsel)],
            out_specs=pl.BlockSpec(memory_space=pl.ANY),
            scratch_shapes=[pltpu.VMEM((FFN_IN_SLOTS, tr) + ROW_TILE, F32),
                            pltpu.VMEM((FFN_OUT_SLOTS, tr) + ROW_TILE, F32),
                            pltpu.VMEM((D_MODEL, EXPERT_FF), BF16),
                            pltpu.VMEM((D_MODEL, EXPERT_FF), BF16),
                            pltpu.VMEM((EXPERT_FF, D_MODEL), BF16),
                            pltpu.SemaphoreType.DMA((FFN_IN_SLOTS,)),
                            pltpu.SemaphoreType.DMA((FFN_OUT_SLOTS,))]),
        out_shape=jax.ShapeDtypeStruct((n_tiles * tr,) + ROW_TILE, F32),
        compiler_params=_cparams(("arbitrary",)),
        name="ffn",
    )(tile_end, xs, wg, wu, wd)


def _combine_kernel(slots_ref, ys_ref, x1_ref, route_ref, mod_ref, y_ref, b1_s, b2_s, sems):
    i = pl.program_id(0)
    tm = x1_ref.shape[0]

    def gather(step, slot):
        def body(j, carry):
            _row_copy(ys_ref, slots_ref[0, step * tm + j], b1_s.at[slot], j, sems.at[slot]).start(
                priority=0)
            _row_copy(ys_ref, slots_ref[1, step * tm + j], b2_s.at[slot], j, sems.at[slot]).start(
                priority=1)
            return carry
        lax.fori_loop(0, tm, body, 0, unroll=8)

    @pl.when(i == 0)
    def _():
        gather(0, 0)

    @pl.when(i + 1 < pl.num_programs(0))
    def _():
        gather(i + 1, (i + 1) % 2)

    slot = i % 2
    for buf in (b1_s, b2_s):
        pltpu.make_async_copy(ys_ref.at[pl.ds(0, tm)], buf.at[slot], sems.at[slot]).wait()
    route = route_ref[...]
    moe = route[:, 4:5] * _from_row_tiles(b1_s[slot]) + route[:, 5:6] * _from_row_tiles(b2_s[slot])
    y_ref[...] = x1_ref[...] + mod_ref[0][5:6] * moe


def _combine(slots, ys, x1, route, mod3, mod_row):
    n = x1.shape[0]
    tm = TM_PRE
    row = lambda w: pl.BlockSpec((tm, w), lambda i, *_: (i, 0))
    return pl.pallas_call(
        _combine_kernel,
        grid_spec=pltpu.PrefetchScalarGridSpec(
            num_scalar_prefetch=1,
            grid=(n // tm,),
            in_specs=[pl.BlockSpec(memory_space=pl.ANY), row(D_MODEL), row(LANES),
                      pl.BlockSpec((1, 6, D_MODEL),
                                   lambda i, *_: (mod_row(i * tm), 0, 0))],
            out_specs=row(D_MODEL),
            scratch_shapes=[pltpu.VMEM((2, tm) + ROW_TILE, F32), pltpu.VMEM((2, tm) + ROW_TILE, F32),
                            pltpu.SemaphoreType.DMA((2,))]),
        out_shape=jax.ShapeDtypeStruct((n, D_MODEL), F32),
        compiler_params=_cparams(("arbitrary",)),
        name="combine",
    )(slots, ys, x1, route, mod3)


def _rope_tables(length):
    rows = length // GRID_W
    r, col = jnp.meshgrid(jnp.arange(rows), jnp.arange(GRID_W), indexing='ij')
    r = r.reshape(-1).astype(F32)
    col = col.reshape(-1).astype(F32)
    half = HEAD_DIM // 2
    inv = ROPE_THETA ** (-jnp.arange(0, half, 2, dtype=F32) / half)
    ang_r = r[:, None] * inv
    ang_c = col[:, None] * inv
    ang = jnp.concatenate([ang_r, ang_r, ang_c, ang_c], axis=-1)
    sign = jnp.where((jnp.arange(HEAD_DIM) // (HEAD_DIM // 4)) % 2 == 0, -1.0, 1.0).astype(F32)
    cos = jnp.tile(jnp.cos(ang), (1, LANES // HEAD_DIM))
    sin = jnp.tile(jnp.sin(ang) * sign, (1, LANES // HEAD_DIM))
    return cos, sin


def _block_diag(w):
    eye = jnp.eye(LRU_BLOCKS, dtype=w.dtype)
    return jnp.einsum('hij,hg->higj', w, eye).reshape(LRU_W, LRU_W)


def _expand_cache_kernel(k_ref, v_ref, kx_ref, vx_ref):
    for ref, xref, one in ((k_ref, kx_ref, 0.0), (v_ref, vx_ref, 1.0)):
        heads = jnp.swapaxes(ref[0, 0], 0, 1)
        col = jnp.concatenate([heads[h] for h in range(N_KV_HEADS)], axis=1)
        _store_expanded(xref, col, one)


def _expand_cache(cache_k, cache_v):
    b, _, t, _, _ = cache_k.shape
    src = pl.BlockSpec((1, 1, t, N_KV_HEADS, HEAD_DIM), lambda i: (i, 0, 0, 0, 0))
    dst = pl.BlockSpec((t, 4 * LANES), lambda i: (i, 0))
    return pl.pallas_call(
        _expand_cache_kernel,
        grid=(b,),
        in_specs=[src, src], out_specs=[dst, dst],
        out_shape=[jax.ShapeDtypeStruct((b * t, 4 * LANES), BF16)] * 2,
        compiler_params=_cparams(("arbitrary",)),
        name="expand_cache",
    )(cache_k, cache_v)


def kernel(x_prompt, x_sample, cache_k, cache_v, state_lru, c, c_ctx, w_mod, b_mod, norm1, norm2,
           w_in, q_norm, k_norm, conv_w, conv_b, lru_wa, lru_ba, lru_wx, lru_bx, lru_lambda, w_out,
           router_grp_w, router_grp_b, router_exp_w, router_exp_b, exp_w_gate, exp_w_up, exp_w_down):
    batch, seq, _ = x_prompt.shape
    dec_batch, dec_seq, _ = x_sample.shape
    past = cache_k.shape[2]
    depth = w_mod.shape[0]
    assert depth == 1

    cvec = jnp.concatenate(
        [c_ctx[None, :], c, jnp.zeros((MOD_ROWS - 1 - dec_batch, D_MODEL), F32)], axis=0)
    mod3 = _modulation(cvec, w_mod[0], b_mod[0][None, :]).reshape(MOD_ROWS, 6, D_MODEL)

    head_id = jnp.arange(QK_W) // HEAD_DIM
    heads = (head_id[:, None] == jnp.arange(LANES)[None, :]).astype(BF16)
    gqk = jnp.concatenate([jnp.tile(q_norm[0], N_HEADS), jnp.tile(k_norm[0], N_KV_HEADS)])[None, :]
    wf = (0.5 * jnp.concatenate([_block_diag(lru_wa[0, 0]), _block_diag(lru_wx[0, 0])], axis=1)).astype(BF16)
    wb = (0.5 * jnp.concatenate([_block_diag(lru_wa[0, 1]), _block_diag(lru_wx[0, 1])], axis=1)).astype(BF16)
    bf = 0.5 * jnp.concatenate([lru_ba[0, 0], lru_bx[0, 0]])[None, :]
    bb = 0.5 * jnp.concatenate([lru_ba[0, 1], lru_bx[0, 1]])[None, :]
    pad = LANES - N_EXPERTS - N_GROUPS
    wr = jnp.concatenate([router_exp_w[0], router_grp_w[0], jnp.zeros((D_MODEL, pad), F32)], axis=1)
    wr_hi = wr.astype(BF16)
    wr2 = jnp.concatenate([wr_hi, (wr - wr_hi.astype(F32)).astype(BF16)], axis=1)
    br = jnp.concatenate([router_exp_b[0], router_grp_b[0], jnp.zeros((pad,), F32)])[None, :]
    g1 = norm1[0][None, :]
    g2 = norm2[0][None, :]
    cw = conv_w[0]
    cb = conv_b[0][None, :]
    lam = lru_lambda[0]
    tri = (jnp.arange(TM_PRE)[:, None] > jnp.arange(TM_PRE)[None, :]).astype(BF16)

    def mixers(x, seq_len, mod_row, tables, extra_k, extra_v, h0, cnt_in):
        q, kx, vx, xr, gb, *cache = _pre(x, mod3, mod_row, g1, w_in[0], heads, gqk, tables, seq_len)
        k_segs = [(kx, seq_len)] + extra_k
        v_segs = [(vx, seq_len)] + extra_v
        attn = _attention(q, k_segs, v_segs, seq_len)
        rec, fin = _lru(xr, gb, h0, cw, cb, wf, wb, bf, bb, lam, seq_len)
        x1, h2, route, cnt, route_t = _post(attn, rec, x, mod3, mod_row, g2, w_out[0], wr2, br, tri,
                                            cnt_in)
        return x1, h2, (route, route_t), cnt, cache, fin

    mod_row_p = lambda tok: 0
    mod_row_s = lambda tok: tok // dec_seq + 1
    xp = x_prompt.reshape(batch * seq, D_MODEL)
    x1p, h2p, (route_p, route_t_p), cnt_p, (kf, vf), fin = mixers(
        xp, seq, mod_row_p, None, [], [], jnp.zeros((batch, 2, LRU_W), F32),
        jnp.zeros((1, LANES), F32))
    xs = x_sample.reshape(dec_batch * dec_seq, D_MODEL)
    ck, cv = _expand_cache(cache_k, cache_v)
    x1s, h2s, (route_s, route_t_s), cnt_all, _, _ = mixers(
        xs, dec_seq, mod_row_s, _rope_tables(dec_seq), [(ck, past)], [(cv, past)],
        state_lru[:, 0], cnt_p)

    n_prompt = batch * seq
    n_tok = n_prompt + dec_batch * dec_seq
    n_tiles = (TOP_K * n_tok + N_EXPERTS * (TR_MOE - 1)) // TR_MOE
    cnt = cnt_all[0, :N_EXPERTS].astype(jnp.int32)
    ntile = (cnt + TR_MOE - 1) // TR_MOE
    tile_end = jnp.cumsum(ntile)
    seg_start = (tile_end - ntile) * TR_MOE
    n_used = tile_end[-1:]
    pad_start = seg_start + cnt
    pad_rows = tile_end * TR_MOE - pad_start

    seg_rows = jnp.broadcast_to(seg_start.astype(F32)[:, None], (N_EXPERTS, LANES))
    slots_p = _plan(route_t_p, seg_rows)
    slots_s = _plan(route_t_s, seg_rows)
    xsort = _dispatch(slots_p, slots_s, pad_start, pad_rows, n_used, h2p, h2s,
                      n_tiles * TR_MOE)
    ysort = _ffn(tile_end, xsort, exp_w_gate[0], exp_w_up[0], exp_w_down[0], n_tiles)
    yp = _combine(slots_p, ysort, x1p, route_p, mod3, mod_row_p)
    ys = _combine(slots_s, ysort, x1s, route_s, mod3, mod_row_s)

    return (yp.reshape(batch, seq, D_MODEL),
            ys.reshape(dec_batch, dec_seq, D_MODEL),
            kf.reshape(batch, 1, seq, N_KV_HEADS, HEAD_DIM),
            vf.reshape(batch, 1, seq, N_KV_HEADS, HEAD_DIM),
            fin.reshape(batch, 1, 2, LRU_W))
```

```python
import functools

import jax
import jax.numpy as jnp
from jax import lax
from jax.experimental import pallas as pl
from jax.experimental.pallas import tpu as pltpu

F32 = jnp.float32
BF16 = jnp.bfloat16

D_MODEL = 1024
GRID_W = 64
ATTN_W = 512
LRU_W = 512
HEAD_DIM = 64
N_HEADS = 8
N_KV_HEADS = 2
KV_W = N_KV_HEADS * HEAD_DIM
LRU_BLOCKS = 8
LRU_BLOCK_W = LRU_W // LRU_BLOCKS
CONV_W = 4
LRU_C = 8.0
IN_W = ATTN_W + 2 * KV_W + 2 * LRU_W
QK_W = ATTN_W + KV_W
N_GROUPS = 4
EXPERTS_PER_GROUP = 8
N_EXPERTS = N_GROUPS * EXPERTS_PER_GROUP
TOP_K = 2
EXPERT_FF = D_MODEL // 4
ROPE_THETA = 10000.0
EPS = 1e-6

LANES = 128
SUBLANES = 8
MOD_ROWS = 8
VMEM_LIMIT = 48 * 1024 * 1024

TM_PRE = 512
TQ_ATT = 1024
TC_LRU = 512
TR_MOE = 256
CH_DISPATCH = 1024
FFN_IN_SLOTS = 4
FFN_OUT_SLOTS = 3


def _cparams(sem):
    return pltpu.CompilerParams(dimension_semantics=sem, vmem_limit_bytes=VMEM_LIMIT)


def _dot(a, b):
    return jnp.dot(a, b, preferred_element_type=F32)


def _dot_nt(a, b):
    return lax.dot_general(a, b, (((1,), (1,)), ((), ())), preferred_element_type=F32)


ROW_TILE = (D_MODEL // LANES, LANES)
LOG2_E = 1.4426950408889634
Q_SCALE = HEAD_DIM ** -0.5 * LOG2_E
ONES_LANE_EVEN = HEAD_DIM
ONES_LANE_ODD = 0


def _to_row_tiles(x):
    cols = jnp.stack([x[:, c * LANES:(c + 1) * LANES] for c in range(D_MODEL // LANES)], axis=0)
    return jnp.swapaxes(cols, 0, 1)


def _from_row_tiles(x3):
    cols = jnp.swapaxes(x3, 0, 1)
    return jnp.concatenate([cols[c] for c in range(D_MODEL // LANES)], axis=1)


def _sigmoid(x):
    return 0.5 * jnp.tanh(0.5 * x) + 0.5


def _split_bf16(x):
    hi = x.astype(BF16)
    lo = (x - hi.astype(F32)).astype(BF16)
    return hi, lo


def _mod_kernel(c_ref, w_ref, b_ref, o_ref):
    c = c_ref[...]
    s = (c * jax.nn.sigmoid(c)).astype(BF16)
    o_ref[...] = _dot(s, w_ref[...].astype(BF16)) + b_ref[...]


def _modulation(cvec, w_mod, b_mod):
    n_out = w_mod.shape[1]
    tn = n_out // 4
    return pl.pallas_call(
        _mod_kernel,
        grid=(n_out // tn,),
        in_specs=[pl.BlockSpec((MOD_ROWS, D_MODEL), lambda j: (0, 0)),
                  pl.BlockSpec((D_MODEL, tn), lambda j: (0, j)),
                  pl.BlockSpec((1, tn), lambda j: (0, j))],
        out_specs=pl.BlockSpec((MOD_ROWS, tn), lambda j: (0, j)),
        out_shape=jax.ShapeDtypeStruct((MOD_ROWS, n_out), F32),
        compiler_params=_cparams(("arbitrary",)),
        name="modulation",
    )(cvec, w_mod, b_mod)


def _split_heads(col):
    return jnp.swapaxes(jnp.stack([col[:, h * HEAD_DIM:(h + 1) * HEAD_DIM]
                                   for h in range(N_KV_HEADS)], axis=0), 0, 1)


def _store_expanded(xref, col, one):
    lane = lax.broadcasted_iota(jnp.int32, col.shape, 1)
    lo_half = lane < HEAD_DIM
    swapped = pltpu.roll(col, HEAD_DIM, 1)
    fill_hi = jnp.where(lane == ONES_LANE_EVEN, one, 0.0)
    fill_lo = jnp.where(lane == ONES_LANE_ODD, one, 0.0)
    xref[:, 0 * LANES:1 * LANES] = jnp.where(lo_half, col, fill_hi).astype(BF16)
    xref[:, 1 * LANES:2 * LANES] = jnp.where(lo_half, fill_lo, swapped).astype(BF16)
    xref[:, 2 * LANES:3 * LANES] = jnp.where(lo_half, swapped, fill_hi).astype(BF16)
    xref[:, 3 * LANES:4 * LANES] = jnp.where(lo_half, fill_lo, col).astype(BF16)


def _pre_kernel(*refs, rope):
    if rope:
        (x_ref, mod_ref, g1_ref, win_ref, heads_ref, heads_t_ref, gqk_ref, cos_ref, sin_ref,
         q_ref, kx_ref, vx_ref, xr_ref, gb_ref, win_s) = refs
    else:
        (x_ref, mod_ref, g1_ref, win_ref, heads_ref, heads_t_ref, gqk_ref,
         q_ref, kx_ref, vx_ref, xr_ref, gb_ref, kf_ref, vf_ref, win_s) = refs

    @pl.when(pl.program_id(0) == 0)
    def _():
        win_s[...] = win_ref[...].astype(BF16)

    x = x_ref[...]
    m = mod_ref[0]
    ms = jnp.mean(x * x, axis=-1, keepdims=True)
    y = x * lax.rsqrt(ms + EPS) * g1_ref[...]
    h = y * (1.0 + m[1:2]) + m[0:1]
    z = _dot(h.astype(BF16), win_s[...])

    qk = z[:, :QK_W]
    ss = _dot((qk * qk).astype(BF16), heads_ref[...])
    hi, lo = _split_bf16(lax.rsqrt(ss * (1.0 / HEAD_DIM) + EPS))
    qk = qk * (_dot(hi, heads_t_ref[...]) + _dot(lo, heads_t_ref[...])) * gqk_ref[...]

    lane = lax.broadcasted_iota(jnp.int32, (x.shape[0], LANES), 1)
    cols = []
    for c in range(QK_W // LANES):
        xc = qk[:, c * LANES:(c + 1) * LANES]
        if rope:
            left = pltpu.roll(xc, LANES - HEAD_DIM // 4, 1)
            right = pltpu.roll(xc, HEAD_DIM // 4, 1)
            rot = jnp.where((lane // (HEAD_DIM // 4)) % 2 == 0, left, right)
            xc = xc * cos_ref[...] + rot * sin_ref[...]
        cols.append(xc)
    for c in range(ATTN_W // LANES):
        q_ref[:, c * LANES:(c + 1) * LANES] = (cols[c] * Q_SCALE).astype(BF16)

    k_col = cols[ATTN_W // LANES]
    v_col = z[:, QK_W:QK_W + KV_W]
    _store_expanded(kx_ref, k_col, 0.0)
    _store_expanded(vx_ref, v_col, 1.0)
    if not rope:
        kf_ref[...] = _split_heads(k_col)
        vf_ref[...] = _split_heads(v_col)

    xr_ref[...] = z[:, QK_W + KV_W:QK_W + KV_W + LRU_W]
    gb_ref[...] = z[:, QK_W + KV_W + LRU_W:]


def _pre(x, mod3, mod_row, g1, w_in, heads, gqk, tables, seq_len):
    n = x.shape[0]
    tm = TM_PRE
    tiles_per_seq = seq_len // tm
    rope = tables is not None
    const = lambda i: (0, 0)
    in_specs = [pl.BlockSpec((tm, D_MODEL), lambda i: (i, 0)),
                pl.BlockSpec((1, 6, D_MODEL), lambda i: (mod_row(i * tm), 0, 0)),
                pl.BlockSpec((1, D_MODEL), const),
                pl.BlockSpec((D_MODEL, IN_W), const),
                pl.BlockSpec((QK_W, LANES), const),
                pl.BlockSpec((LANES, QK_W), const),
                pl.BlockSpec((1, QK_W), const)]
    args = [x, mod3, g1, w_in, heads, heads.T, gqk]
    if rope:
        in_specs += [pl.BlockSpec((tm, LANES), lambda i: (i % tiles_per_seq, 0))] * 2
        args += list(tables)
    row = lambda w: pl.BlockSpec((tm, w), lambda i: (i, 0))
    out_shape = [jax.ShapeDtypeStruct((n, ATTN_W), BF16),
                 jax.ShapeDtypeStruct((n, 4 * LANES), BF16),
                 jax.ShapeDtypeStruct((n, 4 * LANES), BF16),
                 jax.ShapeDtypeStruct((n, LRU_W), F32),
                 jax.ShapeDtypeStruct((n, LRU_W), F32)]
    out_specs = [row(ATTN_W), row(4 * LANES), row(4 * LANES), row(LRU_W), row(LRU_W)]
    if not rope:
        cache = pl.BlockSpec((tm, N_KV_HEADS, HEAD_DIM), lambda i: (i, 0, 0))
        out_shape += [jax.ShapeDtypeStruct((n, N_KV_HEADS, HEAD_DIM), F32)] * 2
        out_specs += [cache, cache]
    return pl.pallas_call(
        functools.partial(_pre_kernel, rope=rope),
        grid=(n // tm,),
        in_specs=in_specs, out_specs=out_specs, out_shape=out_shape,
        scratch_shapes=[pltpu.VMEM((D_MODEL, IN_W), BF16)],
        compiler_params=_cparams(("arbitrary",)),
        name="pre_rope" if rope else "pre",
    )(*args)


def _attn_kernel(*refs, n_seg):
    q_ref = refs[0]
    k_refs = refs[1:1 + n_seg]
    v_refs = refs[1 + n_seg:1 + 2 * n_seg]
    o_ref = refs[1 + 2 * n_seg]
    lane = lax.broadcasted_iota(jnp.int32, (q_ref.shape[0], LANES), 1)
    for c in range(ATTN_W // LANES):
        qc = q_ref[:, c * LANES:(c + 1) * LANES]
        g = c // 2
        accs = []
        for par in range(2):
            sl = slice((2 * g + par) * LANES, (2 * g + par + 1) * LANES)
            ss = [_dot_nt(qc, k[:, sl]) for k in k_refs]
            mx = functools.reduce(jnp.maximum, [jnp.max(s, axis=-1, keepdims=True) for s in ss])
            ps = [jnp.exp2((s - mx).astype(BF16)) for s in ss]
            accs.append(functools.reduce(lambda a, b: a + b,
                                         [_dot(p, v[:, sl]) for p, v in zip(ps, v_refs)]))
        even = accs[0] / accs[0][:, ONES_LANE_EVEN:ONES_LANE_EVEN + 1]
        odd = accs[1] / accs[1][:, ONES_LANE_ODD:ONES_LANE_ODD + 1]
        o_ref[:, c * LANES:(c + 1) * LANES] = jnp.where(lane < HEAD_DIM, even, odd).astype(BF16)


def _attention(q, k_segs, v_segs, seq_len):
    n = q.shape[0]
    tq = min(TQ_ATT, seq_len)
    nq = seq_len // tq
    n_seg = len(k_segs)
    in_specs = [pl.BlockSpec((tq, ATTN_W), lambda b, i: (b * nq + i, 0))]
    for arr, t in list(k_segs) + list(v_segs):
        in_specs.append(pl.BlockSpec((t, 4 * LANES), lambda b, i: (b, 0)))
    return pl.pallas_call(
        functools.partial(_attn_kernel, n_seg=n_seg),
        grid=(n // seq_len, nq),
        in_specs=in_specs,
        out_specs=pl.BlockSpec((tq, ATTN_W), lambda b, i: (b * nq + i, 0)),
        out_shape=jax.ShapeDtypeStruct((n, ATTN_W), BF16),
        compiler_params=_cparams(("arbitrary", "arbitrary")),
        name="attention_%dseg" % n_seg,
    )(q, *[a for a, _ in k_segs], *[a for a, _ in v_segs])


def _log_sigmoid(x):
    return jnp.minimum(x, 0.0) - jnp.log1p(jnp.exp(-jnp.abs(x)))


def _tile_scan(a, b, reverse):
    row = lax.broadcasted_iota(jnp.int32, a.shape, 0)
    d = 1
    while d < SUBLANES:
        if reverse:
            keep = row < SUBLANES - d
            shift = SUBLANES - d
        else:
            keep = row >= d
            shift = d
        a_sh = jnp.where(keep, pltpu.roll(a, shift, 0), 1.0)
        b_sh = jnp.where(keep, pltpu.roll(b, shift, 0), 0.0)
        b = a * b_sh + b
        a = a * a_sh
        d *= 2
    return a, b


def _lru_kernel(xr_ref, gb_ref, h0_ref, cw_ref, cb_ref, wf_ref, wb_ref, bf_ref, bb_ref, lam_ref,
                rec_ref, fin_ref, xpad_s, xc_s, hf_s, a_s, b_s, *, seq_len):
    tc = min(TC_LRU, seq_len)
    n_chunks = seq_len // tc
    n_tiles = tc // SUBLANES
    zpad = jnp.zeros((SUBLANES, LRU_W), F32)
    xpad_s[0:SUBLANES, :] = zpad
    xpad_s[SUBLANES:SUBLANES + seq_len, :] = xr_ref[...]
    xpad_s[SUBLANES + seq_len:2 * SUBLANES + seq_len, :] = zpad

    half_cl = (0.5 * LRU_C * LOG2_E) * _log_sigmoid(lam_ref[...])

    def gates(xcc, w_ref, bias_ref, half_cl_d):
        t = jnp.tanh(_dot(xcc.astype(BF16), w_ref[...]) + bias_ref[...])
        a = jnp.exp2(t[:, :LRU_W] * half_cl_d + half_cl_d)
        half_x = 0.5 * xcc
        a_s[...] = a
        u = 1.0 - a * a
        b_s[...] = jnp.where(u > 0.0, u * lax.rsqrt(u), 0.0) * (t[:, LRU_W:] * half_x + half_x)

    h = h0_ref[0, 0:1, :]
    for c in range(n_chunks):
        base = c * tc
        xcc = cb_ref[...] + functools.reduce(
            lambda u, v: u + v,
            [cw_ref[j:j + 1, :] * xpad_s[base + SUBLANES - 1 + j:base + SUBLANES - 1 + j + tc, :]
             for j in range(CONV_W)])
        xc_s[base:base + tc, :] = xcc
        gates(xcc, wf_ref, bf_ref, half_cl[0:1])

        def fwd_tile(t, hc, base=base):
            r0 = pl.multiple_of(t * SUBLANES, SUBLANES)
            ca, cb = _tile_scan(a_s[pl.ds(r0, SUBLANES), :], b_s[pl.ds(r0, SUBLANES), :], False)
            hh = ca * hc + cb
            hf_s[pl.ds(base + r0, SUBLANES), :] = hh
            return hh[SUBLANES - 1:SUBLANES, :]

        h = lax.fori_loop(0, n_tiles, fwd_tile, h, unroll=4)
    fin_ref[0, 0:1, :] = h

    h = h0_ref[0, 1:2, :]
    for c in reversed(range(n_chunks)):
        base = c * tc
        gates(xc_s[base:base + tc, :], wb_ref, bb_ref, half_cl[1:2])

        def bwd_tile(t, hc, base=base):
            r0 = pl.multiple_of((n_tiles - 1 - t) * SUBLANES, SUBLANES)
            ca, cb = _tile_scan(a_s[pl.ds(r0, SUBLANES), :], b_s[pl.ds(r0, SUBLANES), :], True)
            hh = ca * hc + cb
            gate = jax.nn.gelu(gb_ref[pl.ds(base + r0, SUBLANES), :], approximate=True)
            rec_ref[pl.ds(base + r0, SUBLANES), :] = (
                (hf_s[pl.ds(base + r0, SUBLANES), :] + hh) * gate).astype(rec_ref.dtype)
            return hh[0:1, :]

        h = lax.fori_loop(0, n_tiles, bwd_tile, h, unroll=4)
    fin_ref[0, 1:2, :] = h


def _lru(xr, gb, h0, conv_w, conv_b, wf, wb, bf, bb, lam, seq_len):
    n = xr.shape[0]
    batch = n // seq_len
    const = lambda b: (0, 0)
    seq = pl.BlockSpec((seq_len, LRU_W), lambda b: (b, 0))
    st = pl.BlockSpec((1, 2, LRU_W), lambda b: (b, 0, 0))
    return pl.pallas_call(
        functools.partial(_lru_kernel, seq_len=seq_len),
        grid=(batch,),
        in_specs=[seq, seq, st,
                  pl.BlockSpec((CONV_W, LRU_W), const), pl.BlockSpec((1, LRU_W), const),
                  pl.BlockSpec((LRU_W, 2 * LRU_W), const), pl.BlockSpec((LRU_W, 2 * LRU_W), const),
                  pl.BlockSpec((1, 2 * LRU_W), const), pl.BlockSpec((1, 2 * LRU_W), const),
                  pl.BlockSpec((2, LRU_W), const)],
        out_specs=[seq, st],
        out_shape=[jax.ShapeDtypeStruct((n, LRU_W), BF16),
                   jax.ShapeDtypeStruct((batch, 2, LRU_W), F32)],
        scratch_shapes=[pltpu.VMEM((seq_len + 2 * SUBLANES, LRU_W), F32),
                        pltpu.VMEM((seq_len, LRU_W), F32),
                        pltpu.VMEM((seq_len, LRU_W), F32),
                        pltpu.VMEM((min(TC_LRU, seq_len), LRU_W), F32),
                        pltpu.VMEM((min(TC_LRU, seq_len), LRU_W), F32)],
        compiler_params=_cparams(("arbitrary",)),
        name="lru_%d" % seq_len,
    )(xr, gb, h0, conv_w, conv_b, wf, wb, bf, bb, lam)


def _post_kernel(attn_ref, rec_ref, x_ref, mod_ref, g2_ref, wo_ref, wr2_ref, br_ref,
                 tri_ref, cnt_in_ref, x1_ref, h2_ref, route_ref, cnt_ref, route_t_ref, cnt_s, wo_s):
    @pl.when(pl.program_id(0) == 0)
    def _():
        cnt_s[...] = cnt_in_ref[...]
        wo_s[...] = wo_ref[...].astype(BF16)

    m = mod_ref[0]
    u = _dot(attn_ref[...], wo_s[:ATTN_W, :]) + _dot(rec_ref[...], wo_s[ATTN_W:, :])
    x1 = x_ref[...] + m[2:3] * u
    x1_ref[...] = x1
    ms = jnp.mean(x1 * x1, axis=-1, keepdims=True)
    h2 = x1 * lax.rsqrt(ms + EPS) * g2_ref[...]
    h2 = h2 * (1.0 + m[4:5]) + m[3:4]
    hi, lo = _split_bf16(h2)
    h2_ref[...] = _to_row_tiles(h2)

    hw = _dot(hi, wr2_ref[...])
    logits = hw[:, :LANES] + hw[:, LANES:] + _dot(lo, wr2_ref[:, :LANES]) + br_ref[...]
    lane_i = lax.broadcasted_iota(jnp.int32, logits.shape, 1)
    lane = lane_i.astype(F32)
    lane_group = (lane_i // EXPERTS_PER_GROUP).astype(F32)
    neg = -jnp.inf
    big = float(1 << 20)
    gmask = (lane_i >= N_EXPERTS) & (lane_i < N_EXPERTS + N_GROUPS)
    gl = jnp.where(gmask, logits, neg)
    gmax = jnp.max(gl, axis=-1, keepdims=True)
    gidx = jnp.min(jnp.where(gl == gmax, lane - N_EXPERTS, big), axis=-1, keepdims=True)
    p_sel = 1.0 / jnp.sum(jnp.where(gmask, jnp.exp(gl - gmax), 0.0), axis=-1, keepdims=True)

    emask = (lane_i < N_EXPERTS) & (lane_group == gidx)
    el = jnp.where(emask, logits, neg)
    v1 = jnp.max(el, axis=-1, keepdims=True)
    i1 = jnp.min(jnp.where(el == v1, lane, big), axis=-1, keepdims=True)
    el2 = jnp.where(lane == i1, neg, el)
    v2 = jnp.max(el2, axis=-1, keepdims=True)
    i2 = jnp.min(jnp.where(el2 == v2, lane, big), axis=-1, keepdims=True)
    e2 = jnp.exp(v2 - v1)
    w1 = p_sel / (1.0 + e2)
    w2 = p_sel * e2 / (1.0 + e2)

    oh1 = lane == i1
    oh2 = lane == i2
    oh = jnp.where(oh1, 1.0, 0.0) + jnp.where(oh2, 1.0, 0.0)
    before = _dot(tri_ref[...], oh.astype(BF16)) + cnt_s[...]
    rank1 = jnp.sum(jnp.where(oh1, before, 0.0), axis=-1, keepdims=True)
    rank2 = jnp.sum(jnp.where(oh2, before, 0.0), axis=-1, keepdims=True)
    cnt = cnt_s[...] + jnp.sum(oh, axis=0, keepdims=True)
    cnt_s[...] = cnt
    cnt_ref[...] = cnt
    fields = (i1, i2, rank1, rank2, w1, w2)
    route = jnp.zeros(logits.shape, F32)
    for k, val in enumerate(fields):
        route = jnp.where(lane_i == k, val, route)
    route_ref[...] = route
    route_t_ref[...] = jnp.transpose(route)[:SUBLANES, :]


def _post(attn, rec, x, mod3, mod_row, g2, w_out, wr2, br, tri, cnt_in):
    n = x.shape[0]
    tm = TM_PRE
    const = lambda i: (0, 0)
    row = lambda w: pl.BlockSpec((tm, w), lambda i: (i, 0))
    return pl.pallas_call(
        _post_kernel,
        grid=(n // tm,),
        in_specs=[row(ATTN_W), row(LRU_W), row(D_MODEL),
                  pl.BlockSpec((1, 6, D_MODEL), lambda i: (mod_row(i * tm), 0, 0)),
                  pl.BlockSpec((1, D_MODEL), const),
                  pl.BlockSpec((D_MODEL, D_MODEL), const),
                  pl.BlockSpec((D_MODEL, 2 * LANES), const),
                  pl.BlockSpec((1, LANES), const),
                  pl.BlockSpec((tm, tm), const),
                  pl.BlockSpec((1, LANES), const)],
        out_specs=[row(D_MODEL), pl.BlockSpec((tm,) + ROW_TILE, lambda i: (i, 0, 0)), row(LANES),
                   pl.BlockSpec((1, LANES), const),
                   pl.BlockSpec((SUBLANES, tm), lambda i: (0, i))],
        out_shape=[jax.ShapeDtypeStruct((n, D_MODEL), F32),
                   jax.ShapeDtypeStruct((n,) + ROW_TILE, F32),
                   jax.ShapeDtypeStruct((n, LANES), F32),
                   jax.ShapeDtypeStruct((1, LANES), F32),
                   jax.ShapeDtypeStruct((SUBLANES, n), F32)],
        scratch_shapes=[pltpu.VMEM((1, LANES), F32), pltpu.VMEM((D_MODEL, D_MODEL), BF16)],
        compiler_params=_cparams(("arbitrary",)),
        name="post",
    )(attn, rec, x, mod3, g2, w_out, wr2, br, tri, cnt_in)


def _row_copy(src_ref, src_row, dst_ref, dst_row, sem):
    return pltpu.make_async_copy(src_ref.at[pl.ds(src_row, 1)], dst_ref.at[pl.ds(dst_row, 1)], sem)


def _plan_kernel(route_t_ref, seg_ref, slots_ref):
    route_t = route_t_ref[...]
    t = route_t.shape[1]
    seg = jnp.concatenate([seg_ref[...]] * (t // LANES), axis=1)
    expert = lax.broadcasted_iota(jnp.int32, seg.shape, 0).astype(F32)
    rows = []
    for k in range(TOP_K):
        start = jnp.sum(jnp.where(expert == route_t[k:k + 1, :], seg, 0.0), axis=0, keepdims=True)
        rows.append(start + route_t[TOP_K + k:TOP_K + k + 1, :])
    rows.append(jnp.zeros((SUBLANES - TOP_K, t), F32))
    slots_ref[...] = jnp.concatenate(rows, axis=0).astype(jnp.int32)


def _plan(route_t, seg_rows):
    n = route_t.shape[1]
    tm = TM_PRE
    return pl.pallas_call(
        _plan_kernel,
        grid=(n // tm,),
        in_specs=[pl.BlockSpec((SUBLANES, tm), lambda i: (0, i)),
                  pl.BlockSpec((N_EXPERTS, LANES), lambda i: (0, 0))],
        out_specs=pl.BlockSpec((SUBLANES, tm), lambda i: (0, i)),
        out_shape=jax.ShapeDtypeStruct((SUBLANES, n), jnp.int32),
        compiler_params=_cparams(("arbitrary",)),
        name="plan",
    )(route_t, seg_rows)


def _dispatch_kernel(slots_p_ref, slots_s_ref, pad_start_ref, pad_rows_ref, nu_ref, h2p_ref, h2s_ref,
                     xs_ref, zero_s, sem, zsem, *, n_prompt):
    i = pl.program_id(0)
    ch = CH_DISPATCH

    def zero_fill(op):
        for e in range(N_EXPERTS):
            rows = pad_rows_ref[e]
            for b in range(TR_MOE.bit_length() - 1):
                size = 1 << b
                off = (rows >> (b + 1)) << (b + 1)

                @pl.when(((rows >> b) & 1) == 1)
                def _():
                    op(pltpu.make_async_copy(zero_s.at[pl.ds(0, size)],
                                             xs_ref.at[pl.ds(pad_start_ref[e] + off, size)], zsem))

        def zero_tile(t, carry):
            op(pltpu.make_async_copy(
                zero_s, xs_ref.at[pl.ds(pl.multiple_of(t * TR_MOE, TR_MOE), TR_MOE)], zsem))
            return carry
        lax.fori_loop(nu_ref[0], xs_ref.shape[0] // TR_MOE, zero_tile, 0)

    @pl.when(i == 0)
    def _():
        zero_s[...] = jnp.zeros_like(zero_s)
        zero_fill(lambda cp: cp.start())

    @pl.when(i == pl.num_programs(0) - 1)
    def _():
        zero_fill(lambda cp: cp.wait())

    def scatter(src_ref, slots_ref, first_token):
        n_path = slots_ref.shape[0] // TOP_K

        def body(j, carry):
            for k in range(TOP_K):
                _row_copy(src_ref, j, xs_ref, slots_ref[k * n_path + i * ch + j - first_token],
                          sem).start(priority=k)
            return carry
        lax.fori_loop(0, ch, body, 0, unroll=8)
        for _ in range(TOP_K):
            pltpu.make_async_copy(src_ref, xs_ref.at[pl.ds(0, ch)], sem).wait()

    @pl.when(i < n_prompt // ch)
    def _():
        scatter(h2p_ref, slots_p_ref, 0)

    @pl.when(i >= n_prompt // ch)
    def _():
        scatter(h2s_ref, slots_s_ref, n_prompt)


def _dispatch(slots_p, slots_s, pad_start, pad_rows, n_used, h2p, h2s, n_rows):
    n_prompt = h2p.shape[0]
    n = n_prompt + h2s.shape[0]
    ch = CH_DISPATCH
    npc = n_prompt // ch
    last_p = npc - 1
    return pl.pallas_call(
        functools.partial(_dispatch_kernel, n_prompt=n_prompt),
        grid_spec=pltpu.PrefetchScalarGridSpec(
            num_scalar_prefetch=5,
            grid=(n // ch,),
            in_specs=[pl.BlockSpec((ch,) + ROW_TILE, lambda i, *_: (jnp.minimum(i, last_p), 0, 0)),
                      pl.BlockSpec((ch,) + ROW_TILE, lambda i, *_: (jnp.maximum(i - npc, 0), 0, 0))],
            out_specs=pl.BlockSpec(memory_space=pl.ANY),
            scratch_shapes=[pltpu.VMEM((TR_MOE,) + ROW_TILE, F32), pltpu.SemaphoreType.DMA,
                            pltpu.SemaphoreType.DMA]),
        out_shape=jax.ShapeDtypeStruct((n_rows,) + ROW_TILE, F32),
        compiler_params=_cparams(("arbitrary",)),
        name="dispatch",
    )(slots_p, slots_s, pad_start, pad_rows, n_used, h2p, h2s)


def _ffn_kernel(tend_ref, xs_ref, wg_ref, wu_ref, wd_ref, ys_ref,
                xbuf, ybuf, wg_s, wu_s, wd_s, sem_in, sem_out):
    e = pl.program_id(0)
    n_used = tend_ref[N_EXPERTS - 1]
    t_first = jnp.where(e == 0, 0, tend_ref[jnp.maximum(e - 1, 0)])
    t_last = tend_ref[e]
    n_in = xbuf.shape[0]
    n_out = ybuf.shape[0]

    def tile_rows(t):
        return pl.ds(pl.multiple_of(t * TR_MOE, TR_MOE), TR_MOE)

    def fetch(t):
        return pltpu.make_async_copy(xs_ref.at[tile_rows(t)], xbuf.at[t % n_in], sem_in.at[t % n_in])

    def writeback(t):
        return pltpu.make_async_copy(ybuf.at[t % n_out], ys_ref.at[tile_rows(t)],
                                     sem_out.at[t % n_out])

    @pl.when(e == 0)
    def _():
        for t in range(n_in - 1):
            @pl.when(t < n_used)
            def _():
                fetch(t).start()

    @pl.when(t_last > t_first)
    def _():
        wg_s[...] = wg_ref[0].astype(BF16)
        wu_s[...] = wu_ref[0].astype(BF16)
        wd_s[...] = wd_ref[0].astype(BF16)

    def tile(t, carry):
        fetch(t).wait()

        @pl.when(t + n_in - 1 < n_used)
        def _():
            fetch(t + n_in - 1).start()

        @pl.when(t >= n_out)
        def _():
            writeback(t - n_out).wait()

        x = _from_row_tiles(xbuf[t % n_in]).astype(BF16)
        hg = _dot(x, wg_s[...])
        hu = _dot(x, wu_s[...])
        act = (hg * _sigmoid(hg)) * hu
        ybuf[t % n_out] = _to_row_tiles(_dot(act.astype(BF16), wd_s[...]))
        writeback(t).start()
        return carry

    lax.fori_loop(t_first, t_last, tile, 0)

    @pl.when(e == N_EXPERTS - 1)
    def _():
        for back in range(n_out, 0, -1):
            @pl.when(n_used >= back)
            def _():
                writeback(n_used - back).wait()
        n_all = ys_ref.shape[0] // TR_MOE
        ybuf[0] = jnp.zeros(ybuf.shape[1:], F32)

        def zero_tile(t):
            return pltpu.make_async_copy(ybuf.at[0], ys_ref.at[tile_rows(t)], sem_out.at[0])

        lax.fori_loop(n_used, n_all, lambda t, c: (zero_tile(t).start(), c)[1], 0)
        lax.fori_loop(n_used, n_all, lambda t, c: (zero_tile(t).wait(), c)[1], 0)


def _ffn(tile_end, xs, wg, wu, wd, n_tiles):
    tr = TR_MOE
    wsel = lambda e, tend: (e, 0, 0)
    return pl.pallas_call(
        _ffn_kernel,
        grid_spec=pltpu.PrefetchScalarGridSpec(
            num_scalar_prefetch=1,
            grid=(N_EXPERTS,),
            in_specs=[pl.BlockSpec(memory_space=pl.ANY),
                      pl.BlockSpec((1, D_MODEL, EXPERT_FF), wsel),
                      pl.BlockSpec((1, D_MODEL, EXPERT_FF), wsel),
                      pl.BlockSpec((1, EXPERT_FF, D_MODEL), wsel)],
            out_specs=pl.BlockSpec(memory_space=pl.ANY),
            scratch_shapes=[pltpu.VMEM((FFN_IN_SLOTS, tr) + ROW_TILE, F32),
                            pltpu.VMEM((FFN_OUT_SLOTS, tr) + ROW_TILE, F32),
                            pltpu.VMEM((D_MODEL, EXPERT_FF), BF16),
                            pltpu.VMEM((D_MODEL, EXPERT_FF), BF16),
                            pltpu.VMEM((EXPERT_FF, D_MODEL), BF16),
                            pltpu.SemaphoreType.DMA((FFN_IN_SLOTS,)),
                            pltpu.SemaphoreType.DMA((FFN_OUT_SLOTS,))]),
        out_shape=jax.ShapeDtypeStruct((n_tiles * tr,) + ROW_TILE, F32),
        compiler_params=_cparams(("arbitrary",)),
        name="ffn",
    )(tile_end, xs, wg, wu, wd)


def _combine_kernel(slots_ref, ys_ref, x1_ref, route_ref, mod_ref, y_ref, b1_s, b2_s, sems):
    i = pl.program_id(0)
    tm = x1_ref.shape[0]
    n_path = slots_ref.shape[0] // TOP_K

    def gather(step, slot):
        def body(j, carry):
            _row_copy(ys_ref, slots_ref[step * tm + j], b1_s.at[slot], j, sems.at[slot]).start(
                priority=0)
            _row_copy(ys_ref, slots_ref[n_path + step * tm + j], b2_s.at[slot], j,
                      sems.at[slot]).start(priority=1)
            return carry
        lax.fori_loop(0, tm, body, 0, unroll=8)

    @pl.when(i == 0)
    def _():
        gather(0, 0)

    @pl.when(i + 1 < pl.num_programs(0))
    def _():
        gather(i + 1, (i + 1) % 2)

    slot = i % 2
    for buf in (b1_s, b2_s):
        pltpu.make_async_copy(ys_ref.at[pl.ds(0, tm)], buf.at[slot], sems.at[slot]).wait()
    route = route_ref[...]
    moe = route[:, 4:5] * _from_row_tiles(b1_s[slot]) + route[:, 5:6] * _from_row_tiles(b2_s[slot])
    y_ref[...] = x1_ref[...] + mod_ref[0][5:6] * moe


def _combine(slots, ys, x1, route, mod3, mod_row):
    n = x1.shape[0]
    tm = TM_PRE
    row = lambda w: pl.BlockSpec((tm, w), lambda i, *_: (i, 0))
    return pl.pallas_call(
        _combine_kernel,
        grid_spec=pltpu.PrefetchScalarGridSpec(
            num_scalar_prefetch=1,
            grid=(n // tm,),
            in_specs=[pl.BlockSpec(memory_space=pl.ANY), row(D_MODEL), row(LANES),
                      pl.BlockSpec((1, 6, D_MODEL),
                                   lambda i, *_: (mod_row(i * tm), 0, 0))],
            out_specs=row(D_MODEL),
            scratch_shapes=[pltpu.VMEM((2, tm) + ROW_TILE, F32), pltpu.VMEM((2, tm) + ROW_TILE, F32),
                            pltpu.SemaphoreType.DMA((2,))]),
        out_shape=jax.ShapeDtypeStruct((n, D_MODEL), F32),
        compiler_params=_cparams(("arbitrary",)),
        name="combine",
    )(slots, ys, x1, route, mod3)


def _rope_tables(length):
    rows = length // GRID_W
    r, col = jnp.meshgrid(jnp.arange(rows), jnp.arange(GRID_W), indexing='ij')
    r = r.reshape(-1).astype(F32)
    col = col.reshape(-1).astype(F32)
    half = HEAD_DIM // 2
    inv = ROPE_THETA ** (-jnp.arange(0, half, 2, dtype=F32) / half)
    ang_r = r[:, None] * inv
    ang_c = col[:, None] * inv
    ang = jnp.concatenate([ang_r, ang_r, ang_c, ang_c], axis=-1)
    sign = jnp.where((jnp.arange(HEAD_DIM) // (HEAD_DIM // 4)) % 2 == 0, -1.0, 1.0).astype(F32)
    cos = jnp.tile(jnp.cos(ang), (1, LANES // HEAD_DIM))
    sin = jnp.tile(jnp.sin(ang) * sign, (1, LANES // HEAD_DIM))
    return cos, sin


def _block_diag(w):
    eye = jnp.eye(LRU_BLOCKS, dtype=w.dtype)
    return jnp.einsum('hij,hg->higj', w, eye).reshape(LRU_W, LRU_W)


def _expand_cache_kernel(k_ref, v_ref, kx_ref, vx_ref):
    for ref, xref, one in ((k_ref, kx_ref, 0.0), (v_ref, vx_ref, 1.0)):
        heads = jnp.swapaxes(ref[0, 0], 0, 1)
        col = jnp.concatenate([heads[h] for h in range(N_KV_HEADS)], axis=1)
        _store_expanded(xref, col, one)


def _expand_cache(cache_k, cache_v):
    b, _, t, _, _ = cache_k.shape
    src = pl.BlockSpec((1, 1, t, N_KV_HEADS, HEAD_DIM), lambda i: (i, 0, 0, 0, 0))
    dst = pl.BlockSpec((t, 4 * LANES), lambda i: (i, 0))
    return pl.pallas_call(
        _expand_cache_kernel,
        grid=(b,),
        in_specs=[src, src], out_specs=[dst, dst],
        out_shape=[jax.ShapeDtypeStruct((b * t, 4 * LANES), BF16)] * 2,
        compiler_params=_cparams(("arbitrary",)),
        name="expand_cache",
    )(cache_k, cache_v)


def kernel(x_prompt, x_sample, cache_k, cache_v, state_lru, c, c_ctx, w_mod, b_mod, norm1, norm2,
           w_in, q_norm, k_norm, conv_w, conv_b, lru_wa, lru_ba, lru_wx, lru_bx, lru_lambda, w_out,
           router_grp_w, router_grp_b, router_exp_w, router_exp_b, exp_w_gate, exp_w_up, exp_w_down):
    batch, seq, _ = x_prompt.shape
    dec_batch, dec_seq, _ = x_sample.shape
    past = cache_k.shape[2]
    depth = w_mod.shape[0]
    assert depth == 1

    cvec = jnp.concatenate(
        [c_ctx[None, :], c, jnp.zeros((MOD_ROWS - 1 - dec_batch, D_MODEL), F32)], axis=0)
    mod3 = _modulation(cvec, w_mod[0], b_mod[0][None, :]).reshape(MOD_ROWS, 6, D_MODEL)

    head_id = jnp.arange(QK_W) // HEAD_DIM
    heads = (head_id[:, None] == jnp.arange(LANES)[None, :]).astype(BF16)
    gqk = jnp.concatenate([jnp.tile(q_norm[0], N_HEADS), jnp.tile(k_norm[0], N_KV_HEADS)])[None, :]
    wf = (0.5 * jnp.concatenate([_block_diag(lru_wa[0, 0]), _block_diag(lru_wx[0, 0])], axis=1)).astype(BF16)
    wb = (0.5 * jnp.concatenate([_block_diag(lru_wa[0, 1]), _block_diag(lru_wx[0, 1])], axis=1)).astype(BF16)
    bf = 0.5 * jnp.concatenate([lru_ba[0, 0], lru_bx[0, 0]])[None, :]
    bb = 0.5 * jnp.concatenate([lru_ba[0, 1], lru_bx[0, 1]])[None, :]
    pad = LANES - N_EXPERTS - N_GROUPS
    wr = jnp.concatenate([router_exp_w[0], router_grp_w[0], jnp.zeros((D_MODEL, pad), F32)], axis=1)
    wr_hi = wr.astype(BF16)
    wr2 = jnp.concatenate([wr_hi, (wr - wr_hi.astype(F32)).astype(BF16)], axis=1)
    br = jnp.concatenate([router_exp_b[0], router_grp_b[0], jnp.zeros((pad,), F32)])[None, :]
    g1 = norm1[0][None, :]
    g2 = norm2[0][None, :]
    cw = conv_w[0]
    cb = conv_b[0][None, :]
    lam = lru_lambda[0]
    tri = (jnp.arange(TM_PRE)[:, None] > jnp.arange(TM_PRE)[None, :]).astype(BF16)

    def mixers(x, seq_len, mod_row, tables, extra_k, extra_v, h0, cnt_in):
        q, kx, vx, xr, gb, *cache = _pre(x, mod3, mod_row, g1, w_in[0], heads, gqk, tables, seq_len)
        k_segs = [(kx, seq_len)] + extra_k
        v_segs = [(vx, seq_len)] + extra_v
        attn = _attention(q, k_segs, v_segs, seq_len)
        rec, fin = _lru(xr, gb, h0, cw, cb, wf, wb, bf, bb, lam, seq_len)
        x1, h2, route, cnt, route_t = _post(attn, rec, x, mod3, mod_row, g2, w_out[0], wr2, br, tri,
                                            cnt_in)
        return x1, h2, (route, route_t), cnt, cache, fin

    mod_row_p = lambda tok: 0
    mod_row_s = lambda tok: tok // dec_seq + 1
    xp = x_prompt.reshape(batch * seq, D_MODEL)
    x1p, h2p, (route_p, route_t_p), cnt_p, (kf, vf), fin = mixers(
        xp, seq, mod_row_p, None, [], [], jnp.zeros((batch, 2, LRU_W), F32),
        jnp.zeros((1, LANES), F32))
    xs = x_sample.reshape(dec_batch * dec_seq, D_MODEL)
    ck, cv = _expand_cache(cache_k, cache_v)
    x1s, h2s, (route_s, route_t_s), cnt_all, _, _ = mixers(
        xs, dec_seq, mod_row_s, _rope_tables(dec_seq), [(ck, past)], [(cv, past)],
        state_lru[:, 0], cnt_p)

    n_prompt = batch * seq
    n_tok = n_prompt + dec_batch * dec_seq
    n_tiles = (TOP_K * n_tok + N_EXPERTS * (TR_MOE - 1)) // TR_MOE
    cnt = cnt_all[0, :N_EXPERTS].astype(jnp.int32)
    ntile = (cnt + TR_MOE - 1) // TR_MOE
    tile_end = jnp.cumsum(ntile)
    seg_start = (tile_end - ntile) * TR_MOE
    n_used = tile_end[-1:]
    pad_start = seg_start + cnt
    pad_rows = tile_end * TR_MOE - pad_start

    seg_rows = jnp.broadcast_to(seg_start.astype(F32)[:, None], (N_EXPERTS, LANES))
    slots_p = _plan(route_t_p, seg_rows)[:TOP_K].reshape(-1)
    slots_s = _plan(route_t_s, seg_rows)[:TOP_K].reshape(-1)
    xsort = _dispatch(slots_p, slots_s, pad_start, pad_rows, n_used, h2p, h2s,
                      n_tiles * TR_MOE)
    ysort = _ffn(tile_end, xsort, exp_w_gate[0], exp_w_up[0], exp_w_down[0], n_tiles)
    yp = _combine(slots_p, ysort, x1p, route_p, mod3, mod_row_p)
    ys = _combine(slots_s, ysort, x1s, route_s, mod3, mod_row_s)

    return (yp.reshape(batch, seq, D_MODEL),
            ys.reshape(dec_batch, dec_seq, D_MODEL),
            kf.reshape(batch, 1, seq, N_KV_HEADS, HEAD_DIM),
            vf.reshape(batch, 1, seq, N_KV_HEADS, HEAD_DIM),
            fin.reshape(batch, 1, 2, LRU_W))
```

```python
import functools

import jax
import jax.numpy as jnp
from jax import lax
from jax.experimental import pallas as pl
from jax.experimental.pallas import tpu as pltpu

F32 = jnp.float32
BF16 = jnp.bfloat16

D_MODEL = 1024
GRID_W = 64
ATTN_W = 512
LRU_W = 512
HEAD_DIM = 64
N_HEADS = 8
N_KV_HEADS = 2
KV_W = N_KV_HEADS * HEAD_DIM
LRU_BLOCKS = 8
LRU_BLOCK_W = LRU_W // LRU_BLOCKS
CONV_W = 4
LRU_C = 8.0
IN_W = ATTN_W + 2 * KV_W + 2 * LRU_W
QK_W = ATTN_W + KV_W
N_GROUPS = 4
EXPERTS_PER_GROUP = 8
N_EXPERTS = N_GROUPS * EXPERTS_PER_GROUP
TOP_K = 2
EXPERT_FF = D_MODEL // 4
ROPE_THETA = 10000.0
EPS = 1e-6

LANES = 128
SUBLANES = 8
MOD_ROWS = 8
VMEM_LIMIT = 48 * 1024 * 1024

TM_PRE = 512
TQ_ATT = 1024
TC_LRU = 512
TR_MOE = 256
CH_DISPATCH = 1024
FFN_IN_SLOTS = 4
FFN_OUT_SLOTS = 3


def _cparams(sem):
    return pltpu.CompilerParams(dimension_semantics=sem, vmem_limit_bytes=VMEM_LIMIT)


def _dot(a, b):
    return jnp.dot(a, b, preferred_element_type=F32)


def _dot_nt(a, b):
    return lax.dot_general(a, b, (((1,), (1,)), ((), ())), preferred_element_type=F32)


ROW_TILE = (D_MODEL // LANES, LANES)
LOG2_E = 1.4426950408889634
Q_SCALE = HEAD_DIM ** -0.5 * LOG2_E
ONES_LANE_EVEN = HEAD_DIM
ONES_LANE_ODD = 0


def _to_row_tiles(x):
    cols = jnp.stack([x[:, c * LANES:(c + 1) * LANES] for c in range(D_MODEL // LANES)], axis=0)
    return jnp.swapaxes(cols, 0, 1)


def _from_row_tiles(x3):
    cols = jnp.swapaxes(x3, 0, 1)
    return jnp.concatenate([cols[c] for c in range(D_MODEL // LANES)], axis=1)


def _sigmoid(x):
    return 0.5 * jnp.tanh(0.5 * x) + 0.5


def _split_bf16(x):
    hi = x.astype(BF16)
    lo = (x - hi.astype(F32)).astype(BF16)
    return hi, lo


def _mod_kernel(c_ref, w_ref, b_ref, o_ref):
    c = c_ref[...]
    s = (c * jax.nn.sigmoid(c)).astype(BF16)
    o_ref[...] = _dot(s, w_ref[...].astype(BF16)) + b_ref[...]


def _modulation(cvec, w_mod, b_mod):
    n_out = w_mod.shape[1]
    tn = n_out // 4
    return pl.pallas_call(
        _mod_kernel,
        grid=(n_out // tn,),
        in_specs=[pl.BlockSpec((MOD_ROWS, D_MODEL), lambda j: (0, 0)),
                  pl.BlockSpec((D_MODEL, tn), lambda j: (0, j)),
                  pl.BlockSpec((1, tn), lambda j: (0, j))],
        out_specs=pl.BlockSpec((MOD_ROWS, tn), lambda j: (0, j)),
        out_shape=jax.ShapeDtypeStruct((MOD_ROWS, n_out), F32),
        compiler_params=_cparams(("arbitrary",)),
        name="modulation",
    )(cvec, w_mod, b_mod)


def _store_cache(ref, col):
    seq = ref.shape[-1]
    col_t = jnp.transpose(col)
    for j in range(ref.shape[0]):
        ref[j] = col_t[:, j * seq:(j + 1) * seq].reshape(N_KV_HEADS, HEAD_DIM, seq)


def _store_expanded(xref, col, one):
    lane = lax.broadcasted_iota(jnp.int32, col.shape, 1)
    lo_half = lane < HEAD_DIM
    swapped = pltpu.roll(col, HEAD_DIM, 1)
    fill_hi = jnp.where(lane == ONES_LANE_EVEN, one, 0.0)
    fill_lo = jnp.where(lane == ONES_LANE_ODD, one, 0.0)
    xref[:, 0 * LANES:1 * LANES] = jnp.where(lo_half, col, fill_hi).astype(BF16)
    xref[:, 1 * LANES:2 * LANES] = jnp.where(lo_half, fill_lo, swapped).astype(BF16)
    xref[:, 2 * LANES:3 * LANES] = jnp.where(lo_half, swapped, fill_hi).astype(BF16)
    xref[:, 3 * LANES:4 * LANES] = jnp.where(lo_half, fill_lo, col).astype(BF16)


def _pre_kernel(*refs, rope):
    if rope:
        (x_ref, mod_ref, g1_ref, win_ref, heads_ref, heads_t_ref, gqk_ref, cos_ref, sin_ref,
         q_ref, kx_ref, vx_ref, xr_ref, gb_ref, win_s) = refs
    else:
        (x_ref, mod_ref, g1_ref, win_ref, heads_ref, heads_t_ref, gqk_ref,
         q_ref, kx_ref, vx_ref, xr_ref, gb_ref, kf_ref, vf_ref, win_s) = refs

    @pl.when(pl.program_id(0) == 0)
    def _():
        win_s[...] = win_ref[...].astype(BF16)

    x = x_ref[...]
    m = mod_ref[0]
    ms = jnp.mean(x * x, axis=-1, keepdims=True)
    y = x * lax.rsqrt(ms + EPS) * g1_ref[...]
    h = y * (1.0 + m[1:2]) + m[0:1]
    z = _dot(h.astype(BF16), win_s[...])

    qk = z[:, :QK_W]
    ss = _dot((qk * qk).astype(BF16), heads_ref[...])
    hi, lo = _split_bf16(lax.rsqrt(ss * (1.0 / HEAD_DIM) + EPS))
    qk = qk * (_dot(hi, heads_t_ref[...]) + _dot(lo, heads_t_ref[...])) * gqk_ref[...]

    lane = lax.broadcasted_iota(jnp.int32, (x.shape[0], LANES), 1)
    cols = []
    for c in range(QK_W // LANES):
        xc = qk[:, c * LANES:(c + 1) * LANES]
        if rope:
            left = pltpu.roll(xc, LANES - HEAD_DIM // 4, 1)
            right = pltpu.roll(xc, HEAD_DIM // 4, 1)
            rot = jnp.where((lane // (HEAD_DIM // 4)) % 2 == 0, left, right)
            xc = xc * cos_ref[...] + rot * sin_ref[...]
        cols.append(xc)
    for c in range(ATTN_W // LANES):
        q_ref[:, c * LANES:(c + 1) * LANES] = (cols[c] * Q_SCALE).astype(BF16)

    k_col = cols[ATTN_W // LANES]
    v_col = z[:, QK_W:QK_W + KV_W]
    _store_expanded(kx_ref, k_col, 0.0)
    _store_expanded(vx_ref, v_col, 1.0)
    if not rope:
        _store_cache(kf_ref, k_col)
        _store_cache(vf_ref, v_col)

    xr_ref[...] = z[:, QK_W + KV_W:QK_W + KV_W + LRU_W]
    gb_ref[...] = z[:, QK_W + KV_W + LRU_W:]


def _pre(x, mod3, mod_row, g1, w_in, heads, gqk, tables, seq_len):
    n = x.shape[0]
    tm = TM_PRE
    tiles_per_seq = seq_len // tm
    rope = tables is not None
    const = lambda i: (0, 0)
    in_specs = [pl.BlockSpec((tm, D_MODEL), lambda i: (i, 0)),
                pl.BlockSpec((1, 6, D_MODEL), lambda i: (mod_row(i * tm), 0, 0)),
                pl.BlockSpec((1, D_MODEL), const),
                pl.BlockSpec((D_MODEL, IN_W), const),
                pl.BlockSpec((QK_W, LANES), const),
                pl.BlockSpec((LANES, QK_W), const),
                pl.BlockSpec((1, QK_W), const)]
    args = [x, mod3, g1, w_in, heads, heads.T, gqk]
    if rope:
        in_specs += [pl.BlockSpec((tm, LANES), lambda i: (i % tiles_per_seq, 0))] * 2
        args += list(tables)
    row = lambda w: pl.BlockSpec((tm, w), lambda i: (i, 0))
    out_shape = [jax.ShapeDtypeStruct((n, ATTN_W), BF16),
                 jax.ShapeDtypeStruct((n, 4 * LANES), BF16),
                 jax.ShapeDtypeStruct((n, 4 * LANES), BF16),
                 jax.ShapeDtypeStruct((n, LRU_W), F32),
                 jax.ShapeDtypeStruct((n, LRU_W), F32)]
    out_specs = [row(ATTN_W), row(4 * LANES), row(4 * LANES), row(LRU_W), row(LRU_W)]
    if not rope:
        assert tm % seq_len == 0
        per_tile = tm // seq_len
        cache = pl.BlockSpec((per_tile, N_KV_HEADS, HEAD_DIM, seq_len), lambda i: (i, 0, 0, 0))
        out_shape += [jax.ShapeDtypeStruct((n // seq_len, N_KV_HEADS, HEAD_DIM, seq_len), F32)] * 2
        out_specs += [cache, cache]
    return pl.pallas_call(
        functools.partial(_pre_kernel, rope=rope),
        grid=(n // tm,),
        in_specs=in_specs, out_specs=out_specs, out_shape=out_shape,
        scratch_shapes=[pltpu.VMEM((D_MODEL, IN_W), BF16)],
        compiler_params=_cparams(("arbitrary",)),
        name="pre_rope" if rope else "pre",
    )(*args)


def _attn_kernel(*refs, n_seg):
    q_ref = refs[0]
    k_refs = refs[1:1 + n_seg]
    v_refs = refs[1 + n_seg:1 + 2 * n_seg]
    o_ref = refs[1 + 2 * n_seg]
    lane = lax.broadcasted_iota(jnp.int32, (q_ref.shape[0], LANES), 1)
    for c in range(ATTN_W // LANES):
        qc = q_ref[:, c * LANES:(c + 1) * LANES]
        g = c // 2
        accs = []
        for par in range(2):
            sl = slice((2 * g + par) * LANES, (2 * g + par + 1) * LANES)
            ss = [_dot_nt(qc, k[:, sl]) for k in k_refs]
            mx = functools.reduce(jnp.maximum, [jnp.max(s, axis=-1, keepdims=True) for s in ss])
            ps = [jnp.exp2((s - mx).astype(BF16)) for s in ss]
            accs.append(functools.reduce(lambda a, b: a + b,
                                         [_dot(p, v[:, sl]) for p, v in zip(ps, v_refs)]))
        even = accs[0] / accs[0][:, ONES_LANE_EVEN:ONES_LANE_EVEN + 1]
        odd = accs[1] / accs[1][:, ONES_LANE_ODD:ONES_LANE_ODD + 1]
        o_ref[:, c * LANES:(c + 1) * LANES] = jnp.where(lane < HEAD_DIM, even, odd).astype(BF16)


def _attention(q, k_segs, v_segs, seq_len):
    n = q.shape[0]
    tq = min(TQ_ATT, seq_len)
    nq = seq_len // tq
    n_seg = len(k_segs)
    in_specs = [pl.BlockSpec((tq, ATTN_W), lambda b, i: (b * nq + i, 0))]
    for arr, t in list(k_segs) + list(v_segs):
        in_specs.append(pl.BlockSpec((t, 4 * LANES), lambda b, i: (b, 0)))
    return pl.pallas_call(
        functools.partial(_attn_kernel, n_seg=n_seg),
        grid=(n // seq_len, nq),
        in_specs=in_specs,
        out_specs=pl.BlockSpec((tq, ATTN_W), lambda b, i: (b * nq + i, 0)),
        out_shape=jax.ShapeDtypeStruct((n, ATTN_W), BF16),
        compiler_params=_cparams(("arbitrary", "arbitrary")),
        name="attention_%dseg" % n_seg,
    )(q, *[a for a, _ in k_segs], *[a for a, _ in v_segs])


def _log_sigmoid(x):
    return jnp.minimum(x, 0.0) - jnp.log1p(jnp.exp(-jnp.abs(x)))


def _tile_scan(a, b, reverse):
    row = lax.broadcasted_iota(jnp.int32, a.shape, 0)
    d = 1
    while d < SUBLANES:
        if reverse:
            keep = row < SUBLANES - d
            shift = SUBLANES - d
        else:
            keep = row >= d
            shift = d
        a_sh = jnp.where(keep, pltpu.roll(a, shift, 0), 1.0)
        b_sh = jnp.where(keep, pltpu.roll(b, shift, 0), 0.0)
        b = a * b_sh + b
        a = a * a_sh
        d *= 2
    return a, b


def _lru_kernel(xr_ref, gb_ref, h0_ref, cw_ref, cb_ref, wa_ref, wx_ref, ba_ref, bx_ref, lam_ref,
                rec_ref, fin_ref, xpad_s, xc_s, hf_s, a_s, b_s, wf_ref, wb_ref, bf_ref, bb_ref,
                *, seq_len):
    @pl.when(pl.program_id(0) == 0)
    def _():
        pairs = LANES // LRU_BLOCK_W
        zero = jnp.zeros((LRU_BLOCK_W, LRU_BLOCK_W), F32)
        for d, (w_s, bias_s) in enumerate(((wf_ref, bf_ref), (wb_ref, bb_ref))):
            w_s[...] = jnp.zeros_like(w_s)
            for g, src in enumerate((wa_ref, wx_ref)):
                for p in range(LRU_BLOCKS // pairs):
                    rows = [jnp.concatenate([0.5 * src[d, pairs * p + q] if q == r else zero
                                             for q in range(pairs)], axis=1) for r in range(pairs)]
                    w_s[LANES * p:LANES * (p + 1), g * LRU_W + LANES * p:g * LRU_W + LANES * (p + 1)] = (
                        jnp.concatenate(rows, axis=0).astype(BF16))
            bias_s[:, :LRU_W] = 0.5 * ba_ref[d:d + 1, :]
            bias_s[:, LRU_W:] = 0.5 * bx_ref[d:d + 1, :]

    tc = min(TC_LRU, seq_len)
    n_chunks = seq_len // tc
    n_tiles = tc // SUBLANES
    zpad = jnp.zeros((SUBLANES, LRU_W), F32)
    xpad_s[0:SUBLANES, :] = zpad
    xpad_s[SUBLANES:SUBLANES + seq_len, :] = xr_ref[...]
    xpad_s[SUBLANES + seq_len:2 * SUBLANES + seq_len, :] = zpad

    half_cl = (0.5 * LRU_C * LOG2_E) * _log_sigmoid(lam_ref[...])

    def gates(xcc, w_ref, bias_ref, half_cl_d):
        t = jnp.tanh(_dot(xcc.astype(BF16), w_ref[...]) + bias_ref[...])
        a = jnp.exp2(t[:, :LRU_W] * half_cl_d + half_cl_d)
        half_x = 0.5 * xcc
        a_s[...] = a
        u = 1.0 - a * a
        b_s[...] = jnp.where(u > 0.0, u * lax.rsqrt(u), 0.0) * (t[:, LRU_W:] * half_x + half_x)

    h = h0_ref[0, 0:1, :]
    for c in range(n_chunks):
        base = c * tc
        xcc = cb_ref[...] + functools.reduce(
            lambda u, v: u + v,
            [cw_ref[j:j + 1, :] * xpad_s[base + SUBLANES - 1 + j:base + SUBLANES - 1 + j + tc, :]
             for j in range(CONV_W)])
        xc_s[base:base + tc, :] = xcc
        gates(xcc, wf_ref, bf_ref, half_cl[0:1])

        def fwd_tile(t, hc, base=base):
            r0 = pl.multiple_of(t * SUBLANES, SUBLANES)
            ca, cb = _tile_scan(a_s[pl.ds(r0, SUBLANES), :], b_s[pl.ds(r0, SUBLANES), :], False)
            hh = ca * hc + cb
            hf_s[pl.ds(base + r0, SUBLANES), :] = hh
            return hh[SUBLANES - 1:SUBLANES, :]

        h = lax.fori_loop(0, n_tiles, fwd_tile, h, unroll=4)
    fin_ref[0, 0:1, :] = h

    h = h0_ref[0, 1:2, :]
    for c in reversed(range(n_chunks)):
        base = c * tc
        gates(xc_s[base:base + tc, :], wb_ref, bb_ref, half_cl[1:2])

        def bwd_tile(t, hc, base=base):
            r0 = pl.multiple_of((n_tiles - 1 - t) * SUBLANES, SUBLANES)
            ca, cb = _tile_scan(a_s[pl.ds(r0, SUBLANES), :], b_s[pl.ds(r0, SUBLANES), :], True)
            hh = ca * hc + cb
            gate = jax.nn.gelu(gb_ref[pl.ds(base + r0, SUBLANES), :], approximate=True)
            rec_ref[pl.ds(base + r0, SUBLANES), :] = (
                (hf_s[pl.ds(base + r0, SUBLANES), :] + hh) * gate).astype(rec_ref.dtype)
            return hh[0:1, :]

        h = lax.fori_loop(0, n_tiles, bwd_tile, h, unroll=4)
    fin_ref[0, 1:2, :] = h


def _lru(xr, gb, h0, conv_w, conv_b, wa, wx, ba, bx, lam, seq_len):
    n = xr.shape[0]
    batch = n // seq_len
    const = lambda b: (0, 0)
    blocks = pl.BlockSpec((2, LRU_BLOCKS, LRU_BLOCK_W, LRU_BLOCK_W), lambda b: (0, 0, 0, 0))
    seq = pl.BlockSpec((seq_len, LRU_W), lambda b: (b, 0))
    st = pl.BlockSpec((1, 2, LRU_W), lambda b: (b, 0, 0))
    return pl.pallas_call(
        functools.partial(_lru_kernel, seq_len=seq_len),
        grid=(batch,),
        in_specs=[seq, seq, st,
                  pl.BlockSpec((CONV_W, LRU_W), const), pl.BlockSpec((1, LRU_W), const),
                  blocks, blocks,
                  pl.BlockSpec((2, LRU_W), const), pl.BlockSpec((2, LRU_W), const),
                  pl.BlockSpec((2, LRU_W), const)],
        out_specs=[seq, st],
        out_shape=[jax.ShapeDtypeStruct((n, LRU_W), BF16),
                   jax.ShapeDtypeStruct((batch, 2, LRU_W), F32)],
        scratch_shapes=[pltpu.VMEM((seq_len + 2 * SUBLANES, LRU_W), F32),
                        pltpu.VMEM((seq_len, LRU_W), F32),
                        pltpu.VMEM((seq_len, LRU_W), F32),
                        pltpu.VMEM((min(TC_LRU, seq_len), LRU_W), F32),
                        pltpu.VMEM((min(TC_LRU, seq_len), LRU_W), F32),
                        pltpu.VMEM((LRU_W, 2 * LRU_W), BF16),
                        pltpu.VMEM((LRU_W, 2 * LRU_W), BF16),
                        pltpu.VMEM((1, 2 * LRU_W), F32),
                        pltpu.VMEM((1, 2 * LRU_W), F32)],
        compiler_params=_cparams(("arbitrary",)),
        name="lru_%d" % seq_len,
    )(xr, gb, h0, conv_w, conv_b, wa, wx, ba, bx, lam)


def _post_kernel(attn_ref, rec_ref, x_ref, mod_ref, g2_ref, wo_ref, wr2_ref, br_ref,
                 tri_ref, cnt_in_ref, x1_ref, h2_ref, route_ref, cnt_ref, route_t_ref, cnt_s, wo_s):
    @pl.when(pl.program_id(0) == 0)
    def _():
        cnt_s[...] = cnt_in_ref[...]
        wo_s[...] = wo_ref[...].astype(BF16)

    m = mod_ref[0]
    u = _dot(attn_ref[...], wo_s[:ATTN_W, :]) + _dot(rec_ref[...], wo_s[ATTN_W:, :])
    x1 = x_ref[...] + m[2:3] * u
    x1_ref[...] = x1
    ms = jnp.mean(x1 * x1, axis=-1, keepdims=True)
    h2 = x1 * lax.rsqrt(ms + EPS) * g2_ref[...]
    h2 = h2 * (1.0 + m[4:5]) + m[3:4]
    hi, lo = _split_bf16(h2)
    h2_ref[...] = _to_row_tiles(h2)

    hw = _dot(hi, wr2_ref[...])
    logits = hw[:, :LANES] + hw[:, LANES:] + _dot(lo, wr2_ref[:, :LANES]) + br_ref[...]
    lane_i = lax.broadcasted_iota(jnp.int32, logits.shape, 1)
    lane = lane_i.astype(F32)
    lane_group = (lane_i // EXPERTS_PER_GROUP).astype(F32)
    neg = -jnp.inf
    big = float(1 << 20)
    gmask = (lane_i >= N_EXPERTS) & (lane_i < N_EXPERTS + N_GROUPS)
    gl = jnp.where(gmask, logits, neg)
    gmax = jnp.max(gl, axis=-1, keepdims=True)
    gidx = jnp.min(jnp.where(gl == gmax, lane - N_EXPERTS, big), axis=-1, keepdims=True)
    p_sel = 1.0 / jnp.sum(jnp.where(gmask, jnp.exp(gl - gmax), 0.0), axis=-1, keepdims=True)

    emask = (lane_i < N_EXPERTS) & (lane_group == gidx)
    el = jnp.where(emask, logits, neg)
    v1 = jnp.max(el, axis=-1, keepdims=True)
    i1 = jnp.min(jnp.where(el == v1, lane, big), axis=-1, keepdims=True)
    el2 = jnp.where(lane == i1, neg, el)
    v2 = jnp.max(el2, axis=-1, keepdims=True)
    i2 = jnp.min(jnp.where(el2 == v2, lane, big), axis=-1, keepdims=True)
    e2 = jnp.exp(v2 - v1)
    w1 = p_sel / (1.0 + e2)
    w2 = p_sel * e2 / (1.0 + e2)

    oh1 = lane == i1
    oh2 = lane == i2
    oh = jnp.where(oh1, 1.0, 0.0) + jnp.where(oh2, 1.0, 0.0)
    before = _dot(tri_ref[...], oh.astype(BF16)) + cnt_s[...]
    rank1 = jnp.sum(jnp.where(oh1, before, 0.0), axis=-1, keepdims=True)
    rank2 = jnp.sum(jnp.where(oh2, before, 0.0), axis=-1, keepdims=True)
    cnt = cnt_s[...] + jnp.sum(oh, axis=0, keepdims=True)
    cnt_s[...] = cnt
    cnt_ref[...] = cnt
    fields = (i1, i2, rank1, rank2, w1, w2)
    route = jnp.zeros(logits.shape, F32)
    for k, val in enumerate(fields):
        route = jnp.where(lane_i == k, val, route)
    route_ref[...] = route
    route_t_ref[...] = jnp.transpose(route)[:SUBLANES, :]


def _post(attn, rec, x, mod3, mod_row, g2, w_out, wr2, br, tri, cnt_in):
    n = x.shape[0]
    tm = TM_PRE
    const = lambda i: (0, 0)
    row = lambda w: pl.BlockSpec((tm, w), lambda i: (i, 0))
    return pl.pallas_call(
        _post_kernel,
        grid=(n // tm,),
        in_specs=[row(ATTN_W), row(LRU_W), row(D_MODEL),
                  pl.BlockSpec((1, 6, D_MODEL), lambda i: (mod_row(i * tm), 0, 0)),
                  pl.BlockSpec((1, D_MODEL), const),
                  pl.BlockSpec((D_MODEL, D_MODEL), const),
                  pl.BlockSpec((D_MODEL, 2 * LANES), const),
                  pl.BlockSpec((1, LANES), const),
                  pl.BlockSpec((tm, tm), const),
                  pl.BlockSpec((1, LANES), const)],
        out_specs=[row(D_MODEL), pl.BlockSpec((tm,) + ROW_TILE, lambda i: (i, 0, 0)), row(LANES),
                   pl.BlockSpec((1, LANES), const),
                   pl.BlockSpec((SUBLANES, tm), lambda i: (0, i))],
        out_shape=[jax.ShapeDtypeStruct((n, D_MODEL), F32),
                   jax.ShapeDtypeStruct((n,) + ROW_TILE, F32),
                   jax.ShapeDtypeStruct((n, LANES), F32),
                   jax.ShapeDtypeStruct((1, LANES), F32),
                   jax.ShapeDtypeStruct((SUBLANES, n), F32)],
        scratch_shapes=[pltpu.VMEM((1, LANES), F32), pltpu.VMEM((D_MODEL, D_MODEL), BF16)],
        compiler_params=_cparams(("arbitrary",)),
        name="post",
    )(attn, rec, x, mod3, g2, w_out, wr2, br, tri, cnt_in)


def _row_copy(src_ref, src_row, dst_ref, dst_row, sem):
    return pltpu.make_async_copy(src_ref.at[pl.ds(src_row, 1)], dst_ref.at[pl.ds(dst_row, 1)], sem)


def _plan_kernel(route_t_ref, seg_ref, slots_ref):
    route_t = route_t_ref[...]
    t = route_t.shape[1]
    seg = jnp.concatenate([seg_ref[...]] * (t // LANES), axis=1)
    expert = lax.broadcasted_iota(jnp.int32, seg.shape, 0).astype(F32)
    rows = []
    for k in range(TOP_K):
        start = jnp.sum(jnp.where(expert == route_t[k:k + 1, :], seg, 0.0), axis=0, keepdims=True)
        rows.append(start + route_t[TOP_K + k:TOP_K + k + 1, :])
    rows.append(jnp.zeros((SUBLANES - TOP_K, t), F32))
    slots_ref[...] = jnp.concatenate(rows, axis=0).astype(jnp.int32)


def _plan(route_t, seg_rows):
    n = route_t.shape[1]
    tm = TM_PRE
    return pl.pallas_call(
        _plan_kernel,
        grid=(n // tm,),
        in_specs=[pl.BlockSpec((SUBLANES, tm), lambda i: (0, i)),
                  pl.BlockSpec((N_EXPERTS, LANES), lambda i: (0, 0))],
        out_specs=pl.BlockSpec((SUBLANES, tm), lambda i: (0, i)),
        out_shape=jax.ShapeDtypeStruct((SUBLANES, n), jnp.int32),
        compiler_params=_cparams(("arbitrary",)),
        name="plan",
    )(route_t, seg_rows)


def _dispatch_kernel(slots_p_ref, slots_s_ref, pad_start_ref, pad_rows_ref, nu_ref, h2p_ref, h2s_ref,
                     xs_ref, zero_s, sem, zsem, *, n_prompt):
    i = pl.program_id(0)
    ch = CH_DISPATCH

    def zero_fill(op):
        for e in range(N_EXPERTS):
            rows = pad_rows_ref[e]
            for b in range(TR_MOE.bit_length() - 1):
                size = 1 << b
                off = (rows >> (b + 1)) << (b + 1)

                @pl.when(((rows >> b) & 1) == 1)
                def _():
                    op(pltpu.make_async_copy(zero_s.at[pl.ds(0, size)],
                                             xs_ref.at[pl.ds(pad_start_ref[e] + off, size)], zsem))

        def zero_tile(t, carry):
            op(pltpu.make_async_copy(
                zero_s, xs_ref.at[pl.ds(pl.multiple_of(t * TR_MOE, TR_MOE), TR_MOE)], zsem))
            return carry
        lax.fori_loop(nu_ref[0], xs_ref.shape[0] // TR_MOE, zero_tile, 0)

    @pl.when(i == 0)
    def _():
        zero_s[...] = jnp.zeros_like(zero_s)
        zero_fill(lambda cp: cp.start())

    @pl.when(i == pl.num_programs(0) - 1)
    def _():
        zero_fill(lambda cp: cp.wait())

    def scatter(src_ref, slots_ref, first_token):
        n_path = slots_ref.shape[0] // TOP_K

        def body(j, carry):
            for k in range(TOP_K):
                _row_copy(src_ref, j, xs_ref, slots_ref[k * n_path + i * ch + j - first_token],
                          sem).start(priority=k)
            return carry
        lax.fori_loop(0, ch, body, 0, unroll=8)
        for _ in range(TOP_K):
            pltpu.make_async_copy(src_ref, xs_ref.at[pl.ds(0, ch)], sem).wait()

    @pl.when(i < n_prompt // ch)
    def _():
        scatter(h2p_ref, slots_p_ref, 0)

    @pl.when(i >= n_prompt // ch)
    def _():
        scatter(h2s_ref, slots_s_ref, n_prompt)


def _dispatch(slots_p, slots_s, pad_start, pad_rows, n_used, h2p, h2s, n_rows):
    n_prompt = h2p.shape[0]
    n = n_prompt + h2s.shape[0]
    ch = CH_DISPATCH
    npc = n_prompt // ch
    last_p = npc - 1
    return pl.pallas_call(
        functools.partial(_dispatch_kernel, n_prompt=n_prompt),
        grid_spec=pltpu.PrefetchScalarGridSpec(
            num_scalar_prefetch=5,
            grid=(n // ch,),
            in_specs=[pl.BlockSpec((ch,) + ROW_TILE, lambda i, *_: (jnp.minimum(i, last_p), 0, 0)),
                      pl.BlockSpec((ch,) + ROW_TILE, lambda i, *_: (jnp.maximum(i - npc, 0), 0, 0))],
            out_specs=pl.BlockSpec(memory_space=pl.ANY),
            scratch_shapes=[pltpu.VMEM((TR_MOE,) + ROW_TILE, F32), pltpu.SemaphoreType.DMA,
                            pltpu.SemaphoreType.DMA]),
        out_shape=jax.ShapeDtypeStruct((n_rows,) + ROW_TILE, F32),
        compiler_params=_cparams(("arbitrary",)),
        name="dispatch",
    )(slots_p, slots_s, pad_start, pad_rows, n_used, h2p, h2s)


def _ffn_kernel(tend_ref, xs_ref, wg_ref, wu_ref, wd_ref, ys_ref,
                xbuf, ybuf, wg_s, wu_s, wd_s, sem_in, sem_out):
    e = pl.program_id(0)
    n_used = tend_ref[N_EXPERTS - 1]
    t_first = jnp.where(e == 0, 0, tend_ref[jnp.maximum(e - 1, 0)])
    t_last = tend_ref[e]
    n_in = xbuf.shape[0]
    n_out = ybuf.shape[0]

    def tile_rows(t):
        return pl.ds(pl.multiple_of(t * TR_MOE, TR_MOE), TR_MOE)

    def fetch(t):
        return pltpu.make_async_copy(xs_ref.at[tile_rows(t)], xbuf.at[t % n_in], sem_in.at[t % n_in])

    def writeback(t):
        return pltpu.make_async_copy(ybuf.at[t % n_out], ys_ref.at[tile_rows(t)],
                                     sem_out.at[t % n_out])

    @pl.when(e == 0)
    def _():
        for t in range(n_in - 1):
            @pl.when(t < n_used)
            def _():
                fetch(t).start()

    @pl.when(t_last > t_first)
    def _():
        wg_s[...] = wg_ref[0].astype(BF16)
        wu_s[...] = wu_ref[0].astype(BF16)
        wd_s[...] = wd_ref[0].astype(BF16)

    def tile(t, carry):
        fetch(t).wait()

        @pl.when(t + n_in - 1 < n_used)
        def _():
            fetch(t + n_in - 1).start()

        @pl.when(t >= n_out)
        def _():
            writeback(t - n_out).wait()

        x = _from_row_tiles(xbuf[t % n_in]).astype(BF16)
        hg = _dot(x, wg_s[...])
        hu = _dot(x, wu_s[...])
        act = (hg * _sigmoid(hg)) * hu
        ybuf[t % n_out] = _to_row_tiles(_dot(act.astype(BF16), wd_s[...]))
        writeback(t).start()
        return carry

    lax.fori_loop(t_first, t_last, tile, 0)

    @pl.when(e == N_EXPERTS - 1)
    def _():
        for back in range(n_out, 0, -1):
            @pl.when(n_used >= back)
            def _():
                writeback(n_used - back).wait()
        n_all = ys_ref.shape[0] // TR_MOE
        ybuf[0] = jnp.zeros(ybuf.shape[1:], F32)

        def zero_tile(t):
            return pltpu.make_async_copy(ybuf.at[0], ys_ref.at[tile_rows(t)], sem_out.at[0])

        lax.fori_loop(n_used, n_all, lambda t, c: (zero_tile(t).start(), c)[1], 0)
        lax.fori_loop(n_used, n_all, lambda t, c: (zero_tile(t).wait(), c)[1], 0)


def _ffn(tile_end, xs, wg, wu, wd, n_tiles):
    tr = TR_MOE
    wsel = lambda e, tend: (e, 0, 0)
    return pl.pallas_call(
        _ffn_kernel,
        grid_spec=pltpu.PrefetchScalarGridSpec(
            num_scalar_prefetch=1,
            grid=(N_EXPERTS,),
            in_specs=[pl.BlockSpec(memory_space=pl.ANY),
                      pl.BlockSpec((1, D_MODEL, EXPERT_FF), wsel),
                      pl.BlockSpec((1, D_MODEL, EXPERT_FF), wsel),
                      pl.BlockSpec((1, EXPERT_FF, D_MODEL), wsel)],
            out_specs=pl.BlockSpec(memory_space=pl.ANY),
            scratch_shapes=[pltpu.VMEM((FFN_IN_SLOTS, tr) + ROW_TILE, F32),
                            pltpu.VMEM((FFN_OUT_SLOTS, tr) + ROW_TILE, F32),
                            pltpu.VMEM((D_MODEL, EXPERT_FF), BF16),
                            pltpu.VMEM((D_MODEL, EXPERT_FF), BF16),
                            pltpu.VMEM((EXPERT_FF, D_MODEL), BF16),
                            pltpu.SemaphoreType.DMA((FFN_IN_SLOTS,)),
                            pltpu.SemaphoreType.DMA((FFN_OUT_SLOTS,))]),
        out_shape=jax.ShapeDtypeStruct((n_tiles * tr,) + ROW_TILE, F32),
        compiler_params=_cparams(("arbitrary",)),
        name="ffn",
    )(tile_end, xs, wg, wu, wd)


def _combine_kernel(slots_ref, ys_ref, x1_ref, route_ref, mod_ref, y_ref, b1_s, b2_s, sems):
    i = pl.program_id(0)
    tm = x1_ref.shape[0]
    n_path = slots_ref.shape[0] // TOP_K

    def gather(step, slot):
        def body(j, carry):
            _row_copy(ys_ref, slots_ref[step * tm + j], b1_s.at[slot], j, sems.at[slot]).start(
                priority=0)
            _row_copy(ys_ref, slots_ref[n_path + step * tm + j], b2_s.at[slot], j,
                      sems.at[slot]).start(priority=1)
            return carry
        lax.fori_loop(0, tm, body, 0, unroll=8)

    @pl.when(i == 0)
    def _():
        gather(0, 0)

    @pl.when(i + 1 < pl.num_programs(0))
    def _():
        gather(i + 1, (i + 1) % 2)

    slot = i % 2
    for buf in (b1_s, b2_s):
        pltpu.make_async_copy(ys_ref.at[pl.ds(0, tm)], buf.at[slot], sems.at[slot]).wait()
    route = route_ref[...]
    moe = route[:, 4:5] * _from_row_tiles(b1_s[slot]) + route[:, 5:6] * _from_row_tiles(b2_s[slot])
    y_ref[...] = x1_ref[...] + mod_ref[0][5:6] * moe


def _combine(slots, ys, x1, route, mod3, mod_row):
    n = x1.shape[0]
    tm = TM_PRE
    row = lambda w: pl.BlockSpec((tm, w), lambda i, *_: (i, 0))
    return pl.pallas_call(
        _combine_kernel,
        grid_spec=pltpu.PrefetchScalarGridSpec(
            num_scalar_prefetch=1,
            grid=(n // tm,),
            in_specs=[pl.BlockSpec(memory_space=pl.ANY), row(D_MODEL), row(LANES),
                      pl.BlockSpec((1, 6, D_MODEL),
                                   lambda i, *_: (mod_row(i * tm), 0, 0))],
            out_specs=row(D_MODEL),
            scratch_shapes=[pltpu.VMEM((2, tm) + ROW_TILE, F32), pltpu.VMEM((2, tm) + ROW_TILE, F32),
                            pltpu.SemaphoreType.DMA((2,))]),
        out_shape=jax.ShapeDtypeStruct((n, D_MODEL), F32),
        compiler_params=_cparams(("arbitrary",)),
        name="combine",
    )(slots, ys, x1, route, mod3)


def _rope_tables(length):
    rows = length // GRID_W
    r, col = jnp.meshgrid(jnp.arange(rows), jnp.arange(GRID_W), indexing='ij')
    r = r.reshape(-1).astype(F32)
    col = col.reshape(-1).astype(F32)
    half = HEAD_DIM // 2
    inv = ROPE_THETA ** (-jnp.arange(0, half, 2, dtype=F32) / half)
    ang_r = r[:, None] * inv
    ang_c = col[:, None] * inv
    ang = jnp.concatenate([ang_r, ang_r, ang_c, ang_c], axis=-1)
    sign = jnp.where((jnp.arange(HEAD_DIM) // (HEAD_DIM // 4)) % 2 == 0, -1.0, 1.0).astype(F32)
    cos = jnp.tile(jnp.cos(ang), (1, LANES // HEAD_DIM))
    sin = jnp.tile(jnp.sin(ang) * sign, (1, LANES // HEAD_DIM))
    return cos, sin


def _expand_cache_kernel(k_ref, v_ref, kx_ref, vx_ref):
    for ref, xref, one in ((k_ref, kx_ref, 0.0), (v_ref, vx_ref, 1.0)):
        col = jnp.transpose(ref[0].reshape(KV_W, ref.shape[-1]))
        _store_expanded(xref, col, one)


def _expand_cache(cache_k, cache_v):
    b, _, _, t = cache_k.shape
    src = pl.BlockSpec((1, N_KV_HEADS, HEAD_DIM, t), lambda i: (i, 0, 0, 0))
    dst = pl.BlockSpec((t, 4 * LANES), lambda i: (i, 0))
    return pl.pallas_call(
        _expand_cache_kernel,
        grid=(b,),
        in_specs=[src, src], out_specs=[dst, dst],
        out_shape=[jax.ShapeDtypeStruct((b * t, 4 * LANES), BF16)] * 2,
        compiler_params=_cparams(("arbitrary",)),
        name="expand_cache",
    )(cache_k, cache_v)


def kernel(x_prompt, x_sample, cache_k, cache_v, state_lru, c, c_ctx, w_mod, b_mod, norm1, norm2,
           w_in, q_norm, k_norm, conv_w, conv_b, lru_wa, lru_ba, lru_wx, lru_bx, lru_lambda, w_out,
           router_grp_w, router_grp_b, router_exp_w, router_exp_b, exp_w_gate, exp_w_up, exp_w_down):
    batch, seq, _ = x_prompt.shape
    dec_batch, dec_seq, _ = x_sample.shape
    past = cache_k.shape[2]
    depth = w_mod.shape[0]
    assert depth == 1

    cvec = jnp.concatenate(
        [c_ctx[None, :], c, jnp.zeros((MOD_ROWS - 1 - dec_batch, D_MODEL), F32)], axis=0)
    mod3 = _modulation(cvec, w_mod[0], b_mod[0][None, :]).reshape(MOD_ROWS, 6, D_MODEL)

    head_id = jnp.arange(QK_W) // HEAD_DIM
    heads = (head_id[:, None] == jnp.arange(LANES)[None, :]).astype(BF16)
    gqk = jnp.concatenate([jnp.tile(q_norm[0], N_HEADS), jnp.tile(k_norm[0], N_KV_HEADS)])[None, :]
    pad = LANES - N_EXPERTS - N_GROUPS
    wr = jnp.concatenate([router_exp_w[0], router_grp_w[0], jnp.zeros((D_MODEL, pad), F32)], axis=1)
    wr_hi = wr.astype(BF16)
    wr2 = jnp.concatenate([wr_hi, (wr - wr_hi.astype(F32)).astype(BF16)], axis=1)
    br = jnp.concatenate([router_exp_b[0], router_grp_b[0], jnp.zeros((pad,), F32)])[None, :]
    g1 = norm1[0][None, :]
    g2 = norm2[0][None, :]
    cw = conv_w[0]
    cb = conv_b[0][None, :]
    lam = lru_lambda[0]
    tri = (jnp.arange(TM_PRE)[:, None] > jnp.arange(TM_PRE)[None, :]).astype(BF16)

    def mixers(x, seq_len, mod_row, tables, extra_k, extra_v, h0, cnt_in):
        q, kx, vx, xr, gb, *cache = _pre(x, mod3, mod_row, g1, w_in[0], heads, gqk, tables, seq_len)
        k_segs = [(kx, seq_len)] + extra_k
        v_segs = [(vx, seq_len)] + extra_v
        attn = _attention(q, k_segs, v_segs, seq_len)
        rec, fin = _lru(xr, gb, h0, cw, cb, lru_wa[0], lru_wx[0], lru_ba[0], lru_bx[0], lam, seq_len)
        x1, h2, route, cnt, route_t = _post(attn, rec, x, mod3, mod_row, g2, w_out[0], wr2, br, tri,
                                            cnt_in)
        return x1, h2, (route, route_t), cnt, cache, fin

    mod_row_p = lambda tok: 0
    mod_row_s = lambda tok: tok // dec_seq + 1
    xp = x_prompt.reshape(batch * seq, D_MODEL)
    x1p, h2p, (route_p, route_t_p), cnt_p, (kf, vf), fin = mixers(
        xp, seq, mod_row_p, None, [], [], jnp.zeros((batch, 2, LRU_W), F32),
        jnp.zeros((1, LANES), F32))
    xs = x_sample.reshape(dec_batch * dec_seq, D_MODEL)
    ck, cv = _expand_cache(jnp.transpose(cache_k[:, 0], (0, 2, 3, 1)),
                           jnp.transpose(cache_v[:, 0], (0, 2, 3, 1)))
    x1s, h2s, (route_s, route_t_s), cnt_all, _, _ = mixers(
        xs, dec_seq, mod_row_s, _rope_tables(dec_seq), [(ck, past)], [(cv, past)],
        state_lru[:, 0], cnt_p)

    n_prompt = batch * seq
    n_tok = n_prompt + dec_batch * dec_seq
    n_tiles = (TOP_K * n_tok + N_EXPERTS * (TR_MOE - 1)) // TR_MOE
    cnt = cnt_all[0, :N_EXPERTS].astype(jnp.int32)
    ntile = (cnt + TR_MOE - 1) // TR_MOE
    tile_end = jnp.cumsum(ntile)
    seg_start = (tile_end - ntile) * TR_MOE
    n_used = tile_end[-1:]
    pad_start = seg_start + cnt
    pad_rows = tile_end * TR_MOE - pad_start

    seg_rows = jnp.broadcast_to(seg_start.astype(F32)[:, None], (N_EXPERTS, LANES))
    slots_p = _plan(route_t_p, seg_rows)[:TOP_K].reshape(-1)
    slots_s = _plan(route_t_s, seg_rows)[:TOP_K].reshape(-1)
    xsort = _dispatch(slots_p, slots_s, pad_start, pad_rows, n_used, h2p, h2s,
                      n_tiles * TR_MOE)
    ysort = _ffn(tile_end, xsort, exp_w_gate[0], exp_w_up[0], exp_w_down[0], n_tiles)
    yp = _combine(slots_p, ysort, x1p, route_p, mod3, mod_row_p)
    ys = _combine(slots_s, ysort, x1s, route_s, mod3, mod_row_s)

    return (yp.reshape(batch, seq, D_MODEL),
            ys.reshape(dec_batch, dec_seq, D_MODEL),
            jnp.transpose(kf, (0, 3, 1, 2))[:, None],
            jnp.transpose(vf, (0, 3, 1, 2))[:, None],
            fin.reshape(batch, 1, 2, LRU_W))
```

```python
import functools

import jax
import jax.numpy as jnp
from jax import lax
from jax.experimental import pallas as pl
from jax.experimental.pallas import tpu as pltpu

F32 = jnp.float32
BF16 = jnp.bfloat16

D_MODEL = 1024
GRID_W = 64
ATTN_W = 512
LRU_W = 512
HEAD_DIM = 64
N_HEADS = 8
N_KV_HEADS = 2
KV_W = N_KV_HEADS * HEAD_DIM
LRU_BLOCKS = 8
LRU_BLOCK_W = LRU_W // LRU_BLOCKS
CONV_W = 4
LRU_C = 8.0
IN_W = ATTN_W + 2 * KV_W + 2 * LRU_W
QK_W = ATTN_W + KV_W
N_GROUPS = 4
EXPERTS_PER_GROUP = 8
N_EXPERTS = N_GROUPS * EXPERTS_PER_GROUP
TOP_K = 2
EXPERT_FF = D_MODEL // 4
ROPE_THETA = 10000.0
EPS = 1e-6

LANES = 128
SUBLANES = 8
MOD_ROWS = 8
VMEM_LIMIT = 48 * 1024 * 1024

TM_PRE = 512
TQ_ATT = 1024
TC_LRU = 512
TR_MOE = 256
CH_DISPATCH = 1024
COMBINE_CHUNK = 32
FFN_IN_SLOTS = 4
FFN_OUT_SLOTS = 3


def _cparams(sem):
    return pltpu.CompilerParams(dimension_semantics=sem, vmem_limit_bytes=VMEM_LIMIT)


def _dot(a, b):
    return jnp.dot(a, b, preferred_element_type=F32)


def _dot_nt(a, b):
    return lax.dot_general(a, b, (((1,), (1,)), ((), ())), preferred_element_type=F32)


ROW_TILE = (D_MODEL // LANES, LANES)
LOG2_E = 1.4426950408889634
Q_SCALE = HEAD_DIM ** -0.5 * LOG2_E
ONES_LANE_EVEN = HEAD_DIM
ONES_LANE_ODD = 0


def _to_row_tiles(x):
    cols = jnp.stack([x[:, c * LANES:(c + 1) * LANES] for c in range(D_MODEL // LANES)], axis=0)
    return jnp.swapaxes(cols, 0, 1)


def _from_row_tiles(x3):
    cols = jnp.swapaxes(x3, 0, 1)
    return jnp.concatenate([cols[c] for c in range(D_MODEL // LANES)], axis=1)


def _sigmoid(x):
    return 0.5 * jnp.tanh(0.5 * x) + 0.5


def _split_bf16(x):
    hi = x.astype(BF16)
    lo = (x - hi.astype(F32)).astype(BF16)
    return hi, lo


def _mod_kernel(c_ref, w_ref, b_ref, o_ref):
    c = c_ref[...]
    s = (c * jax.nn.sigmoid(c)).astype(BF16)
    o_ref[...] = _dot(s, w_ref[...].astype(BF16)) + b_ref[...]


def _modulation(cvec, w_mod, b_mod):
    n_out = w_mod.shape[1]
    tn = n_out // 4
    return pl.pallas_call(
        _mod_kernel,
        grid=(n_out // tn,),
        in_specs=[pl.BlockSpec((MOD_ROWS, D_MODEL), lambda j: (0, 0)),
                  pl.BlockSpec((D_MODEL, tn), lambda j: (0, j)),
                  pl.BlockSpec((1, tn), lambda j: (0, j))],
        out_specs=pl.BlockSpec((MOD_ROWS, tn), lambda j: (0, j)),
        out_shape=jax.ShapeDtypeStruct((MOD_ROWS, n_out), F32),
        compiler_params=_cparams(("arbitrary",)),
        name="modulation",
    )(cvec, w_mod, b_mod)


def _store_cache(ref, col):
    seq = ref.shape[-1]
    col_t = jnp.transpose(col)
    for j in range(ref.shape[0]):
        ref[j] = col_t[:, j * seq:(j + 1) * seq].reshape(N_KV_HEADS, HEAD_DIM, seq)


def _store_expanded(xref, col, one):
    lane = lax.broadcasted_iota(jnp.int32, col.shape, 1)
    lo_half = lane < HEAD_DIM
    swapped = pltpu.roll(col, HEAD_DIM, 1)
    fill_hi = jnp.where(lane == ONES_LANE_EVEN, one, 0.0)
    fill_lo = jnp.where(lane == ONES_LANE_ODD, one, 0.0)
    xref[:, 0 * LANES:1 * LANES] = jnp.where(lo_half, col, fill_hi).astype(BF16)
    xref[:, 1 * LANES:2 * LANES] = jnp.where(lo_half, fill_lo, swapped).astype(BF16)
    xref[:, 2 * LANES:3 * LANES] = jnp.where(lo_half, swapped, fill_hi).astype(BF16)
    xref[:, 3 * LANES:4 * LANES] = jnp.where(lo_half, fill_lo, col).astype(BF16)


def _pre_kernel(*refs, rope):
    if rope:
        (x_ref, mod_ref, g1_ref, win_ref, heads_ref, heads_t_ref, gqk_ref, cos_ref, sin_ref,
         q_ref, kx_ref, vx_ref, xr_ref, gb_ref, win_s) = refs
    else:
        (x_ref, mod_ref, g1_ref, win_ref, heads_ref, heads_t_ref, gqk_ref,
         q_ref, kx_ref, vx_ref, xr_ref, gb_ref, kf_ref, vf_ref, win_s) = refs

    @pl.when(pl.program_id(0) == 0)
    def _():
        win_s[...] = win_ref[...].astype(BF16)

    x = x_ref[...]
    m = mod_ref[0]
    ms = jnp.mean(x * x, axis=-1, keepdims=True)
    y = x * lax.rsqrt(ms + EPS) * g1_ref[...]
    h = y * (1.0 + m[1:2]) + m[0:1]
    z = _dot(h.astype(BF16), win_s[...])

    qk = z[:, :QK_W]
    ss = _dot((qk * qk).astype(BF16), heads_ref[...])
    hi, lo = _split_bf16(lax.rsqrt(ss * (1.0 / HEAD_DIM) + EPS))
    qk = qk * (_dot(hi, heads_t_ref[...]) + _dot(lo, heads_t_ref[...])) * gqk_ref[...]

    lane = lax.broadcasted_iota(jnp.int32, (x.shape[0], LANES), 1)
    cols = []
    for c in range(QK_W // LANES):
        xc = qk[:, c * LANES:(c + 1) * LANES]
        if rope:
            left = pltpu.roll(xc, LANES - HEAD_DIM // 4, 1)
            right = pltpu.roll(xc, HEAD_DIM // 4, 1)
            rot = jnp.where((lane // (HEAD_DIM // 4)) % 2 == 0, left, right)
            xc = xc * cos_ref[...] + rot * sin_ref[...]
        cols.append(xc)
    for c in range(ATTN_W // LANES):
        q_ref[:, c * LANES:(c + 1) * LANES] = (cols[c] * Q_SCALE).astype(BF16)

    k_col = cols[ATTN_W // LANES]
    v_col = z[:, QK_W:QK_W + KV_W]
    _store_expanded(kx_ref, k_col, 0.0)
    _store_expanded(vx_ref, v_col, 1.0)
    if not rope:
        _store_cache(kf_ref, k_col)
        _store_cache(vf_ref, v_col)

    xr_ref[...] = z[:, QK_W + KV_W:QK_W + KV_W + LRU_W]
    gb_ref[...] = z[:, QK_W + KV_W + LRU_W:]


def _pre(x, mod3, mod_row, g1, w_in, heads, gqk, tables, seq_len):
    n = x.shape[0]
    tm = TM_PRE
    tiles_per_seq = seq_len // tm
    rope = tables is not None
    const = lambda i: (0, 0)
    in_specs = [pl.BlockSpec((tm, D_MODEL), lambda i: (i, 0)),
                pl.BlockSpec((1, 6, D_MODEL), lambda i: (mod_row(i * tm), 0, 0)),
                pl.BlockSpec((1, D_MODEL), const),
                pl.BlockSpec((D_MODEL, IN_W), const),
                pl.BlockSpec((QK_W, LANES), const),
                pl.BlockSpec((LANES, QK_W), const),
                pl.BlockSpec((1, QK_W), const)]
    args = [x, mod3, g1, w_in, heads, heads.T, gqk]
    if rope:
        in_specs += [pl.BlockSpec((tm, LANES), lambda i: (i % tiles_per_seq, 0))] * 2
        args += list(tables)
    row = lambda w: pl.BlockSpec((tm, w), lambda i: (i, 0))
    out_shape = [jax.ShapeDtypeStruct((n, ATTN_W), BF16),
                 jax.ShapeDtypeStruct((n, 4 * LANES), BF16),
                 jax.ShapeDtypeStruct((n, 4 * LANES), BF16),
                 jax.ShapeDtypeStruct((n, LRU_W), F32),
                 jax.ShapeDtypeStruct((n, LRU_W), F32)]
    out_specs = [row(ATTN_W), row(4 * LANES), row(4 * LANES), row(LRU_W), row(LRU_W)]
    if not rope:
        assert tm % seq_len == 0
        per_tile = tm // seq_len
        cache = pl.BlockSpec((per_tile, N_KV_HEADS, HEAD_DIM, seq_len), lambda i: (i, 0, 0, 0))
        out_shape += [jax.ShapeDtypeStruct((n // seq_len, N_KV_HEADS, HEAD_DIM, seq_len), F32)] * 2
        out_specs += [cache, cache]
    return pl.pallas_call(
        functools.partial(_pre_kernel, rope=rope),
        grid=(n // tm,),
        in_specs=in_specs, out_specs=out_specs, out_shape=out_shape,
        scratch_shapes=[pltpu.VMEM((D_MODEL, IN_W), BF16)],
        compiler_params=_cparams(("arbitrary",)),
        name="pre_rope" if rope else "pre",
    )(*args)


def _attn_kernel(*refs, n_seg):
    q_ref = refs[0]
    k_refs = refs[1:1 + n_seg]
    v_refs = refs[1 + n_seg:1 + 2 * n_seg]
    o_ref = refs[1 + 2 * n_seg]
    lane = lax.broadcasted_iota(jnp.int32, (q_ref.shape[0], LANES), 1)
    for c in range(ATTN_W // LANES):
        qc = q_ref[:, c * LANES:(c + 1) * LANES]
        g = c // 2
        accs = []
        for par in range(2):
            sl = slice((2 * g + par) * LANES, (2 * g + par + 1) * LANES)
            ss = [_dot_nt(qc, k[:, sl]) for k in k_refs]
            mx = functools.reduce(jnp.maximum, [jnp.max(s, axis=-1, keepdims=True) for s in ss])
            ps = [jnp.exp2((s - mx).astype(BF16)) for s in ss]
            accs.append(functools.reduce(lambda a, b: a + b,
                                         [_dot(p, v[:, sl]) for p, v in zip(ps, v_refs)]))
        even = accs[0] / accs[0][:, ONES_LANE_EVEN:ONES_LANE_EVEN + 1]
        odd = accs[1] / accs[1][:, ONES_LANE_ODD:ONES_LANE_ODD + 1]
        o_ref[:, c * LANES:(c + 1) * LANES] = jnp.where(lane < HEAD_DIM, even, odd).astype(BF16)


def _attention(q, k_segs, v_segs, seq_len):
    n = q.shape[0]
    tq = min(TQ_ATT, seq_len)
    nq = seq_len // tq
    n_seg = len(k_segs)
    in_specs = [pl.BlockSpec((tq, ATTN_W), lambda b, i: (b * nq + i, 0))]
    for arr, t in list(k_segs) + list(v_segs):
        in_specs.append(pl.BlockSpec((t, 4 * LANES), lambda b, i: (b, 0)))
    return pl.pallas_call(
        functools.partial(_attn_kernel, n_seg=n_seg),
        grid=(n // seq_len, nq),
        in_specs=in_specs,
        out_specs=pl.BlockSpec((tq, ATTN_W), lambda b, i: (b * nq + i, 0)),
        out_shape=jax.ShapeDtypeStruct((n, ATTN_W), BF16),
        compiler_params=_cparams(("arbitrary", "arbitrary")),
        name="attention_%dseg" % n_seg,
    )(q, *[a for a, _ in k_segs], *[a for a, _ in v_segs])


def _log_sigmoid(x):
    return jnp.minimum(x, 0.0) - jnp.log1p(jnp.exp(-jnp.abs(x)))


def _tile_scan(a, b, reverse):
    row = lax.broadcasted_iota(jnp.int32, a.shape, 0)
    d = 1
    while d < SUBLANES:
        if reverse:
            keep = row < SUBLANES - d
            shift = SUBLANES - d
        else:
            keep = row >= d
            shift = d
        a_sh = jnp.where(keep, pltpu.roll(a, shift, 0), 1.0)
        b_sh = jnp.where(keep, pltpu.roll(b, shift, 0), 0.0)
        b = a * b_sh + b
        a = a * a_sh
        d *= 2
    return a, b


def _lru_kernel(xr_ref, gb_ref, h0_ref, cw_ref, cb_ref, wa_ref, wx_ref, ba_ref, bx_ref, lam_ref,
                rec_ref, fin_ref, xpad_s, xc_s, hf_s, a_s, b_s, wf_ref, wb_ref, bf_ref, bb_ref,
                *, seq_len):
    @pl.when(pl.program_id(0) == 0)
    def _():
        pairs = LANES // LRU_BLOCK_W
        zero = jnp.zeros((LRU_BLOCK_W, LRU_BLOCK_W), F32)
        for d, (w_s, bias_s) in enumerate(((wf_ref, bf_ref), (wb_ref, bb_ref))):
            w_s[...] = jnp.zeros_like(w_s)
            for g, src in enumerate((wa_ref, wx_ref)):
                for p in range(LRU_BLOCKS // pairs):
                    rows = [jnp.concatenate([0.5 * src[d, pairs * p + q] if q == r else zero
                                             for q in range(pairs)], axis=1) for r in range(pairs)]
                    w_s[LANES * p:LANES * (p + 1), g * LRU_W + LANES * p:g * LRU_W + LANES * (p + 1)] = (
                        jnp.concatenate(rows, axis=0).astype(BF16))
            bias_s[:, :LRU_W] = 0.5 * ba_ref[d:d + 1, :]
            bias_s[:, LRU_W:] = 0.5 * bx_ref[d:d + 1, :]

    tc = min(TC_LRU, seq_len)
    n_chunks = seq_len // tc
    n_tiles = tc // SUBLANES
    zpad = jnp.zeros((SUBLANES, LRU_W), F32)
    xpad_s[0:SUBLANES, :] = zpad
    xpad_s[SUBLANES:SUBLANES + seq_len, :] = xr_ref[...]
    xpad_s[SUBLANES + seq_len:2 * SUBLANES + seq_len, :] = zpad

    half_cl = (0.5 * LRU_C * LOG2_E) * _log_sigmoid(lam_ref[...])

    def gates(xcc, w_ref, bias_ref, half_cl_d):
        t = jnp.tanh(_dot(xcc.astype(BF16), w_ref[...]) + bias_ref[...])
        a = jnp.exp2(t[:, :LRU_W] * half_cl_d + half_cl_d)
        half_x = 0.5 * xcc
        a_s[...] = a
        u = 1.0 - a * a
        b_s[...] = jnp.where(u > 0.0, u * lax.rsqrt(u), 0.0) * (t[:, LRU_W:] * half_x + half_x)

    h = h0_ref[0, 0:1, :]
    for c in range(n_chunks):
        base = c * tc
        xcc = cb_ref[...] + functools.reduce(
            lambda u, v: u + v,
            [cw_ref[j:j + 1, :] * xpad_s[base + SUBLANES - 1 + j:base + SUBLANES - 1 + j + tc, :]
             for j in range(CONV_W)])
        xc_s[base:base + tc, :] = xcc
        gates(xcc, wf_ref, bf_ref, half_cl[0:1])

        def fwd_tile(t, hc, base=base):
            r0 = pl.multiple_of(t * SUBLANES, SUBLANES)
            ca, cb = _tile_scan(a_s[pl.ds(r0, SUBLANES), :], b_s[pl.ds(r0, SUBLANES), :], False)
            hh = ca * hc + cb
            hf_s[pl.ds(base + r0, SUBLANES), :] = hh
            return hh[SUBLANES - 1:SUBLANES, :]

        h = lax.fori_loop(0, n_tiles, fwd_tile, h, unroll=4)
    fin_ref[0, 0:1, :] = h

    h = h0_ref[0, 1:2, :]
    for c in reversed(range(n_chunks)):
        base = c * tc
        gates(xc_s[base:base + tc, :], wb_ref, bb_ref, half_cl[1:2])

        def bwd_tile(t, hc, base=base):
            r0 = pl.multiple_of((n_tiles - 1 - t) * SUBLANES, SUBLANES)
            ca, cb = _tile_scan(a_s[pl.ds(r0, SUBLANES), :], b_s[pl.ds(r0, SUBLANES), :], True)
            hh = ca * hc + cb
            gate = jax.nn.gelu(gb_ref[pl.ds(base + r0, SUBLANES), :], approximate=True)
            rec_ref[pl.ds(base + r0, SUBLANES), :] = (
                (hf_s[pl.ds(base + r0, SUBLANES), :] + hh) * gate).astype(rec_ref.dtype)
            return hh[0:1, :]

        h = lax.fori_loop(0, n_tiles, bwd_tile, h, unroll=4)
    fin_ref[0, 1:2, :] = h


def _lru(xr, gb, h0, conv_w, conv_b, wa, wx, ba, bx, lam, seq_len):
    n = xr.shape[0]
    batch = n // seq_len
    const = lambda b: (0, 0)
    blocks = pl.BlockSpec((2, LRU_BLOCKS, LRU_BLOCK_W, LRU_BLOCK_W), lambda b: (0, 0, 0, 0))
    seq = pl.BlockSpec((seq_len, LRU_W), lambda b: (b, 0))
    st = pl.BlockSpec((1, 2, LRU_W), lambda b: (b, 0, 0))
    return pl.pallas_call(
        functools.partial(_lru_kernel, seq_len=seq_len),
        grid=(batch,),
        in_specs=[seq, seq, st,
                  pl.BlockSpec((CONV_W, LRU_W), const), pl.BlockSpec((1, LRU_W), const),
                  blocks, blocks,
                  pl.BlockSpec((2, LRU_W), const), pl.BlockSpec((2, LRU_W), const),
                  pl.BlockSpec((2, LRU_W), const)],
        out_specs=[seq, st],
        out_shape=[jax.ShapeDtypeStruct((n, LRU_W), BF16),
                   jax.ShapeDtypeStruct((batch, 2, LRU_W), F32)],
        scratch_shapes=[pltpu.VMEM((seq_len + 2 * SUBLANES, LRU_W), F32),
                        pltpu.VMEM((seq_len, LRU_W), F32),
                        pltpu.VMEM((seq_len, LRU_W), F32),
                        pltpu.VMEM((min(TC_LRU, seq_len), LRU_W), F32),
                        pltpu.VMEM((min(TC_LRU, seq_len), LRU_W), F32),
                        pltpu.VMEM((LRU_W, 2 * LRU_W), BF16),
                        pltpu.VMEM((LRU_W, 2 * LRU_W), BF16),
                        pltpu.VMEM((1, 2 * LRU_W), F32),
                        pltpu.VMEM((1, 2 * LRU_W), F32)],
        compiler_params=_cparams(("arbitrary",)),
        name="lru_%d" % seq_len,
    )(xr, gb, h0, conv_w, conv_b, wa, wx, ba, bx, lam)


def _post_kernel(attn_ref, rec_ref, x_ref, mod_ref, g2_ref, wo_ref, wr2_ref, br_ref,
                 tri_ref, cnt_in_ref, x1_ref, h2_ref, route_ref, cnt_ref, route_t_ref, cnt_s, wo_s):
    @pl.when(pl.program_id(0) == 0)
    def _():
        cnt_s[...] = cnt_in_ref[...]
        wo_s[...] = wo_ref[...].astype(BF16)

    m = mod_ref[0]
    u = _dot(attn_ref[...], wo_s[:ATTN_W, :]) + _dot(rec_ref[...], wo_s[ATTN_W:, :])
    x1 = x_ref[...] + m[2:3] * u
    x1_ref[...] = x1
    ms = jnp.mean(x1 * x1, axis=-1, keepdims=True)
    h2 = x1 * lax.rsqrt(ms + EPS) * g2_ref[...]
    h2 = h2 * (1.0 + m[4:5]) + m[3:4]
    hi, lo = _split_bf16(h2)
    h2_ref[...] = _to_row_tiles(h2)

    hw = _dot(hi, wr2_ref[...])
    logits = hw[:, :LANES] + hw[:, LANES:] + _dot(lo, wr2_ref[:, :LANES]) + br_ref[...]
    lane_i = lax.broadcasted_iota(jnp.int32, logits.shape, 1)
    lane = lane_i.astype(F32)
    lane_group = (lane_i // EXPERTS_PER_GROUP).astype(F32)
    neg = -jnp.inf
    big = float(1 << 20)
    gmask = (lane_i >= N_EXPERTS) & (lane_i < N_EXPERTS + N_GROUPS)
    gl = jnp.where(gmask, logits, neg)
    gmax = jnp.max(gl, axis=-1, keepdims=True)
    gidx = jnp.min(jnp.where(gl == gmax, lane - N_EXPERTS, big), axis=-1, keepdims=True)
    p_sel = 1.0 / jnp.sum(jnp.where(gmask, jnp.exp(gl - gmax), 0.0), axis=-1, keepdims=True)

    emask = (lane_i < N_EXPERTS) & (lane_group == gidx)
    el = jnp.where(emask, logits, neg)
    v1 = jnp.max(el, axis=-1, keepdims=True)
    i1 = jnp.min(jnp.where(el == v1, lane, big), axis=-1, keepdims=True)
    el2 = jnp.where(lane == i1, neg, el)
    v2 = jnp.max(el2, axis=-1, keepdims=True)
    i2 = jnp.min(jnp.where(el2 == v2, lane, big), axis=-1, keepdims=True)
    e2 = jnp.exp(v2 - v1)
    w1 = p_sel / (1.0 + e2)
    w2 = p_sel * e2 / (1.0 + e2)

    oh1 = lane == i1
    oh2 = lane == i2
    oh = jnp.where(oh1, 1.0, 0.0) + jnp.where(oh2, 1.0, 0.0)
    before = _dot(tri_ref[...], oh.astype(BF16)) + cnt_s[...]
    rank1 = jnp.sum(jnp.where(oh1, before, 0.0), axis=-1, keepdims=True)
    rank2 = jnp.sum(jnp.where(oh2, before, 0.0), axis=-1, keepdims=True)
    cnt = cnt_s[...] + jnp.sum(oh, axis=0, keepdims=True)
    cnt_s[...] = cnt
    cnt_ref[...] = cnt
    fields = (i1, i2, rank1, rank2, w1, w2)
    route = jnp.zeros(logits.shape, F32)
    for k, val in enumerate(fields):
        route = jnp.where(lane_i == k, val, route)
    route_ref[...] = route
    route_t_ref[...] = jnp.transpose(route)[:SUBLANES, :]


def _post(attn, rec, x, mod3, mod_row, g2, w_out, wr2, br, tri, cnt_in):
    n = x.shape[0]
    tm = TM_PRE
    const = lambda i: (0, 0)
    row = lambda w: pl.BlockSpec((tm, w), lambda i: (i, 0))
    return pl.pallas_call(
        _post_kernel,
        grid=(n // tm,),
        in_specs=[row(ATTN_W), row(LRU_W), row(D_MODEL),
                  pl.BlockSpec((1, 6, D_MODEL), lambda i: (mod_row(i * tm), 0, 0)),
                  pl.BlockSpec((1, D_MODEL), const),
                  pl.BlockSpec((D_MODEL, D_MODEL), const),
                  pl.BlockSpec((D_MODEL, 2 * LANES), const),
                  pl.BlockSpec((1, LANES), const),
                  pl.BlockSpec((tm, tm), const),
                  pl.BlockSpec((1, LANES), const)],
        out_specs=[row(D_MODEL), pl.BlockSpec((tm,) + ROW_TILE, lambda i: (i, 0, 0)), row(LANES),
                   pl.BlockSpec((1, LANES), const),
                   pl.BlockSpec((SUBLANES, tm), lambda i: (0, i))],
        out_shape=[jax.ShapeDtypeStruct((n, D_MODEL), F32),
                   jax.ShapeDtypeStruct((n,) + ROW_TILE, F32),
                   jax.ShapeDtypeStruct((n, LANES), F32),
                   jax.ShapeDtypeStruct((1, LANES), F32),
                   jax.ShapeDtypeStruct((SUBLANES, n), F32)],
        scratch_shapes=[pltpu.VMEM((1, LANES), F32), pltpu.VMEM((D_MODEL, D_MODEL), BF16)],
        compiler_params=_cparams(("arbitrary",)),
        name="post",
    )(attn, rec, x, mod3, g2, w_out, wr2, br, tri, cnt_in)


def _row_copy(src_ref, src_row, dst_ref, dst_row, sem):
    return pltpu.make_async_copy(src_ref.at[pl.ds(src_row, 1)], dst_ref.at[pl.ds(dst_row, 1)], sem)


def _plan_kernel(route_t_ref, seg_ref, slots_ref):
    route_t = route_t_ref[...]
    t = route_t.shape[1]
    seg = jnp.concatenate([seg_ref[...]] * (t // LANES), axis=1)
    expert = lax.broadcasted_iota(jnp.int32, seg.shape, 0).astype(F32)
    rows = []
    for k in range(TOP_K):
        start = jnp.sum(jnp.where(expert == route_t[k:k + 1, :], seg, 0.0), axis=0, keepdims=True)
        rows.append(start + route_t[TOP_K + k:TOP_K + k + 1, :])
    rows.append(jnp.zeros((SUBLANES - TOP_K, t), F32))
    slots_ref[...] = jnp.concatenate(rows, axis=0).astype(jnp.int32)


def _plan(route_t, seg_rows):
    n = route_t.shape[1]
    tm = TM_PRE
    return pl.pallas_call(
        _plan_kernel,
        grid=(n // tm,),
        in_specs=[pl.BlockSpec((SUBLANES, tm), lambda i: (0, i)),
                  pl.BlockSpec((N_EXPERTS, LANES), lambda i: (0, 0))],
        out_specs=pl.BlockSpec((SUBLANES, tm), lambda i: (0, i)),
        out_shape=jax.ShapeDtypeStruct((SUBLANES, n), jnp.int32),
        compiler_params=_cparams(("arbitrary",)),
        name="plan",
    )(route_t, seg_rows)


def _dispatch_kernel(slots_p_ref, slots_s_ref, pad_start_ref, pad_rows_ref, nu_ref, h2p_ref, h2s_ref,
                     xs_ref, zero_s, sem, zsem, *, n_prompt):
    i = pl.program_id(0)
    ch = CH_DISPATCH

    def zero_fill(op):
        for e in range(N_EXPERTS):
            rows = pad_rows_ref[e]
            for b in range(TR_MOE.bit_length() - 1):
                size = 1 << b
                off = (rows >> (b + 1)) << (b + 1)

                @pl.when(((rows >> b) & 1) == 1)
                def _():
                    op(pltpu.make_async_copy(zero_s.at[pl.ds(0, size)],
                                             xs_ref.at[pl.ds(pad_start_ref[e] + off, size)], zsem))

        def zero_tile(t, carry):
            op(pltpu.make_async_copy(
                zero_s, xs_ref.at[pl.ds(pl.multiple_of(t * TR_MOE, TR_MOE), TR_MOE)], zsem))
            return carry
        lax.fori_loop(nu_ref[0], xs_ref.shape[0] // TR_MOE, zero_tile, 0)

    @pl.when(i == 0)
    def _():
        zero_s[...] = jnp.zeros_like(zero_s)
        zero_fill(lambda cp: cp.start())

    @pl.when(i == pl.num_programs(0) - 1)
    def _():
        zero_fill(lambda cp: cp.wait())

    def scatter(src_ref, slots_ref, first_token):
        n_path = slots_ref.shape[0] // TOP_K

        def body(j, carry):
            for k in range(TOP_K):
                _row_copy(src_ref, j, xs_ref, slots_ref[k * n_path + i * ch + j - first_token],
                          sem).start(priority=k)
            return carry
        lax.fori_loop(0, ch, body, 0, unroll=8)
        for _ in range(TOP_K):
            pltpu.make_async_copy(src_ref, xs_ref.at[pl.ds(0, ch)], sem).wait()

    @pl.when(i < n_prompt // ch)
    def _():
        scatter(h2p_ref, slots_p_ref, 0)

    @pl.when(i >= n_prompt // ch)
    def _():
        scatter(h2s_ref, slots_s_ref, n_prompt)


def _dispatch(slots_p, slots_s, pad_start, pad_rows, n_used, h2p, h2s, n_rows):
    n_prompt = h2p.shape[0]
    n = n_prompt + h2s.shape[0]
    ch = CH_DISPATCH
    npc = n_prompt // ch
    last_p = npc - 1
    return pl.pallas_call(
        functools.partial(_dispatch_kernel, n_prompt=n_prompt),
        grid_spec=pltpu.PrefetchScalarGridSpec(
            num_scalar_prefetch=5,
            grid=(n // ch,),
            in_specs=[pl.BlockSpec((ch,) + ROW_TILE, lambda i, *_: (jnp.minimum(i, last_p), 0, 0)),
                      pl.BlockSpec((ch,) + ROW_TILE, lambda i, *_: (jnp.maximum(i - npc, 0), 0, 0))],
            out_specs=pl.BlockSpec(memory_space=pl.ANY),
            scratch_shapes=[pltpu.VMEM((TR_MOE,) + ROW_TILE, F32), pltpu.SemaphoreType.DMA,
                            pltpu.SemaphoreType.DMA]),
        out_shape=jax.ShapeDtypeStruct((n_rows,) + ROW_TILE, F32),
        compiler_params=_cparams(("arbitrary",)),
        name="dispatch",
    )(slots_p, slots_s, pad_start, pad_rows, n_used, h2p, h2s)


def _ffn_kernel(tend_ref, xs_ref, wg_ref, wu_ref, wd_ref, ys_ref,
                xbuf, ybuf, wg_s, wu_s, wd_s, sem_in, sem_out):
    e = pl.program_id(0)
    n_used = tend_ref[N_EXPERTS - 1]
    t_first = jnp.where(e == 0, 0, tend_ref[jnp.maximum(e - 1, 0)])
    t_last = tend_ref[e]
    n_in = xbuf.shape[0]
    n_out = ybuf.shape[0]

    def tile_rows(t):
        return pl.ds(pl.multiple_of(t * TR_MOE, TR_MOE), TR_MOE)

    def fetch(t):
        return pltpu.make_async_copy(xs_ref.at[tile_rows(t)], xbuf.at[t % n_in], sem_in.at[t % n_in])

    def writeback(t):
        return pltpu.make_async_copy(ybuf.at[t % n_out], ys_ref.at[tile_rows(t)],
                                     sem_out.at[t % n_out])

    @pl.when(e == 0)
    def _():
        for t in range(n_in - 1):
            @pl.when(t < n_used)
            def _():
                fetch(t).start()

    @pl.when(t_last > t_first)
    def _():
        wg_s[...] = wg_ref[0].astype(BF16)
        wu_s[...] = wu_ref[0].astype(BF16)
        wd_s[...] = wd_ref[0].astype(BF16)

    def tile(t, carry):
        fetch(t).wait()

        @pl.when(t + n_in - 1 < n_used)
        def _():
            fetch(t + n_in - 1).start()

        @pl.when(t >= n_out)
        def _():
            writeback(t - n_out).wait()

        x = _from_row_tiles(xbuf[t % n_in]).astype(BF16)
        hg = _dot(x, wg_s[...])
        hu = _dot(x, wu_s[...])
        act = (hg * _sigmoid(hg)) * hu
        ybuf[t % n_out] = _to_row_tiles(_dot(act.astype(BF16), wd_s[...]))
        writeback(t).start()
        return carry

    lax.fori_loop(t_first, t_last, tile, 0)

    @pl.when(e == N_EXPERTS - 1)
    def _():
        for back in range(n_out, 0, -1):
            @pl.when(n_used >= back)
            def _():
                writeback(n_used - back).wait()
        n_all = ys_ref.shape[0] // TR_MOE
        ybuf[0] = jnp.zeros(ybuf.shape[1:], F32)

        def zero_tile(t):
            return pltpu.make_async_copy(ybuf.at[0], ys_ref.at[tile_rows(t)], sem_out.at[0])

        lax.fori_loop(n_used, n_all, lambda t, c: (zero_tile(t).start(), c)[1], 0)
        lax.fori_loop(n_used, n_all, lambda t, c: (zero_tile(t).wait(), c)[1], 0)


def _ffn(tile_end, xs, wg, wu, wd, n_tiles):
    tr = TR_MOE
    wsel = lambda e, tend: (e, 0, 0)
    return pl.pallas_call(
        _ffn_kernel,
        grid_spec=pltpu.PrefetchScalarGridSpec(
            num_scalar_prefetch=1,
            grid=(N_EXPERTS,),
            in_specs=[pl.BlockSpec(memory_space=pl.ANY),
                      pl.BlockSpec((1, D_MODEL, EXPERT_FF), wsel),
                      pl.BlockSpec((1, D_MODEL, EXPERT_FF), wsel),
                      pl.BlockSpec((1, EXPERT_FF, D_MODEL), wsel)],
            out_specs=pl.BlockSpec(memory_space=pl.ANY),
            scratch_shapes=[pltpu.VMEM((FFN_IN_SLOTS, tr) + ROW_TILE, F32),
                            pltpu.VMEM((FFN_OUT_SLOTS, tr) + ROW_TILE, F32),
                            pltpu.VMEM((D_MODEL, EXPERT_FF), BF16),
                            pltpu.VMEM((D_MODEL, EXPERT_FF), BF16),
                            pltpu.VMEM((EXPERT_FF, D_MODEL), BF16),
                            pltpu.SemaphoreType.DMA((FFN_IN_SLOTS,)),
                            pltpu.SemaphoreType.DMA((FFN_OUT_SLOTS,))]),
        out_shape=jax.ShapeDtypeStruct((n_tiles * tr,) + ROW_TILE, F32),
        compiler_params=_cparams(("arbitrary",)),
        name="ffn",
    )(tile_end, xs, wg, wu, wd)


def _combine_kernel(slots_ref, ys_ref, x1_ref, route_ref, mod_ref, y_ref, b1_s, b2_s, sems):
    i = pl.program_id(0)
    tm = x1_ref.shape[0]
    n_path = slots_ref.shape[0] // TOP_K

    n_steps = pl.num_programs(0)
    ch = COMBINE_CHUNK

    def gather_chunk(step, slot, c):
        for t in range(ch):
            j = c * ch + t
            _row_copy(ys_ref, slots_ref[step * tm + j], b1_s.at[slot], j, sems.at[slot]).start(
                priority=0)
            _row_copy(ys_ref, slots_ref[n_path + step * tm + j], b2_s.at[slot], j,
                      sems.at[slot]).start(priority=1)

    def wait_slot(slot):
        for buf in (b1_s, b2_s):
            pltpu.make_async_copy(ys_ref.at[pl.ds(0, tm)], buf.at[slot], sems.at[slot]).wait()

    @pl.when(i == 0)
    def _():
        lax.fori_loop(0, tm // ch, lambda c, carry: (gather_chunk(0, 0, c), carry)[1], 0)

    slot = i % 2
    wait_slot(slot)
    gate = mod_ref[0][5:6]
    nxt = (i + 1) % n_steps

    def body(c, carry):
        rows = pl.ds(pl.multiple_of(c * ch, ch), ch)
        route = route_ref[rows, :]
        moe = (route[:, 4:5] * _from_row_tiles(b1_s[slot, rows])
               + route[:, 5:6] * _from_row_tiles(b2_s[slot, rows]))
        y_ref[rows, :] = x1_ref[rows, :] + gate * moe
        gather_chunk(nxt, 1 - slot, c)
        return carry
    lax.fori_loop(0, tm // ch, body, 0)

    @pl.when(i == n_steps - 1)
    def _():
        wait_slot(1 - slot)


def _combine(slots, ys, x1, route, mod3, mod_row):
    n = x1.shape[0]
    tm = TM_PRE
    row = lambda w: pl.BlockSpec((tm, w), lambda i, *_: (i, 0))
    return pl.pallas_call(
        _combine_kernel,
        grid_spec=pltpu.PrefetchScalarGridSpec(
            num_scalar_prefetch=1,
            grid=(n // tm,),
            in_specs=[pl.BlockSpec(memory_space=pl.ANY), row(D_MODEL), row(LANES),
                      pl.BlockSpec((1, 6, D_MODEL),
                                   lambda i, *_: (mod_row(i * tm), 0, 0))],
            out_specs=row(D_MODEL),
            scratch_shapes=[pltpu.VMEM((2, tm) + ROW_TILE, F32), pltpu.VMEM((2, tm) + ROW_TILE, F32),
                            pltpu.SemaphoreType.DMA((2,))]),
        out_shape=jax.ShapeDtypeStruct((n, D_MODEL), F32),
        compiler_params=_cparams(("arbitrary",)),
        name="combine",
    )(slots, ys, x1, route, mod3)


def _rope_tables(length):
    rows = length // GRID_W
    r, col = jnp.meshgrid(jnp.arange(rows), jnp.arange(GRID_W), indexing='ij')
    r = r.reshape(-1).astype(F32)
    col = col.reshape(-1).astype(F32)
    half = HEAD_DIM // 2
    inv = ROPE_THETA ** (-jnp.arange(0, half, 2, dtype=F32) / half)
    ang_r = r[:, None] * inv
    ang_c = col[:, None] * inv
    ang = jnp.concatenate([ang_r, ang_r, ang_c, ang_c], axis=-1)
    sign = jnp.where((jnp.arange(HEAD_DIM) // (HEAD_DIM // 4)) % 2 == 0, -1.0, 1.0).astype(F32)
    cos = jnp.tile(jnp.cos(ang), (1, LANES // HEAD_DIM))
    sin = jnp.tile(jnp.sin(ang) * sign, (1, LANES // HEAD_DIM))
    return cos, sin


def _expand_cache_kernel(k_ref, v_ref, kx_ref, vx_ref):
    for ref, xref, one in ((k_ref, kx_ref, 0.0), (v_ref, vx_ref, 1.0)):
        col = jnp.transpose(ref[0].reshape(KV_W, ref.shape[-1]))
        _store_expanded(xref, col, one)


def _expand_cache(cache_k, cache_v):
    b, _, _, t = cache_k.shape
    src = pl.BlockSpec((1, N_KV_HEADS, HEAD_DIM, t), lambda i: (i, 0, 0, 0))
    dst = pl.BlockSpec((t, 4 * LANES), lambda i: (i, 0))
    return pl.pallas_call(
        _expand_cache_kernel,
        grid=(b,),
        in_specs=[src, src], out_specs=[dst, dst],
        out_shape=[jax.ShapeDtypeStruct((b * t, 4 * LANES), BF16)] * 2,
        compiler_params=_cparams(("arbitrary",)),
        name="expand_cache",
    )(cache_k, cache_v)


def kernel(x_prompt, x_sample, cache_k, cache_v, state_lru, c, c_ctx, w_mod, b_mod, norm1, norm2,
           w_in, q_norm, k_norm, conv_w, conv_b, lru_wa, lru_ba, lru_wx, lru_bx, lru_lambda, w_out,
           router_grp_w, router_grp_b, router_exp_w, router_exp_b, exp_w_gate, exp_w_up, exp_w_down):
    batch, seq, _ = x_prompt.shape
    dec_batch, dec_seq, _ = x_sample.shape
    past = cache_k.shape[2]
    depth = w_mod.shape[0]
    assert depth == 1

    cvec = jnp.concatenate(
        [c_ctx[None, :], c, jnp.zeros((MOD_ROWS - 1 - dec_batch, D_MODEL), F32)], axis=0)
    mod3 = _modulation(cvec, w_mod[0], b_mod[0][None, :]).reshape(MOD_ROWS, 6, D_MODEL)

    head_id = jnp.arange(QK_W) // HEAD_DIM
    heads = (head_id[:, None] == jnp.arange(LANES)[None, :]).astype(BF16)
    gqk = jnp.concatenate([jnp.tile(q_norm[0], N_HEADS), jnp.tile(k_norm[0], N_KV_HEADS)])[None, :]
    pad = LANES - N_EXPERTS - N_GROUPS
    wr = jnp.concatenate([router_exp_w[0], router_grp_w[0], jnp.zeros((D_MODEL, pad), F32)], axis=1)
    wr_hi = wr.astype(BF16)
    wr2 = jnp.concatenate([wr_hi, (wr - wr_hi.astype(F32)).astype(BF16)], axis=1)
    br = jnp.concatenate([router_exp_b[0], router_grp_b[0], jnp.zeros((pad,), F32)])[None, :]
    g1 = norm1[0][None, :]
    g2 = norm2[0][None, :]
    cw = conv_w[0]
    cb = conv_b[0][None, :]
    lam = lru_lambda[0]
    tri = (jnp.arange(TM_PRE)[:, None] > jnp.arange(TM_PRE)[None, :]).astype(BF16)

    def mixers(x, seq_len, mod_row, tables, extra_k, extra_v, h0, cnt_in):
        q, kx, vx, xr, gb, *cache = _pre(x, mod3, mod_row, g1, w_in[0], heads, gqk, tables, seq_len)
        k_segs = [(kx, seq_len)] + extra_k
        v_segs = [(vx, seq_len)] + extra_v
        attn = _attention(q, k_segs, v_segs, seq_len)
        rec, fin = _lru(xr, gb, h0, cw, cb, lru_wa[0], lru_wx[0], lru_ba[0], lru_bx[0], lam, seq_len)
        x1, h2, route, cnt, route_t = _post(attn, rec, x, mod3, mod_row, g2, w_out[0], wr2, br, tri,
                                            cnt_in)
        return x1, h2, (route, route_t), cnt, cache, fin

    mod_row_p = lambda tok: 0
    mod_row_s = lambda tok: tok // dec_seq + 1
    xp = x_prompt.reshape(batch * seq, D_MODEL)
    x1p, h2p, (route_p, route_t_p), cnt_p, (kf, vf), fin = mixers(
        xp, seq, mod_row_p, None, [], [], jnp.zeros((batch, 2, LRU_W), F32),
        jnp.zeros((1, LANES), F32))
    xs = x_sample.reshape(dec_batch * dec_seq, D_MODEL)
    ck, cv = _expand_cache(jnp.transpose(cache_k[:, 0], (0, 2, 3, 1)),
                           jnp.transpose(cache_v[:, 0], (0, 2, 3, 1)))
    x1s, h2s, (route_s, route_t_s), cnt_all, _, _ = mixers(
        xs, dec_seq, mod_row_s, _rope_tables(dec_seq), [(ck, past)], [(cv, past)],
        state_lru[:, 0], cnt_p)

    n_prompt = batch * seq
    n_tok = n_prompt + dec_batch * dec_seq
    n_tiles = (TOP_K * n_tok + N_EXPERTS * (TR_MOE - 1)) // TR_MOE
    cnt = cnt_all[0, :N_EXPERTS].astype(jnp.int32)
    ntile = (cnt + TR_MOE - 1) // TR_MOE
    tile_end = jnp.cumsum(ntile)
    seg_start = (tile_end - ntile) * TR_MOE
    n_used = tile_end[-1:]
    pad_start = seg_start + cnt
    pad_rows = tile_end * TR_MOE - pad_start

    seg_rows = jnp.broadcast_to(seg_start.astype(F32)[:, None], (N_EXPERTS, LANES))
    slots_p = _plan(route_t_p, seg_rows)[:TOP_K].reshape(-1)
    slots_s = _plan(route_t_s, seg_rows)[:TOP_K].reshape(-1)
    xsort = _dispatch(slots_p, slots_s, pad_start, pad_rows, n_used, h2p, h2s,
                      n_tiles * TR_MOE)
    ysort = _ffn(tile_end, xsort, exp_w_gate[0], exp_w_up[0], exp_w_down[0], n_tiles)
    yp = _combine(slots_p, ysort, x1p, route_p, mod3, mod_row_p)
    ys = _combine(slots_s, ysort, x1s, route_s, mod3, mod_row_s)

    return (yp.reshape(batch, seq, D_MODEL),
            ys.reshape(dec_batch, dec_seq, D_MODEL),
            jnp.transpose(kf, (0, 3, 1, 2))[:, None],
            jnp.transpose(vf, (0, 3, 1, 2))[:, None],
            fin.reshape(batch, 1, 2, LRU_W))
```

```python
import functools

import jax
import jax.numpy as jnp
from jax import lax
from jax.experimental import pallas as pl
from jax.experimental.pallas import tpu as pltpu

F32 = jnp.float32
BF16 = jnp.bfloat16

D_MODEL = 1024
GRID_W = 64
ATTN_W = 512
LRU_W = 512
HEAD_DIM = 64
N_HEADS = 8
N_KV_HEADS = 2
KV_W = N_KV_HEADS * HEAD_DIM
LRU_BLOCKS = 8
LRU_BLOCK_W = LRU_W // LRU_BLOCKS
CONV_W = 4
LRU_C = 8.0
IN_W = ATTN_W + 2 * KV_W + 2 * LRU_W
QK_W = ATTN_W + KV_W
N_GROUPS = 4
EXPERTS_PER_GROUP = 8
N_EXPERTS = N_GROUPS * EXPERTS_PER_GROUP
TOP_K = 2
EXPERT_FF = D_MODEL // 4
ROPE_THETA = 10000.0
EPS = 1e-6

LANES = 128
SUBLANES = 8
MOD_ROWS = 8
VMEM_LIMIT = 48 * 1024 * 1024

TM_PRE = 512
TQ_ATT = 1024
TC_LRU = 512
TR_MOE = 256
CH_DISPATCH = 1024
COMBINE_CHUNK = 32
FFN_IN_SLOTS = 4
FFN_OUT_SLOTS = 3


def _cparams(sem):
    return pltpu.CompilerParams(dimension_semantics=sem, vmem_limit_bytes=VMEM_LIMIT)


def _dot(a, b):
    return jnp.dot(a, b, preferred_element_type=F32)


def _dot_nt(a, b):
    return lax.dot_general(a, b, (((1,), (1,)), ((), ())), preferred_element_type=F32)


ROW_TILE = (D_MODEL // LANES, LANES)
LOG2_E = 1.4426950408889634
Q_SCALE = HEAD_DIM ** -0.5 * LOG2_E
ONES_LANE_EVEN = HEAD_DIM
ONES_LANE_ODD = 0


def _to_row_tiles(x):
    cols = jnp.stack([x[:, c * LANES:(c + 1) * LANES] for c in range(D_MODEL // LANES)], axis=0)
    return jnp.swapaxes(cols, 0, 1)


def _from_row_tiles(x3):
    cols = jnp.swapaxes(x3, 0, 1)
    return jnp.concatenate([cols[c] for c in range(D_MODEL // LANES)], axis=1)


def _sigmoid(x):
    return 0.5 * jnp.tanh(0.5 * x) + 0.5


def _split_bf16(x):
    hi = x.astype(BF16)
    lo = (x - hi.astype(F32)).astype(BF16)
    return hi, lo


def _mod_kernel(c_ref, w_ref, b_ref, o_ref):
    c = c_ref[...]
    s = (c * jax.nn.sigmoid(c)).astype(BF16)
    o_ref[...] = _dot(s, w_ref[...].astype(BF16)) + b_ref[...]


def _modulation(cvec, w_mod, b_mod):
    n_out = w_mod.shape[1]
    tn = n_out // 4
    return pl.pallas_call(
        _mod_kernel,
        grid=(n_out // tn,),
        in_specs=[pl.BlockSpec((MOD_ROWS, D_MODEL), lambda j: (0, 0)),
                  pl.BlockSpec((D_MODEL, tn), lambda j: (0, j)),
                  pl.BlockSpec((1, tn), lambda j: (0, j))],
        out_specs=pl.BlockSpec((MOD_ROWS, tn), lambda j: (0, j)),
        out_shape=jax.ShapeDtypeStruct((MOD_ROWS, n_out), F32),
        compiler_params=_cparams(("arbitrary",)),
        name="modulation",
    )(cvec, w_mod, b_mod)


def _store_cache(ref, col):
    seq = ref.shape[-1]
    col_t = jnp.transpose(col)
    for j in range(ref.shape[0]):
        ref[j] = col_t[:, j * seq:(j + 1) * seq].reshape(N_KV_HEADS, HEAD_DIM, seq)


def _store_expanded(xref, col, one):
    lane = lax.broadcasted_iota(jnp.int32, col.shape, 1)
    lo_half = lane < HEAD_DIM
    swapped = pltpu.roll(col, HEAD_DIM, 1)
    fill_hi = jnp.where(lane == ONES_LANE_EVEN, one, 0.0)
    fill_lo = jnp.where(lane == ONES_LANE_ODD, one, 0.0)
    xref[:, 0 * LANES:1 * LANES] = jnp.where(lo_half, col, fill_hi).astype(BF16)
    xref[:, 1 * LANES:2 * LANES] = jnp.where(lo_half, fill_lo, swapped).astype(BF16)
    xref[:, 2 * LANES:3 * LANES] = jnp.where(lo_half, swapped, fill_hi).astype(BF16)
    xref[:, 3 * LANES:4 * LANES] = jnp.where(lo_half, fill_lo, col).astype(BF16)


def _pre_kernel(*refs, rope):
    if rope:
        (x_ref, mod_ref, g1_ref, win_ref, heads_ref, heads_t_ref, gqk_ref, cos_ref, sin_ref,
         q_ref, kx_ref, vx_ref, xr_ref, gb_ref, win_s) = refs
    else:
        (x_ref, mod_ref, g1_ref, win_ref, heads_ref, heads_t_ref, gqk_ref,
         q_ref, kx_ref, vx_ref, xr_ref, gb_ref, kf_ref, vf_ref, win_s) = refs

    @pl.when(pl.program_id(0) == 0)
    def _():
        win_s[...] = win_ref[...].astype(BF16)

    x = x_ref[...]
    m = mod_ref[0]
    ms = jnp.mean(x * x, axis=-1, keepdims=True)
    y = x * lax.rsqrt(ms + EPS) * g1_ref[...]
    h = y * (1.0 + m[1:2]) + m[0:1]
    z = _dot(h.astype(BF16), win_s[...])

    qk = z[:, :QK_W]
    ss = _dot((qk * qk).astype(BF16), heads_ref[...])
    hi, lo = _split_bf16(lax.rsqrt(ss * (1.0 / HEAD_DIM) + EPS))
    qk = qk * (_dot(hi, heads_t_ref[...]) + _dot(lo, heads_t_ref[...])) * gqk_ref[...]

    lane = lax.broadcasted_iota(jnp.int32, (x.shape[0], LANES), 1)
    cols = []
    for c in range(QK_W // LANES):
        xc = qk[:, c * LANES:(c + 1) * LANES]
        if rope:
            left = pltpu.roll(xc, LANES - HEAD_DIM // 4, 1)
            right = pltpu.roll(xc, HEAD_DIM // 4, 1)
            rot = jnp.where((lane // (HEAD_DIM // 4)) % 2 == 0, left, right)
            xc = xc * cos_ref[...] + rot * sin_ref[...]
        cols.append(xc)
    for c in range(ATTN_W // LANES):
        q_ref[:, c * LANES:(c + 1) * LANES] = (cols[c] * Q_SCALE).astype(BF16)

    k_col = cols[ATTN_W // LANES]
    v_col = z[:, QK_W:QK_W + KV_W]
    _store_expanded(kx_ref, k_col, 0.0)
    _store_expanded(vx_ref, v_col, 1.0)
    if not rope:
        _store_cache(kf_ref, k_col)
        _store_cache(vf_ref, v_col)

    xr_ref[...] = z[:, QK_W + KV_W:QK_W + KV_W + LRU_W]
    gb_ref[...] = z[:, QK_W + KV_W + LRU_W:]


def _pre(x, mod3, mod_row, g1, w_in, heads, gqk, tables, seq_len):
    n = x.shape[0]
    tm = TM_PRE
    tiles_per_seq = seq_len // tm
    rope = tables is not None
    const = lambda i: (0, 0)
    in_specs = [pl.BlockSpec((tm, D_MODEL), lambda i: (i, 0)),
                pl.BlockSpec((1, 6, D_MODEL), lambda i: (mod_row(i * tm), 0, 0)),
                pl.BlockSpec((1, D_MODEL), const),
                pl.BlockSpec((D_MODEL, IN_W), const),
                pl.BlockSpec((QK_W, LANES), const),
                pl.BlockSpec((LANES, QK_W), const),
                pl.BlockSpec((1, QK_W), const)]
    args = [x, mod3, g1, w_in, heads, heads.T, gqk]
    if rope:
        in_specs += [pl.BlockSpec((tm, LANES), lambda i: (i % tiles_per_seq, 0))] * 2
        args += list(tables)
    row = lambda w: pl.BlockSpec((tm, w), lambda i: (i, 0))
    out_shape = [jax.ShapeDtypeStruct((n, ATTN_W), BF16),
                 jax.ShapeDtypeStruct((n, 4 * LANES), BF16),
                 jax.ShapeDtypeStruct((n, 4 * LANES), BF16),
                 jax.ShapeDtypeStruct((n, LRU_W), F32),
                 jax.ShapeDtypeStruct((n, LRU_W), F32)]
    out_specs = [row(ATTN_W), row(4 * LANES), row(4 * LANES), row(LRU_W), row(LRU_W)]
    if not rope:
        assert tm % seq_len == 0
        per_tile = tm // seq_len
        cache = pl.BlockSpec((per_tile, N_KV_HEADS, HEAD_DIM, seq_len), lambda i: (i, 0, 0, 0))
        out_shape += [jax.ShapeDtypeStruct((n // seq_len, N_KV_HEADS, HEAD_DIM, seq_len), F32)] * 2
        out_specs += [cache, cache]
    return pl.pallas_call(
        functools.partial(_pre_kernel, rope=rope),
        grid=(n // tm,),
        in_specs=in_specs, out_specs=out_specs, out_shape=out_shape,
        scratch_shapes=[pltpu.VMEM((D_MODEL, IN_W), BF16)],
        compiler_params=_cparams(("arbitrary",)),
        name="pre_rope" if rope else "pre",
    )(*args)


def _attn_kernel(*refs, n_seg):
    q_ref = refs[0]
    k_refs = refs[1:1 + n_seg]
    v_refs = refs[1 + n_seg:1 + 2 * n_seg]
    o_ref = refs[1 + 2 * n_seg]
    lane = lax.broadcasted_iota(jnp.int32, (q_ref.shape[0], LANES), 1)
    for c in range(ATTN_W // LANES):
        qc = q_ref[:, c * LANES:(c + 1) * LANES]
        g = c // 2
        accs = []
        for par in range(2):
            sl = slice((2 * g + par) * LANES, (2 * g + par + 1) * LANES)
            ss = [_dot_nt(qc, k[:, sl]) for k in k_refs]
            mx = functools.reduce(jnp.maximum, [jnp.max(s, axis=-1, keepdims=True) for s in ss])
            ps = [jnp.exp2((s - mx).astype(BF16)) for s in ss]
            accs.append(functools.reduce(lambda a, b: a + b,
                                         [_dot(p, v[:, sl]) for p, v in zip(ps, v_refs)]))
        even = accs[0] / accs[0][:, ONES_LANE_EVEN:ONES_LANE_EVEN + 1]
        odd = accs[1] / accs[1][:, ONES_LANE_ODD:ONES_LANE_ODD + 1]
        o_ref[:, c * LANES:(c + 1) * LANES] = jnp.where(lane < HEAD_DIM, even, odd).astype(BF16)


def _attention(q, k_segs, v_segs, seq_len):
    n = q.shape[0]
    tq = min(TQ_ATT, seq_len)
    nq = seq_len // tq
    n_seg = len(k_segs)
    in_specs = [pl.BlockSpec((tq, ATTN_W), lambda b, i: (b * nq + i, 0))]
    for arr, t in list(k_segs) + list(v_segs):
        in_specs.append(pl.BlockSpec((t, 4 * LANES), lambda b, i: (b, 0)))
    return pl.pallas_call(
        functools.partial(_attn_kernel, n_seg=n_seg),
        grid=(n // seq_len, nq),
        in_specs=in_specs,
        out_specs=pl.BlockSpec((tq, ATTN_W), lambda b, i: (b * nq + i, 0)),
        out_shape=jax.ShapeDtypeStruct((n, ATTN_W), BF16),
        compiler_params=_cparams(("arbitrary", "arbitrary")),
        name="attention_%dseg" % n_seg,
    )(q, *[a for a, _ in k_segs], *[a for a, _ in v_segs])


def _log_sigmoid(x):
    return jnp.minimum(x, 0.0) - jnp.log1p(jnp.exp(-jnp.abs(x)))


def _tile_scan(a, b, reverse):
    row = lax.broadcasted_iota(jnp.int32, a.shape, 0)
    d = 1
    while d < SUBLANES:
        if reverse:
            keep = row < SUBLANES - d
            shift = SUBLANES - d
        else:
            keep = row >= d
            shift = d
        a_sh = jnp.where(keep, pltpu.roll(a, shift, 0), 1.0)
        b_sh = jnp.where(keep, pltpu.roll(b, shift, 0), 0.0)
        b = a * b_sh + b
        a = a * a_sh
        d *= 2
    return a, b


def _lru_kernel(xr_ref, gb_ref, h0_ref, cw_ref, cb_ref, wa_ref, wx_ref, ba_ref, bx_ref, lam_ref,
                rec_ref, fin_ref, xpad_s, xc_s, hf_s, a_s, b_s, wf_ref, wb_ref, bf_ref, bb_ref,
                *, seq_len):
    @pl.when(pl.program_id(0) == 0)
    def _():
        pairs = LANES // LRU_BLOCK_W
        zero = jnp.zeros((LRU_BLOCK_W, LRU_BLOCK_W), F32)
        for d, (w_s, bias_s) in enumerate(((wf_ref, bf_ref), (wb_ref, bb_ref))):
            w_s[...] = jnp.zeros_like(w_s)
            for g, src in enumerate((wa_ref, wx_ref)):
                for p in range(LRU_BLOCKS // pairs):
                    rows = [jnp.concatenate([0.5 * src[d, pairs * p + q] if q == r else zero
                                             for q in range(pairs)], axis=1) for r in range(pairs)]
                    w_s[LANES * p:LANES * (p + 1), g * LRU_W + LANES * p:g * LRU_W + LANES * (p + 1)] = (
                        jnp.concatenate(rows, axis=0).astype(BF16))
            bias_s[:, :LRU_W] = 0.5 * ba_ref[d:d + 1, :]
            bias_s[:, LRU_W:] = 0.5 * bx_ref[d:d + 1, :]

    tc = min(TC_LRU, seq_len)
    n_chunks = seq_len // tc
    n_tiles = tc // SUBLANES
    zpad = jnp.zeros((SUBLANES, LRU_W), F32)
    xpad_s[0:SUBLANES, :] = zpad
    xpad_s[SUBLANES:SUBLANES + seq_len, :] = xr_ref[...]
    xpad_s[SUBLANES + seq_len:2 * SUBLANES + seq_len, :] = zpad

    half_cl = (0.5 * LRU_C * LOG2_E) * _log_sigmoid(lam_ref[...])

    def gates(xcc, w_ref, bias_ref, half_cl_d):
        t = jnp.tanh(_dot(xcc.astype(BF16), w_ref[...]) + bias_ref[...])
        a = jnp.exp2(t[:, :LRU_W] * half_cl_d + half_cl_d)
        half_x = 0.5 * xcc
        a_s[...] = a
        u = 1.0 - a * a
        b_s[...] = jnp.where(u > 0.0, u * lax.rsqrt(u), 0.0) * (t[:, LRU_W:] * half_x + half_x)

    h = h0_ref[0, 0:1, :]
    for c in range(n_chunks):
        base = c * tc
        xcc = cb_ref[...] + functools.reduce(
            lambda u, v: u + v,
            [cw_ref[j:j + 1, :] * xpad_s[base + SUBLANES - 1 + j:base + SUBLANES - 1 + j + tc, :]
             for j in range(CONV_W)])
        xc_s[base:base + tc, :] = xcc
        gates(xcc, wf_ref, bf_ref, half_cl[0:1])

        def fwd_tile(t, hc, base=base):
            r0 = pl.multiple_of(t * SUBLANES, SUBLANES)
            ca, cb = _tile_scan(a_s[pl.ds(r0, SUBLANES), :], b_s[pl.ds(r0, SUBLANES), :], False)
            hh = ca * hc + cb
            hf_s[pl.ds(base + r0, SUBLANES), :] = hh
            return hh[SUBLANES - 1:SUBLANES, :]

        h = lax.fori_loop(0, n_tiles, fwd_tile, h, unroll=4)
    fin_ref[0, 0:1, :] = h

    h = h0_ref[0, 1:2, :]
    for c in reversed(range(n_chunks)):
        base = c * tc
        gates(xc_s[base:base + tc, :], wb_ref, bb_ref, half_cl[1:2])

        def bwd_tile(t, hc, base=base):
            r0 = pl.multiple_of((n_tiles - 1 - t) * SUBLANES, SUBLANES)
            ca, cb = _tile_scan(a_s[pl.ds(r0, SUBLANES), :], b_s[pl.ds(r0, SUBLANES), :], True)
            hh = ca * hc + cb
            gate = jax.nn.gelu(gb_ref[pl.ds(base + r0, SUBLANES), :], approximate=True)
            rec_ref[pl.ds(base + r0, SUBLANES), :] = (
                (hf_s[pl.ds(base + r0, SUBLANES), :] + hh) * gate).astype(rec_ref.dtype)
            return hh[0:1, :]

        h = lax.fori_loop(0, n_tiles, bwd_tile, h, unroll=4)
    fin_ref[0, 1:2, :] = h


def _lru(xr, gb, h0, conv_w, conv_b, wa, wx, ba, bx, lam, seq_len):
    n = xr.shape[0]
    batch = n // seq_len
    const = lambda b: (0, 0)
    blocks = pl.BlockSpec((2, LRU_BLOCKS, LRU_BLOCK_W, LRU_BLOCK_W), lambda b: (0, 0, 0, 0))
    seq = pl.BlockSpec((seq_len, LRU_W), lambda b: (b, 0))
    st = pl.BlockSpec((1, 2, LRU_W), lambda b: (b, 0, 0))
    return pl.pallas_call(
        functools.partial(_lru_kernel, seq_len=seq_len),
        grid=(batch,),
        in_specs=[seq, seq, st,
                  pl.BlockSpec((CONV_W, LRU_W), const), pl.BlockSpec((1, LRU_W), const),
                  blocks, blocks,
                  pl.BlockSpec((2, LRU_W), const), pl.BlockSpec((2, LRU_W), const),
                  pl.BlockSpec((2, LRU_W), const)],
        out_specs=[seq, st],
        out_shape=[jax.ShapeDtypeStruct((n, LRU_W), BF16),
                   jax.ShapeDtypeStruct((batch, 2, LRU_W), F32)],
        scratch_shapes=[pltpu.VMEM((seq_len + 2 * SUBLANES, LRU_W), F32),
                        pltpu.VMEM((seq_len, LRU_W), F32),
                        pltpu.VMEM((seq_len, LRU_W), F32),
                        pltpu.VMEM((min(TC_LRU, seq_len), LRU_W), F32),
                        pltpu.VMEM((min(TC_LRU, seq_len), LRU_W), F32),
                        pltpu.VMEM((LRU_W, 2 * LRU_W), BF16),
                        pltpu.VMEM((LRU_W, 2 * LRU_W), BF16),
                        pltpu.VMEM((1, 2 * LRU_W), F32),
                        pltpu.VMEM((1, 2 * LRU_W), F32)],
        compiler_params=_cparams(("arbitrary",)),
        name="lru_%d" % seq_len,
    )(xr, gb, h0, conv_w, conv_b, wa, wx, ba, bx, lam)


def _post_kernel(attn_ref, rec_ref, x_ref, mod_ref, g2_ref, wo_ref, wr2_ref, br_ref,
                 tri_ref, cnt_in_ref, x1_ref, h2_ref, route_ref, cnt_ref, route_t_ref, cnt_s, wo_s):
    @pl.when(pl.program_id(0) == 0)
    def _():
        cnt_s[...] = cnt_in_ref[...]
        wo_s[...] = wo_ref[...].astype(BF16)

    m = mod_ref[0]
    u = _dot(attn_ref[...], wo_s[:ATTN_W, :]) + _dot(rec_ref[...], wo_s[ATTN_W:, :])
    x1 = x_ref[...] + m[2:3] * u
    x1_ref[...] = x1
    ms = jnp.mean(x1 * x1, axis=-1, keepdims=True)
    h2 = x1 * lax.rsqrt(ms + EPS) * g2_ref[...]
    h2 = h2 * (1.0 + m[4:5]) + m[3:4]
    hi, lo = _split_bf16(h2)
    h2_ref[...] = _to_row_tiles(hi)

    hw = _dot(hi, wr2_ref[...])
    logits = hw[:, :LANES] + hw[:, LANES:] + _dot(lo, wr2_ref[:, :LANES]) + br_ref[...]
    lane_i = lax.broadcasted_iota(jnp.int32, logits.shape, 1)
    lane = lane_i.astype(F32)
    lane_group = (lane_i // EXPERTS_PER_GROUP).astype(F32)
    neg = -jnp.inf
    big = float(1 << 20)
    gmask = (lane_i >= N_EXPERTS) & (lane_i < N_EXPERTS + N_GROUPS)
    gl = jnp.where(gmask, logits, neg)
    gmax = jnp.max(gl, axis=-1, keepdims=True)
    gidx = jnp.min(jnp.where(gl == gmax, lane - N_EXPERTS, big), axis=-1, keepdims=True)
    p_sel = 1.0 / jnp.sum(jnp.where(gmask, jnp.exp(gl - gmax), 0.0), axis=-1, keepdims=True)

    emask = (lane_i < N_EXPERTS) & (lane_group == gidx)
    el = jnp.where(emask, logits, neg)
    v1 = jnp.max(el, axis=-1, keepdims=True)
    i1 = jnp.min(jnp.where(el == v1, lane, big), axis=-1, keepdims=True)
    el2 = jnp.where(lane == i1, neg, el)
    v2 = jnp.max(el2, axis=-1, keepdims=True)
    i2 = jnp.min(jnp.where(el2 == v2, lane, big), axis=-1, keepdims=True)
    e2 = jnp.exp(v2 - v1)
    w1 = p_sel / (1.0 + e2)
    w2 = p_sel * e2 / (1.0 + e2)

    oh1 = lane == i1
    oh2 = lane == i2
    oh = jnp.where(oh1, 1.0, 0.0) + jnp.where(oh2, 1.0, 0.0)
    before = _dot(tri_ref[...], oh.astype(BF16)) + cnt_s[...]
    rank1 = jnp.sum(jnp.where(oh1, before, 0.0), axis=-1, keepdims=True)
    rank2 = jnp.sum(jnp.where(oh2, before, 0.0), axis=-1, keepdims=True)
    cnt = cnt_s[...] + jnp.sum(oh, axis=0, keepdims=True)
    cnt_s[...] = cnt
    cnt_ref[...] = cnt
    fields = (i1, i2, rank1, rank2, w1, w2)
    route = jnp.zeros(logits.shape, F32)
    for k, val in enumerate(fields):
        route = jnp.where(lane_i == k, val, route)
    route_ref[...] = route
    route_t_ref[...] = jnp.transpose(route)[:SUBLANES, :]


def _post(attn, rec, x, mod3, mod_row, g2, w_out, wr2, br, tri, cnt_in):
    n = x.shape[0]
    tm = TM_PRE
    const = lambda i: (0, 0)
    row = lambda w: pl.BlockSpec((tm, w), lambda i: (i, 0))
    return pl.pallas_call(
        _post_kernel,
        grid=(n // tm,),
        in_specs=[row(ATTN_W), row(LRU_W), row(D_MODEL),
                  pl.BlockSpec((1, 6, D_MODEL), lambda i: (mod_row(i * tm), 0, 0)),
                  pl.BlockSpec((1, D_MODEL), const),
                  pl.BlockSpec((D_MODEL, D_MODEL), const),
                  pl.BlockSpec((D_MODEL, 2 * LANES), const),
                  pl.BlockSpec((1, LANES), const),
                  pl.BlockSpec((tm, tm), const),
                  pl.BlockSpec((1, LANES), const)],
        out_specs=[row(D_MODEL), pl.BlockSpec((tm,) + ROW_TILE, lambda i: (i, 0, 0)), row(LANES),
                   pl.BlockSpec((1, LANES), const),
                   pl.BlockSpec((SUBLANES, tm), lambda i: (0, i))],
        out_shape=[jax.ShapeDtypeStruct((n, D_MODEL), F32),
                   jax.ShapeDtypeStruct((n,) + ROW_TILE, BF16),
                   jax.ShapeDtypeStruct((n, LANES), F32),
                   jax.ShapeDtypeStruct((1, LANES), F32),
                   jax.ShapeDtypeStruct((SUBLANES, n), F32)],
        scratch_shapes=[pltpu.VMEM((1, LANES), F32), pltpu.VMEM((D_MODEL, D_MODEL), BF16)],
        compiler_params=_cparams(("arbitrary",)),
        name="post",
    )(attn, rec, x, mod3, g2, w_out, wr2, br, tri, cnt_in)


def _row_copy(src_ref, src_row, dst_ref, dst_row, sem):
    return pltpu.make_async_copy(src_ref.at[pl.ds(src_row, 1)], dst_ref.at[pl.ds(dst_row, 1)], sem)


def _plan_kernel(route_t_ref, seg_ref, slots_ref):
    route_t = route_t_ref[...]
    t = route_t.shape[1]
    seg = jnp.concatenate([seg_ref[...]] * (t // LANES), axis=1)
    expert = lax.broadcasted_iota(jnp.int32, seg.shape, 0).astype(F32)
    rows = []
    for k in range(TOP_K):
        start = jnp.sum(jnp.where(expert == route_t[k:k + 1, :], seg, 0.0), axis=0, keepdims=True)
        rows.append(start + route_t[TOP_K + k:TOP_K + k + 1, :])
    rows.append(jnp.zeros((SUBLANES - TOP_K, t), F32))
    slots_ref[...] = jnp.concatenate(rows, axis=0).astype(jnp.int32)


def _plan(route_t, seg_rows):
    n = route_t.shape[1]
    tm = TM_PRE
    return pl.pallas_call(
        _plan_kernel,
        grid=(n // tm,),
        in_specs=[pl.BlockSpec((SUBLANES, tm), lambda i: (0, i)),
                  pl.BlockSpec((N_EXPERTS, LANES), lambda i: (0, 0))],
        out_specs=pl.BlockSpec((SUBLANES, tm), lambda i: (0, i)),
        out_shape=jax.ShapeDtypeStruct((SUBLANES, n), jnp.int32),
        compiler_params=_cparams(("arbitrary",)),
        name="plan",
    )(route_t, seg_rows)


def _dispatch_kernel(slots_p_ref, slots_s_ref, pad_start_ref, pad_rows_ref, nu_ref, h2p_ref, h2s_ref,
                     xs_ref, zero_s, sem, zsem, *, n_prompt):
    i = pl.program_id(0)
    ch = CH_DISPATCH

    def zero_fill(op):
        for e in range(N_EXPERTS):
            rows = pad_rows_ref[e]
            for b in range(TR_MOE.bit_length() - 1):
                size = 1 << b
                off = (rows >> (b + 1)) << (b + 1)

                @pl.when(((rows >> b) & 1) == 1)
                def _():
                    op(pltpu.make_async_copy(zero_s.at[pl.ds(0, size)],
                                             xs_ref.at[pl.ds(pad_start_ref[e] + off, size)], zsem))

        def zero_tile(t, carry):
            op(pltpu.make_async_copy(
                zero_s, xs_ref.at[pl.ds(pl.multiple_of(t * TR_MOE, TR_MOE), TR_MOE)], zsem))
            return carry
        lax.fori_loop(nu_ref[0], xs_ref.shape[0] // TR_MOE, zero_tile, 0)

    @pl.when(i == 0)
    def _():
        zero_s[...] = jnp.zeros_like(zero_s)
        zero_fill(lambda cp: cp.start())

    @pl.when(i == pl.num_programs(0) - 1)
    def _():
        zero_fill(lambda cp: cp.wait())

    def scatter(src_ref, slots_ref, first_token):
        n_path = slots_ref.shape[0] // TOP_K

        def body(j, carry):
            for k in range(TOP_K):
                _row_copy(src_ref, j, xs_ref, slots_ref[k * n_path + i * ch + j - first_token],
                          sem).start(priority=k)
            return carry
        lax.fori_loop(0, ch, body, 0, unroll=8)
        for _ in range(TOP_K):
            pltpu.make_async_copy(src_ref, xs_ref.at[pl.ds(0, ch)], sem).wait()

    @pl.when(i < n_prompt // ch)
    def _():
        scatter(h2p_ref, slots_p_ref, 0)

    @pl.when(i >= n_prompt // ch)
    def _():
        scatter(h2s_ref, slots_s_ref, n_prompt)


def _dispatch(slots_p, slots_s, pad_start, pad_rows, n_used, h2p, h2s, n_rows):
    n_prompt = h2p.shape[0]
    n = n_prompt + h2s.shape[0]
    ch = CH_DISPATCH
    npc = n_prompt // ch
    last_p = npc - 1
    return pl.pallas_call(
        functools.partial(_dispatch_kernel, n_prompt=n_prompt),
        grid_spec=pltpu.PrefetchScalarGridSpec(
            num_scalar_prefetch=5,
            grid=(n // ch,),
            in_specs=[pl.BlockSpec((ch,) + ROW_TILE, lambda i, *_: (jnp.minimum(i, last_p), 0, 0)),
                      pl.BlockSpec((ch,) + ROW_TILE, lambda i, *_: (jnp.maximum(i - npc, 0), 0, 0))],
            out_specs=pl.BlockSpec(memory_space=pl.ANY),
            scratch_shapes=[pltpu.VMEM((TR_MOE,) + ROW_TILE, BF16), pltpu.SemaphoreType.DMA,
                            pltpu.SemaphoreType.DMA]),
        out_shape=jax.ShapeDtypeStruct((n_rows,) + ROW_TILE, BF16),
        compiler_params=_cparams(("arbitrary",)),
        name="dispatch",
    )(slots_p, slots_s, pad_start, pad_rows, n_used, h2p, h2s)


def _ffn_kernel(tend_ref, xs_ref, wg_ref, wu_ref, wd_ref, ys_ref,
                xbuf, ybuf, wg_s, wu_s, wd_s, sem_in, sem_out):
    e = pl.program_id(0)
    n_used = tend_ref[N_EXPERTS - 1]
    t_first = jnp.where(e == 0, 0, tend_ref[jnp.maximum(e - 1, 0)])
    t_last = tend_ref[e]
    n_in = xbuf.shape[0]
    n_out = ybuf.shape[0]

    def tile_rows(t):
        return pl.ds(pl.multiple_of(t * TR_MOE, TR_MOE), TR_MOE)

    def fetch(t):
        return pltpu.make_async_copy(xs_ref.at[tile_rows(t)], xbuf.at[t % n_in], sem_in.at[t % n_in])

    def writeback(t):
        return pltpu.make_async_copy(ybuf.at[t % n_out], ys_ref.at[tile_rows(t)],
                                     sem_out.at[t % n_out])

    @pl.when(e == 0)
    def _():
        for t in range(n_in - 1):
            @pl.when(t < n_used)
            def _():
                fetch(t).start()

    @pl.when(t_last > t_first)
    def _():
        wg_s[...] = wg_ref[0].astype(BF16)
        wu_s[...] = wu_ref[0].astype(BF16)
        wd_s[...] = wd_ref[0].astype(BF16)

    def tile(t, carry):
        fetch(t).wait()

        @pl.when(t + n_in - 1 < n_used)
        def _():
            fetch(t + n_in - 1).start()

        @pl.when(t >= n_out)
        def _():
            writeback(t - n_out).wait()

        x = _from_row_tiles(xbuf[t % n_in])
        hg = _dot(x, wg_s[...])
        hu = _dot(x, wu_s[...])
        act = (hg * _sigmoid(hg)) * hu
        ybuf[t % n_out] = _to_row_tiles(_dot(act.astype(BF16), wd_s[...]).astype(BF16))
        writeback(t).start()
        return carry

    lax.fori_loop(t_first, t_last, tile, 0)

    @pl.when(e == N_EXPERTS - 1)
    def _():
        for back in range(n_out, 0, -1):
            @pl.when(n_used >= back)
            def _():
                writeback(n_used - back).wait()
        n_all = ys_ref.shape[0] // TR_MOE
        ybuf[0] = jnp.zeros(ybuf.shape[1:], ybuf.dtype)

        def zero_tile(t):
            return pltpu.make_async_copy(ybuf.at[0], ys_ref.at[tile_rows(t)], sem_out.at[0])

        lax.fori_loop(n_used, n_all, lambda t, c: (zero_tile(t).start(), c)[1], 0)
        lax.fori_loop(n_used, n_all, lambda t, c: (zero_tile(t).wait(), c)[1], 0)


def _ffn(tile_end, xs, wg, wu, wd, n_tiles):
    tr = TR_MOE
    wsel = lambda e, tend: (e, 0, 0)
    return pl.pallas_call(
        _ffn_kernel,
        grid_spec=pltpu.PrefetchScalarGridSpec(
            num_scalar_prefetch=1,
            grid=(N_EXPERTS,),
            in_specs=[pl.BlockSpec(memory_space=pl.ANY),
                      pl.BlockSpec((1, D_MODEL, EXPERT_FF), wsel),
                      pl.BlockSpec((1, D_MODEL, EXPERT_FF), wsel),
                      pl.BlockSpec((1, EXPERT_FF, D_MODEL), wsel)],
            out_specs=pl.BlockSpec(memory_space=pl.ANY),
            scratch_shapes=[pltpu.VMEM((FFN_IN_SLOTS, tr) + ROW_TILE, BF16),
                            pltpu.VMEM((FFN_OUT_SLOTS, tr) + ROW_TILE, BF16),
                            pltpu.VMEM((D_MODEL, EXPERT_FF), BF16),
                            pltpu.VMEM((D_MODEL, EXPERT_FF), BF16),
                            pltpu.VMEM((EXPERT_FF, D_MODEL), BF16),
                            pltpu.SemaphoreType.DMA((FFN_IN_SLOTS,)),
                            pltpu.SemaphoreType.DMA((FFN_OUT_SLOTS,))]),
        out_shape=jax.ShapeDtypeStruct((n_tiles * tr,) + ROW_TILE, BF16),
        compiler_params=_cparams(("arbitrary",)),
        name="ffn",
    )(tile_end, xs, wg, wu, wd)


def _combine_kernel(slots_ref, ys_ref, x1_ref, route_ref, mod_ref, y_ref, b1_s, b2_s, sems):
    i = pl.program_id(0)
    tm = x1_ref.shape[0]
    n_path = slots_ref.shape[0] // TOP_K

    n_steps = pl.num_programs(0)
    ch = COMBINE_CHUNK

    def gather_chunk(step, slot, c):
        for t in range(ch):
            j = c * ch + t
            _row_copy(ys_ref, slots_ref[step * tm + j], b1_s.at[slot], j, sems.at[slot]).start(
                priority=0)
            _row_copy(ys_ref, slots_ref[n_path + step * tm + j], b2_s.at[slot], j,
                      sems.at[slot]).start(priority=1)

    def wait_slot(slot):
        for buf in (b1_s, b2_s):
            pltpu.make_async_copy(ys_ref.at[pl.ds(0, tm)], buf.at[slot], sems.at[slot]).wait()

    @pl.when(i == 0)
    def _():
        lax.fori_loop(0, tm // ch, lambda c, carry: (gather_chunk(0, 0, c), carry)[1], 0)

    slot = i % 2
    wait_slot(slot)
    gate = mod_ref[0][5:6]
    nxt = (i + 1) % n_steps

    def body(c, carry):
        rows = pl.ds(pl.multiple_of(c * ch, ch), ch)
        route = route_ref[rows, :]
        moe = (route[:, 4:5] * _from_row_tiles(b1_s[slot, rows]).astype(F32)
               + route[:, 5:6] * _from_row_tiles(b2_s[slot, rows]).astype(F32))
        y_ref[rows, :] = x1_ref[rows, :] + gate * moe
        gather_chunk(nxt, 1 - slot, c)
        return carry
    lax.fori_loop(0, tm // ch, body, 0)

    @pl.when(i == n_steps - 1)
    def _():
        wait_slot(1 - slot)


def _combine(slots, ys, x1, route, mod3, mod_row):
    n = x1.shape[0]
    tm = TM_PRE
    row = lambda w: pl.BlockSpec((tm, w), lambda i, *_: (i, 0))
    return pl.pallas_call(
        _combine_kernel,
        grid_spec=pltpu.PrefetchScalarGridSpec(
            num_scalar_prefetch=1,
            grid=(n // tm,),
            in_specs=[pl.BlockSpec(memory_space=pl.ANY), row(D_MODEL), row(LANES),
                      pl.BlockSpec((1, 6, D_MODEL),
                                   lambda i, *_: (mod_row(i * tm), 0, 0))],
            out_specs=row(D_MODEL),
            scratch_shapes=[pltpu.VMEM((2, tm) + ROW_TILE, BF16), pltpu.VMEM((2, tm) + ROW_TILE, BF16),
                            pltpu.SemaphoreType.DMA((2,))]),
        out_shape=jax.ShapeDtypeStruct((n, D_MODEL), F32),
        compiler_params=_cparams(("arbitrary",)),
        name="combine",
    )(slots, ys, x1, route, mod3)


def _rope_tables(length):
    rows = length // GRID_W
    r, col = jnp.meshgrid(jnp.arange(rows), jnp.arange(GRID_W), indexing='ij')
    r = r.reshape(-1).astype(F32)
    col = col.reshape(-1).astype(F32)
    half = HEAD_DIM // 2
    inv = ROPE_THETA ** (-jnp.arange(0, half, 2, dtype=F32) / half)
    ang_r = r[:, None] * inv
    ang_c = col[:, None] * inv
    ang = jnp.concatenate([ang_r, ang_r, ang_c, ang_c], axis=-1)
    sign = jnp.where((jnp.arange(HEAD_DIM) // (HEAD_DIM // 4)) % 2 == 0, -1.0, 1.0).astype(F32)
    cos = jnp.tile(jnp.cos(ang), (1, LANES // HEAD_DIM))
    sin = jnp.tile(jnp.sin(ang) * sign, (1, LANES // HEAD_DIM))
    return cos, sin


def _expand_cache_kernel(k_ref, v_ref, kx_ref, vx_ref):
    for ref, xref, one in ((k_ref, kx_ref, 0.0), (v_ref, vx_ref, 1.0)):
        col = jnp.transpose(ref[0].reshape(KV_W, ref.shape[-1]))
        _store_expanded(xref, col, one)


def _expand_cache(cache_k, cache_v):
    b, _, _, t = cache_k.shape
    src = pl.BlockSpec((1, N_KV_HEADS, HEAD_DIM, t), lambda i: (i, 0, 0, 0))
    dst = pl.BlockSpec((t, 4 * LANES), lambda i: (i, 0))
    return pl.pallas_call(
        _expand_cache_kernel,
        grid=(b,),
        in_specs=[src, src], out_specs=[dst, dst],
        out_shape=[jax.ShapeDtypeStruct((b * t, 4 * LANES), BF16)] * 2,
        compiler_params=_cparams(("arbitrary",)),
        name="expand_cache",
    )(cache_k, cache_v)


def kernel(x_prompt, x_sample, cache_k, cache_v, state_lru, c, c_ctx, w_mod, b_mod, norm1, norm2,
           w_in, q_norm, k_norm, conv_w, conv_b, lru_wa, lru_ba, lru_wx, lru_bx, lru_lambda, w_out,
           router_grp_w, router_grp_b, router_exp_w, router_exp_b, exp_w_gate, exp_w_up, exp_w_down):
    batch, seq, _ = x_prompt.shape
    dec_batch, dec_seq, _ = x_sample.shape
    past = cache_k.shape[2]
    depth = w_mod.shape[0]
    assert depth == 1

    cvec = jnp.concatenate(
        [c_ctx[None, :], c, jnp.zeros((MOD_ROWS - 1 - dec_batch, D_MODEL), F32)], axis=0)
    mod3 = _modulation(cvec, w_mod[0], b_mod[0][None, :]).reshape(MOD_ROWS, 6, D_MODEL)

    head_id = jnp.arange(QK_W) // HEAD_DIM
    heads = (head_id[:, None] == jnp.arange(LANES)[None, :]).astype(BF16)
    gqk = jnp.concatenate([jnp.tile(q_norm[0], N_HEADS), jnp.tile(k_norm[0], N_KV_HEADS)])[None, :]
    pad = LANES - N_EXPERTS - N_GROUPS
    wr = jnp.concatenate([router_exp_w[0], router_grp_w[0], jnp.zeros((D_MODEL, pad), F32)], axis=1)
    wr_hi = wr.astype(BF16)
    wr2 = jnp.concatenate([wr_hi, (wr - wr_hi.astype(F32)).astype(BF16)], axis=1)
    br = jnp.concatenate([router_exp_b[0], router_grp_b[0], jnp.zeros((pad,), F32)])[None, :]
    g1 = norm1[0][None, :]
    g2 = norm2[0][None, :]
    cw = conv_w[0]
    cb = conv_b[0][None, :]
    lam = lru_lambda[0]
    tri = (jnp.arange(TM_PRE)[:, None] > jnp.arange(TM_PRE)[None, :]).astype(BF16)

    def mixers(x, seq_len, mod_row, tables, extra_k, extra_v, h0, cnt_in):
        q, kx, vx, xr, gb, *cache = _pre(x, mod3, mod_row, g1, w_in[0], heads, gqk, tables, seq_len)
        k_segs = [(kx, seq_len)] + extra_k
        v_segs = [(vx, seq_len)] + extra_v
        attn = _attention(q, k_segs, v_segs, seq_len)
        rec, fin = _lru(xr, gb, h0, cw, cb, lru_wa[0], lru_wx[0], lru_ba[0], lru_bx[0], lam, seq_len)
        x1, h2, route, cnt, route_t = _post(attn, rec, x, mod3, mod_row, g2, w_out[0], wr2, br, tri,
                                            cnt_in)
        return x1, h2, (route, route_t), cnt, cache, fin

    mod_row_p = lambda tok: 0
    mod_row_s = lambda tok: tok // dec_seq + 1
    xp = x_prompt.reshape(batch * seq, D_MODEL)
    x1p, h2p, (route_p, route_t_p), cnt_p, (kf, vf), fin = mixers(
        xp, seq, mod_row_p, None, [], [], jnp.zeros((batch, 2, LRU_W), F32),
        jnp.zeros((1, LANES), F32))
    xs = x_sample.reshape(dec_batch * dec_seq, D_MODEL)
    ck, cv = _expand_cache(jnp.transpose(cache_k[:, 0], (0, 2, 3, 1)),
                           jnp.transpose(cache_v[:, 0], (0, 2, 3, 1)))
    x1s, h2s, (route_s, route_t_s), cnt_all, _, _ = mixers(
        xs, dec_seq, mod_row_s, _rope_tables(dec_seq), [(ck, past)], [(cv, past)],
        state_lru[:, 0], cnt_p)

    n_prompt = batch * seq
    n_tok = n_prompt + dec_batch * dec_seq
    n_tiles = (TOP_K * n_tok + N_EXPERTS * (TR_MOE - 1)) // TR_MOE
    cnt = cnt_all[0, :N_EXPERTS].astype(jnp.int32)
    ntile = (cnt + TR_MOE - 1) // TR_MOE
    tile_end = jnp.cumsum(ntile)
    seg_start = (tile_end - ntile) * TR_MOE
    n_used = tile_end[-1:]
    pad_start = seg_start + cnt
    pad_rows = tile_end * TR_MOE - pad_start

    seg_rows = jnp.broadcast_to(seg_start.astype(F32)[:, None], (N_EXPERTS, LANES))
    slots_p = _plan(route_t_p, seg_rows)[:TOP_K].reshape(-1)
    slots_s = _plan(route_t_s, seg_rows)[:TOP_K].reshape(-1)
    xsort = _dispatch(slots_p, slots_s, pad_start, pad_rows, n_used, h2p, h2s,
                      n_tiles * TR_MOE)
    ysort = _ffn(tile_end, xsort, exp_w_gate[0], exp_w_up[0], exp_w_down[0], n_tiles)
    yp = _combine(slots_p, ysort, x1p, route_p, mod3, mod_row_p)
    ys = _combine(slots_s, ysort, x1s, route_s, mod3, mod_row_s)

    return (yp.reshape(batch, seq, D_MODEL),
            ys.reshape(dec_batch, dec_seq, D_MODEL),
            jnp.transpose(kf, (0, 3, 1, 2))[:, None],
            jnp.transpose(vf, (0, 3, 1, 2))[:, None],
            fin.reshape(batch, 1, 2, LRU_W))
```

```python
import functools

import jax
import jax.numpy as jnp
from jax import lax
from jax.experimental import pallas as pl
from jax.experimental.pallas import tpu as pltpu

F32 = jnp.float32
BF16 = jnp.bfloat16

D_MODEL = 1024
GRID_W = 64
ATTN_W = 512
LRU_W = 512
HEAD_DIM = 64
N_HEADS = 8
N_KV_HEADS = 2
KV_W = N_KV_HEADS * HEAD_DIM
LRU_BLOCKS = 8
LRU_BLOCK_W = LRU_W // LRU_BLOCKS
CONV_W = 4
LRU_C = 8.0
IN_W = ATTN_W + 2 * KV_W + 2 * LRU_W
QK_W = ATTN_W + KV_W
N_GROUPS = 4
EXPERTS_PER_GROUP = 8
N_EXPERTS = N_GROUPS * EXPERTS_PER_GROUP
TOP_K = 2
EXPERT_FF = D_MODEL // 4
ROPE_THETA = 10000.0
EPS = 1e-6

LANES = 128
SUBLANES = 8
MOD_ROWS = 8
VMEM_LIMIT = 48 * 1024 * 1024

TM_PRE = 512
TQ_ATT = 1024
TC_LRU = 512
TR_MOE = 256
CH_DISPATCH = 1024
COMBINE_CHUNK = 32
FFN_IN_SLOTS = 4
FFN_OUT_SLOTS = 3


def _cparams(sem):
    return pltpu.CompilerParams(dimension_semantics=sem, vmem_limit_bytes=VMEM_LIMIT)


def _dot(a, b):
    return jnp.dot(a, b, preferred_element_type=F32)


def _dot_nt(a, b):
    return lax.dot_general(a, b, (((1,), (1,)), ((), ())), preferred_element_type=F32)


ROW_TILE = (D_MODEL // LANES, LANES)
LOG2_E = 1.4426950408889634
Q_SCALE = HEAD_DIM ** -0.5 * LOG2_E
ONES_LANE_EVEN = HEAD_DIM
ONES_LANE_ODD = 0


def _to_row_tiles(x):
    cols = jnp.stack([x[:, c * LANES:(c + 1) * LANES] for c in range(D_MODEL // LANES)], axis=0)
    return jnp.swapaxes(cols, 0, 1)


def _from_row_tiles(x3):
    cols = jnp.swapaxes(x3, 0, 1)
    return jnp.concatenate([cols[c] for c in range(D_MODEL // LANES)], axis=1)


def _sigmoid(x):
    return 0.5 * jnp.tanh(0.5 * x) + 0.5


def _split_bf16(x):
    hi = x.astype(BF16)
    lo = (x - hi.astype(F32)).astype(BF16)
    return hi, lo


def _mod_kernel(c_ref, w_ref, b_ref, o_ref):
    c = c_ref[...]
    s = (c * jax.nn.sigmoid(c)).astype(BF16)
    o_ref[...] = _dot(s, w_ref[...].astype(BF16)) + b_ref[...]


def _modulation(cvec, w_mod, b_mod):
    n_out = w_mod.shape[1]
    tn = n_out // 4
    return pl.pallas_call(
        _mod_kernel,
        grid=(n_out // tn,),
        in_specs=[pl.BlockSpec((MOD_ROWS, D_MODEL), lambda j: (0, 0)),
                  pl.BlockSpec((D_MODEL, tn), lambda j: (0, j)),
                  pl.BlockSpec((1, tn), lambda j: (0, j))],
        out_specs=pl.BlockSpec((MOD_ROWS, tn), lambda j: (0, j)),
        out_shape=jax.ShapeDtypeStruct((MOD_ROWS, n_out), F32),
        compiler_params=_cparams(("arbitrary",)),
        name="modulation",
    )(cvec, w_mod, b_mod)


def _store_cache(ref, col):
    seq = ref.shape[-1]
    col_t = jnp.transpose(col)
    for j in range(ref.shape[0]):
        ref[j] = col_t[:, j * seq:(j + 1) * seq].reshape(N_KV_HEADS, HEAD_DIM, seq)


def _store_expanded(xref, col, one):
    lane = lax.broadcasted_iota(jnp.int32, col.shape, 1)
    lo_half = lane < HEAD_DIM
    swapped = pltpu.roll(col, HEAD_DIM, 1)
    fill_hi = jnp.where(lane == ONES_LANE_EVEN, one, 0.0)
    fill_lo = jnp.where(lane == ONES_LANE_ODD, one, 0.0)
    xref[:, 0 * LANES:1 * LANES] = jnp.where(lo_half, col, fill_hi).astype(BF16)
    xref[:, 1 * LANES:2 * LANES] = jnp.where(lo_half, fill_lo, swapped).astype(BF16)
    xref[:, 2 * LANES:3 * LANES] = jnp.where(lo_half, swapped, fill_hi).astype(BF16)
    xref[:, 3 * LANES:4 * LANES] = jnp.where(lo_half, fill_lo, col).astype(BF16)


def _pre_kernel(*refs, rope):
    if rope:
        (x_ref, mod_ref, g1_ref, win_ref, heads_ref, heads_t_ref, gqk_ref, cos_ref, sin_ref,
         q_ref, kx_ref, vx_ref, xr_ref, gb_ref, win_s) = refs
    else:
        (x_ref, mod_ref, g1_ref, win_ref, heads_ref, heads_t_ref, gqk_ref,
         q_ref, kx_ref, vx_ref, xr_ref, gb_ref, kf_ref, vf_ref, win_s) = refs

    @pl.when(pl.program_id(0) == 0)
    def _():
        win_s[...] = win_ref[...].astype(BF16)

    x = x_ref[...]
    m = mod_ref[0]
    ms = jnp.mean(x * x, axis=-1, keepdims=True)
    y = x * lax.rsqrt(ms + EPS) * g1_ref[...]
    h = y * (1.0 + m[1:2]) + m[0:1]
    z = _dot(h.astype(BF16), win_s[...])

    qk = z[:, :QK_W]
    ss = _dot((qk * qk).astype(BF16), heads_ref[...])
    hi, lo = _split_bf16(lax.rsqrt(ss * (1.0 / HEAD_DIM) + EPS))
    qk = qk * (_dot(hi, heads_t_ref[...]) + _dot(lo, heads_t_ref[...])) * gqk_ref[...]

    lane = lax.broadcasted_iota(jnp.int32, (x.shape[0], LANES), 1)
    cols = []
    for c in range(QK_W // LANES):
        xc = qk[:, c * LANES:(c + 1) * LANES]
        if rope:
            left = pltpu.roll(xc, LANES - HEAD_DIM // 4, 1)
            right = pltpu.roll(xc, HEAD_DIM // 4, 1)
            rot = jnp.where((lane // (HEAD_DIM // 4)) % 2 == 0, left, right)
            xc = xc * cos_ref[...] + rot * sin_ref[...]
        cols.append(xc)
    for c in range(ATTN_W // LANES):
        q_ref[:, c * LANES:(c + 1) * LANES] = (cols[c] * Q_SCALE).astype(BF16)

    k_col = cols[ATTN_W // LANES]
    v_col = z[:, QK_W:QK_W + KV_W]
    _store_expanded(kx_ref, k_col, 0.0)
    _store_expanded(vx_ref, v_col, 1.0)
    if not rope:
        _store_cache(kf_ref, k_col)
        _store_cache(vf_ref, v_col)

    xr_ref[...] = z[:, QK_W + KV_W:QK_W + KV_W + LRU_W]
    gb_ref[...] = z[:, QK_W + KV_W + LRU_W:]


def _pre(x, mod3, mod_row, g1, w_in, heads, gqk, tables, seq_len):
    n = x.shape[0]
    tm = TM_PRE
    tiles_per_seq = seq_len // tm
    rope = tables is not None
    const = lambda i: (0, 0)
    in_specs = [pl.BlockSpec((tm, D_MODEL), lambda i: (i, 0)),
                pl.BlockSpec((1, 6, D_MODEL), lambda i: (mod_row(i * tm), 0, 0)),
                pl.BlockSpec((1, D_MODEL), const),
                pl.BlockSpec((D_MODEL, IN_W), const),
                pl.BlockSpec((QK_W, LANES), const),
                pl.BlockSpec((LANES, QK_W), const),
                pl.BlockSpec((1, QK_W), const)]
    args = [x, mod3, g1, w_in, heads, heads.T, gqk]
    if rope:
        in_specs += [pl.BlockSpec((tm, LANES), lambda i: (i % tiles_per_seq, 0))] * 2
        args += list(tables)
    row = lambda w: pl.BlockSpec((tm, w), lambda i: (i, 0))
    out_shape = [jax.ShapeDtypeStruct((n, ATTN_W), BF16),
                 jax.ShapeDtypeStruct((n, 4 * LANES), BF16),
                 jax.ShapeDtypeStruct((n, 4 * LANES), BF16),
                 jax.ShapeDtypeStruct((n, LRU_W), F32),
                 jax.ShapeDtypeStruct((n, LRU_W), F32)]
    out_specs = [row(ATTN_W), row(4 * LANES), row(4 * LANES), row(LRU_W), row(LRU_W)]
    if not rope:
        assert tm % seq_len == 0
        per_tile = tm // seq_len
        cache = pl.BlockSpec((per_tile, N_KV_HEADS, HEAD_DIM, seq_len), lambda i: (i, 0, 0, 0))
        out_shape += [jax.ShapeDtypeStruct((n // seq_len, N_KV_HEADS, HEAD_DIM, seq_len), F32)] * 2
        out_specs += [cache, cache]
    return pl.pallas_call(
        functools.partial(_pre_kernel, rope=rope),
        grid=(n // tm,),
        in_specs=in_specs, out_specs=out_specs, out_shape=out_shape,
        scratch_shapes=[pltpu.VMEM((D_MODEL, IN_W), BF16)],
        compiler_params=_cparams(("arbitrary",)),
        name="pre_rope" if rope else "pre",
    )(*args)


def _attn_kernel(*refs, n_seg):
    q_ref = refs[0]
    k_refs = refs[1:1 + n_seg]
    v_refs = refs[1 + n_seg:1 + 2 * n_seg]
    o_ref = refs[1 + 2 * n_seg]
    lane = lax.broadcasted_iota(jnp.int32, (q_ref.shape[0], LANES), 1)
    for c in range(ATTN_W // LANES):
        qc = q_ref[:, c * LANES:(c + 1) * LANES]
        g = c // 2
        accs = []
        for par in range(2):
            sl = slice((2 * g + par) * LANES, (2 * g + par + 1) * LANES)
            ss = [_dot_nt(qc, k[:, sl]) for k in k_refs]
            mx = functools.reduce(jnp.maximum, [jnp.max(s, axis=-1, keepdims=True) for s in ss])
            ps = [jnp.exp2((s - mx).astype(BF16)) for s in ss]
            accs.append(functools.reduce(lambda a, b: a + b,
                                         [_dot(p, v[:, sl]) for p, v in zip(ps, v_refs)]))
        even = accs[0] / accs[0][:, ONES_LANE_EVEN:ONES_LANE_EVEN + 1]
        odd = accs[1] / accs[1][:, ONES_LANE_ODD:ONES_LANE_ODD + 1]
        o_ref[:, c * LANES:(c + 1) * LANES] = jnp.where(lane < HEAD_DIM, even, odd).astype(BF16)


def _attention(q, k_segs, v_segs, seq_len):
    n = q.shape[0]
    tq = min(TQ_ATT, seq_len)
    nq = seq_len // tq
    n_seg = len(k_segs)
    in_specs = [pl.BlockSpec((tq, ATTN_W), lambda b, i: (b * nq + i, 0))]
    for arr, t in list(k_segs) + list(v_segs):
        in_specs.append(pl.BlockSpec((t, 4 * LANES), lambda b, i: (b, 0)))
    return pl.pallas_call(
        functools.partial(_attn_kernel, n_seg=n_seg),
        grid=(n // seq_len, nq),
        in_specs=in_specs,
        out_specs=pl.BlockSpec((tq, ATTN_W), lambda b, i: (b * nq + i, 0)),
        out_shape=jax.ShapeDtypeStruct((n, ATTN_W), BF16),
        compiler_params=_cparams(("arbitrary", "arbitrary")),
        name="attention_%dseg" % n_seg,
    )(q, *[a for a, _ in k_segs], *[a for a, _ in v_segs])


def _log_sigmoid(x):
    return jnp.minimum(x, 0.0) - jnp.log1p(jnp.exp(-jnp.abs(x)))


def _tile_scan(a, b, reverse):
    row = lax.broadcasted_iota(jnp.int32, a.shape, 0)
    d = 1
    while d < SUBLANES:
        if reverse:
            keep = row < SUBLANES - d
            shift = SUBLANES - d
        else:
            keep = row >= d
            shift = d
        a_sh = jnp.where(keep, pltpu.roll(a, shift, 0), 1.0)
        b_sh = jnp.where(keep, pltpu.roll(b, shift, 0), 0.0)
        b = a * b_sh + b
        a = a * a_sh
        d *= 2
    return a, b


def _lru_kernel(xr_ref, gb_ref, h0_ref, cw_ref, cb_ref, wa_ref, wx_ref, ba_ref, bx_ref, lam_ref,
                rec_ref, fin_ref, xpad_s, xc_s, hf_s, a_s, b_s, wf_ref, wb_ref, bf_ref, bb_ref,
                *, seq_len):
    @pl.when(pl.program_id(0) == 0)
    def _():
        pairs = LANES // LRU_BLOCK_W
        zero = jnp.zeros((LRU_BLOCK_W, LRU_BLOCK_W), F32)
        for d, (w_s, bias_s) in enumerate(((wf_ref, bf_ref), (wb_ref, bb_ref))):
            w_s[...] = jnp.zeros_like(w_s)
            for g, src in enumerate((wa_ref, wx_ref)):
                for p in range(LRU_BLOCKS // pairs):
                    rows = [jnp.concatenate([0.5 * src[d, pairs * p + q] if q == r else zero
                                             for q in range(pairs)], axis=1) for r in range(pairs)]
                    w_s[LANES * p:LANES * (p + 1), g * LRU_W + LANES * p:g * LRU_W + LANES * (p + 1)] = (
                        jnp.concatenate(rows, axis=0).astype(BF16))
            bias_s[:, :LRU_W] = 0.5 * ba_ref[d:d + 1, :]
            bias_s[:, LRU_W:] = 0.5 * bx_ref[d:d + 1, :]

    tc = min(TC_LRU, seq_len)
    n_chunks = seq_len // tc
    n_tiles = tc // SUBLANES
    zpad = jnp.zeros((SUBLANES, LRU_W), F32)
    xpad_s[0:SUBLANES, :] = zpad
    xpad_s[SUBLANES:SUBLANES + seq_len, :] = xr_ref[...]
    xpad_s[SUBLANES + seq_len:2 * SUBLANES + seq_len, :] = zpad

    half_cl = (0.5 * LRU_C * LOG2_E) * _log_sigmoid(lam_ref[...])

    def gates(xcc, w_ref, bias_ref, half_cl_d):
        t = jnp.tanh(_dot(xcc.astype(BF16), w_ref[...]) + bias_ref[...])
        a = jnp.exp2(t[:, :LRU_W] * half_cl_d + half_cl_d)
        half_x = 0.5 * xcc
        a_s[...] = a
        u = 1.0 - a * a
        b_s[...] = jnp.where(u > 0.0, u * lax.rsqrt(u), 0.0) * (t[:, LRU_W:] * half_x + half_x)

    h = h0_ref[0, 0:1, :]
    for c in range(n_chunks):
        base = c * tc
        xcc = cb_ref[...] + functools.reduce(
            lambda u, v: u + v,
            [cw_ref[j:j + 1, :] * xpad_s[base + SUBLANES - 1 + j:base + SUBLANES - 1 + j + tc, :]
             for j in range(CONV_W)])
        xc_s[base:base + tc, :] = xcc
        gates(xcc, wf_ref, bf_ref, half_cl[0:1])

        def fwd_tile(t, hc, base=base):
            r0 = pl.multiple_of(t * SUBLANES, SUBLANES)
            ca, cb = _tile_scan(a_s[pl.ds(r0, SUBLANES), :], b_s[pl.ds(r0, SUBLANES), :], False)
            hh = ca * hc + cb
            hf_s[pl.ds(base + r0, SUBLANES), :] = hh
            return hh[SUBLANES - 1:SUBLANES, :]

        h = lax.fori_loop(0, n_tiles, fwd_tile, h, unroll=4)
    fin_ref[0, 0:1, :] = h

    h = h0_ref[0, 1:2, :]
    for c in reversed(range(n_chunks)):
        base = c * tc
        gates(xc_s[base:base + tc, :], wb_ref, bb_ref, half_cl[1:2])

        def bwd_tile(t, hc, base=base):
            r0 = pl.multiple_of((n_tiles - 1 - t) * SUBLANES, SUBLANES)
            ca, cb = _tile_scan(a_s[pl.ds(r0, SUBLANES), :], b_s[pl.ds(r0, SUBLANES), :], True)
            hh = ca * hc + cb
            gate = jax.nn.gelu(gb_ref[pl.ds(base + r0, SUBLANES), :], approximate=True)
            rec_ref[pl.ds(base + r0, SUBLANES), :] = (
                (hf_s[pl.ds(base + r0, SUBLANES), :] + hh) * gate).astype(rec_ref.dtype)
            return hh[0:1, :]

        h = lax.fori_loop(0, n_tiles, bwd_tile, h, unroll=4)
    fin_ref[0, 1:2, :] = h


def _lru(xr, gb, h0, conv_w, conv_b, wa, wx, ba, bx, lam, seq_len):
    n = xr.shape[0]
    batch = n // seq_len
    const = lambda b: (0, 0)
    blocks = pl.BlockSpec((2, LRU_BLOCKS, LRU_BLOCK_W, LRU_BLOCK_W), lambda b: (0, 0, 0, 0))
    seq = pl.BlockSpec((seq_len, LRU_W), lambda b: (b, 0))
    st = pl.BlockSpec((1, 2, LRU_W), lambda b: (b, 0, 0))
    return pl.pallas_call(
        functools.partial(_lru_kernel, seq_len=seq_len),
        grid=(batch,),
        in_specs=[seq, seq, st,
                  pl.BlockSpec((CONV_W, LRU_W), const), pl.BlockSpec((1, LRU_W), const),
                  blocks, blocks,
                  pl.BlockSpec((2, LRU_W), const), pl.BlockSpec((2, LRU_W), const),
                  pl.BlockSpec((2, LRU_W), const)],
        out_specs=[seq, st],
        out_shape=[jax.ShapeDtypeStruct((n, LRU_W), BF16),
                   jax.ShapeDtypeStruct((batch, 2, LRU_W), F32)],
        scratch_shapes=[pltpu.VMEM((seq_len + 2 * SUBLANES, LRU_W), F32),
                        pltpu.VMEM((seq_len, LRU_W), F32),
                        pltpu.VMEM((seq_len, LRU_W), F32),
                        pltpu.VMEM((min(TC_LRU, seq_len), LRU_W), F32),
                        pltpu.VMEM((min(TC_LRU, seq_len), LRU_W), F32),
                        pltpu.VMEM((LRU_W, 2 * LRU_W), BF16),
                        pltpu.VMEM((LRU_W, 2 * LRU_W), BF16),
                        pltpu.VMEM((1, 2 * LRU_W), F32),
                        pltpu.VMEM((1, 2 * LRU_W), F32)],
        compiler_params=_cparams(("arbitrary",)),
        name="lru_%d" % seq_len,
    )(xr, gb, h0, conv_w, conv_b, wa, wx, ba, bx, lam)


def _post_kernel(attn_ref, rec_ref, x_ref, mod_ref, g2_ref, wo_ref, wr2_ref, br_ref,
                 tri_ref, cnt_in_ref, x1_ref, h2_ref, route_ref, cnt_ref, route_t_ref, cnt_s, wo_s):
    @pl.when(pl.program_id(0) == 0)
    def _():
        cnt_s[...] = cnt_in_ref[...]
        wo_s[...] = wo_ref[...].astype(BF16)

    m = mod_ref[0]
    u = _dot(attn_ref[...], wo_s[:ATTN_W, :]) + _dot(rec_ref[...], wo_s[ATTN_W:, :])
    x1 = x_ref[...] + m[2:3] * u
    x1_ref[...] = x1
    ms = jnp.mean(x1 * x1, axis=-1, keepdims=True)
    h2 = x1 * lax.rsqrt(ms + EPS) * g2_ref[...]
    h2 = h2 * (1.0 + m[4:5]) + m[3:4]
    hi, lo = _split_bf16(h2)
    h2_ref[...] = _to_row_tiles(hi)

    hw = _dot(hi, wr2_ref[...])
    logits = hw[:, :LANES] + hw[:, LANES:] + _dot(lo, wr2_ref[:, :LANES]) + br_ref[...]
    lane_i = lax.broadcasted_iota(jnp.int32, logits.shape, 1)
    lane = lane_i.astype(F32)
    lane_group = (lane_i // EXPERTS_PER_GROUP).astype(F32)
    neg = -jnp.inf
    big = float(1 << 20)
    gmask = (lane_i >= N_EXPERTS) & (lane_i < N_EXPERTS + N_GROUPS)
    gl = jnp.where(gmask, logits, neg)
    gmax = jnp.max(gl, axis=-1, keepdims=True)
    gidx = jnp.min(jnp.where(gl == gmax, lane - N_EXPERTS, big), axis=-1, keepdims=True)
    p_sel = 1.0 / jnp.sum(jnp.where(gmask, jnp.exp(gl - gmax), 0.0), axis=-1, keepdims=True)

    emask = (lane_i < N_EXPERTS) & (lane_group == gidx)
    el = jnp.where(emask, logits, neg)
    v1 = jnp.max(el, axis=-1, keepdims=True)
    i1 = jnp.min(jnp.where(el == v1, lane, big), axis=-1, keepdims=True)
    el2 = jnp.where(lane == i1, neg, el)
    v2 = jnp.max(el2, axis=-1, keepdims=True)
    i2 = jnp.min(jnp.where(el2 == v2, lane, big), axis=-1, keepdims=True)
    e2 = jnp.exp(v2 - v1)
    w1 = p_sel / (1.0 + e2)
    w2 = p_sel * e2 / (1.0 + e2)

    oh1 = lane == i1
    oh2 = lane == i2
    oh = jnp.where(oh1, 1.0, 0.0) + jnp.where(oh2, 1.0, 0.0)
    before = _dot(tri_ref[...], oh.astype(BF16)) + cnt_s[...]
    rank1 = jnp.sum(jnp.where(oh1, before, 0.0), axis=-1, keepdims=True)
    rank2 = jnp.sum(jnp.where(oh2, before, 0.0), axis=-1, keepdims=True)
    cnt = cnt_s[...] + jnp.sum(oh, axis=0, keepdims=True)
    cnt_s[...] = cnt
    cnt_ref[...] = cnt
    fields = (i1, i2, rank1, rank2, w1, w2)
    route = jnp.zeros(logits.shape, F32)
    for k, val in enumerate(fields):
        route = jnp.where(lane_i == k, val, route)
    route_ref[...] = route
    route_t_ref[...] = jnp.transpose(route)[:SUBLANES, :]


def _post(attn, rec, x, mod3, mod_row, g2, w_out, wr2, br, tri, cnt_in):
    n = x.shape[0]
    tm = TM_PRE
    const = lambda i: (0, 0)
    row = lambda w: pl.BlockSpec((tm, w), lambda i: (i, 0))
    return pl.pallas_call(
        _post_kernel,
        grid=(n // tm,),
        in_specs=[row(ATTN_W), row(LRU_W), row(D_MODEL),
                  pl.BlockSpec((1, 6, D_MODEL), lambda i: (mod_row(i * tm), 0, 0)),
                  pl.BlockSpec((1, D_MODEL), const),
                  pl.BlockSpec((D_MODEL, D_MODEL), const),
                  pl.BlockSpec((D_MODEL, 2 * LANES), const),
                  pl.BlockSpec((1, LANES), const),
                  pl.BlockSpec((tm, tm), const),
                  pl.BlockSpec((1, LANES), const)],
        out_specs=[row(D_MODEL), pl.BlockSpec((tm,) + ROW_TILE, lambda i: (i, 0, 0)), row(LANES),
                   pl.BlockSpec((1, LANES), const),
                   pl.BlockSpec((SUBLANES, tm), lambda i: (0, i))],
        out_shape=[jax.ShapeDtypeStruct((n, D_MODEL), F32),
                   jax.ShapeDtypeStruct((n,) + ROW_TILE, BF16),
                   jax.ShapeDtypeStruct((n, LANES), F32),
                   jax.ShapeDtypeStruct((1, LANES), F32),
                   jax.ShapeDtypeStruct((SUBLANES, n), F32)],
        scratch_shapes=[pltpu.VMEM((1, LANES), F32), pltpu.VMEM((D_MODEL, D_MODEL), BF16)],
        compiler_params=_cparams(("arbitrary",)),
        name="post",
    )(attn, rec, x, mod3, g2, w_out, wr2, br, tri, cnt_in)


def _row_copy(src_ref, src_row, dst_ref, dst_row, sem):
    return pltpu.make_async_copy(src_ref.at[pl.ds(src_row, 1)], dst_ref.at[pl.ds(dst_row, 1)], sem)


def _plan_kernel(route_t_ref, seg_ref, slots_ref):
    route_t = route_t_ref[...]
    t = route_t.shape[1]
    seg = jnp.concatenate([seg_ref[...]] * (t // LANES), axis=1)
    expert = lax.broadcasted_iota(jnp.int32, seg.shape, 0).astype(F32)
    rows = []
    for k in range(TOP_K):
        start = jnp.sum(jnp.where(expert == route_t[k:k + 1, :], seg, 0.0), axis=0, keepdims=True)
        rows.append(start + route_t[TOP_K + k:TOP_K + k + 1, :])
    rows.append(jnp.zeros((SUBLANES - TOP_K, t), F32))
    slots_ref[...] = jnp.concatenate(rows, axis=0).astype(jnp.int32)


def _plan(route_t, seg_rows):
    n = route_t.shape[1]
    tm = TM_PRE
    return pl.pallas_call(
        _plan_kernel,
        grid=(n // tm,),
        in_specs=[pl.BlockSpec((SUBLANES, tm), lambda i: (0, i)),
                  pl.BlockSpec((N_EXPERTS, LANES), lambda i: (0, 0))],
        out_specs=pl.BlockSpec((SUBLANES, tm), lambda i: (0, i)),
        out_shape=jax.ShapeDtypeStruct((SUBLANES, n), jnp.int32),
        compiler_params=_cparams(("arbitrary",)),
        name="plan",
    )(route_t, seg_rows)


def _dispatch_kernel(slots_p_ref, slots_s_ref, pad_start_ref, pad_rows_ref, nu_ref, h2p_ref, h2s_ref,
                     xs_ref, zero_s, sem, zsem, *, n_prompt):
    i = pl.program_id(0)
    ch = CH_DISPATCH

    def zero_fill(op):
        for e in range(N_EXPERTS):
            rows = pad_rows_ref[e]
            for b in range(TR_MOE.bit_length() - 1):
                size = 1 << b
                off = (rows >> (b + 1)) << (b + 1)

                @pl.when(((rows >> b) & 1) == 1)
                def _():
                    op(pltpu.make_async_copy(zero_s.at[pl.ds(0, size)],
                                             xs_ref.at[pl.ds(pad_start_ref[e] + off, size)], zsem))

        def zero_tile(t, carry):
            op(pltpu.make_async_copy(
                zero_s, xs_ref.at[pl.ds(pl.multiple_of(t * TR_MOE, TR_MOE), TR_MOE)], zsem))
            return carry
        lax.fori_loop(nu_ref[0], xs_ref.shape[0] // TR_MOE, zero_tile, 0)

    @pl.when(i == 0)
    def _():
        zero_s[...] = jnp.zeros_like(zero_s)
        zero_fill(lambda cp: cp.start())

    @pl.when(i == pl.num_programs(0) - 1)
    def _():
        zero_fill(lambda cp: cp.wait())

    def scatter(src_ref, slots_ref, first_token):
        n_path = slots_ref.shape[0] // TOP_K

        def body(j, carry):
            for k in range(TOP_K):
                _row_copy(src_ref, j, xs_ref, slots_ref[k * n_path + i * ch + j - first_token],
                          sem).start(priority=k)
            return carry
        lax.fori_loop(0, ch, body, 0, unroll=8)
        for _ in range(TOP_K):
            pltpu.make_async_copy(src_ref, xs_ref.at[pl.ds(0, ch)], sem).wait()

    @pl.when(i < n_prompt // ch)
    def _():
        scatter(h2p_ref, slots_p_ref, 0)

    @pl.when(i >= n_prompt // ch)
    def _():
        scatter(h2s_ref, slots_s_ref, n_prompt)


def _dispatch(slots_p, slots_s, pad_start, pad_rows, n_used, h2p, h2s, n_rows):
    n_prompt = h2p.shape[0]
    n = n_prompt + h2s.shape[0]
    ch = CH_DISPATCH
    npc = n_prompt // ch
    last_p = npc - 1
    return pl.pallas_call(
        functools.partial(_dispatch_kernel, n_prompt=n_prompt),
        grid_spec=pltpu.PrefetchScalarGridSpec(
            num_scalar_prefetch=5,
            grid=(n // ch,),
            in_specs=[pl.BlockSpec((ch,) + ROW_TILE, lambda i, *_: (jnp.minimum(i, last_p), 0, 0)),
                      pl.BlockSpec((ch,) + ROW_TILE, lambda i, *_: (jnp.maximum(i - npc, 0), 0, 0))],
            out_specs=pl.BlockSpec(memory_space=pl.ANY),
            scratch_shapes=[pltpu.VMEM((TR_MOE,) + ROW_TILE, BF16), pltpu.SemaphoreType.DMA,
                            pltpu.SemaphoreType.DMA]),
        out_shape=jax.ShapeDtypeStruct((n_rows,) + ROW_TILE, BF16),
        compiler_params=_cparams(("arbitrary",)),
        name="dispatch",
    )(slots_p, slots_s, pad_start, pad_rows, n_used, h2p, h2s)


def _ffn_kernel(tend_ref, xs_ref, wg_ref, wu_ref, wd_ref, ys_ref,
                xbuf, ybuf, wg_s, wu_s, wd_s, sem_in, sem_out):
    e = pl.program_id(0)
    n_used = tend_ref[N_EXPERTS - 1]
    t_first = jnp.where(e == 0, 0, tend_ref[jnp.maximum(e - 1, 0)])
    t_last = tend_ref[e]
    n_in = xbuf.shape[0]
    n_out = ybuf.shape[0]

    def tile_rows(t):
        return pl.ds(pl.multiple_of(t * TR_MOE, TR_MOE), TR_MOE)

    def fetch(t):
        return pltpu.make_async_copy(xs_ref.at[tile_rows(t)], xbuf.at[t % n_in], sem_in.at[t % n_in])

    def writeback(t):
        return pltpu.make_async_copy(ybuf.at[t % n_out], ys_ref.at[tile_rows(t)],
                                     sem_out.at[t % n_out])

    @pl.when(e == 0)
    def _():
        for t in range(n_in - 1):
            @pl.when(t < n_used)
            def _():
                fetch(t).start()

    @pl.when(t_last > t_first)
    def _():
        wg_s[...] = wg_ref[0].astype(BF16)
        wu_s[...] = wu_ref[0].astype(BF16)
        wd_s[...] = wd_ref[0].astype(BF16)

    def tile(t, carry):
        fetch(t).wait()

        @pl.when(t + n_in - 1 < n_used)
        def _():
            fetch(t + n_in - 1).start()

        @pl.when(t >= n_out)
        def _():
            writeback(t - n_out).wait()

        x = _from_row_tiles(xbuf[t % n_in])
        hg = _dot(x, wg_s[...])
        hu = _dot(x, wu_s[...])
        act = (hg * _sigmoid(hg)) * hu
        ybuf[t % n_out] = _to_row_tiles(_dot(act.astype(BF16), wd_s[...]).astype(BF16))
        writeback(t).start()
        return carry

    lax.fori_loop(t_first, t_last, tile, 0)

    @pl.when(e == N_EXPERTS - 1)
    def _():
        for back in range(n_out, 0, -1):
            @pl.when(n_used >= back)
            def _():
                writeback(n_used - back).wait()
        n_all = ys_ref.shape[0] // TR_MOE
        ybuf[0] = jnp.zeros(ybuf.shape[1:], ybuf.dtype)

        def zero_tile(t):
            return pltpu.make_async_copy(ybuf.at[0], ys_ref.at[tile_rows(t)], sem_out.at[0])

        lax.fori_loop(n_used, n_all, lambda t, c: (zero_tile(t).start(), c)[1], 0)
        lax.fori_loop(n_used, n_all, lambda t, c: (zero_tile(t).wait(), c)[1], 0)


def _ffn(tile_end, xs, wg, wu, wd, n_tiles):
    tr = TR_MOE
    wsel = lambda e, tend: (e, 0, 0)
    return pl.pallas_call(
        _ffn_kernel,
        grid_spec=pltpu.PrefetchScalarGridSpec(
            num_scalar_prefetch=1,
            grid=(N_EXPERTS,),
            in_specs=[pl.BlockSpec(memory_space=pl.ANY),
                      pl.BlockSpec((1, D_MODEL, EXPERT_FF), wsel),
                      pl.BlockSpec((1, D_MODEL, EXPERT_FF), wsel),
                      pl.BlockSpec((1, EXPERT_FF, D_MODEL), wsel)],
            out_specs=pl.BlockSpec(memory_space=pl.ANY),
            scratch_shapes=[pltpu.VMEM((FFN_IN_SLOTS, tr) + ROW_TILE, BF16),
                            pltpu.VMEM((FFN_OUT_SLOTS, tr) + ROW_TILE, BF16),
                            pltpu.VMEM((D_MODEL, EXPERT_FF), BF16),
                            pltpu.VMEM((D_MODEL, EXPERT_FF), BF16),
                            pltpu.VMEM((EXPERT_FF, D_MODEL), BF16),
                            pltpu.SemaphoreType.DMA((FFN_IN_SLOTS,)),
                            pltpu.SemaphoreType.DMA((FFN_OUT_SLOTS,))]),
        out_shape=jax.ShapeDtypeStruct((n_tiles * tr,) + ROW_TILE, BF16),
        compiler_params=_cparams(("arbitrary",)),
        name="ffn",
    )(tile_end, xs, wg, wu, wd)


def _combine_kernel(slots_ref, ys_ref, x1_ref, route_ref, mod_ref, y_ref, b1_s, b2_s, sems):
    i = pl.program_id(0)
    tm = x1_ref.shape[0]
    n_path = slots_ref.shape[0] // TOP_K

    n_steps = pl.num_programs(0)
    ch = COMBINE_CHUNK

    def gather_chunk(step, slot, c):
        for t in range(ch):
            j = c * ch + t
            _row_copy(ys_ref, slots_ref[step * tm + j], b1_s.at[slot], j, sems.at[slot]).start(
                priority=0)
            _row_copy(ys_ref, slots_ref[n_path + step * tm + j], b2_s.at[slot], j,
                      sems.at[slot]).start(priority=1)

    def wait_slot(slot):
        for buf in (b1_s, b2_s):
            pltpu.make_async_copy(ys_ref.at[pl.ds(0, tm)], buf.at[slot], sems.at[slot]).wait()

    @pl.when(i == 0)
    def _():
        lax.fori_loop(0, tm // ch, lambda c, carry: (gather_chunk(0, 0, c), carry)[1], 0)

    slot = i % 2
    wait_slot(slot)
    gate = mod_ref[0][5:6]
    nxt = (i + 1) % n_steps

    def body(c, carry):
        rows = pl.ds(pl.multiple_of(c * ch, ch), ch)
        route = route_ref[rows, :]
        moe = (route[:, 4:5] * _from_row_tiles(b1_s[slot, rows]).astype(F32)
               + route[:, 5:6] * _from_row_tiles(b2_s[slot, rows]).astype(F32))
        y_ref[rows, :] = x1_ref[rows, :] + gate * moe
        gather_chunk(nxt, 1 - slot, c)
        return carry
    lax.fori_loop(0, tm // ch, body, 0)

    @pl.when(i == n_steps - 1)
    def _():
        wait_slot(1 - slot)


def _combine(slots, ys, x1, route, mod3, mod_row):
    n = x1.shape[0]
    tm = TM_PRE
    row = lambda w: pl.BlockSpec((tm, w), lambda i, *_: (i, 0))
    return pl.pallas_call(
        _combine_kernel,
        grid_spec=pltpu.PrefetchScalarGridSpec(
            num_scalar_prefetch=1,
            grid=(n // tm,),
            in_specs=[pl.BlockSpec(memory_space=pl.ANY), row(D_MODEL), row(LANES),
                      pl.BlockSpec((1, 6, D_MODEL),
                                   lambda i, *_: (mod_row(i * tm), 0, 0))],
            out_specs=row(D_MODEL),
            scratch_shapes=[pltpu.VMEM((2, tm) + ROW_TILE, BF16), pltpu.VMEM((2, tm) + ROW_TILE, BF16),
                            pltpu.SemaphoreType.DMA((2,))]),
        out_shape=jax.ShapeDtypeStruct((n, D_MODEL), F32),
        compiler_params=_cparams(("arbitrary",)),
        name="combine",
    )(slots, ys, x1, route, mod3)


def _rope_tables(length):
    rows = length // GRID_W
    r, col = jnp.meshgrid(jnp.arange(rows), jnp.arange(GRID_W), indexing='ij')
    r = r.reshape(-1).astype(F32)
    col = col.reshape(-1).astype(F32)
    half = HEAD_DIM // 2
    inv = ROPE_THETA ** (-jnp.arange(0, half, 2, dtype=F32) / half)
    ang_r = r[:, None] * inv
    ang_c = col[:, None] * inv
    ang = jnp.concatenate([ang_r, ang_r, ang_c, ang_c], axis=-1)
    sign = jnp.where((jnp.arange(HEAD_DIM) // (HEAD_DIM // 4)) % 2 == 0, -1.0, 1.0).astype(F32)
    cos = jnp.tile(jnp.cos(ang), (1, LANES // HEAD_DIM))
    sin = jnp.tile(jnp.sin(ang) * sign, (1, LANES // HEAD_DIM))
    return cos, sin


def _expand_cache_kernel(k_ref, v_ref, kx_ref, vx_ref):
    for ref, xref, one in ((k_ref, kx_ref, 0.0), (v_ref, vx_ref, 1.0)):
        col = jnp.transpose(ref[0].reshape(KV_W, ref.shape[-1]))
        _store_expanded(xref, col, one)


def _expand_cache(cache_k, cache_v):
    b, _, _, t = cache_k.shape
    src = pl.BlockSpec((1, N_KV_HEADS, HEAD_DIM, t), lambda i: (i, 0, 0, 0))
    dst = pl.BlockSpec((t, 4 * LANES), lambda i: (i, 0))
    return pl.pallas_call(
        _expand_cache_kernel,
        grid=(b,),
        in_specs=[src, src], out_specs=[dst, dst],
        out_shape=[jax.ShapeDtypeStruct((b * t, 4 * LANES), BF16)] * 2,
        compiler_params=_cparams(("arbitrary",)),
        name="expand_cache",
    )(cache_k, cache_v)


def kernel(x_prompt, x_sample, cache_k, cache_v, state_lru, c, c_ctx, w_mod, b_mod, norm1, norm2,
           w_in, q_norm, k_norm, conv_w, conv_b, lru_wa, lru_ba, lru_wx, lru_bx, lru_lambda, w_out,
           router_grp_w, router_grp_b, router_exp_w, router_exp_b, exp_w_gate, exp_w_up, exp_w_down):
    batch, seq, _ = x_prompt.shape
    dec_batch, dec_seq, _ = x_sample.shape
    past = cache_k.shape[2]
    depth = w_mod.shape[0]
    assert depth == 1
    assert x_prompt.shape[2] == D_MODEL and w_in.shape[1:] == (D_MODEL, IN_W)
    assert exp_w_gate.shape[1:] == (N_EXPERTS, D_MODEL, EXPERT_FF)
    assert cache_k.shape[0] == dec_batch and cache_k.shape[3:] == (N_KV_HEADS, HEAD_DIM)
    assert (batch * seq) % CH_DISPATCH == 0 and (dec_batch * dec_seq) % CH_DISPATCH == 0
    assert dec_batch < MOD_ROWS

    cvec = jnp.concatenate(
        [c_ctx[None, :], c, jnp.zeros((MOD_ROWS - 1 - dec_batch, D_MODEL), F32)], axis=0)
    mod3 = _modulation(cvec, w_mod[0], b_mod[0][None, :]).reshape(MOD_ROWS, 6, D_MODEL)

    head_id = jnp.arange(QK_W) // HEAD_DIM
    heads = (head_id[:, None] == jnp.arange(LANES)[None, :]).astype(BF16)
    gqk = jnp.concatenate([jnp.tile(q_norm[0], N_HEADS), jnp.tile(k_norm[0], N_KV_HEADS)])[None, :]
    pad = LANES - N_EXPERTS - N_GROUPS
    wr = jnp.concatenate([router_exp_w[0], router_grp_w[0], jnp.zeros((D_MODEL, pad), F32)], axis=1)
    wr_hi = wr.astype(BF16)
    wr2 = jnp.concatenate([wr_hi, (wr - wr_hi.astype(F32)).astype(BF16)], axis=1)
    br = jnp.concatenate([router_exp_b[0], router_grp_b[0], jnp.zeros((pad,), F32)])[None, :]
    g1 = norm1[0][None, :]
    g2 = norm2[0][None, :]
    cw = conv_w[0]
    cb = conv_b[0][None, :]
    lam = lru_lambda[0]
    tri = (jnp.arange(TM_PRE)[:, None] > jnp.arange(TM_PRE)[None, :]).astype(BF16)

    def mixers(x, seq_len, mod_row, tables, extra_k, extra_v, h0, cnt_in):
        q, kx, vx, xr, gb, *cache = _pre(x, mod3, mod_row, g1, w_in[0], heads, gqk, tables, seq_len)
        k_segs = [(kx, seq_len)] + extra_k
        v_segs = [(vx, seq_len)] + extra_v
        attn = _attention(q, k_segs, v_segs, seq_len)
        rec, fin = _lru(xr, gb, h0, cw, cb, lru_wa[0], lru_wx[0], lru_ba[0], lru_bx[0], lam, seq_len)
        x1, h2, route, cnt, route_t = _post(attn, rec, x, mod3, mod_row, g2, w_out[0], wr2, br, tri,
                                            cnt_in)
        return x1, h2, (route, route_t), cnt, cache, fin

    mod_row_p = lambda tok: 0
    mod_row_s = lambda tok: tok // dec_seq + 1
    xp = x_prompt.reshape(batch * seq, D_MODEL)
    x1p, h2p, (route_p, route_t_p), cnt_p, (kf, vf), fin = mixers(
        xp, seq, mod_row_p, None, [], [], jnp.zeros((batch, 2, LRU_W), F32),
        jnp.zeros((1, LANES), F32))
    xs = x_sample.reshape(dec_batch * dec_seq, D_MODEL)
    ck, cv = _expand_cache(jnp.transpose(cache_k[:, 0], (0, 2, 3, 1)),
                           jnp.transpose(cache_v[:, 0], (0, 2, 3, 1)))
    x1s, h2s, (route_s, route_t_s), cnt_all, _, _ = mixers(
        xs, dec_seq, mod_row_s, _rope_tables(dec_seq), [(ck, past)], [(cv, past)],
        state_lru[:, 0], cnt_p)

    n_prompt = batch * seq
    n_tok = n_prompt + dec_batch * dec_seq
    n_tiles = (TOP_K * n_tok + N_EXPERTS * (TR_MOE - 1)) // TR_MOE
    cnt = cnt_all[0, :N_EXPERTS].astype(jnp.int32)
    ntile = (cnt + TR_MOE - 1) // TR_MOE
    tile_end = jnp.cumsum(ntile)
    seg_start = (tile_end - ntile) * TR_MOE
    n_used = tile_end[-1:]
    pad_start = seg_start + cnt
    pad_rows = tile_end * TR_MOE - pad_start

    seg_rows = jnp.broadcast_to(seg_start.astype(F32)[:, None], (N_EXPERTS, LANES))
    slots_p = _plan(route_t_p, seg_rows)[:TOP_K].reshape(-1)
    slots_s = _plan(route_t_s, seg_rows)[:TOP_K].reshape(-1)
    xsort = _dispatch(slots_p, slots_s, pad_start, pad_rows, n_used, h2p, h2s,
                      n_tiles * TR_MOE)
    ysort = _ffn(tile_end, xsort, exp_w_gate[0], exp_w_up[0], exp_w_down[0], n_tiles)
    yp = _combine(slots_p, ysort, x1p, route_p, mod3, mod_row_p)
    ys = _combine(slots_s, ysort, x1s, route_s, mod3, mod_row_s)

    return (yp.reshape(batch, seq, D_MODEL),
            ys.reshape(dec_batch, dec_seq, D_MODEL),
            jnp.transpose(kf, (0, 3, 1, 2))[:, None],
            jnp.transpose(vf, (0, 3, 1, 2))[:, None],
            fin.reshape(batch, 1, 2, LRU_W))
```

```python
import functools

import jax
import jax.numpy as jnp
from jax import lax
from jax.experimental import pallas as pl
from jax.experimental.pallas import tpu as pltpu

F32 = jnp.float32
BF16 = jnp.bfloat16

D_MODEL = 1024
GRID_W = 64
ATTN_W = 512
LRU_W = 512
HEAD_DIM = 64
N_HEADS = 8
N_KV_HEADS = 2
KV_W = N_KV_HEADS * HEAD_DIM
LRU_BLOCKS = 8
LRU_BLOCK_W = LRU_W // LRU_BLOCKS
CONV_W = 4
LRU_C = 8.0
IN_W = ATTN_W + 2 * KV_W + 2 * LRU_W
QK_W = ATTN_W + KV_W
N_GROUPS = 4
EXPERTS_PER_GROUP = 8
N_EXPERTS = N_GROUPS * EXPERTS_PER_GROUP
TOP_K = 2
EXPERT_FF = D_MODEL // 4
ROPE_THETA = 10000.0
EPS = 1e-6

LANES = 128
SUBLANES = 8
MOD_ROWS = 8
VMEM_LIMIT = 48 * 1024 * 1024

TM_PRE = 512
TQ_ATT = 1024
TC_LRU = 512
TR_MOE = 512
CH_DISPATCH = 1024
COMBINE_CHUNK = 32
FFN_IN_SLOTS = 4
FFN_OUT_SLOTS = 3


def _cparams(sem):
    return pltpu.CompilerParams(dimension_semantics=sem, vmem_limit_bytes=VMEM_LIMIT)


def _dot(a, b):
    return jnp.dot(a, b, preferred_element_type=F32)


def _dot_nt(a, b):
    return lax.dot_general(a, b, (((1,), (1,)), ((), ())), preferred_element_type=F32)


ROW_TILE = (D_MODEL // LANES, LANES)
LOG2_E = 1.4426950408889634
Q_SCALE = HEAD_DIM ** -0.5 * LOG2_E
ONES_LANE_EVEN = HEAD_DIM
ONES_LANE_ODD = 0


def _to_row_tiles(x):
    cols = jnp.stack([x[:, c * LANES:(c + 1) * LANES] for c in range(D_MODEL // LANES)], axis=0)
    return jnp.swapaxes(cols, 0, 1)


def _from_row_tiles(x3):
    cols = jnp.swapaxes(x3, 0, 1)
    return jnp.concatenate([cols[c] for c in range(D_MODEL // LANES)], axis=1)


def _sigmoid(x):
    return 0.5 * jnp.tanh(0.5 * x) + 0.5


def _split_bf16(x):
    hi = x.astype(BF16)
    lo = (x - hi.astype(F32)).astype(BF16)
    return hi, lo


def _mod_kernel(c_ref, w_ref, b_ref, o_ref):
    c = c_ref[...]
    s = (c * jax.nn.sigmoid(c)).astype(BF16)
    o_ref[...] = _dot(s, w_ref[...].astype(BF16)) + b_ref[...]


def _modulation(cvec, w_mod, b_mod):
    n_out = w_mod.shape[1]
    tn = n_out // 4
    return pl.pallas_call(
        _mod_kernel,
        grid=(n_out // tn,),
        in_specs=[pl.BlockSpec((MOD_ROWS, D_MODEL), lambda j: (0, 0)),
                  pl.BlockSpec((D_MODEL, tn), lambda j: (0, j)),
                  pl.BlockSpec((1, tn), lambda j: (0, j))],
        out_specs=pl.BlockSpec((MOD_ROWS, tn), lambda j: (0, j)),
        out_shape=jax.ShapeDtypeStruct((MOD_ROWS, n_out), F32),
        compiler_params=_cparams(("arbitrary",)),
        name="modulation",
    )(cvec, w_mod, b_mod)


def _store_cache(ref, col):
    seq = ref.shape[-1]
    col_t = jnp.transpose(col)
    for j in range(ref.shape[0]):
        ref[j] = col_t[:, j * seq:(j + 1) * seq].reshape(N_KV_HEADS, HEAD_DIM, seq)


def _store_expanded(xref, col, one):
    lane = lax.broadcasted_iota(jnp.int32, col.shape, 1)
    lo_half = lane < HEAD_DIM
    swapped = pltpu.roll(col, HEAD_DIM, 1)
    fill_hi = jnp.where(lane == ONES_LANE_EVEN, one, 0.0)
    fill_lo = jnp.where(lane == ONES_LANE_ODD, one, 0.0)
    xref[:, 0 * LANES:1 * LANES] = jnp.where(lo_half, col, fill_hi).astype(BF16)
    xref[:, 1 * LANES:2 * LANES] = jnp.where(lo_half, fill_lo, swapped).astype(BF16)
    xref[:, 2 * LANES:3 * LANES] = jnp.where(lo_half, swapped, fill_hi).astype(BF16)
    xref[:, 3 * LANES:4 * LANES] = jnp.where(lo_half, fill_lo, col).astype(BF16)


def _pre_kernel(*refs, rope):
    if rope:
        (x_ref, mod_ref, g1_ref, win_ref, heads_ref, heads_t_ref, gqk_ref, cos_ref, sin_ref,
         q_ref, kx_ref, vx_ref, xr_ref, gb_ref, win_s) = refs
    else:
        (x_ref, mod_ref, g1_ref, win_ref, heads_ref, heads_t_ref, gqk_ref,
         q_ref, kx_ref, vx_ref, xr_ref, gb_ref, kf_ref, vf_ref, win_s) = refs

    @pl.when(pl.program_id(0) == 0)
    def _():
        win_s[...] = win_ref[...].astype(BF16)

    x = x_ref[...]
    m = mod_ref[0]
    ms = jnp.mean(x * x, axis=-1, keepdims=True)
    y = x * lax.rsqrt(ms + EPS) * g1_ref[...]
    h = y * (1.0 + m[1:2]) + m[0:1]
    z = _dot(h.astype(BF16), win_s[...])

    qk = z[:, :QK_W]
    ss = _dot((qk * qk).astype(BF16), heads_ref[...])
    hi, lo = _split_bf16(lax.rsqrt(ss * (1.0 / HEAD_DIM) + EPS))
    qk = qk * (_dot(hi, heads_t_ref[...]) + _dot(lo, heads_t_ref[...])) * gqk_ref[...]

    lane = lax.broadcasted_iota(jnp.int32, (x.shape[0], LANES), 1)
    cols = []
    for c in range(QK_W // LANES):
        xc = qk[:, c * LANES:(c + 1) * LANES]
        if rope:
            left = pltpu.roll(xc, LANES - HEAD_DIM // 4, 1)
            right = pltpu.roll(xc, HEAD_DIM // 4, 1)
            rot = jnp.where((lane // (HEAD_DIM // 4)) % 2 == 0, left, right)
            xc = xc * cos_ref[...] + rot * sin_ref[...]
        cols.append(xc)
    for c in range(ATTN_W // LANES):
        q_ref[:, c * LANES:(c + 1) * LANES] = (cols[c] * Q_SCALE).astype(BF16)

    k_col = cols[ATTN_W // LANES]
    v_col = z[:, QK_W:QK_W + KV_W]
    _store_expanded(kx_ref, k_col, 0.0)
    _store_expanded(vx_ref, v_col, 1.0)
    if not rope:
        _store_cache(kf_ref, k_col)
        _store_cache(vf_ref, v_col)

    xr_ref[...] = z[:, QK_W + KV_W:QK_W + KV_W + LRU_W]
    gb_ref[...] = z[:, QK_W + KV_W + LRU_W:]


def _pre(x, mod3, mod_row, g1, w_in, heads, gqk, tables, seq_len):
    n = x.shape[0]
    tm = TM_PRE
    tiles_per_seq = seq_len // tm
    rope = tables is not None
    const = lambda i: (0, 0)
    in_specs = [pl.BlockSpec((tm, D_MODEL), lambda i: (i, 0)),
                pl.BlockSpec((1, 6, D_MODEL), lambda i: (mod_row(i * tm), 0, 0)),
                pl.BlockSpec((1, D_MODEL), const),
                pl.BlockSpec((D_MODEL, IN_W), const),
                pl.BlockSpec((QK_W, LANES), const),
                pl.BlockSpec((LANES, QK_W), const),
                pl.BlockSpec((1, QK_W), const)]
    args = [x, mod3, g1, w_in, heads, heads.T, gqk]
    if rope:
        in_specs += [pl.BlockSpec((tm, LANES), lambda i: (i % tiles_per_seq, 0))] * 2
        args += list(tables)
    row = lambda w: pl.BlockSpec((tm, w), lambda i: (i, 0))
    out_shape = [jax.ShapeDtypeStruct((n, ATTN_W), BF16),
                 jax.ShapeDtypeStruct((n, 4 * LANES), BF16),
                 jax.ShapeDtypeStruct((n, 4 * LANES), BF16),
                 jax.ShapeDtypeStruct((n, LRU_W), F32),
                 jax.ShapeDtypeStruct((n, LRU_W), F32)]
    out_specs = [row(ATTN_W), row(4 * LANES), row(4 * LANES), row(LRU_W), row(LRU_W)]
    if not rope:
        assert tm % seq_len == 0
        per_tile = tm // seq_len
        cache = pl.BlockSpec((per_tile, N_KV_HEADS, HEAD_DIM, seq_len), lambda i: (i, 0, 0, 0))
        out_shape += [jax.ShapeDtypeStruct((n // seq_len, N_KV_HEADS, HEAD_DIM, seq_len), F32)] * 2
        out_specs += [cache, cache]
    return pl.pallas_call(
        functools.partial(_pre_kernel, rope=rope),
        grid=(n // tm,),
        in_specs=in_specs, out_specs=out_specs, out_shape=out_shape,
        scratch_shapes=[pltpu.VMEM((D_MODEL, IN_W), BF16)],
        compiler_params=_cparams(("arbitrary",)),
        name="pre_rope" if rope else "pre",
    )(*args)


def _attn_kernel(*refs, n_seg):
    q_ref = refs[0]
    k_refs = refs[1:1 + n_seg]
    v_refs = refs[1 + n_seg:1 + 2 * n_seg]
    o_ref = refs[1 + 2 * n_seg]
    lane = lax.broadcasted_iota(jnp.int32, (q_ref.shape[0], LANES), 1)
    for c in range(ATTN_W // LANES):
        qc = q_ref[:, c * LANES:(c + 1) * LANES]
        g = c // 2
        accs = []
        for par in range(2):
            sl = slice((2 * g + par) * LANES, (2 * g + par + 1) * LANES)
            ss = [_dot_nt(qc, k[:, sl]) for k in k_refs]
            mx = functools.reduce(jnp.maximum, [jnp.max(s, axis=-1, keepdims=True) for s in ss])
            ps = [jnp.exp2((s - mx).astype(BF16)) for s in ss]
            accs.append(functools.reduce(lambda a, b: a + b,
                                         [_dot(p, v[:, sl]) for p, v in zip(ps, v_refs)]))
        even = accs[0] / accs[0][:, ONES_LANE_EVEN:ONES_LANE_EVEN + 1]
        odd = accs[1] / accs[1][:, ONES_LANE_ODD:ONES_LANE_ODD + 1]
        o_ref[:, c * LANES:(c + 1) * LANES] = jnp.where(lane < HEAD_DIM, even, odd).astype(BF16)


def _attention(q, k_segs, v_segs, seq_len):
    n = q.shape[0]
    tq = min(TQ_ATT, seq_len)
    nq = seq_len // tq
    n_seg = len(k_segs)
    in_specs = [pl.BlockSpec((tq, ATTN_W), lambda b, i: (b * nq + i, 0))]
    for arr, t in list(k_segs) + list(v_segs):
        in_specs.append(pl.BlockSpec((t, 4 * LANES), lambda b, i: (b, 0)))
    return pl.pallas_call(
        functools.partial(_attn_kernel, n_seg=n_seg),
        grid=(n // seq_len, nq),
        in_specs=in_specs,
        out_specs=pl.BlockSpec((tq, ATTN_W), lambda b, i: (b * nq + i, 0)),
        out_shape=jax.ShapeDtypeStruct((n, ATTN_W), BF16),
        compiler_params=_cparams(("arbitrary", "arbitrary")),
        name="attention_%dseg" % n_seg,
    )(q, *[a for a, _ in k_segs], *[a for a, _ in v_segs])


def _log_sigmoid(x):
    return jnp.minimum(x, 0.0) - jnp.log1p(jnp.exp(-jnp.abs(x)))


def _tile_scan(a, b, reverse):
    row = lax.broadcasted_iota(jnp.int32, a.shape, 0)
    d = 1
    while d < SUBLANES:
        if reverse:
            keep = row < SUBLANES - d
            shift = SUBLANES - d
        else:
            keep = row >= d
            shift = d
        a_sh = jnp.where(keep, pltpu.roll(a, shift, 0), 1.0)
        b_sh = jnp.where(keep, pltpu.roll(b, shift, 0), 0.0)
        b = a * b_sh + b
        a = a * a_sh
        d *= 2
    return a, b


def _lru_kernel(xr_ref, gb_ref, h0_ref, cw_ref, cb_ref, wa_ref, wx_ref, ba_ref, bx_ref, lam_ref,
                rec_ref, fin_ref, xpad_s, xc_s, hf_s, a_s, b_s, wf_ref, wb_ref, bf_ref, bb_ref,
                *, seq_len):
    @pl.when(pl.program_id(0) == 0)
    def _():
        pairs = LANES // LRU_BLOCK_W
        zero = jnp.zeros((LRU_BLOCK_W, LRU_BLOCK_W), F32)
        for d, (w_s, bias_s) in enumerate(((wf_ref, bf_ref), (wb_ref, bb_ref))):
            w_s[...] = jnp.zeros_like(w_s)
            for g, src in enumerate((wa_ref, wx_ref)):
                for p in range(LRU_BLOCKS // pairs):
                    rows = [jnp.concatenate([0.5 * src[d, pairs * p + q] if q == r else zero
                                             for q in range(pairs)], axis=1) for r in range(pairs)]
                    w_s[LANES * p:LANES * (p + 1), g * LRU_W + LANES * p:g * LRU_W + LANES * (p + 1)] = (
                        jnp.concatenate(rows, axis=0).astype(BF16))
            bias_s[:, :LRU_W] = 0.5 * ba_ref[d:d + 1, :]
            bias_s[:, LRU_W:] = 0.5 * bx_ref[d:d + 1, :]

    tc = min(TC_LRU, seq_len)
    n_chunks = seq_len // tc
    n_tiles = tc // SUBLANES
    zpad = jnp.zeros((SUBLANES, LRU_W), F32)
    xpad_s[0:SUBLANES, :] = zpad
    xpad_s[SUBLANES:SUBLANES + seq_len, :] = xr_ref[...]
    xpad_s[SUBLANES + seq_len:2 * SUBLANES + seq_len, :] = zpad

    half_cl = (0.5 * LRU_C * LOG2_E) * _log_sigmoid(lam_ref[...])

    def gates(xcc, w_ref, bias_ref, half_cl_d):
        t = jnp.tanh(_dot(xcc.astype(BF16), w_ref[...]) + bias_ref[...])
        a = jnp.exp2(t[:, :LRU_W] * half_cl_d + half_cl_d)
        half_x = 0.5 * xcc
        a_s[...] = a
        u = 1.0 - a * a
        b_s[...] = jnp.where(u > 0.0, u * lax.rsqrt(u), 0.0) * (t[:, LRU_W:] * half_x + half_x)

    h = h0_ref[0, 0:1, :]
    for c in range(n_chunks):
        base = c * tc
        xcc = cb_ref[...] + functools.reduce(
            lambda u, v: u + v,
            [cw_ref[j:j + 1, :] * xpad_s[base + SUBLANES - 1 + j:base + SUBLANES - 1 + j + tc, :]
             for j in range(CONV_W)])
        xc_s[base:base + tc, :] = xcc
        gates(xcc, wf_ref, bf_ref, half_cl[0:1])

        def fwd_tile(t, hc, base=base):
            r0 = pl.multiple_of(t * SUBLANES, SUBLANES)
            ca, cb = _tile_scan(a_s[pl.ds(r0, SUBLANES), :], b_s[pl.ds(r0, SUBLANES), :], False)
            hh = ca * hc + cb
            hf_s[pl.ds(base + r0, SUBLANES), :] = hh
            return hh[SUBLANES - 1:SUBLANES, :]

        h = lax.fori_loop(0, n_tiles, fwd_tile, h, unroll=4)
    fin_ref[0, 0:1, :] = h

    h = h0_ref[0, 1:2, :]
    for c in reversed(range(n_chunks)):
        base = c * tc
        gates(xc_s[base:base + tc, :], wb_ref, bb_ref, half_cl[1:2])

        def bwd_tile(t, hc, base=base):
            r0 = pl.multiple_of((n_tiles - 1 - t) * SUBLANES, SUBLANES)
            ca, cb = _tile_scan(a_s[pl.ds(r0, SUBLANES), :], b_s[pl.ds(r0, SUBLANES), :], True)
            hh = ca * hc + cb
            gate = jax.nn.gelu(gb_ref[pl.ds(base + r0, SUBLANES), :], approximate=True)
            rec_ref[pl.ds(base + r0, SUBLANES), :] = (
                (hf_s[pl.ds(base + r0, SUBLANES), :] + hh) * gate).astype(rec_ref.dtype)
            return hh[0:1, :]

        h = lax.fori_loop(0, n_tiles, bwd_tile, h, unroll=4)
    fin_ref[0, 1:2, :] = h


def _lru(xr, gb, h0, conv_w, conv_b, wa, wx, ba, bx, lam, seq_len):
    n = xr.shape[0]
    batch = n // seq_len
    const = lambda b: (0, 0)
    blocks = pl.BlockSpec((2, LRU_BLOCKS, LRU_BLOCK_W, LRU_BLOCK_W), lambda b: (0, 0, 0, 0))
    seq = pl.BlockSpec((seq_len, LRU_W), lambda b: (b, 0))
    st = pl.BlockSpec((1, 2, LRU_W), lambda b: (b, 0, 0))
    return pl.pallas_call(
        functools.partial(_lru_kernel, seq_len=seq_len),
        grid=(batch,),
        in_specs=[seq, seq, st,
                  pl.BlockSpec((CONV_W, LRU_W), const), pl.BlockSpec((1, LRU_W), const),
                  blocks, blocks,
                  pl.BlockSpec((2, LRU_W), const), pl.BlockSpec((2, LRU_W), const),
                  pl.BlockSpec((2, LRU_W), const)],
        out_specs=[seq, st],
        out_shape=[jax.ShapeDtypeStruct((n, LRU_W), BF16),
                   jax.ShapeDtypeStruct((batch, 2, LRU_W), F32)],
        scratch_shapes=[pltpu.VMEM((seq_len + 2 * SUBLANES, LRU_W), F32),
                        pltpu.VMEM((seq_len, LRU_W), F32),
                        pltpu.VMEM((seq_len, LRU_W), F32),
                        pltpu.VMEM((min(TC_LRU, seq_len), LRU_W), F32),
                        pltpu.VMEM((min(TC_LRU, seq_len), LRU_W), F32),
                        pltpu.VMEM((LRU_W, 2 * LRU_W), BF16),
                        pltpu.VMEM((LRU_W, 2 * LRU_W), BF16),
                        pltpu.VMEM((1, 2 * LRU_W), F32),
                        pltpu.VMEM((1, 2 * LRU_W), F32)],
        compiler_params=_cparams(("arbitrary",)),
        name="lru_%d" % seq_len,
    )(xr, gb, h0, conv_w, conv_b, wa, wx, ba, bx, lam)


def _post_kernel(attn_ref, rec_ref, x_ref, mod_ref, g2_ref, wo_ref, wr2_ref, br_ref,
                 tri_ref, cnt_in_ref, x1_ref, h2_ref, route_ref, cnt_ref, route_t_ref, cnt_s, wo_s):
    @pl.when(pl.program_id(0) == 0)
    def _():
        cnt_s[...] = cnt_in_ref[...]
        wo_s[...] = wo_ref[...].astype(BF16)

    m = mod_ref[0]
    u = _dot(attn_ref[...], wo_s[:ATTN_W, :]) + _dot(rec_ref[...], wo_s[ATTN_W:, :])
    x1 = x_ref[...] + m[2:3] * u
    x1_ref[...] = x1
    ms = jnp.mean(x1 * x1, axis=-1, keepdims=True)
    h2 = x1 * lax.rsqrt(ms + EPS) * g2_ref[...]
    h2 = h2 * (1.0 + m[4:5]) + m[3:4]
    hi, lo = _split_bf16(h2)
    h2_ref[...] = _to_row_tiles(hi)

    hw = _dot(hi, wr2_ref[...])
    logits = hw[:, :LANES] + hw[:, LANES:] + _dot(lo, wr2_ref[:, :LANES]) + br_ref[...]
    lane_i = lax.broadcasted_iota(jnp.int32, logits.shape, 1)
    lane = lane_i.astype(F32)
    lane_group = (lane_i // EXPERTS_PER_GROUP).astype(F32)
    neg = -jnp.inf
    big = float(1 << 20)
    gmask = (lane_i >= N_EXPERTS) & (lane_i < N_EXPERTS + N_GROUPS)
    gl = jnp.where(gmask, logits, neg)
    gmax = jnp.max(gl, axis=-1, keepdims=True)
    gidx = jnp.min(jnp.where(gl == gmax, lane - N_EXPERTS, big), axis=-1, keepdims=True)
    p_sel = 1.0 / jnp.sum(jnp.where(gmask, jnp.exp(gl - gmax), 0.0), axis=-1, keepdims=True)

    emask = (lane_i < N_EXPERTS) & (lane_group == gidx)
    el = jnp.where(emask, logits, neg)
    v1 = jnp.max(el, axis=-1, keepdims=True)
    i1 = jnp.min(jnp.where(el == v1, lane, big), axis=-1, keepdims=True)
    el2 = jnp.where(lane == i1, neg, el)
    v2 = jnp.max(el2, axis=-1, keepdims=True)
    i2 = jnp.min(jnp.where(el2 == v2, lane, big), axis=-1, keepdims=True)
    e2 = jnp.exp(v2 - v1)
    w1 = p_sel / (1.0 + e2)
    w2 = p_sel * e2 / (1.0 + e2)

    oh1 = lane == i1
    oh2 = lane == i2
    oh = jnp.where(oh1, 1.0, 0.0) + jnp.where(oh2, 1.0, 0.0)
    before = _dot(tri_ref[...], oh.astype(BF16)) + cnt_s[...]
    rank1 = jnp.sum(jnp.where(oh1, before, 0.0), axis=-1, keepdims=True)
    rank2 = jnp.sum(jnp.where(oh2, before, 0.0), axis=-1, keepdims=True)
    cnt = cnt_s[...] + jnp.sum(oh, axis=0, keepdims=True)
    cnt_s[...] = cnt
    cnt_ref[...] = cnt
    fields = (i1, i2, rank1, rank2, w1, w2)
    route = jnp.zeros(logits.shape, F32)
    for k, val in enumerate(fields):
        route = jnp.where(lane_i == k, val, route)
    route_ref[...] = route
    route_t_ref[...] = jnp.transpose(route)[:SUBLANES, :]


def _post(attn, rec, x, mod3, mod_row, g2, w_out, wr2, br, tri, cnt_in):
    n = x.shape[0]
    tm = TM_PRE
    const = lambda i: (0, 0)
    row = lambda w: pl.BlockSpec((tm, w), lambda i: (i, 0))
    return pl.pallas_call(
        _post_kernel,
        grid=(n // tm,),
        in_specs=[row(ATTN_W), row(LRU_W), row(D_MODEL),
                  pl.BlockSpec((1, 6, D_MODEL), lambda i: (mod_row(i * tm), 0, 0)),
                  pl.BlockSpec((1, D_MODEL), const),
                  pl.BlockSpec((D_MODEL, D_MODEL), const),
                  pl.BlockSpec((D_MODEL, 2 * LANES), const),
                  pl.BlockSpec((1, LANES), const),
                  pl.BlockSpec((tm, tm), const),
                  pl.BlockSpec((1, LANES), const)],
        out_specs=[row(D_MODEL), pl.BlockSpec((tm,) + ROW_TILE, lambda i: (i, 0, 0)), row(LANES),
                   pl.BlockSpec((1, LANES), const),
                   pl.BlockSpec((SUBLANES, tm), lambda i: (0, i))],
        out_shape=[jax.ShapeDtypeStruct((n, D_MODEL), F32),
                   jax.ShapeDtypeStruct((n,) + ROW_TILE, BF16),
                   jax.ShapeDtypeStruct((n, LANES), F32),
                   jax.ShapeDtypeStruct((1, LANES), F32),
                   jax.ShapeDtypeStruct((SUBLANES, n), F32)],
        scratch_shapes=[pltpu.VMEM((1, LANES), F32), pltpu.VMEM((D_MODEL, D_MODEL), BF16)],
        compiler_params=_cparams(("arbitrary",)),
        name="post",
    )(attn, rec, x, mod3, g2, w_out, wr2, br, tri, cnt_in)


def _row_copy(src_ref, src_row, dst_ref, dst_row, sem):
    return pltpu.make_async_copy(src_ref.at[pl.ds(src_row, 1)], dst_ref.at[pl.ds(dst_row, 1)], sem)


def _plan_kernel(route_t_ref, seg_ref, slots_ref):
    route_t = route_t_ref[...]
    t = route_t.shape[1]
    seg = jnp.concatenate([seg_ref[...]] * (t // LANES), axis=1)
    expert = lax.broadcasted_iota(jnp.int32, seg.shape, 0).astype(F32)
    rows = []
    for k in range(TOP_K):
        start = jnp.sum(jnp.where(expert == route_t[k:k + 1, :], seg, 0.0), axis=0, keepdims=True)
        rows.append(start + route_t[TOP_K + k:TOP_K + k + 1, :])
    rows.append(jnp.zeros((SUBLANES - TOP_K, t), F32))
    slots_ref[...] = jnp.concatenate(rows, axis=0).astype(jnp.int32)


def _plan(route_t, seg_rows):
    n = route_t.shape[1]
    tm = TM_PRE
    return pl.pallas_call(
        _plan_kernel,
        grid=(n // tm,),
        in_specs=[pl.BlockSpec((SUBLANES, tm), lambda i: (0, i)),
                  pl.BlockSpec((N_EXPERTS, LANES), lambda i: (0, 0))],
        out_specs=pl.BlockSpec((SUBLANES, tm), lambda i: (0, i)),
        out_shape=jax.ShapeDtypeStruct((SUBLANES, n), jnp.int32),
        compiler_params=_cparams(("arbitrary",)),
        name="plan",
    )(route_t, seg_rows)


def _dispatch_kernel(slots_p_ref, slots_s_ref, pad_start_ref, pad_rows_ref, nu_ref, h2p_ref, h2s_ref,
                     xs_ref, zero_s, sem, zsem, *, n_prompt):
    i = pl.program_id(0)
    ch = CH_DISPATCH

    def zero_fill(op):
        for e in range(N_EXPERTS):
            rows = pad_rows_ref[e]
            for b in range(TR_MOE.bit_length() - 1):
                size = 1 << b
                off = (rows >> (b + 1)) << (b + 1)

                @pl.when(((rows >> b) & 1) == 1)
                def _():
                    op(pltpu.make_async_copy(zero_s.at[pl.ds(0, size)],
                                             xs_ref.at[pl.ds(pad_start_ref[e] + off, size)], zsem))

        def zero_tile(t, carry):
            op(pltpu.make_async_copy(
                zero_s, xs_ref.at[pl.ds(pl.multiple_of(t * TR_MOE, TR_MOE), TR_MOE)], zsem))
            return carry
        lax.fori_loop(nu_ref[0], xs_ref.shape[0] // TR_MOE, zero_tile, 0)

    @pl.when(i == 0)
    def _():
        zero_s[...] = jnp.zeros_like(zero_s)
        zero_fill(lambda cp: cp.start())

    @pl.when(i == pl.num_programs(0) - 1)
    def _():
        zero_fill(lambda cp: cp.wait())

    def scatter(src_ref, slots_ref, first_token):
        n_path = slots_ref.shape[0] // TOP_K

        def body(j, carry):
            for k in range(TOP_K):
                _row_copy(src_ref, j, xs_ref, slots_ref[k * n_path + i * ch + j - first_token],
                          sem).start(priority=k)
            return carry
        lax.fori_loop(0, ch, body, 0, unroll=8)
        for _ in range(TOP_K):
            pltpu.make_async_copy(src_ref, xs_ref.at[pl.ds(0, ch)], sem).wait()

    @pl.when(i < n_prompt // ch)
    def _():
        scatter(h2p_ref, slots_p_ref, 0)

    @pl.when(i >= n_prompt // ch)
    def _():
        scatter(h2s_ref, slots_s_ref, n_prompt)


def _dispatch(slots_p, slots_s, pad_start, pad_rows, n_used, h2p, h2s, n_rows):
    n_prompt = h2p.shape[0]
    n = n_prompt + h2s.shape[0]
    ch = CH_DISPATCH
    npc = n_prompt // ch
    last_p = npc - 1
    return pl.pallas_call(
        functools.partial(_dispatch_kernel, n_prompt=n_prompt),
        grid_spec=pltpu.PrefetchScalarGridSpec(
            num_scalar_prefetch=5,
            grid=(n // ch,),
            in_specs=[pl.BlockSpec((ch,) + ROW_TILE, lambda i, *_: (jnp.minimum(i, last_p), 0, 0)),
                      pl.BlockSpec((ch,) + ROW_TILE, lambda i, *_: (jnp.maximum(i - npc, 0), 0, 0))],
            out_specs=pl.BlockSpec(memory_space=pl.ANY),
            scratch_shapes=[pltpu.VMEM((TR_MOE,) + ROW_TILE, BF16), pltpu.SemaphoreType.DMA,
                            pltpu.SemaphoreType.DMA]),
        out_shape=jax.ShapeDtypeStruct((n_rows,) + ROW_TILE, BF16),
        compiler_params=_cparams(("arbitrary",)),
        name="dispatch",
    )(slots_p, slots_s, pad_start, pad_rows, n_used, h2p, h2s)


def _ffn_kernel(tend_ref, xs_ref, wg_ref, wu_ref, wd_ref, ys_ref,
                xbuf, ybuf, wg_s, wu_s, wd_s, sem_in, sem_out):
    e = pl.program_id(0)
    n_used = tend_ref[N_EXPERTS - 1]
    t_first = jnp.where(e == 0, 0, tend_ref[jnp.maximum(e - 1, 0)])
    t_last = tend_ref[e]
    n_in = xbuf.shape[0]
    n_out = ybuf.shape[0]

    def tile_rows(t):
        return pl.ds(pl.multiple_of(t * TR_MOE, TR_MOE), TR_MOE)

    def fetch(t):
        return pltpu.make_async_copy(xs_ref.at[tile_rows(t)], xbuf.at[t % n_in], sem_in.at[t % n_in])

    def writeback(t):
        return pltpu.make_async_copy(ybuf.at[t % n_out], ys_ref.at[tile_rows(t)],
                                     sem_out.at[t % n_out])

    @pl.when(e == 0)
    def _():
        for t in range(n_in - 1):
            @pl.when(t < n_used)
            def _():
                fetch(t).start()

    @pl.when(t_last > t_first)
    def _():
        wg_s[...] = wg_ref[0].astype(BF16)
        wu_s[...] = wu_ref[0].astype(BF16)
        wd_s[...] = wd_ref[0].astype(BF16)

    def tile(t, carry):
        fetch(t).wait()

        @pl.when(t + n_in - 1 < n_used)
        def _():
            fetch(t + n_in - 1).start()

        @pl.when(t >= n_out)
        def _():
            writeback(t - n_out).wait()

        x = _from_row_tiles(xbuf[t % n_in])
        hg = _dot(x, wg_s[...])
        hu = _dot(x, wu_s[...])
        act = (hg * _sigmoid(hg)) * hu
        ybuf[t % n_out] = _to_row_tiles(_dot(act.astype(BF16), wd_s[...]).astype(BF16))
        writeback(t).start()
        return carry

    lax.fori_loop(t_first, t_last, tile, 0)

    @pl.when(e == N_EXPERTS - 1)
    def _():
        for back in range(n_out, 0, -1):
            @pl.when(n_used >= back)
            def _():
                writeback(n_used - back).wait()
        n_all = ys_ref.shape[0] // TR_MOE
        ybuf[0] = jnp.zeros(ybuf.shape[1:], ybuf.dtype)

        def zero_tile(t):
            return pltpu.make_async_copy(ybuf.at[0], ys_ref.at[tile_rows(t)], sem_out.at[0])

        lax.fori_loop(n_used, n_all, lambda t, c: (zero_tile(t).start(), c)[1], 0)
        lax.fori_loop(n_used, n_all, lambda t, c: (zero_tile(t).wait(), c)[1], 0)


def _ffn(tile_end, xs, wg, wu, wd, n_tiles):
    tr = TR_MOE
    wsel = lambda e, tend: (e, 0, 0)
    return pl.pallas_call(
        _ffn_kernel,
        grid_spec=pltpu.PrefetchScalarGridSpec(
            num_scalar_prefetch=1,
            grid=(N_EXPERTS,),
            in_specs=[pl.BlockSpec(memory_space=pl.ANY),
                      pl.BlockSpec((1, D_MODEL, EXPERT_FF), wsel),
                      pl.BlockSpec((1, D_MODEL, EXPERT_FF), wsel),
                      pl.BlockSpec((1, EXPERT_FF, D_MODEL), wsel)],
            out_specs=pl.BlockSpec(memory_space=pl.ANY),
            scratch_shapes=[pltpu.VMEM((FFN_IN_SLOTS, tr) + ROW_TILE, BF16),
                            pltpu.VMEM((FFN_OUT_SLOTS, tr) + ROW_TILE, BF16),
                            pltpu.VMEM((D_MODEL, EXPERT_FF), BF16),
                            pltpu.VMEM((D_MODEL, EXPERT_FF), BF16),
                            pltpu.VMEM((EXPERT_FF, D_MODEL), BF16),
                            pltpu.SemaphoreType.DMA((FFN_IN_SLOTS,)),
                            pltpu.SemaphoreType.DMA((FFN_OUT_SLOTS,))]),
        out_shape=jax.ShapeDtypeStruct((n_tiles * tr,) + ROW_TILE, BF16),
        compiler_params=_cparams(("arbitrary",)),
        name="ffn",
    )(tile_end, xs, wg, wu, wd)


def _combine_kernel(slots_ref, ys_ref, x1_ref, route_ref, mod_ref, y_ref, b1_s, b2_s, sems):
    i = pl.program_id(0)
    tm = x1_ref.shape[0]
    n_path = slots_ref.shape[0] // TOP_K

    n_steps = pl.num_programs(0)
    ch = COMBINE_CHUNK

    def gather_chunk(step, slot, c):
        for t in range(ch):
            j = c * ch + t
            _row_copy(ys_ref, slots_ref[step * tm + j], b1_s.at[slot], j, sems.at[slot]).start(
                priority=0)
            _row_copy(ys_ref, slots_ref[n_path + step * tm + j], b2_s.at[slot], j,
                      sems.at[slot]).start(priority=1)

    def wait_slot(slot):
        for buf in (b1_s, b2_s):
            pltpu.make_async_copy(ys_ref.at[pl.ds(0, tm)], buf.at[slot], sems.at[slot]).wait()

    @pl.when(i == 0)
    def _():
        lax.fori_loop(0, tm // ch, lambda c, carry: (gather_chunk(0, 0, c), carry)[1], 0)

    slot = i % 2
    wait_slot(slot)
    gate = mod_ref[0][5:6]
    nxt = (i + 1) % n_steps

    def body(c, carry):
        rows = pl.ds(pl.multiple_of(c * ch, ch), ch)
        route = route_ref[rows, :]
        moe = (route[:, 4:5] * _from_row_tiles(b1_s[slot, rows]).astype(F32)
               + route[:, 5:6] * _from_row_tiles(b2_s[slot, rows]).astype(F32))
        y_ref[rows, :] = x1_ref[rows, :] + gate * moe
        gather_chunk(nxt, 1 - slot, c)
        return carry
    lax.fori_loop(0, tm // ch, body, 0)

    @pl.when(i == n_steps - 1)
    def _():
        wait_slot(1 - slot)


def _combine(slots, ys, x1, route, mod3, mod_row):
    n = x1.shape[0]
    tm = TM_PRE
    row = lambda w: pl.BlockSpec((tm, w), lambda i, *_: (i, 0))
    return pl.pallas_call(
        _combine_kernel,
        grid_spec=pltpu.PrefetchScalarGridSpec(
            num_scalar_prefetch=1,
            grid=(n // tm,),
            in_specs=[pl.BlockSpec(memory_space=pl.ANY), row(D_MODEL), row(LANES),
                      pl.BlockSpec((1, 6, D_MODEL),
                                   lambda i, *_: (mod_row(i * tm), 0, 0))],
            out_specs=row(D_MODEL),
            scratch_shapes=[pltpu.VMEM((2, tm) + ROW_TILE, BF16), pltpu.VMEM((2, tm) + ROW_TILE, BF16),
                            pltpu.SemaphoreType.DMA((2,))]),
        out_shape=jax.ShapeDtypeStruct((n, D_MODEL), F32),
        compiler_params=_cparams(("arbitrary",)),
        name="combine",
    )(slots, ys, x1, route, mod3)


def _rope_tables(length):
    rows = length // GRID_W
    r, col = jnp.meshgrid(jnp.arange(rows), jnp.arange(GRID_W), indexing='ij')
    r = r.reshape(-1).astype(F32)
    col = col.reshape(-1).astype(F32)
    half = HEAD_DIM // 2
    inv = ROPE_THETA ** (-jnp.arange(0, half, 2, dtype=F32) / half)
    ang_r = r[:, None] * inv
    ang_c = col[:, None] * inv
    ang = jnp.concatenate([ang_r, ang_r, ang_c, ang_c], axis=-1)
    sign = jnp.where((jnp.arange(HEAD_DIM) // (HEAD_DIM // 4)) % 2 == 0, -1.0, 1.0).astype(F32)
    cos = jnp.tile(jnp.cos(ang), (1, LANES // HEAD_DIM))
    sin = jnp.tile(jnp.sin(ang) * sign, (1, LANES // HEAD_DIM))
    return cos, sin


def _expand_cache_kernel(k_ref, v_ref, kx_ref, vx_ref):
    for ref, xref, one in ((k_ref, kx_ref, 0.0), (v_ref, vx_ref, 1.0)):
        col = jnp.transpose(ref[0].reshape(KV_W, ref.shape[-1]))
        _store_expanded(xref, col, one)


def _expand_cache(cache_k, cache_v):
    b, _, _, t = cache_k.shape
    src = pl.BlockSpec((1, N_KV_HEADS, HEAD_DIM, t), lambda i: (i, 0, 0, 0))
    dst = pl.BlockSpec((t, 4 * LANES), lambda i: (i, 0))
    return pl.pallas_call(
        _expand_cache_kernel,
        grid=(b,),
        in_specs=[src, src], out_specs=[dst, dst],
        out_shape=[jax.ShapeDtypeStruct((b * t, 4 * LANES), BF16)] * 2,
        compiler_params=_cparams(("arbitrary",)),
        name="expand_cache",
    )(cache_k, cache_v)


def kernel(x_prompt, x_sample, cache_k, cache_v, state_lru, c, c_ctx, w_mod, b_mod, norm1, norm2,
           w_in, q_norm, k_norm, conv_w, conv_b, lru_wa, lru_ba, lru_wx, lru_bx, lru_lambda, w_out,
           router_grp_w, router_grp_b, router_exp_w, router_exp_b, exp_w_gate, exp_w_up, exp_w_down):
    batch, seq, _ = x_prompt.shape
    dec_batch, dec_seq, _ = x_sample.shape
    past = cache_k.shape[2]
    depth = w_mod.shape[0]
    assert depth == 1
    assert x_prompt.shape[2] == D_MODEL and w_in.shape[1:] == (D_MODEL, IN_W)
    assert exp_w_gate.shape[1:] == (N_EXPERTS, D_MODEL, EXPERT_FF)
    assert cache_k.shape[0] == dec_batch and cache_k.shape[3:] == (N_KV_HEADS, HEAD_DIM)
    assert (batch * seq) % CH_DISPATCH == 0 and (dec_batch * dec_seq) % CH_DISPATCH == 0
    assert dec_batch < MOD_ROWS

    cvec = jnp.concatenate(
        [c_ctx[None, :], c, jnp.zeros((MOD_ROWS - 1 - dec_batch, D_MODEL), F32)], axis=0)
    mod3 = _modulation(cvec, w_mod[0], b_mod[0][None, :]).reshape(MOD_ROWS, 6, D_MODEL)

    head_id = jnp.arange(QK_W) // HEAD_DIM
    heads = (head_id[:, None] == jnp.arange(LANES)[None, :]).astype(BF16)
    gqk = jnp.concatenate([jnp.tile(q_norm[0], N_HEADS), jnp.tile(k_norm[0], N_KV_HEADS)])[None, :]
    pad = LANES - N_EXPERTS - N_GROUPS
    wr = jnp.concatenate([router_exp_w[0], router_grp_w[0], jnp.zeros((D_MODEL, pad), F32)], axis=1)
    wr_hi = wr.astype(BF16)
    wr2 = jnp.concatenate([wr_hi, (wr - wr_hi.astype(F32)).astype(BF16)], axis=1)
    br = jnp.concatenate([router_exp_b[0], router_grp_b[0], jnp.zeros((pad,), F32)])[None, :]
    g1 = norm1[0][None, :]
    g2 = norm2[0][None, :]
    cw = conv_w[0]
    cb = conv_b[0][None, :]
    lam = lru_lambda[0]
    tri = (jnp.arange(TM_PRE)[:, None] > jnp.arange(TM_PRE)[None, :]).astype(BF16)

    def mixers(x, seq_len, mod_row, tables, extra_k, extra_v, h0, cnt_in):
        q, kx, vx, xr, gb, *cache = _pre(x, mod3, mod_row, g1, w_in[0], heads, gqk, tables, seq_len)
        k_segs = [(kx, seq_len)] + extra_k
        v_segs = [(vx, seq_len)] + extra_v
        attn = _attention(q, k_segs, v_segs, seq_len)
        rec, fin = _lru(xr, gb, h0, cw, cb, lru_wa[0], lru_wx[0], lru_ba[0], lru_bx[0], lam, seq_len)
        x1, h2, route, cnt, route_t = _post(attn, rec, x, mod3, mod_row, g2, w_out[0], wr2, br, tri,
                                            cnt_in)
        return x1, h2, (route, route_t), cnt, cache, fin

    mod_row_p = lambda tok: 0
    mod_row_s = lambda tok: tok // dec_seq + 1
    xp = x_prompt.reshape(batch * seq, D_MODEL)
    x1p, h2p, (route_p, route_t_p), cnt_p, (kf, vf), fin = mixers(
        xp, seq, mod_row_p, None, [], [], jnp.zeros((batch, 2, LRU_W), F32),
        jnp.zeros((1, LANES), F32))
    xs = x_sample.reshape(dec_batch * dec_seq, D_MODEL)
    ck, cv = _expand_cache(jnp.transpose(cache_k[:, 0], (0, 2, 3, 1)),
                           jnp.transpose(cache_v[:, 0], (0, 2, 3, 1)))
    x1s, h2s, (route_s, route_t_s), cnt_all, _, _ = mixers(
        xs, dec_seq, mod_row_s, _rope_tables(dec_seq), [(ck, past)], [(cv, past)],
        state_lru[:, 0], cnt_p)

    n_prompt = batch * seq
    n_tok = n_prompt + dec_batch * dec_seq
    n_tiles = (TOP_K * n_tok + N_EXPERTS * (TR_MOE - 1)) // TR_MOE
    cnt = cnt_all[0, :N_EXPERTS].astype(jnp.int32)
    ntile = (cnt + TR_MOE - 1) // TR_MOE
    tile_end = jnp.cumsum(ntile)
    seg_start = (tile_end - ntile) * TR_MOE
    n_used = tile_end[-1:]
    pad_start = seg_start + cnt
    pad_rows = tile_end * TR_MOE - pad_start

    seg_rows = jnp.broadcast_to(seg_start.astype(F32)[:, None], (N_EXPERTS, LANES))
    slots_p = _plan(route_t_p, seg_rows)[:TOP_K].reshape(-1)
    slots_s = _plan(route_t_s, seg_rows)[:TOP_K].reshape(-1)
    xsort = _dispatch(slots_p, slots_s, pad_start, pad_rows, n_used, h2p, h2s,
                      n_tiles * TR_MOE)
    ysort = _ffn(tile_end, xsort, exp_w_gate[0], exp_w_up[0], exp_w_down[0], n_tiles)
    yp = _combine(slots_p, ysort, x1p, route_p, mod3, mod_row_p)
    ys = _combine(slots_s, ysort, x1s, route_s, mod3, mod_row_s)

    return (yp.reshape(batch, seq, D_MODEL),
            ys.reshape(dec_batch, dec_seq, D_MODEL),
            jnp.transpose(kf, (0, 3, 1, 2))[:, None],
            jnp.transpose(vf, (0, 3, 1, 2))[:, None],
            fin.reshape(batch, 1, 2, LRU_W))
```

```python
import functools

import jax
import jax.numpy as jnp
from jax import lax
from jax.experimental import pallas as pl
from jax.experimental.pallas import tpu as pltpu

F32 = jnp.float32
BF16 = jnp.bfloat16

D_MODEL = 1024
GRID_W = 64
ATTN_W = 512
LRU_W = 512
HEAD_DIM = 64
N_HEADS = 8
N_KV_HEADS = 2
KV_W = N_KV_HEADS * HEAD_DIM
LRU_BLOCKS = 8
LRU_BLOCK_W = LRU_W // LRU_BLOCKS
CONV_W = 4
LRU_C = 8.0
IN_W = ATTN_W + 2 * KV_W + 2 * LRU_W
QK_W = ATTN_W + KV_W
N_GROUPS = 4
EXPERTS_PER_GROUP = 8
N_EXPERTS = N_GROUPS * EXPERTS_PER_GROUP
TOP_K = 2
EXPERT_FF = D_MODEL // 4
ROPE_THETA = 10000.0
EPS = 1e-6

LANES = 128
SUBLANES = 8
MOD_ROWS = 8
VMEM_LIMIT = 48 * 1024 * 1024

TM_PRE = 512
TQ_ATT = 1024
TC_LRU = 512
TR_MOE = 512
CH_DISPATCH = 2048
COMBINE_CHUNK = 32
FFN_IN_SLOTS = 4
FFN_OUT_SLOTS = 3


def _cparams(sem):
    return pltpu.CompilerParams(dimension_semantics=sem, vmem_limit_bytes=VMEM_LIMIT)


def _dot(a, b):
    return jnp.dot(a, b, preferred_element_type=F32)


def _dot_nt(a, b):
    return lax.dot_general(a, b, (((1,), (1,)), ((), ())), preferred_element_type=F32)


ROW_TILE = (D_MODEL // LANES, LANES)
LOG2_E = 1.4426950408889634
Q_SCALE = HEAD_DIM ** -0.5 * LOG2_E
ONES_LANE_EVEN = HEAD_DIM
ONES_LANE_ODD = 0


def _to_row_tiles(x):
    cols = jnp.stack([x[:, c * LANES:(c + 1) * LANES] for c in range(D_MODEL // LANES)], axis=0)
    return jnp.swapaxes(cols, 0, 1)


def _from_row_tiles(x3):
    cols = jnp.swapaxes(x3, 0, 1)
    return jnp.concatenate([cols[c] for c in range(D_MODEL // LANES)], axis=1)


def _sigmoid(x):
    return 0.5 * jnp.tanh(0.5 * x) + 0.5


def _split_bf16(x):
    hi = x.astype(BF16)
    lo = (x - hi.astype(F32)).astype(BF16)
    return hi, lo


def _mod_kernel(c_ref, w_ref, b_ref, o_ref):
    c = c_ref[...]
    s = (c * jax.nn.sigmoid(c)).astype(BF16)
    o_ref[...] = _dot(s, w_ref[...].astype(BF16)) + b_ref[...]


def _modulation(cvec, w_mod, b_mod):
    n_out = w_mod.shape[1]
    tn = n_out // 4
    return pl.pallas_call(
        _mod_kernel,
        grid=(n_out // tn,),
        in_specs=[pl.BlockSpec((MOD_ROWS, D_MODEL), lambda j: (0, 0)),
                  pl.BlockSpec((D_MODEL, tn), lambda j: (0, j)),
                  pl.BlockSpec((1, tn), lambda j: (0, j))],
        out_specs=pl.BlockSpec((MOD_ROWS, tn), lambda j: (0, j)),
        out_shape=jax.ShapeDtypeStruct((MOD_ROWS, n_out), F32),
        compiler_params=_cparams(("arbitrary",)),
        name="modulation",
    )(cvec, w_mod, b_mod)


def _store_cache(ref, col):
    seq = ref.shape[-1]
    col_t = jnp.transpose(col)
    for j in range(ref.shape[0]):
        ref[j] = col_t[:, j * seq:(j + 1) * seq].reshape(N_KV_HEADS, HEAD_DIM, seq)


def _store_expanded(xref, col, one):
    lane = lax.broadcasted_iota(jnp.int32, col.shape, 1)
    lo_half = lane < HEAD_DIM
    swapped = pltpu.roll(col, HEAD_DIM, 1)
    fill_hi = jnp.where(lane == ONES_LANE_EVEN, one, 0.0)
    fill_lo = jnp.where(lane == ONES_LANE_ODD, one, 0.0)
    xref[:, 0 * LANES:1 * LANES] = jnp.where(lo_half, col, fill_hi).astype(BF16)
    xref[:, 1 * LANES:2 * LANES] = jnp.where(lo_half, fill_lo, swapped).astype(BF16)
    xref[:, 2 * LANES:3 * LANES] = jnp.where(lo_half, swapped, fill_hi).astype(BF16)
    xref[:, 3 * LANES:4 * LANES] = jnp.where(lo_half, fill_lo, col).astype(BF16)


def _pre_kernel(*refs, rope):
    if rope:
        (x_ref, mod_ref, g1_ref, win_ref, heads_ref, heads_t_ref, gqk_ref, cos_ref, sin_ref,
         q_ref, kx_ref, vx_ref, xr_ref, gb_ref, win_s) = refs
    else:
        (x_ref, mod_ref, g1_ref, win_ref, heads_ref, heads_t_ref, gqk_ref,
         q_ref, kx_ref, vx_ref, xr_ref, gb_ref, kf_ref, vf_ref, win_s) = refs

    @pl.when(pl.program_id(0) == 0)
    def _():
        win_s[...] = win_ref[...].astype(BF16)

    x = x_ref[...]
    m = mod_ref[0]
    ms = jnp.mean(x * x, axis=-1, keepdims=True)
    y = x * lax.rsqrt(ms + EPS) * g1_ref[...]
    h = y * (1.0 + m[1:2]) + m[0:1]
    z = _dot(h.astype(BF16), win_s[...])

    qk = z[:, :QK_W]
    ss = _dot((qk * qk).astype(BF16), heads_ref[...])
    hi, lo = _split_bf16(lax.rsqrt(ss * (1.0 / HEAD_DIM) + EPS))
    qk = qk * (_dot(hi, heads_t_ref[...]) + _dot(lo, heads_t_ref[...])) * gqk_ref[...]

    lane = lax.broadcasted_iota(jnp.int32, (x.shape[0], LANES), 1)
    cols = []
    for c in range(QK_W // LANES):
        xc = qk[:, c * LANES:(c + 1) * LANES]
        if rope:
            left = pltpu.roll(xc, LANES - HEAD_DIM // 4, 1)
            right = pltpu.roll(xc, HEAD_DIM // 4, 1)
            rot = jnp.where((lane // (HEAD_DIM // 4)) % 2 == 0, left, right)
            xc = xc * cos_ref[...] + rot * sin_ref[...]
        cols.append(xc)
    for c in range(ATTN_W // LANES):
        q_ref[:, c * LANES:(c + 1) * LANES] = (cols[c] * Q_SCALE).astype(BF16)

    k_col = cols[ATTN_W // LANES]
    v_col = z[:, QK_W:QK_W + KV_W]
    _store_expanded(kx_ref, k_col, 0.0)
    _store_expanded(vx_ref, v_col, 1.0)
    if not rope:
        _store_cache(kf_ref, k_col)
        _store_cache(vf_ref, v_col)

    xr_ref[...] = z[:, QK_W + KV_W:QK_W + KV_W + LRU_W]
    gb_ref[...] = z[:, QK_W + KV_W + LRU_W:]


def _pre(x, mod3, mod_row, g1, w_in, heads, gqk, tables, seq_len):
    n = x.shape[0]
    tm = TM_PRE
    tiles_per_seq = seq_len // tm
    rope = tables is not None
    const = lambda i: (0, 0)
    in_specs = [pl.BlockSpec((tm, D_MODEL), lambda i: (i, 0)),
                pl.BlockSpec((1, 6, D_MODEL), lambda i: (mod_row(i * tm), 0, 0)),
                pl.BlockSpec((1, D_MODEL), const),
                pl.BlockSpec((D_MODEL, IN_W), const),
                pl.BlockSpec((QK_W, LANES), const),
                pl.BlockSpec((LANES, QK_W), const),
                pl.BlockSpec((1, QK_W), const)]
    args = [x, mod3, g1, w_in, heads, heads.T, gqk]
    if rope:
        in_specs += [pl.BlockSpec((tm, LANES), lambda i: (i % tiles_per_seq, 0))] * 2
        args += list(tables)
    row = lambda w: pl.BlockSpec((tm, w), lambda i: (i, 0))
    out_shape = [jax.ShapeDtypeStruct((n, ATTN_W), BF16),
                 jax.ShapeDtypeStruct((n, 4 * LANES), BF16),
                 jax.ShapeDtypeStruct((n, 4 * LANES), BF16),
                 jax.ShapeDtypeStruct((n, LRU_W), F32),
                 jax.ShapeDtypeStruct((n, LRU_W), F32)]
    out_specs = [row(ATTN_W), row(4 * LANES), row(4 * LANES), row(LRU_W), row(LRU_W)]
    if not rope:
        assert tm % seq_len == 0
        per_tile = tm // seq_len
        cache = pl.BlockSpec((per_tile, N_KV_HEADS, HEAD_DIM, seq_len), lambda i: (i, 0, 0, 0))
        out_shape += [jax.ShapeDtypeStruct((n // seq_len, N_KV_HEADS, HEAD_DIM, seq_len), F32)] * 2
        out_specs += [cache, cache]
    return pl.pallas_call(
        functools.partial(_pre_kernel, rope=rope),
        grid=(n // tm,),
        in_specs=in_specs, out_specs=out_specs, out_shape=out_shape,
        scratch_shapes=[pltpu.VMEM((D_MODEL, IN_W), BF16)],
        compiler_params=_cparams(("arbitrary",)),
        name="pre_rope" if rope else "pre",
    )(*args)


def _attn_kernel(*refs, n_seg):
    q_ref = refs[0]
    k_refs = refs[1:1 + n_seg]
    v_refs = refs[1 + n_seg:1 + 2 * n_seg]
    o_ref = refs[1 + 2 * n_seg]
    lane = lax.broadcasted_iota(jnp.int32, (q_ref.shape[0], LANES), 1)
    for c in range(ATTN_W // LANES):
        qc = q_ref[:, c * LANES:(c + 1) * LANES]
        g = c // 2
        accs = []
        for par in range(2):
            sl = slice((2 * g + par) * LANES, (2 * g + par + 1) * LANES)
            ss = [_dot_nt(qc, k[:, sl]) for k in k_refs]
            mx = functools.reduce(jnp.maximum, [jnp.max(s, axis=-1, keepdims=True) for s in ss])
            ps = [jnp.exp2((s - mx).astype(BF16)) for s in ss]
            accs.append(functools.reduce(lambda a, b: a + b,
                                         [_dot(p, v[:, sl]) for p, v in zip(ps, v_refs)]))
        even = accs[0] / accs[0][:, ONES_LANE_EVEN:ONES_LANE_EVEN + 1]
        odd = accs[1] / accs[1][:, ONES_LANE_ODD:ONES_LANE_ODD + 1]
        o_ref[:, c * LANES:(c + 1) * LANES] = jnp.where(lane < HEAD_DIM, even, odd).astype(BF16)


def _attention(q, k_segs, v_segs, seq_len):
    n = q.shape[0]
    tq = min(TQ_ATT, seq_len)
    nq = seq_len // tq
    n_seg = len(k_segs)
    in_specs = [pl.BlockSpec((tq, ATTN_W), lambda b, i: (b * nq + i, 0))]
    for arr, t in list(k_segs) + list(v_segs):
        in_specs.append(pl.BlockSpec((t, 4 * LANES), lambda b, i: (b, 0)))
    return pl.pallas_call(
        functools.partial(_attn_kernel, n_seg=n_seg),
        grid=(n // seq_len, nq),
        in_specs=in_specs,
        out_specs=pl.BlockSpec((tq, ATTN_W), lambda b, i: (b * nq + i, 0)),
        out_shape=jax.ShapeDtypeStruct((n, ATTN_W), BF16),
        compiler_params=_cparams(("arbitrary", "arbitrary")),
        name="attention_%dseg" % n_seg,
    )(q, *[a for a, _ in k_segs], *[a for a, _ in v_segs])


def _log_sigmoid(x):
    return jnp.minimum(x, 0.0) - jnp.log1p(jnp.exp(-jnp.abs(x)))


def _tile_scan(a, b, reverse):
    row = lax.broadcasted_iota(jnp.int32, a.shape, 0)
    d = 1
    while d < SUBLANES:
        if reverse:
            keep = row < SUBLANES - d
            shift = SUBLANES - d
        else:
            keep = row >= d
            shift = d
        a_sh = jnp.where(keep, pltpu.roll(a, shift, 0), 1.0)
        b_sh = jnp.where(keep, pltpu.roll(b, shift, 0), 0.0)
        b = a * b_sh + b
        a = a * a_sh
        d *= 2
    return a, b


def _lru_kernel(xr_ref, gb_ref, h0_ref, cw_ref, cb_ref, wa_ref, wx_ref, ba_ref, bx_ref, lam_ref,
                rec_ref, fin_ref, xpad_s, xc_s, hf_s, a_s, b_s, wf_ref, wb_ref, bf_ref, bb_ref,
                *, seq_len):
    @pl.when(pl.program_id(0) == 0)
    def _():
        pairs = LANES // LRU_BLOCK_W
        zero = jnp.zeros((LRU_BLOCK_W, LRU_BLOCK_W), F32)
        for d, (w_s, bias_s) in enumerate(((wf_ref, bf_ref), (wb_ref, bb_ref))):
            w_s[...] = jnp.zeros_like(w_s)
            for g, src in enumerate((wa_ref, wx_ref)):
                for p in range(LRU_BLOCKS // pairs):
                    rows = [jnp.concatenate([0.5 * src[d, pairs * p + q] if q == r else zero
                                             for q in range(pairs)], axis=1) for r in range(pairs)]
                    w_s[LANES * p:LANES * (p + 1), g * LRU_W + LANES * p:g * LRU_W + LANES * (p + 1)] = (
                        jnp.concatenate(rows, axis=0).astype(BF16))
            bias_s[:, :LRU_W] = 0.5 * ba_ref[d:d + 1, :]
            bias_s[:, LRU_W:] = 0.5 * bx_ref[d:d + 1, :]

    tc = min(TC_LRU, seq_len)
    n_chunks = seq_len // tc
    n_tiles = tc // SUBLANES
    zpad = jnp.zeros((SUBLANES, LRU_W), F32)
    xpad_s[0:SUBLANES, :] = zpad
    xpad_s[SUBLANES:SUBLANES + seq_len, :] = xr_ref[...]
    xpad_s[SUBLANES + seq_len:2 * SUBLANES + seq_len, :] = zpad

    half_cl = (0.5 * LRU_C * LOG2_E) * _log_sigmoid(lam_ref[...])

    def gates(xcc, w_ref, bias_ref, half_cl_d):
        t = jnp.tanh(_dot(xcc.astype(BF16), w_ref[...]) + bias_ref[...])
        a = jnp.exp2(t[:, :LRU_W] * half_cl_d + half_cl_d)
        half_x = 0.5 * xcc
        a_s[...] = a
        u = 1.0 - a * a
        b_s[...] = jnp.where(u > 0.0, u * lax.rsqrt(u), 0.0) * (t[:, LRU_W:] * half_x + half_x)

    h = h0_ref[0, 0:1, :]
    for c in range(n_chunks):
        base = c * tc
        xcc = cb_ref[...] + functools.reduce(
            lambda u, v: u + v,
            [cw_ref[j:j + 1, :] * xpad_s[base + SUBLANES - 1 + j:base + SUBLANES - 1 + j + tc, :]
             for j in range(CONV_W)])
        xc_s[base:base + tc, :] = xcc
        gates(xcc, wf_ref, bf_ref, half_cl[0:1])

        def fwd_tile(t, hc, base=base):
            r0 = pl.multiple_of(t * SUBLANES, SUBLANES)
            ca, cb = _tile_scan(a_s[pl.ds(r0, SUBLANES), :], b_s[pl.ds(r0, SUBLANES), :], False)
            hh = ca * hc + cb
            hf_s[pl.ds(base + r0, SUBLANES), :] = hh
            return hh[SUBLANES - 1:SUBLANES, :]

        h = lax.fori_loop(0, n_tiles, fwd_tile, h, unroll=16)
    fin_ref[0, 0:1, :] = h

    h = h0_ref[0, 1:2, :]
    for c in reversed(range(n_chunks)):
        base = c * tc
        gates(xc_s[base:base + tc, :], wb_ref, bb_ref, half_cl[1:2])

        def bwd_tile(t, hc, base=base):
            r0 = pl.multiple_of((n_tiles - 1 - t) * SUBLANES, SUBLANES)
            ca, cb = _tile_scan(a_s[pl.ds(r0, SUBLANES), :], b_s[pl.ds(r0, SUBLANES), :], True)
            hh = ca * hc + cb
            gate = jax.nn.gelu(gb_ref[pl.ds(base + r0, SUBLANES), :], approximate=True)
            rec_ref[pl.ds(base + r0, SUBLANES), :] = (
                (hf_s[pl.ds(base + r0, SUBLANES), :] + hh) * gate).astype(rec_ref.dtype)
            return hh[0:1, :]

        h = lax.fori_loop(0, n_tiles, bwd_tile, h, unroll=16)
    fin_ref[0, 1:2, :] = h


def _lru(xr, gb, h0, conv_w, conv_b, wa, wx, ba, bx, lam, seq_len):
    n = xr.shape[0]
    batch = n // seq_len
    const = lambda b: (0, 0)
    blocks = pl.BlockSpec((2, LRU_BLOCKS, LRU_BLOCK_W, LRU_BLOCK_W), lambda b: (0, 0, 0, 0))
    seq = pl.BlockSpec((seq_len, LRU_W), lambda b: (b, 0))
    st = pl.BlockSpec((1, 2, LRU_W), lambda b: (b, 0, 0))
    return pl.pallas_call(
        functools.partial(_lru_kernel, seq_len=seq_len),
        grid=(batch,),
        in_specs=[seq, seq, st,
                  pl.BlockSpec((CONV_W, LRU_W), const), pl.BlockSpec((1, LRU_W), const),
                  blocks, blocks,
                  pl.BlockSpec((2, LRU_W), const), pl.BlockSpec((2, LRU_W), const),
                  pl.BlockSpec((2, LRU_W), const)],
        out_specs=[seq, st],
        out_shape=[jax.ShapeDtypeStruct((n, LRU_W), BF16),
                   jax.ShapeDtypeStruct((batch, 2, LRU_W), F32)],
        scratch_shapes=[pltpu.VMEM((seq_len + 2 * SUBLANES, LRU_W), F32),
                        pltpu.VMEM((seq_len, LRU_W), F32),
                        pltpu.VMEM((seq_len, LRU_W), F32),
                        pltpu.VMEM((min(TC_LRU, seq_len), LRU_W), F32),
                        pltpu.VMEM((min(TC_LRU, seq_len), LRU_W), F32),
                        pltpu.VMEM((LRU_W, 2 * LRU_W), BF16),
                        pltpu.VMEM((LRU_W, 2 * LRU_W), BF16),
                        pltpu.VMEM((1, 2 * LRU_W), F32),
                        pltpu.VMEM((1, 2 * LRU_W), F32)],
        compiler_params=_cparams(("arbitrary",)),
        name="lru_%d" % seq_len,
    )(xr, gb, h0, conv_w, conv_b, wa, wx, ba, bx, lam)


def _post_kernel(attn_ref, rec_ref, x_ref, mod_ref, g2_ref, wo_ref, wr2_ref, br_ref,
                 tri_ref, cnt_in_ref, x1_ref, h2_ref, route_ref, cnt_ref, route_t_ref, cnt_s, wo_s):
    @pl.when(pl.program_id(0) == 0)
    def _():
        cnt_s[...] = cnt_in_ref[...]
        wo_s[...] = wo_ref[...].astype(BF16)

    m = mod_ref[0]
    u = _dot(attn_ref[...], wo_s[:ATTN_W, :]) + _dot(rec_ref[...], wo_s[ATTN_W:, :])
    x1 = x_ref[...] + m[2:3] * u
    x1_ref[...] = x1
    ms = jnp.mean(x1 * x1, axis=-1, keepdims=True)
    h2 = x1 * lax.rsqrt(ms + EPS) * g2_ref[...]
    h2 = h2 * (1.0 + m[4:5]) + m[3:4]
    hi, lo = _split_bf16(h2)
    h2_ref[...] = _to_row_tiles(hi)

    hw = _dot(hi, wr2_ref[...])
    logits = hw[:, :LANES] + hw[:, LANES:] + _dot(lo, wr2_ref[:, :LANES]) + br_ref[...]
    lane_i = lax.broadcasted_iota(jnp.int32, logits.shape, 1)
    lane = lane_i.astype(F32)
    lane_group = (lane_i // EXPERTS_PER_GROUP).astype(F32)
    neg = -jnp.inf
    big = float(1 << 20)
    gmask = (lane_i >= N_EXPERTS) & (lane_i < N_EXPERTS + N_GROUPS)
    gl = jnp.where(gmask, logits, neg)
    gmax = jnp.max(gl, axis=-1, keepdims=True)
    gidx = jnp.min(jnp.where(gl == gmax, lane - N_EXPERTS, big), axis=-1, keepdims=True)
    p_sel = 1.0 / jnp.sum(jnp.where(gmask, jnp.exp(gl - gmax), 0.0), axis=-1, keepdims=True)

    emask = (lane_i < N_EXPERTS) & (lane_group == gidx)
    el = jnp.where(emask, logits, neg)
    v1 = jnp.max(el, axis=-1, keepdims=True)
    i1 = jnp.min(jnp.where(el == v1, lane, big), axis=-1, keepdims=True)
    el2 = jnp.where(lane == i1, neg, el)
    v2 = jnp.max(el2, axis=-1, keepdims=True)
    i2 = jnp.min(jnp.where(el2 == v2, lane, big), axis=-1, keepdims=True)
    e2 = jnp.exp(v2 - v1)
    w1 = p_sel / (1.0 + e2)
    w2 = p_sel * e2 / (1.0 + e2)

    oh1 = lane == i1
    oh2 = lane == i2
    oh = jnp.where(oh1, 1.0, 0.0) + jnp.where(oh2, 1.0, 0.0)
    before = _dot(tri_ref[...], oh.astype(BF16)) + cnt_s[...]
    rank1 = jnp.sum(jnp.where(oh1, before, 0.0), axis=-1, keepdims=True)
    rank2 = jnp.sum(jnp.where(oh2, before, 0.0), axis=-1, keepdims=True)
    cnt = cnt_s[...] + jnp.sum(oh, axis=0, keepdims=True)
    cnt_s[...] = cnt
    cnt_ref[...] = cnt
    fields = (i1, i2, rank1, rank2, w1, w2)
    route = jnp.zeros(logits.shape, F32)
    for k, val in enumerate(fields):
        route = jnp.where(lane_i == k, val, route)
    route_ref[...] = route
    route_t_ref[...] = jnp.transpose(route)[:SUBLANES, :]


def _post(attn, rec, x, mod3, mod_row, g2, w_out, wr2, br, tri, cnt_in):
    n = x.shape[0]
    tm = TM_PRE
    const = lambda i: (0, 0)
    row = lambda w: pl.BlockSpec((tm, w), lambda i: (i, 0))
    return pl.pallas_call(
        _post_kernel,
        grid=(n // tm,),
        in_specs=[row(ATTN_W), row(LRU_W), row(D_MODEL),
                  pl.BlockSpec((1, 6, D_MODEL), lambda i: (mod_row(i * tm), 0, 0)),
                  pl.BlockSpec((1, D_MODEL), const),
                  pl.BlockSpec((D_MODEL, D_MODEL), const),
                  pl.BlockSpec((D_MODEL, 2 * LANES), const),
                  pl.BlockSpec((1, LANES), const),
                  pl.BlockSpec((tm, tm), const),
                  pl.BlockSpec((1, LANES), const)],
        out_specs=[row(D_MODEL), pl.BlockSpec((tm,) + ROW_TILE, lambda i: (i, 0, 0)), row(LANES),
                   pl.BlockSpec((1, LANES), const),
                   pl.BlockSpec((SUBLANES, tm), lambda i: (0, i))],
        out_shape=[jax.ShapeDtypeStruct((n, D_MODEL), F32),
                   jax.ShapeDtypeStruct((n,) + ROW_TILE, BF16),
                   jax.ShapeDtypeStruct((n, LANES), F32),
                   jax.ShapeDtypeStruct((1, LANES), F32),
                   jax.ShapeDtypeStruct((SUBLANES, n), F32)],
        scratch_shapes=[pltpu.VMEM((1, LANES), F32), pltpu.VMEM((D_MODEL, D_MODEL), BF16)],
        compiler_params=_cparams(("arbitrary",)),
        name="post",
    )(attn, rec, x, mod3, g2, w_out, wr2, br, tri, cnt_in)


def _row_copy(src_ref, src_row, dst_ref, dst_row, sem):
    return pltpu.make_async_copy(src_ref.at[pl.ds(src_row, 1)], dst_ref.at[pl.ds(dst_row, 1)], sem)


def _plan_kernel(route_t_ref, seg_ref, slots_ref):
    route_t = route_t_ref[...]
    t = route_t.shape[1]
    seg = jnp.concatenate([seg_ref[...]] * (t // LANES), axis=1)
    expert = lax.broadcasted_iota(jnp.int32, seg.shape, 0).astype(F32)
    rows = []
    for k in range(TOP_K):
        start = jnp.sum(jnp.where(expert == route_t[k:k + 1, :], seg, 0.0), axis=0, keepdims=True)
        rows.append(start + route_t[TOP_K + k:TOP_K + k + 1, :])
    rows.append(jnp.zeros((SUBLANES - TOP_K, t), F32))
    slots_ref[...] = jnp.concatenate(rows, axis=0).astype(jnp.int32)


def _plan(route_t, seg_rows):
    n = route_t.shape[1]
    tm = TM_PRE
    return pl.pallas_call(
        _plan_kernel,
        grid=(n // tm,),
        in_specs=[pl.BlockSpec((SUBLANES, tm), lambda i: (0, i)),
                  pl.BlockSpec((N_EXPERTS, LANES), lambda i: (0, 0))],
        out_specs=pl.BlockSpec((SUBLANES, tm), lambda i: (0, i)),
        out_shape=jax.ShapeDtypeStruct((SUBLANES, n), jnp.int32),
        compiler_params=_cparams(("arbitrary",)),
        name="plan",
    )(route_t, seg_rows)


def _dispatch_kernel(slots_p_ref, slots_s_ref, pad_start_ref, pad_rows_ref, nu_ref, h2p_ref, h2s_ref,
                     xs_ref, zero_s, sem, zsem, *, n_prompt):
    i = pl.program_id(0)
    ch = CH_DISPATCH

    def zero_fill(op):
        for e in range(N_EXPERTS):
            rows = pad_rows_ref[e]
            for b in range(TR_MOE.bit_length() - 1):
                size = 1 << b
                off = (rows >> (b + 1)) << (b + 1)

                @pl.when(((rows >> b) & 1) == 1)
                def _():
                    op(pltpu.make_async_copy(zero_s.at[pl.ds(0, size)],
                                             xs_ref.at[pl.ds(pad_start_ref[e] + off, size)], zsem))

        def zero_tile(t, carry):
            op(pltpu.make_async_copy(
                zero_s, xs_ref.at[pl.ds(pl.multiple_of(t * TR_MOE, TR_MOE), TR_MOE)], zsem))
            return carry
        lax.fori_loop(nu_ref[0], xs_ref.shape[0] // TR_MOE, zero_tile, 0)

    @pl.when(i == 0)
    def _():
        zero_s[...] = jnp.zeros_like(zero_s)
        zero_fill(lambda cp: cp.start())

    @pl.when(i == pl.num_programs(0) - 1)
    def _():
        zero_fill(lambda cp: cp.wait())

    def scatter(src_ref, slots_ref, first_token):
        n_path = slots_ref.shape[0] // TOP_K

        def body(j, carry):
            for k in range(TOP_K):
                _row_copy(src_ref, j, xs_ref, slots_ref[k * n_path + i * ch + j - first_token],
                          sem).start(priority=k)
            return carry
        lax.fori_loop(0, ch, body, 0, unroll=8)
        for _ in range(TOP_K):
            pltpu.make_async_copy(src_ref, xs_ref.at[pl.ds(0, ch)], sem).wait()

    @pl.when(i < n_prompt // ch)
    def _():
        scatter(h2p_ref, slots_p_ref, 0)

    @pl.when(i >= n_prompt // ch)
    def _():
        scatter(h2s_ref, slots_s_ref, n_prompt)


def _dispatch(slots_p, slots_s, pad_start, pad_rows, n_used, h2p, h2s, n_rows):
    n_prompt = h2p.shape[0]
    n = n_prompt + h2s.shape[0]
    ch = CH_DISPATCH
    npc = n_prompt // ch
    last_p = npc - 1
    return pl.pallas_call(
        functools.partial(_dispatch_kernel, n_prompt=n_prompt),
        grid_spec=pltpu.PrefetchScalarGridSpec(
            num_scalar_prefetch=5,
            grid=(n // ch,),
            in_specs=[pl.BlockSpec((ch,) + ROW_TILE, lambda i, *_: (jnp.minimum(i, last_p), 0, 0)),
                      pl.BlockSpec((ch,) + ROW_TILE, lambda i, *_: (jnp.maximum(i - npc, 0), 0, 0))],
            out_specs=pl.BlockSpec(memory_space=pl.ANY),
            scratch_shapes=[pltpu.VMEM((TR_MOE,) + ROW_TILE, BF16), pltpu.SemaphoreType.DMA,
                            pltpu.SemaphoreType.DMA]),
        out_shape=jax.ShapeDtypeStruct((n_rows,) + ROW_TILE, BF16),
        compiler_params=_cparams(("arbitrary",)),
        name="dispatch",
    )(slots_p, slots_s, pad_start, pad_rows, n_used, h2p, h2s)


def _ffn_kernel(tend_ref, xs_ref, wg_ref, wu_ref, wd_ref, ys_ref,
                xbuf, ybuf, wg_s, wu_s, wd_s, sem_in, sem_out):
    e = pl.program_id(0)
    n_used = tend_ref[N_EXPERTS - 1]
    t_first = jnp.where(e == 0, 0, tend_ref[jnp.maximum(e - 1, 0)])
    t_last = tend_ref[e]
    n_in = xbuf.shape[0]
    n_out = ybuf.shape[0]

    def tile_rows(t):
        return pl.ds(pl.multiple_of(t * TR_MOE, TR_MOE), TR_MOE)

    def fetch(t):
        return pltpu.make_async_copy(xs_ref.at[tile_rows(t)], xbuf.at[t % n_in], sem_in.at[t % n_in])

    def writeback(t):
        return pltpu.make_async_copy(ybuf.at[t % n_out], ys_ref.at[tile_rows(t)],
                                     sem_out.at[t % n_out])

    @pl.when(e == 0)
    def _():
        for t in range(n_in - 1):
            @pl.when(t < n_used)
            def _():
                fetch(t).start()

    @pl.when(t_last > t_first)
    def _():
        wg_s[...] = wg_ref[0].astype(BF16)
        wu_s[...] = wu_ref[0].astype(BF16)
        wd_s[...] = wd_ref[0].astype(BF16)

    def tile(t, carry):
        fetch(t).wait()

        @pl.when(t + n_in - 1 < n_used)
        def _():
            fetch(t + n_in - 1).start()

        @pl.when(t >= n_out)
        def _():
            writeback(t - n_out).wait()

        x = _from_row_tiles(xbuf[t % n_in])
        hg = _dot(x, wg_s[...])
        hu = _dot(x, wu_s[...])
        act = (hg * _sigmoid(hg)) * hu
        ybuf[t % n_out] = _to_row_tiles(_dot(act.astype(BF16), wd_s[...]).astype(BF16))
        writeback(t).start()
        return carry

    lax.fori_loop(t_first, t_last, tile, 0)

    @pl.when(e == N_EXPERTS - 1)
    def _():
        for back in range(n_out, 0, -1):
            @pl.when(n_used >= back)
            def _():
                writeback(n_used - back).wait()
        n_all = ys_ref.shape[0] // TR_MOE
        ybuf[0] = jnp.zeros(ybuf.shape[1:], ybuf.dtype)

        def zero_tile(t):
            return pltpu.make_async_copy(ybuf.at[0], ys_ref.at[tile_rows(t)], sem_out.at[0])

        lax.fori_loop(n_used, n_all, lambda t, c: (zero_tile(t).start(), c)[1], 0)
        lax.fori_loop(n_used, n_all, lambda t, c: (zero_tile(t).wait(), c)[1], 0)


def _ffn(tile_end, xs, wg, wu, wd, n_tiles):
    tr = TR_MOE
    wsel = lambda e, tend: (e, 0, 0)
    return pl.pallas_call(
        _ffn_kernel,
        grid_spec=pltpu.PrefetchScalarGridSpec(
            num_scalar_prefetch=1,
            grid=(N_EXPERTS,),
            in_specs=[pl.BlockSpec(memory_space=pl.ANY),
                      pl.BlockSpec((1, D_MODEL, EXPERT_FF), wsel),
                      pl.BlockSpec((1, D_MODEL, EXPERT_FF), wsel),
                      pl.BlockSpec((1, EXPERT_FF, D_MODEL), wsel)],
            out_specs=pl.BlockSpec(memory_space=pl.ANY),
            scratch_shapes=[pltpu.VMEM((FFN_IN_SLOTS, tr) + ROW_TILE, BF16),
                            pltpu.VMEM((FFN_OUT_SLOTS, tr) + ROW_TILE, BF16),
                            pltpu.VMEM((D_MODEL, EXPERT_FF), BF16),
                            pltpu.VMEM((D_MODEL, EXPERT_FF), BF16),
                            pltpu.VMEM((EXPERT_FF, D_MODEL), BF16),
                            pltpu.SemaphoreType.DMA((FFN_IN_SLOTS,)),
                            pltpu.SemaphoreType.DMA((FFN_OUT_SLOTS,))]),
        out_shape=jax.ShapeDtypeStruct((n_tiles * tr,) + ROW_TILE, BF16),
        compiler_params=_cparams(("arbitrary",)),
        name="ffn",
    )(tile_end, xs, wg, wu, wd)


def _combine_kernel(slots_ref, ys_ref, x1_ref, route_ref, mod_ref, y_ref, b1_s, b2_s, sems):
    i = pl.program_id(0)
    tm = x1_ref.shape[0]
    n_path = slots_ref.shape[0] // TOP_K

    n_steps = pl.num_programs(0)
    ch = COMBINE_CHUNK

    def gather_chunk(step, slot, c):
        for t in range(ch):
            j = c * ch + t
            _row_copy(ys_ref, slots_ref[step * tm + j], b1_s.at[slot], j, sems.at[slot]).start(
                priority=0)
            _row_copy(ys_ref, slots_ref[n_path + step * tm + j], b2_s.at[slot], j,
                      sems.at[slot]).start(priority=1)

    def wait_slot(slot):
        for buf in (b1_s, b2_s):
            pltpu.make_async_copy(ys_ref.at[pl.ds(0, tm)], buf.at[slot], sems.at[slot]).wait()

    @pl.when(i == 0)
    def _():
        lax.fori_loop(0, tm // ch, lambda c, carry: (gather_chunk(0, 0, c), carry)[1], 0)

    slot = i % 2
    wait_slot(slot)
    gate = mod_ref[0][5:6]
    nxt = (i + 1) % n_steps

    def body(c, carry):
        rows = pl.ds(pl.multiple_of(c * ch, ch), ch)
        route = route_ref[rows, :]
        moe = (route[:, 4:5] * _from_row_tiles(b1_s[slot, rows]).astype(F32)
               + route[:, 5:6] * _from_row_tiles(b2_s[slot, rows]).astype(F32))
        y_ref[rows, :] = x1_ref[rows, :] + gate * moe
        gather_chunk(nxt, 1 - slot, c)
        return carry
    lax.fori_loop(0, tm // ch, body, 0)

    @pl.when(i == n_steps - 1)
    def _():
        wait_slot(1 - slot)


def _combine(slots, ys, x1, route, mod3, mod_row):
    n = x1.shape[0]
    tm = TM_PRE
    row = lambda w: pl.BlockSpec((tm, w), lambda i, *_: (i, 0))
    return pl.pallas_call(
        _combine_kernel,
        grid_spec=pltpu.PrefetchScalarGridSpec(
            num_scalar_prefetch=1,
            grid=(n // tm,),
            in_specs=[pl.BlockSpec(memory_space=pl.ANY), row(D_MODEL), row(LANES),
                      pl.BlockSpec((1, 6, D_MODEL),
                                   lambda i, *_: (mod_row(i * tm), 0, 0))],
            out_specs=row(D_MODEL),
            scratch_shapes=[pltpu.VMEM((2, tm) + ROW_TILE, BF16), pltpu.VMEM((2, tm) + ROW_TILE, BF16),
                            pltpu.SemaphoreType.DMA((2,))]),
        out_shape=jax.ShapeDtypeStruct((n, D_MODEL), F32),
        compiler_params=_cparams(("arbitrary",)),
        name="combine",
    )(slots, ys, x1, route, mod3)


def _rope_tables(length):
    rows = length // GRID_W
    r, col = jnp.meshgrid(jnp.arange(rows), jnp.arange(GRID_W), indexing='ij')
    r = r.reshape(-1).astype(F32)
    col = col.reshape(-1).astype(F32)
    half = HEAD_DIM // 2
    inv = ROPE_THETA ** (-jnp.arange(0, half, 2, dtype=F32) / half)
    ang_r = r[:, None] * inv
    ang_c = col[:, None] * inv
    ang = jnp.concatenate([ang_r, ang_r, ang_c, ang_c], axis=-1)
    sign = jnp.where((jnp.arange(HEAD_DIM) // (HEAD_DIM // 4)) % 2 == 0, -1.0, 1.0).astype(F32)
    cos = jnp.tile(jnp.cos(ang), (1, LANES // HEAD_DIM))
    sin = jnp.tile(jnp.sin(ang) * sign, (1, LANES // HEAD_DIM))
    return cos, sin


def _expand_cache_kernel(k_ref, v_ref, kx_ref, vx_ref):
    for ref, xref, one in ((k_ref, kx_ref, 0.0), (v_ref, vx_ref, 1.0)):
        col = jnp.transpose(ref[0].reshape(KV_W, ref.shape[-1]))
        _store_expanded(xref, col, one)


def _expand_cache(cache_k, cache_v):
    b, _, _, t = cache_k.shape
    src = pl.BlockSpec((1, N_KV_HEADS, HEAD_DIM, t), lambda i: (i, 0, 0, 0))
    dst = pl.BlockSpec((t, 4 * LANES), lambda i: (i, 0))
    return pl.pallas_call(
        _expand_cache_kernel,
        grid=(b,),
        in_specs=[src, src], out_specs=[dst, dst],
        out_shape=[jax.ShapeDtypeStruct((b * t, 4 * LANES), BF16)] * 2,
        compiler_params=_cparams(("arbitrary",)),
        name="expand_cache",
    )(cache_k, cache_v)


def kernel(x_prompt, x_sample, cache_k, cache_v, state_lru, c, c_ctx, w_mod, b_mod, norm1, norm2,
           w_in, q_norm, k_norm, conv_w, conv_b, lru_wa, lru_ba, lru_wx, lru_bx, lru_lambda, w_out,
           router_grp_w, router_grp_b, router_exp_w, router_exp_b, exp_w_gate, exp_w_up, exp_w_down):
    batch, seq, _ = x_prompt.shape
    dec_batch, dec_seq, _ = x_sample.shape
    past = cache_k.shape[2]
    depth = w_mod.shape[0]
    assert depth == 1
    assert x_prompt.shape[2] == D_MODEL and w_in.shape[1:] == (D_MODEL, IN_W)
    assert exp_w_gate.shape[1:] == (N_EXPERTS, D_MODEL, EXPERT_FF)
    assert cache_k.shape[0] == dec_batch and cache_k.shape[3:] == (N_KV_HEADS, HEAD_DIM)
    assert (batch * seq) % CH_DISPATCH == 0 and (dec_batch * dec_seq) % CH_DISPATCH == 0
    assert dec_batch < MOD_ROWS

    cvec = jnp.concatenate(
        [c_ctx[None, :], c, jnp.zeros((MOD_ROWS - 1 - dec_batch, D_MODEL), F32)], axis=0)
    mod3 = _modulation(cvec, w_mod[0], b_mod[0][None, :]).reshape(MOD_ROWS, 6, D_MODEL)

    head_id = jnp.arange(QK_W) // HEAD_DIM
    heads = (head_id[:, None] == jnp.arange(LANES)[None, :]).astype(BF16)
    gqk = jnp.concatenate([jnp.tile(q_norm[0], N_HEADS), jnp.tile(k_norm[0], N_KV_HEADS)])[None, :]
    pad = LANES - N_EXPERTS - N_GROUPS
    wr = jnp.concatenate([router_exp_w[0], router_grp_w[0], jnp.zeros((D_MODEL, pad), F32)], axis=1)
    wr_hi = wr.astype(BF16)
    wr2 = jnp.concatenate([wr_hi, (wr - wr_hi.astype(F32)).astype(BF16)], axis=1)
    br = jnp.concatenate([router_exp_b[0], router_grp_b[0], jnp.zeros((pad,), F32)])[None, :]
    g1 = norm1[0][None, :]
    g2 = norm2[0][None, :]
    cw = conv_w[0]
    cb = conv_b[0][None, :]
    lam = lru_lambda[0]
    tri = (jnp.arange(TM_PRE)[:, None] > jnp.arange(TM_PRE)[None, :]).astype(BF16)

    def mixers(x, seq_len, mod_row, tables, extra_k, extra_v, h0, cnt_in):
        q, kx, vx, xr, gb, *cache = _pre(x, mod3, mod_row, g1, w_in[0], heads, gqk, tables, seq_len)
        k_segs = [(kx, seq_len)] + extra_k
        v_segs = [(vx, seq_len)] + extra_v
        attn = _attention(q, k_segs, v_segs, seq_len)
        rec, fin = _lru(xr, gb, h0, cw, cb, lru_wa[0], lru_wx[0], lru_ba[0], lru_bx[0], lam, seq_len)
        x1, h2, route, cnt, route_t = _post(attn, rec, x, mod3, mod_row, g2, w_out[0], wr2, br, tri,
                                            cnt_in)
        return x1, h2, (route, route_t), cnt, cache, fin

    mod_row_p = lambda tok: 0
    mod_row_s = lambda tok: tok // dec_seq + 1
    xp = x_prompt.reshape(batch * seq, D_MODEL)
    x1p, h2p, (route_p, route_t_p), cnt_p, (kf, vf), fin = mixers(
        xp, seq, mod_row_p, None, [], [], jnp.zeros((batch, 2, LRU_W), F32),
        jnp.zeros((1, LANES), F32))
    xs = x_sample.reshape(dec_batch * dec_seq, D_MODEL)
    ck, cv = _expand_cache(jnp.transpose(cache_k[:, 0], (0, 2, 3, 1)),
                           jnp.transpose(cache_v[:, 0], (0, 2, 3, 1)))
    x1s, h2s, (route_s, route_t_s), cnt_all, _, _ = mixers(
        xs, dec_seq, mod_row_s, _rope_tables(dec_seq), [(ck, past)], [(cv, past)],
        state_lru[:, 0], cnt_p)

    n_prompt = batch * seq
    n_tok = n_prompt + dec_batch * dec_seq
    n_tiles = (TOP_K * n_tok + N_EXPERTS * (TR_MOE - 1)) // TR_MOE
    cnt = cnt_all[0, :N_EXPERTS].astype(jnp.int32)
    ntile = (cnt + TR_MOE - 1) // TR_MOE
    tile_end = jnp.cumsum(ntile)
    seg_start = (tile_end - ntile) * TR_MOE
    n_used = tile_end[-1:]
    pad_start = seg_start + cnt
    pad_rows = tile_end * TR_MOE - pad_start

    seg_rows = jnp.broadcast_to(seg_start.astype(F32)[:, None], (N_EXPERTS, LANES))
    slots_p = _plan(route_t_p, seg_rows)[:TOP_K].reshape(-1)
    slots_s = _plan(route_t_s, seg_rows)[:TOP_K].reshape(-1)
    xsort = _dispatch(slots_p, slots_s, pad_start, pad_rows, n_used, h2p, h2s,
                      n_tiles * TR_MOE)
    ysort = _ffn(tile_end, xsort, exp_w_gate[0], exp_w_up[0], exp_w_down[0], n_tiles)
    yp = _combine(slots_p, ysort, x1p, route_p, mod3, mod_row_p)
    ys = _combine(slots_s, ysort, x1s, route_s, mod3, mod_row_s)

    return (yp.reshape(batch, seq, D_MODEL),
            ys.reshape(dec_batch, dec_seq, D_MODEL),
            jnp.transpose(kf, (0, 3, 1, 2))[:, None],
            jnp.transpose(vf, (0, 3, 1, 2))[:, None],
            fin.reshape(batch, 1, 2, LRU_W))
```

```python
import functools

import jax
import jax.numpy as jnp
from jax import lax
from jax.experimental import pallas as pl
from jax.experimental.pallas import tpu as pltpu

F32 = jnp.float32
BF16 = jnp.bfloat16

D_MODEL = 1024
GRID_W = 64
ATTN_W = 512
LRU_W = 512
HEAD_DIM = 64
N_HEADS = 8
N_KV_HEADS = 2
KV_W = N_KV_HEADS * HEAD_DIM
LRU_BLOCKS = 8
LRU_BLOCK_W = LRU_W // LRU_BLOCKS
CONV_W = 4
LRU_C = 8.0
IN_W = ATTN_W + 2 * KV_W + 2 * LRU_W
QK_W = ATTN_W + KV_W
N_GROUPS = 4
EXPERTS_PER_GROUP = 8
N_EXPERTS = N_GROUPS * EXPERTS_PER_GROUP
TOP_K = 2
EXPERT_FF = D_MODEL // 4
ROPE_THETA = 10000.0
EPS = 1e-6

LANES = 128
SUBLANES = 8
MOD_ROWS = 8
VMEM_LIMIT = 48 * 1024 * 1024

TM_PRE = 512
TQ_ATT = 1024
TC_LRU = 512
TR_MOE = 512
CH_DISPATCH = 2048
COMBINE_CHUNK = 32
FFN_IN_SLOTS = 4
FFN_OUT_SLOTS = 3


def _cparams(sem):
    return pltpu.CompilerParams(dimension_semantics=sem, vmem_limit_bytes=VMEM_LIMIT)


def _dot(a, b):
    return jnp.dot(a, b, preferred_element_type=F32)


def _dot_nt(a, b):
    return lax.dot_general(a, b, (((1,), (1,)), ((), ())), preferred_element_type=F32)


ROW_TILE = (D_MODEL // LANES, LANES)
LOG2_E = 1.4426950408889634
Q_SCALE = HEAD_DIM ** -0.5 * LOG2_E
ONES_LANE_EVEN = HEAD_DIM
ONES_LANE_ODD = 0


def _to_row_tiles(x):
    cols = jnp.stack([x[:, c * LANES:(c + 1) * LANES] for c in range(D_MODEL // LANES)], axis=0)
    return jnp.swapaxes(cols, 0, 1)


def _from_row_tiles(x3):
    cols = jnp.swapaxes(x3, 0, 1)
    return jnp.concatenate([cols[c] for c in range(D_MODEL // LANES)], axis=1)


def _sigmoid(x):
    return 0.5 * jnp.tanh(0.5 * x) + 0.5


def _split_bf16(x):
    hi = x.astype(BF16)
    lo = (x - hi.astype(F32)).astype(BF16)
    return hi, lo


def _mod_kernel(c_ref, w_ref, b_ref, o_ref):
    c = c_ref[...]
    s = (c * jax.nn.sigmoid(c)).astype(BF16)
    o_ref[...] = _dot(s, w_ref[...].astype(BF16)) + b_ref[...]


def _modulation(cvec, w_mod, b_mod):
    n_out = w_mod.shape[1]
    tn = n_out // 4
    return pl.pallas_call(
        _mod_kernel,
        grid=(n_out // tn,),
        in_specs=[pl.BlockSpec((MOD_ROWS, D_MODEL), lambda j: (0, 0)),
                  pl.BlockSpec((D_MODEL, tn), lambda j: (0, j)),
                  pl.BlockSpec((1, tn), lambda j: (0, j))],
        out_specs=pl.BlockSpec((MOD_ROWS, tn), lambda j: (0, j)),
        out_shape=jax.ShapeDtypeStruct((MOD_ROWS, n_out), F32),
        compiler_params=_cparams(("arbitrary",)),
        name="modulation",
    )(cvec, w_mod, b_mod)


def _store_cache(ref, col):
    seq = ref.shape[-1]
    col_t = jnp.transpose(col)
    for j in range(ref.shape[0]):
        ref[j] = col_t[:, j * seq:(j + 1) * seq].reshape(N_KV_HEADS, HEAD_DIM, seq)


def _store_expanded(xref, col, one):
    lane = lax.broadcasted_iota(jnp.int32, col.shape, 1)
    lo_half = lane < HEAD_DIM
    swapped = pltpu.roll(col, HEAD_DIM, 1)
    fill_hi = jnp.where(lane == ONES_LANE_EVEN, one, 0.0)
    fill_lo = jnp.where(lane == ONES_LANE_ODD, one, 0.0)
    xref[:, 0 * LANES:1 * LANES] = jnp.where(lo_half, col, fill_hi).astype(BF16)
    xref[:, 1 * LANES:2 * LANES] = jnp.where(lo_half, fill_lo, swapped).astype(BF16)
    xref[:, 2 * LANES:3 * LANES] = jnp.where(lo_half, swapped, fill_hi).astype(BF16)
    xref[:, 3 * LANES:4 * LANES] = jnp.where(lo_half, fill_lo, col).astype(BF16)


def _pre_kernel(*refs, rope):
    if rope:
        (x_ref, mod_ref, g1_ref, win_ref, heads_ref, heads_t_ref, gqk_ref, cos_ref, sin_ref,
         q_ref, kx_ref, vx_ref, xr_ref, gb_ref, win_s) = refs
    else:
        (x_ref, mod_ref, g1_ref, win_ref, heads_ref, heads_t_ref, gqk_ref,
         q_ref, kx_ref, vx_ref, xr_ref, gb_ref, kf_ref, vf_ref, win_s) = refs

    @pl.when(pl.program_id(0) == 0)
    def _():
        win_s[...] = win_ref[...].astype(BF16)

    x = x_ref[...]
    m = mod_ref[0]
    ms = jnp.mean(x * x, axis=-1, keepdims=True)
    y = x * lax.rsqrt(ms + EPS) * g1_ref[...]
    h = y * (1.0 + m[1:2]) + m[0:1]
    z = _dot(h.astype(BF16), win_s[...])

    qk = z[:, :QK_W]
    ss = _dot((qk * qk).astype(BF16), heads_ref[...])
    hi, lo = _split_bf16(lax.rsqrt(ss * (1.0 / HEAD_DIM) + EPS))
    qk = qk * (_dot(hi, heads_t_ref[...]) + _dot(lo, heads_t_ref[...])) * gqk_ref[...]

    lane = lax.broadcasted_iota(jnp.int32, (x.shape[0], LANES), 1)
    cols = []
    for c in range(QK_W // LANES):
        xc = qk[:, c * LANES:(c + 1) * LANES]
        if rope:
            left = pltpu.roll(xc, LANES - HEAD_DIM // 4, 1)
            right = pltpu.roll(xc, HEAD_DIM // 4, 1)
            rot = jnp.where((lane // (HEAD_DIM // 4)) % 2 == 0, left, right)
            xc = xc * cos_ref[...] + rot * sin_ref[...]
        cols.append(xc)
    for c in range(ATTN_W // LANES):
        q_ref[:, c * LANES:(c + 1) * LANES] = (cols[c] * Q_SCALE).astype(BF16)

    k_col = cols[ATTN_W // LANES]
    v_col = z[:, QK_W:QK_W + KV_W]
    _store_expanded(kx_ref, k_col, 0.0)
    _store_expanded(vx_ref, v_col, 1.0)
    if not rope:
        _store_cache(kf_ref, k_col)
        _store_cache(vf_ref, v_col)

    xr_ref[...] = z[:, QK_W + KV_W:QK_W + KV_W + LRU_W]
    gb_ref[...] = z[:, QK_W + KV_W + LRU_W:]


def _pre(x, mod3, mod_row, g1, w_in, heads, gqk, tables, seq_len):
    n = x.shape[0]
    tm = TM_PRE
    tiles_per_seq = seq_len // tm
    rope = tables is not None
    const = lambda i: (0, 0)
    in_specs = [pl.BlockSpec((tm, D_MODEL), lambda i: (i, 0)),
                pl.BlockSpec((1, 6, D_MODEL), lambda i: (mod_row(i * tm), 0, 0)),
                pl.BlockSpec((1, D_MODEL), const),
                pl.BlockSpec((D_MODEL, IN_W), const),
                pl.BlockSpec((QK_W, LANES), const),
                pl.BlockSpec((LANES, QK_W), const),
                pl.BlockSpec((1, QK_W), const)]
    args = [x, mod3, g1, w_in, heads, heads.T, gqk]
    if rope:
        in_specs += [pl.BlockSpec((tm, LANES), lambda i: (i % tiles_per_seq, 0))] * 2
        args += list(tables)
    row = lambda w: pl.BlockSpec((tm, w), lambda i: (i, 0))
    out_shape = [jax.ShapeDtypeStruct((n, ATTN_W), BF16),
                 jax.ShapeDtypeStruct((n, 4 * LANES), BF16),
                 jax.ShapeDtypeStruct((n, 4 * LANES), BF16),
                 jax.ShapeDtypeStruct((n, LRU_W), F32),
                 jax.ShapeDtypeStruct((n, LRU_W), F32)]
    out_specs = [row(ATTN_W), row(4 * LANES), row(4 * LANES), row(LRU_W), row(LRU_W)]
    if not rope:
        assert tm % seq_len == 0
        per_tile = tm // seq_len
        cache = pl.BlockSpec((per_tile, N_KV_HEADS, HEAD_DIM, seq_len), lambda i: (i, 0, 0, 0))
        out_shape += [jax.ShapeDtypeStruct((n // seq_len, N_KV_HEADS, HEAD_DIM, seq_len), F32)] * 2
        out_specs += [cache, cache]
    return pl.pallas_call(
        functools.partial(_pre_kernel, rope=rope),
        grid=(n // tm,),
        in_specs=in_specs, out_specs=out_specs, out_shape=out_shape,
        scratch_shapes=[pltpu.VMEM((D_MODEL, IN_W), BF16)],
        compiler_params=_cparams(("arbitrary",)),
        name="pre_rope" if rope else "pre",
    )(*args)


def _attn_kernel(*refs, n_seg):
    q_ref = refs[0]
    k_refs = refs[1:1 + n_seg]
    v_refs = refs[1 + n_seg:1 + 2 * n_seg]
    o_ref = refs[1 + 2 * n_seg]
    lane = lax.broadcasted_iota(jnp.int32, (q_ref.shape[0], LANES), 1)
    for c in range(ATTN_W // LANES):
        qc = q_ref[:, c * LANES:(c + 1) * LANES]
        g = c // 2
        accs = []
        for par in range(2):
            sl = slice((2 * g + par) * LANES, (2 * g + par + 1) * LANES)
            ss = [_dot_nt(qc, k[:, sl]) for k in k_refs]
            mx = functools.reduce(jnp.maximum, [jnp.max(s, axis=-1, keepdims=True) for s in ss])
            ps = [jnp.exp2((s - mx).astype(BF16)) for s in ss]
            accs.append(functools.reduce(lambda a, b: a + b,
                                         [_dot(p, v[:, sl]) for p, v in zip(ps, v_refs)]))
        even = accs[0] / accs[0][:, ONES_LANE_EVEN:ONES_LANE_EVEN + 1]
        odd = accs[1] / accs[1][:, ONES_LANE_ODD:ONES_LANE_ODD + 1]
        o_ref[:, c * LANES:(c + 1) * LANES] = jnp.where(lane < HEAD_DIM, even, odd).astype(BF16)


def _attention(q, k_segs, v_segs, seq_len):
    n = q.shape[0]
    tq = min(TQ_ATT, seq_len)
    nq = seq_len // tq
    n_seg = len(k_segs)
    in_specs = [pl.BlockSpec((tq, ATTN_W), lambda b, i: (b * nq + i, 0))]
    for arr, t in list(k_segs) + list(v_segs):
        in_specs.append(pl.BlockSpec((t, 4 * LANES), lambda b, i: (b, 0)))
    return pl.pallas_call(
        functools.partial(_attn_kernel, n_seg=n_seg),
        grid=(n // seq_len, nq),
        in_specs=in_specs,
        out_specs=pl.BlockSpec((tq, ATTN_W), lambda b, i: (b * nq + i, 0)),
        out_shape=jax.ShapeDtypeStruct((n, ATTN_W), BF16),
        compiler_params=_cparams(("arbitrary", "arbitrary")),
        name="attention_%dseg" % n_seg,
    )(q, *[a for a, _ in k_segs], *[a for a, _ in v_segs])


def _log_sigmoid(x):
    return jnp.minimum(x, 0.0) - jnp.log1p(jnp.exp(-jnp.abs(x)))


def _tile_scan(a, b, reverse):
    row = lax.broadcasted_iota(jnp.int32, a.shape, 0)
    d = 1
    while d < SUBLANES:
        if reverse:
            keep = row < SUBLANES - d
            shift = SUBLANES - d
        else:
            keep = row >= d
            shift = d
        a_sh = jnp.where(keep, pltpu.roll(a, shift, 0), 1.0)
        b_sh = jnp.where(keep, pltpu.roll(b, shift, 0), 0.0)
        b = a * b_sh + b
        a = a * a_sh
        d *= 2
    return a, b


def _lru_kernel(xr_ref, gb_ref, h0_ref, cw_ref, cb_ref, wa_ref, wx_ref, ba_ref, bx_ref, lam_ref,
                rec_ref, fin_ref, xpad_s, xc_s, hf_s, a_s, b_s, wf_ref, wb_ref, bf_ref, bb_ref,
                *, seq_len):
    @pl.when(pl.program_id(0) == 0)
    def _():
        pairs = LANES // LRU_BLOCK_W
        zero = jnp.zeros((LRU_BLOCK_W, LRU_BLOCK_W), F32)
        for d, (w_s, bias_s) in enumerate(((wf_ref, bf_ref), (wb_ref, bb_ref))):
            w_s[...] = jnp.zeros_like(w_s)
            for g, src in enumerate((wa_ref, wx_ref)):
                for p in range(LRU_BLOCKS // pairs):
                    rows = [jnp.concatenate([0.5 * src[d, pairs * p + q] if q == r else zero
                                             for q in range(pairs)], axis=1) for r in range(pairs)]
                    w_s[LANES * p:LANES * (p + 1), g * LRU_W + LANES * p:g * LRU_W + LANES * (p + 1)] = (
                        jnp.concatenate(rows, axis=0).astype(BF16))
            bias_s[:, :LRU_W] = 0.5 * ba_ref[d:d + 1, :]
            bias_s[:, LRU_W:] = 0.5 * bx_ref[d:d + 1, :]

    tc = min(TC_LRU, seq_len)
    n_chunks = seq_len // tc
    n_tiles = tc // SUBLANES
    zpad = jnp.zeros((SUBLANES, LRU_W), F32)
    xpad_s[0:SUBLANES, :] = zpad
    xpad_s[SUBLANES:SUBLANES + seq_len, :] = xr_ref[...]
    xpad_s[SUBLANES + seq_len:2 * SUBLANES + seq_len, :] = zpad

    half_cl = (0.5 * LRU_C * LOG2_E) * _log_sigmoid(lam_ref[...])

    def gates(xcc, w_ref, bias_ref, half_cl_d):
        t = jnp.tanh(_dot(xcc.astype(BF16), w_ref[...]) + bias_ref[...])
        a = jnp.exp2(t[:, :LRU_W] * half_cl_d + half_cl_d)
        half_x = 0.5 * xcc
        a_s[...] = a
        u = 1.0 - a * a
        b_s[...] = jnp.where(u > 0.0, u * lax.rsqrt(u), 0.0) * (t[:, LRU_W:] * half_x + half_x)

    h = h0_ref[0, 0:1, :]
    for c in range(n_chunks):
        base = c * tc
        xcc = cb_ref[...] + functools.reduce(
            lambda u, v: u + v,
            [cw_ref[j:j + 1, :] * xpad_s[base + SUBLANES - 1 + j:base + SUBLANES - 1 + j + tc, :]
             for j in range(CONV_W)])
        xc_s[base:base + tc, :] = xcc
        gates(xcc, wf_ref, bf_ref, half_cl[0:1])

        def fwd_tile(t, hc, base=base):
            r0 = pl.multiple_of(t * SUBLANES, SUBLANES)
            ca, cb = _tile_scan(a_s[pl.ds(r0, SUBLANES), :], b_s[pl.ds(r0, SUBLANES), :], False)
            hh = ca * hc + cb
            hf_s[pl.ds(base + r0, SUBLANES), :] = hh
            return hh[SUBLANES - 1:SUBLANES, :]

        h = lax.fori_loop(0, n_tiles, fwd_tile, h, unroll=16)
    fin_ref[0, 0:1, :] = h

    h = h0_ref[0, 1:2, :]
    for c in reversed(range(n_chunks)):
        base = c * tc
        gates(xc_s[base:base + tc, :], wb_ref, bb_ref, half_cl[1:2])

        def bwd_tile(t, hc, base=base):
            r0 = pl.multiple_of((n_tiles - 1 - t) * SUBLANES, SUBLANES)
            ca, cb = _tile_scan(a_s[pl.ds(r0, SUBLANES), :], b_s[pl.ds(r0, SUBLANES), :], True)
            hh = ca * hc + cb
            gate = jax.nn.gelu(gb_ref[pl.ds(base + r0, SUBLANES), :], approximate=True)
            rec_ref[pl.ds(base + r0, SUBLANES), :] = (
                (hf_s[pl.ds(base + r0, SUBLANES), :] + hh) * gate).astype(rec_ref.dtype)
            return hh[0:1, :]

        h = lax.fori_loop(0, n_tiles, bwd_tile, h, unroll=16)
    fin_ref[0, 1:2, :] = h


def _lru(xr, gb, h0, conv_w, conv_b, wa, wx, ba, bx, lam, seq_len):
    n = xr.shape[0]
    batch = n // seq_len
    const = lambda b: (0, 0)
    blocks = pl.BlockSpec((2, LRU_BLOCKS, LRU_BLOCK_W, LRU_BLOCK_W), lambda b: (0, 0, 0, 0))
    seq = pl.BlockSpec((seq_len, LRU_W), lambda b: (b, 0))
    st = pl.BlockSpec((1, 2, LRU_W), lambda b: (b, 0, 0))
    return pl.pallas_call(
        functools.partial(_lru_kernel, seq_len=seq_len),
        grid=(batch,),
        in_specs=[seq, seq, st,
                  pl.BlockSpec((CONV_W, LRU_W), const), pl.BlockSpec((1, LRU_W), const),
                  blocks, blocks,
                  pl.BlockSpec((2, LRU_W), const), pl.BlockSpec((2, LRU_W), const),
                  pl.BlockSpec((2, LRU_W), const)],
        out_specs=[seq, st],
        out_shape=[jax.ShapeDtypeStruct((n, LRU_W), BF16),
                   jax.ShapeDtypeStruct((batch, 2, LRU_W), F32)],
        scratch_shapes=[pltpu.VMEM((seq_len + 2 * SUBLANES, LRU_W), F32),
                        pltpu.VMEM((seq_len, LRU_W), F32),
                        pltpu.VMEM((seq_len, LRU_W), F32),
                        pltpu.VMEM((min(TC_LRU, seq_len), LRU_W), F32),
                        pltpu.VMEM((min(TC_LRU, seq_len), LRU_W), F32),
                        pltpu.VMEM((LRU_W, 2 * LRU_W), BF16),
                        pltpu.VMEM((LRU_W, 2 * LRU_W), BF16),
                        pltpu.VMEM((1, 2 * LRU_W), F32),
                        pltpu.VMEM((1, 2 * LRU_W), F32)],
        compiler_params=_cparams(("arbitrary",)),
        name="lru_%d" % seq_len,
    )(xr, gb, h0, conv_w, conv_b, wa, wx, ba, bx, lam)


def _post_kernel(attn_ref, rec_ref, x_ref, mod_ref, g2_ref, wo_ref, wr2_ref, br_ref,
                 tri_ref, cnt_in_ref, x1_ref, h2_ref, route_ref, cnt_ref, route_t_ref, cnt_s, wo_s):
    @pl.when(pl.program_id(0) == 0)
    def _():
        cnt_s[...] = cnt_in_ref[...]
        wo_s[...] = wo_ref[...].astype(BF16)

    m = mod_ref[0]
    u = _dot(attn_ref[...], wo_s[:ATTN_W, :]) + _dot(rec_ref[...], wo_s[ATTN_W:, :])
    x1 = x_ref[...] + m[2:3] * u
    x1_ref[...] = x1
    ms = jnp.mean(x1 * x1, axis=-1, keepdims=True)
    h2 = x1 * lax.rsqrt(ms + EPS) * g2_ref[...]
    h2 = h2 * (1.0 + m[4:5]) + m[3:4]
    hi, lo = _split_bf16(h2)
    h2_ref[...] = _to_row_tiles(hi)

    hw = _dot(hi, wr2_ref[...])
    logits = hw[:, :LANES] + hw[:, LANES:] + _dot(lo, wr2_ref[:, :LANES]) + br_ref[...]
    lane_i = lax.broadcasted_iota(jnp.int32, logits.shape, 1)
    lane = lane_i.astype(F32)
    lane_group = (lane_i // EXPERTS_PER_GROUP).astype(F32)
    neg = -jnp.inf
    big = float(1 << 20)
    gmask = (lane_i >= N_EXPERTS) & (lane_i < N_EXPERTS + N_GROUPS)
    gl = jnp.where(gmask, logits, neg)
    gmax = jnp.max(gl, axis=-1, keepdims=True)
    gidx = jnp.min(jnp.where(gl == gmax, lane - N_EXPERTS, big), axis=-1, keepdims=True)
    p_sel = 1.0 / jnp.sum(jnp.where(gmask, jnp.exp(gl - gmax), 0.0), axis=-1, keepdims=True)

    emask = (lane_i < N_EXPERTS) & (lane_group == gidx)
    el = jnp.where(emask, logits, neg)
    v1 = jnp.max(el, axis=-1, keepdims=True)
    i1 = jnp.min(jnp.where(el == v1, lane, big), axis=-1, keepdims=True)
    el2 = jnp.where(lane == i1, neg, el)
    v2 = jnp.max(el2, axis=-1, keepdims=True)
    i2 = jnp.min(jnp.where(el2 == v2, lane, big), axis=-1, keepdims=True)
    e2 = jnp.exp(v2 - v1)
    w1 = p_sel / (1.0 + e2)
    w2 = p_sel * e2 / (1.0 + e2)

    oh1 = lane == i1
    oh2 = lane == i2
    oh = jnp.where(oh1, 1.0, 0.0) + jnp.where(oh2, 1.0, 0.0)
    before = _dot(tri_ref[...], oh.astype(BF16)) + cnt_s[...]
    rank1 = jnp.sum(jnp.where(oh1, before, 0.0), axis=-1, keepdims=True)
    rank2 = jnp.sum(jnp.where(oh2, before, 0.0), axis=-1, keepdims=True)
    cnt = cnt_s[...] + jnp.sum(oh, axis=0, keepdims=True)
    cnt_s[...] = cnt
    cnt_ref[...] = cnt
    fields = (i1, i2, rank1, rank2, w1, w2)
    route = jnp.zeros(logits.shape, F32)
    for k, val in enumerate(fields):
        route = jnp.where(lane_i == k, val, route)
    route_ref[...] = route
    route_t_ref[...] = jnp.transpose(route)[:SUBLANES, :]


def _post(attn, rec, x, mod3, mod_row, g2, w_out, wr2, br, tri, cnt_in):
    n = x.shape[0]
    tm = TM_PRE
    const = lambda i: (0, 0)
    row = lambda w: pl.BlockSpec((tm, w), lambda i: (i, 0))
    return pl.pallas_call(
        _post_kernel,
        grid=(n // tm,),
        in_specs=[row(ATTN_W), row(LRU_W), row(D_MODEL),
                  pl.BlockSpec((1, 6, D_MODEL), lambda i: (mod_row(i * tm), 0, 0)),
                  pl.BlockSpec((1, D_MODEL), const),
                  pl.BlockSpec((D_MODEL, D_MODEL), const),
                  pl.BlockSpec((D_MODEL, 2 * LANES), const),
                  pl.BlockSpec((1, LANES), const),
                  pl.BlockSpec((tm, tm), const),
                  pl.BlockSpec((1, LANES), const)],
        out_specs=[row(D_MODEL), pl.BlockSpec((tm,) + ROW_TILE, lambda i: (i, 0, 0)), row(LANES),
                   pl.BlockSpec((1, LANES), const),
                   pl.BlockSpec((SUBLANES, tm), lambda i: (0, i))],
        out_shape=[jax.ShapeDtypeStruct((n, D_MODEL), F32),
                   jax.ShapeDtypeStruct((n,) + ROW_TILE, BF16),
                   jax.ShapeDtypeStruct((n, LANES), F32),
                   jax.ShapeDtypeStruct((1, LANES), F32),
                   jax.ShapeDtypeStruct((SUBLANES, n), F32)],
        scratch_shapes=[pltpu.VMEM((1, LANES), F32), pltpu.VMEM((D_MODEL, D_MODEL), BF16)],
        compiler_params=_cparams(("arbitrary",)),
        name="post",
    )(attn, rec, x, mod3, g2, w_out, wr2, br, tri, cnt_in)


def _row_copy(src_ref, src_row, dst_ref, dst_row, sem):
    return pltpu.make_async_copy(src_ref.at[pl.ds(src_row, 1)], dst_ref.at[pl.ds(dst_row, 1)], sem)


def _plan_kernel(route_t_ref, seg_ref, slots_ref):
    route_t = route_t_ref[...]
    t = route_t.shape[1]
    seg = jnp.concatenate([seg_ref[...]] * (t // LANES), axis=1)
    expert = lax.broadcasted_iota(jnp.int32, seg.shape, 0).astype(F32)
    rows = []
    for k in range(TOP_K):
        start = jnp.sum(jnp.where(expert == route_t[k:k + 1, :], seg, 0.0), axis=0, keepdims=True)
        rows.append(start + route_t[TOP_K + k:TOP_K + k + 1, :])
    rows.append(jnp.zeros((SUBLANES - TOP_K, t), F32))
    slots_ref[...] = jnp.concatenate(rows, axis=0).astype(jnp.int32)


def _plan(route_t, seg_rows):
    n = route_t.shape[1]
    tm = TM_PRE
    return pl.pallas_call(
        _plan_kernel,
        grid=(n // tm,),
        in_specs=[pl.BlockSpec((SUBLANES, tm), lambda i: (0, i)),
                  pl.BlockSpec((N_EXPERTS, LANES), lambda i: (0, 0))],
        out_specs=pl.BlockSpec((SUBLANES, tm), lambda i: (0, i)),
        out_shape=jax.ShapeDtypeStruct((SUBLANES, n), jnp.int32),
        compiler_params=_cparams(("arbitrary",)),
        name="plan",
    )(route_t, seg_rows)


def _dispatch_kernel(slots_p_ref, slots_s_ref, pad_start_ref, pad_rows_ref, nu_ref, h2p_ref, h2s_ref,
                     xs_ref, zero_s, sem, zsem, *, n_prompt):
    i = pl.program_id(0)
    ch = CH_DISPATCH

    def zero_fill(op):
        for e in range(N_EXPERTS):
            rows = pad_rows_ref[e]
            for b in range(TR_MOE.bit_length() - 1):
                size = 1 << b
                off = (rows >> (b + 1)) << (b + 1)

                @pl.when(((rows >> b) & 1) == 1)
                def _():
                    op(pltpu.make_async_copy(zero_s.at[pl.ds(0, size)],
                                             xs_ref.at[pl.ds(pad_start_ref[e] + off, size)], zsem))

        def zero_tile(t, carry):
            op(pltpu.make_async_copy(
                zero_s, xs_ref.at[pl.ds(pl.multiple_of(t * TR_MOE, TR_MOE), TR_MOE)], zsem))
            return carry
        lax.fori_loop(nu_ref[0], xs_ref.shape[0] // TR_MOE, zero_tile, 0)

    @pl.when(i == 0)
    def _():
        zero_s[...] = jnp.zeros_like(zero_s)
        zero_fill(lambda cp: cp.start())

    @pl.when(i == pl.num_programs(0) - 1)
    def _():
        zero_fill(lambda cp: cp.wait())

    def scatter(src_ref, slots_ref, first_token):
        n_path = slots_ref.shape[0] // TOP_K

        def body(j, carry):
            for k in range(TOP_K):
                _row_copy(src_ref, j, xs_ref, slots_ref[k * n_path + i * ch + j - first_token],
                          sem).start(priority=k)
            return carry
        lax.fori_loop(0, ch, body, 0, unroll=8)
        for _ in range(TOP_K):
            pltpu.make_async_copy(src_ref, xs_ref.at[pl.ds(0, ch)], sem).wait()

    @pl.when(i < n_prompt // ch)
    def _():
        scatter(h2p_ref, slots_p_ref, 0)

    @pl.when(i >= n_prompt // ch)
    def _():
        scatter(h2s_ref, slots_s_ref, n_prompt)


def _dispatch(slots_p, slots_s, pad_start, pad_rows, n_used, h2p, h2s, n_rows):
    n_prompt = h2p.shape[0]
    n = n_prompt + h2s.shape[0]
    ch = CH_DISPATCH
    npc = n_prompt // ch
    last_p = npc - 1
    return pl.pallas_call(
        functools.partial(_dispatch_kernel, n_prompt=n_prompt),
        grid_spec=pltpu.PrefetchScalarGridSpec(
            num_scalar_prefetch=5,
            grid=(n // ch,),
            in_specs=[pl.BlockSpec((ch,) + ROW_TILE, lambda i, *_: (jnp.minimum(i, last_p), 0, 0)),
                      pl.BlockSpec((ch,) + ROW_TILE, lambda i, *_: (jnp.maximum(i - npc, 0), 0, 0))],
            out_specs=pl.BlockSpec(memory_space=pl.ANY),
            scratch_shapes=[pltpu.VMEM((TR_MOE,) + ROW_TILE, BF16), pltpu.SemaphoreType.DMA,
                            pltpu.SemaphoreType.DMA]),
        out_shape=jax.ShapeDtypeStruct((n_rows,) + ROW_TILE, BF16),
        compiler_params=_cparams(("arbitrary",)),
        name="dispatch",
    )(slots_p, slots_s, pad_start, pad_rows, n_used, h2p, h2s)


def _ffn_kernel(tend_ref, xs_ref, wg_ref, wu_ref, wd_ref, ys_ref,
                xbuf, ybuf, wgu_s, wd_s, sem_in, sem_out):
    e = pl.program_id(0)
    n_used = tend_ref[N_EXPERTS - 1]
    t_first = jnp.where(e == 0, 0, tend_ref[jnp.maximum(e - 1, 0)])
    t_last = tend_ref[e]
    n_in = xbuf.shape[0]
    n_out = ybuf.shape[0]

    def tile_rows(t):
        return pl.ds(pl.multiple_of(t * TR_MOE, TR_MOE), TR_MOE)

    def fetch(t):
        return pltpu.make_async_copy(xs_ref.at[tile_rows(t)], xbuf.at[t % n_in], sem_in.at[t % n_in])

    def writeback(t):
        return pltpu.make_async_copy(ybuf.at[t % n_out], ys_ref.at[tile_rows(t)],
                                     sem_out.at[t % n_out])

    @pl.when(e == 0)
    def _():
        for t in range(n_in - 1):
            @pl.when(t < n_used)
            def _():
                fetch(t).start()

    @pl.when(t_last > t_first)
    def _():
        wgu_s[:, :EXPERT_FF] = wg_ref[0].astype(BF16)
        wgu_s[:, EXPERT_FF:] = wu_ref[0].astype(BF16)
        wd_s[...] = wd_ref[0].astype(BF16)

    def tile(t, carry):
        fetch(t).wait()

        @pl.when(t + n_in - 1 < n_used)
        def _():
            fetch(t + n_in - 1).start()

        @pl.when(t >= n_out)
        def _():
            writeback(t - n_out).wait()

        x = _from_row_tiles(xbuf[t % n_in])
        hgu = _dot(x, wgu_s[...])
        hg = hgu[:, :EXPERT_FF]
        hu = hgu[:, EXPERT_FF:]
        act = (hg * _sigmoid(hg)) * hu
        ybuf[t % n_out] = _to_row_tiles(_dot(act.astype(BF16), wd_s[...]).astype(BF16))
        writeback(t).start()
        return carry

    lax.fori_loop(t_first, t_last, tile, 0)

    @pl.when(e == N_EXPERTS - 1)
    def _():
        for back in range(n_out, 0, -1):
            @pl.when(n_used >= back)
            def _():
                writeback(n_used - back).wait()
        n_all = ys_ref.shape[0] // TR_MOE
        ybuf[0] = jnp.zeros(ybuf.shape[1:], ybuf.dtype)

        def zero_tile(t):
            return pltpu.make_async_copy(ybuf.at[0], ys_ref.at[tile_rows(t)], sem_out.at[0])

        lax.fori_loop(n_used, n_all, lambda t, c: (zero_tile(t).start(), c)[1], 0)
        lax.fori_loop(n_used, n_all, lambda t, c: (zero_tile(t).wait(), c)[1], 0)


def _ffn(tile_end, xs, wg, wu, wd, n_tiles):
    tr = TR_MOE
    wsel = lambda e, tend: (e, 0, 0)
    return pl.pallas_call(
        _ffn_kernel,
        grid_spec=pltpu.PrefetchScalarGridSpec(
            num_scalar_prefetch=1,
            grid=(N_EXPERTS,),
            in_specs=[pl.BlockSpec(memory_space=pl.ANY),
                      pl.BlockSpec((1, D_MODEL, EXPERT_FF), wsel),
                      pl.BlockSpec((1, D_MODEL, EXPERT_FF), wsel),
                      pl.BlockSpec((1, EXPERT_FF, D_MODEL), wsel)],
            out_specs=pl.BlockSpec(memory_space=pl.ANY),
            scratch_shapes=[pltpu.VMEM((FFN_IN_SLOTS, tr) + ROW_TILE, BF16),
                            pltpu.VMEM((FFN_OUT_SLOTS, tr) + ROW_TILE, BF16),
                            pltpu.VMEM((D_MODEL, 2 * EXPERT_FF), BF16),
                            pltpu.VMEM((EXPERT_FF, D_MODEL), BF16),
                            pltpu.SemaphoreType.DMA((FFN_IN_SLOTS,)),
                            pltpu.SemaphoreType.DMA((FFN_OUT_SLOTS,))]),
        out_shape=jax.ShapeDtypeStruct((n_tiles * tr,) + ROW_TILE, BF16),
        compiler_params=_cparams(("arbitrary",)),
        name="ffn",
    )(tile_end, xs, wg, wu, wd)


def _combine_kernel(slots_ref, ys_ref, x1_ref, route_ref, mod_ref, y_ref, b1_s, b2_s, sems):
    i = pl.program_id(0)
    tm = x1_ref.shape[0]
    n_path = slots_ref.shape[0] // TOP_K

    n_steps = pl.num_programs(0)
    ch = COMBINE_CHUNK

    def gather_chunk(step, slot, c):
        for t in range(ch):
            j = c * ch + t
            _row_copy(ys_ref, slots_ref[step * tm + j], b1_s.at[slot], j, sems.at[slot]).start(
                priority=0)
            _row_copy(ys_ref, slots_ref[n_path + step * tm + j], b2_s.at[slot], j,
                      sems.at[slot]).start(priority=1)

    def wait_slot(slot):
        for buf in (b1_s, b2_s):
            pltpu.make_async_copy(ys_ref.at[pl.ds(0, tm)], buf.at[slot], sems.at[slot]).wait()

    @pl.when(i == 0)
    def _():
        lax.fori_loop(0, tm // ch, lambda c, carry: (gather_chunk(0, 0, c), carry)[1], 0)

    slot = i % 2
    wait_slot(slot)
    gate = mod_ref[0][5:6]
    nxt = (i + 1) % n_steps

    def body(c, carry):
        rows = pl.ds(pl.multiple_of(c * ch, ch), ch)
        route = route_ref[rows, :]
        moe = (route[:, 4:5] * _from_row_tiles(b1_s[slot, rows]).astype(F32)
               + route[:, 5:6] * _from_row_tiles(b2_s[slot, rows]).astype(F32))
        y_ref[rows, :] = x1_ref[rows, :] + gate * moe
        gather_chunk(nxt, 1 - slot, c)
        return carry
    lax.fori_loop(0, tm // ch, body, 0)

    @pl.when(i == n_steps - 1)
    def _():
        wait_slot(1 - slot)


def _combine(slots, ys, x1, route, mod3, mod_row):
    n = x1.shape[0]
    tm = TM_PRE
    row = lambda w: pl.BlockSpec((tm, w), lambda i, *_: (i, 0))
    return pl.pallas_call(
        _combine_kernel,
        grid_spec=pltpu.PrefetchScalarGridSpec(
            num_scalar_prefetch=1,
            grid=(n // tm,),
            in_specs=[pl.BlockSpec(memory_space=pl.ANY), row(D_MODEL), row(LANES),
                      pl.BlockSpec((1, 6, D_MODEL),
                                   lambda i, *_: (mod_row(i * tm), 0, 0))],
            out_specs=row(D_MODEL),
            scratch_shapes=[pltpu.VMEM((2, tm) + ROW_TILE, BF16), pltpu.VMEM((2, tm) + ROW_TILE, BF16),
                            pltpu.SemaphoreType.DMA((2,))]),
        out_shape=jax.ShapeDtypeStruct((n, D_MODEL), F32),
        compiler_params=_cparams(("arbitrary",)),
        name="combine",
    )(slots, ys, x1, route, mod3)


def _rope_tables(length):
    rows = length // GRID_W
    r, col = jnp.meshgrid(jnp.arange(rows), jnp.arange(GRID_W), indexing='ij')
    r = r.reshape(-1).astype(F32)
    col = col.reshape(-1).astype(F32)
    half = HEAD_DIM // 2
    inv = ROPE_THETA ** (-jnp.arange(0, half, 2, dtype=F32) / half)
    ang_r = r[:, None] * inv
    ang_c = col[:, None] * inv
    ang = jnp.concatenate([ang_r, ang_r, ang_c, ang_c], axis=-1)
    sign = jnp.where((jnp.arange(HEAD_DIM) // (HEAD_DIM // 4)) % 2 == 0, -1.0, 1.0).astype(F32)
    cos = jnp.tile(jnp.cos(ang), (1, LANES // HEAD_DIM))
    sin = jnp.tile(jnp.sin(ang) * sign, (1, LANES // HEAD_DIM))
    return cos, sin


def _expand_cache_kernel(k_ref, v_ref, kx_ref, vx_ref):
    for ref, xref, one in ((k_ref, kx_ref, 0.0), (v_ref, vx_ref, 1.0)):
        col = jnp.transpose(ref[0].reshape(KV_W, ref.shape[-1]))
        _store_expanded(xref, col, one)


def _expand_cache(cache_k, cache_v):
    b, _, _, t = cache_k.shape
    src = pl.BlockSpec((1, N_KV_HEADS, HEAD_DIM, t), lambda i: (i, 0, 0, 0))
    dst = pl.BlockSpec((t, 4 * LANES), lambda i: (i, 0))
    return pl.pallas_call(
        _expand_cache_kernel,
        grid=(b,),
        in_specs=[src, src], out_specs=[dst, dst],
        out_shape=[jax.ShapeDtypeStruct((b * t, 4 * LANES), BF16)] * 2,
        compiler_params=_cparams(("arbitrary",)),
        name="expand_cache",
    )(cache_k, cache_v)


def kernel(x_prompt, x_sample, cache_k, cache_v, state_lru, c, c_ctx, w_mod, b_mod, norm1, norm2,
           w_in, q_norm, k_norm, conv_w, conv_b, lru_wa, lru_ba, lru_wx, lru_bx, lru_lambda, w_out,
           router_grp_w, router_grp_b, router_exp_w, router_exp_b, exp_w_gate, exp_w_up, exp_w_down):
    batch, seq, _ = x_prompt.shape
    dec_batch, dec_seq, _ = x_sample.shape
    past = cache_k.shape[2]
    depth = w_mod.shape[0]
    assert depth == 1
    assert x_prompt.shape[2] == D_MODEL and w_in.shape[1:] == (D_MODEL, IN_W)
    assert exp_w_gate.shape[1:] == (N_EXPERTS, D_MODEL, EXPERT_FF)
    assert cache_k.shape[0] == dec_batch and cache_k.shape[3:] == (N_KV_HEADS, HEAD_DIM)
    assert (batch * seq) % CH_DISPATCH == 0 and (dec_batch * dec_seq) % CH_DISPATCH == 0
    assert dec_batch < MOD_ROWS

    cvec = jnp.concatenate(
        [c_ctx[None, :], c, jnp.zeros((MOD_ROWS - 1 - dec_batch, D_MODEL), F32)], axis=0)
    mod3 = _modulation(cvec, w_mod[0], b_mod[0][None, :]).reshape(MOD_ROWS, 6, D_MODEL)

    head_id = jnp.arange(QK_W) // HEAD_DIM
    heads = (head_id[:, None] == jnp.arange(LANES)[None, :]).astype(BF16)
    gqk = jnp.concatenate([jnp.tile(q_norm[0], N_HEADS), jnp.tile(k_norm[0], N_KV_HEADS)])[None, :]
    pad = LANES - N_EXPERTS - N_GROUPS
    wr = jnp.concatenate([router_exp_w[0], router_grp_w[0], jnp.zeros((D_MODEL, pad), F32)], axis=1)
    wr_hi = wr.astype(BF16)
    wr2 = jnp.concatenate([wr_hi, (wr - wr_hi.astype(F32)).astype(BF16)], axis=1)
    br = jnp.concatenate([router_exp_b[0], router_grp_b[0], jnp.zeros((pad,), F32)])[None, :]
    g1 = norm1[0][None, :]
    g2 = norm2[0][None, :]
    cw = conv_w[0]
    cb = conv_b[0][None, :]
    lam = lru_lambda[0]
    tri = (jnp.arange(TM_PRE)[:, None] > jnp.arange(TM_PRE)[None, :]).astype(BF16)

    def mixers(x, seq_len, mod_row, tables, extra_k, extra_v, h0, cnt_in):
        q, kx, vx, xr, gb, *cache = _pre(x, mod3, mod_row, g1, w_in[0], heads, gqk, tables, seq_len)
        k_segs = [(kx, seq_len)] + extra_k
        v_segs = [(vx, seq_len)] + extra_v
        attn = _attention(q, k_segs, v_segs, seq_len)
        rec, fin = _lru(xr, gb, h0, cw, cb, lru_wa[0], lru_wx[0], lru_ba[0], lru_bx[0], lam, seq_len)
        x1, h2, route, cnt, route_t = _post(attn, rec, x, mod3, mod_row, g2, w_out[0], wr2, br, tri,
                                            cnt_in)
        return x1, h2, (route, route_t), cnt, cache, fin

    mod_row_p = lambda tok: 0
    mod_row_s = lambda tok: tok // dec_seq + 1
    xp = x_prompt.reshape(batch * seq, D_MODEL)
    x1p, h2p, (route_p, route_t_p), cnt_p, (kf, vf), fin = mixers(
        xp, seq, mod_row_p, None, [], [], jnp.zeros((batch, 2, LRU_W), F32),
        jnp.zeros((1, LANES), F32))
    xs = x_sample.reshape(dec_batch * dec_seq, D_MODEL)
    ck, cv = _expand_cache(jnp.transpose(cache_k[:, 0], (0, 2, 3, 1)),
                           jnp.transpose(cache_v[:, 0], (0, 2, 3, 1)))
    x1s, h2s, (route_s, route_t_s), cnt_all, _, _ = mixers(
        xs, dec_seq, mod_row_s, _rope_tables(dec_seq), [(ck, past)], [(cv, past)],
        state_lru[:, 0], cnt_p)

    n_prompt = batch * seq
    n_tok = n_prompt + dec_batch * dec_seq
    n_tiles = (TOP_K * n_tok + N_EXPERTS * (TR_MOE - 1)) // TR_MOE
    cnt = cnt_all[0, :N_EXPERTS].astype(jnp.int32)
    ntile = (cnt + TR_MOE - 1) // TR_MOE
    tile_end = jnp.cumsum(ntile)
    seg_start = (tile_end - ntile) * TR_MOE
    n_used = tile_end[-1:]
    pad_start = seg_start + cnt
    pad_rows = tile_end * TR_MOE - pad_start

    seg_rows = jnp.broadcast_to(seg_start.astype(F32)[:, None], (N_EXPERTS, LANES))
    slots_p = _plan(route_t_p, seg_rows)[:TOP_K].reshape(-1)
    slots_s = _plan(route_t_s, seg_rows)[:TOP_K].reshape(-1)
    xsort = _dispatch(slots_p, slots_s, pad_start, pad_rows, n_used, h2p, h2s,
                      n_tiles * TR_MOE)
    ysort = _ffn(tile_end, xsort, exp_w_gate[0], exp_w_up[0], exp_w_down[0], n_tiles)
    yp = _combine(slots_p, ysort, x1p, route_p, mod3, mod_row_p)
    ys = _combine(slots_s, ysort, x1s, route_s, mod3, mod_row_s)

    return (yp.reshape(batch, seq, D_MODEL),
            ys.reshape(dec_batch, dec_seq, D_MODEL),
            jnp.transpose(kf, (0, 3, 1, 2))[:, None],
            jnp.transpose(vf, (0, 3, 1, 2))[:, None],
            fin.reshape(batch, 1, 2, LRU_W))
```
